```python
import math
import jax, jax.numpy as jnp
from jax import lax
import numpy as np

D_MODEL = 1024
BATCH = 16
SEQ = 4096
DEPTH = 4

CHUNK = 64
D_PLE = 256
W_BRANCH = D_MODEL // 2
N_BRANCH_COLS = 6 * W_BRANCH
CONV_A_WIDTH = 31
GMLP_BLOCK = 2 * CHUNK
N_HEADS_B = 8
HEAD_DIM_B = W_BRANCH // N_HEADS_B
POOL_WINDOWS = (2, 4, 8, 16)
N_GROUPS_C = len(POOL_WINDOWS)
GROUP_DIM_C = W_BRANCH // N_GROUPS_C
CONV_D_WIDTH = 3
DEEPNORM_ALPHA = (2.0 * DEPTH) ** 0.25
DEEPNORM_BETA = (8.0 * DEPTH) ** -0.25
LN_EPS = 1e-5

kernel_name = 'hybrid_conv_gmlp_pool_shortconv_deepnorm'


def layer_norm(x, g, b):
    xf = x.astype(jnp.float32)
    mu = jnp.mean(xf, axis=-1, keepdims=True)
    var = jnp.mean(jnp.square(xf - mu), axis=-1, keepdims=True)
    y = (xf - mu) * lax.rsqrt(var + LN_EPS)
    return (y * g.astype(jnp.float32) + b.astype(jnp.float32)).astype(x.dtype)


def causal_depthwise_conv(x, w):
    k, c = w.shape
    return lax.conv_general_dilated(
        x, w[:, None, :], window_strides=(1,), padding=[(k - 1, 0)],
        dimension_numbers=('NWC', 'WIO', 'NWC'), feature_group_count=c)


def even_mixer(x, w_in, b_in, conv_w, conv_b, ln_a_g, ln_a_b, ln_v_g, ln_v_b, w_s, b_s, w_out, b_out):
    bt, s, _ = x.shape
    z = jnp.einsum('bsd,de->bse', x, w_in) + b_in
    a_val, a_glu, a_gate, u, v, g_gate = jnp.split(z, 6, axis=-1)
    a = a_val * jax.nn.sigmoid(a_glu)
    a = causal_depthwise_conv(a, conv_w) + conv_b
    a = jax.nn.silu(layer_norm(a, ln_a_g, ln_a_b)) * jax.nn.silu(a_gate)
    u = jax.nn.gelu(u)
    v = layer_norm(jax.nn.gelu(v), ln_v_g, ln_v_b)
    v = v.reshape(bt, s // GMLP_BLOCK, GMLP_BLOCK, N_HEADS_B, HEAD_DIM_B)
    mask = jnp.tril(jnp.ones((GMLP_BLOCK, GMLP_BLOCK), dtype=bool))
    w_s = jnp.where(mask[None], w_s, jnp.zeros_like(w_s))
    sg = jnp.einsum('hts,bnshd->bnthd', w_s, v) + b_s.T[:, :, None]
    g = u * sg.reshape(bt, s, W_BRANCH) * jax.nn.silu(g_gate)
    y = jnp.concatenate([a, g], axis=-1)
    return jnp.einsum('bse,ed->bsd', y, w_out) + b_out


def odd_mixer(x, w_in, b_in, w_pool, pool_scale, conv_w, w_out, b_out):
    bt, s, _ = x.shape
    z = jnp.einsum('bsd,de->bse', x, w_in) + b_in
    c_val, c_gate, d_h, d_b, d_c, d_gate = jnp.split(z, 6, axis=-1)
    vg = c_val.reshape(bt, s, N_GROUPS_C, GROUP_DIM_C)
    cs = jnp.cumsum(vg.astype(jnp.float32), axis=1)
    pos = jnp.arange(1, s + 1, dtype=jnp.float32)
    means = []
    for gi, win in enumerate(POOL_WINDOWS):
        c_g = cs[:, :, gi]
        lag = jnp.pad(c_g, ((0, 0), (win, 0), (0, 0)))[:, :s]
        means.append((c_g - lag) / jnp.minimum(pos, float(win))[:, None])
    pooled = jnp.stack(means, axis=2).astype(vg.dtype) - vg
    c = jnp.einsum('bsgc,gce->bsge', pooled, w_pool).reshape(bt, s, W_BRANCH) * pool_scale
    c = c * jax.nn.silu(c_gate)
    d = d_b * causal_depthwise_conv(d_c * d_h, conv_w)
    d = d * jax.nn.silu(d_gate)
    y = jnp.concatenate([c, d], axis=-1)
    return jnp.einsum('bse,ed->bsd', y, w_out) + b_out


def _fwd_setup_inputs(seed: int = 0) -> dict:
    key = jax.random.key(seed)
    ks = jax.random.split(key, 26)
    ne = (DEPTH + 1) // 2
    no = DEPTH // 2
    f32 = jnp.float32
    nrm = lambda k, shape, scale: jax.random.normal(k, shape, f32) * scale
    return {
        'x': nrm(ks[0], (BATCH, SEQ, D_MODEL), 1.0),
        'p': nrm(ks[1], (DEPTH, BATCH, SEQ, D_PLE), 1.0),
        'w_in_e': nrm(ks[2], (ne, D_MODEL, N_BRANCH_COLS), D_MODEL ** -0.5),
        'b_in_e': nrm(ks[3], (ne, N_BRANCH_COLS), 0.02),
        'conv_a_w': nrm(ks[4], (ne, CONV_A_WIDTH, W_BRANCH), CONV_A_WIDTH ** -0.5),
        'conv_a_b': nrm(ks[5], (ne, W_BRANCH), 0.02),
        'ln_a_g': 1.0 + nrm(ks[6], (ne, W_BRANCH), 0.05),
        'ln_a_b': nrm(ks[7], (ne, W_BRANCH), 0.02),
        'ln_v_g': 1.0 + nrm(ks[8], (ne, W_BRANCH), 0.05),
        'ln_v_b': nrm(ks[9], (ne, W_BRANCH), 0.02),
        'w_s': nrm(ks[10], (ne, N_HEADS_B, GMLP_BLOCK, GMLP_BLOCK), 0.5 * GMLP_BLOCK ** -0.5),
        'b_s': 1.0 + nrm(ks[11], (ne, N_HEADS_B, GMLP_BLOCK), 0.1),
        'w_out_e': nrm(ks[12], (ne, 2 * W_BRANCH, D_MODEL), DEEPNORM_BETA * (2 * W_BRANCH) ** -0.5),
        'b_out_e': nrm(ks[13], (ne, D_MODEL), 0.02),
        'w_in_o': nrm(ks[14], (no, D_MODEL, N_BRANCH_COLS), D_MODEL ** -0.5),
        'b_in_o': nrm(ks[15], (no, N_BRANCH_COLS), 0.02),
        'w_pool': nrm(ks[16], (no, N_GROUPS_C, GROUP_DIM_C, GROUP_DIM_C), GROUP_DIM_C ** -0.5),
        'pool_scale': 1.0 + nrm(ks[17], (no, W_BRANCH), 0.1),
        'conv_d_w': nrm(ks[18], (no, CONV_D_WIDTH, W_BRANCH), CONV_D_WIDTH ** -0.5),
        'w_out_o': nrm(ks[19], (no, 2 * W_BRANCH, D_MODEL), DEEPNORM_BETA * (2 * W_BRANCH) ** -0.5),
        'b_out_o': nrm(ks[20], (no, D_MODEL), 0.02),
        'ln_g': 1.0 + nrm(ks[21], (DEPTH, D_MODEL), 0.05),
        'ln_b': nrm(ks[22], (DEPTH, D_MODEL), 0.02),
        'w_ple': nrm(ks[23], (DEPTH, D_PLE, D_MODEL), D_PLE ** -0.5),
        'w_ple_gate': nrm(ks[24], (DEPTH, D_MODEL, D_MODEL), D_MODEL ** -0.5),
        'b_ple_gate': nrm(ks[25], (DEPTH, D_MODEL), 0.02),
    }


def _fwd_reference(x, p, w_in_e, b_in_e, conv_a_w, conv_a_b, ln_a_g, ln_a_b, ln_v_g, ln_v_b, w_s, b_s,
              w_out_e, b_out_e, w_in_o, b_in_o, w_pool, pool_scale, conv_d_w, w_out_o, b_out_o,
              ln_g, ln_b, w_ple, w_ple_gate, b_ple_gate):
    for i in range(DEPTH):
        j = i // 2
        if i % 2 == 0:
            out = even_mixer(x, w_in_e[j], b_in_e[j], conv_a_w[j], conv_a_b[j], ln_a_g[j], ln_a_b[j],
                             ln_v_g[j], ln_v_b[j], w_s[j], b_s[j], w_out_e[j], b_out_e[j])
        else:
            out = odd_mixer(x, w_in_o[j], b_in_o[j], w_pool[j], pool_scale[j], conv_d_w[j],
                            w_out_o[j], b_out_o[j])
        h = layer_norm(DEEPNORM_ALPHA * x + out, ln_g[i], ln_b[i])
        gate = jax.nn.sigmoid(jnp.einsum('bsd,de->bse', h, w_ple_gate[i]) + b_ple_gate[i])
        x = h + gate * jnp.einsum('bsk,kd->bsd', p[i], w_ple[i])
    return x


import jax as _jax
import jax.numpy as _jnp

TWIN_FORMAT = 'train_step'
FWD_PARAMS = ['x', 'p', 'w_in_e', 'b_in_e', 'conv_a_w', 'conv_a_b', 'ln_a_g', 'ln_a_b', 'ln_v_g', 'ln_v_b', 'w_s', 'b_s', 'w_out_e', 'b_out_e', 'w_in_o', 'b_in_o', 'w_pool', 'pool_scale', 'conv_d_w', 'w_out_o', 'b_out_o', 'ln_g', 'ln_b', 'w_ple', 'w_ple_gate', 'b_ple_gate']
TWIN_WEIGHTS = ['w_in_e', 'b_in_e', 'conv_a_w', 'conv_a_b', 'ln_a_g', 'ln_a_b', 'ln_v_g', 'ln_v_b', 'w_s', 'b_s', 'w_out_e', 'b_out_e', 'w_in_o', 'b_in_o', 'w_pool', 'pool_scale', 'conv_d_w', 'w_out_o', 'b_out_o', 'ln_g', 'ln_b', 'w_ple', 'w_ple_gate', 'b_ple_gate']
TWIN_DIFF_INPUT = 'x'
TWIN_INPUTS = ['x', 'p', 'w_in_e', 'b_in_e', 'conv_a_w', 'conv_a_b', 'ln_a_g', 'ln_a_b', 'ln_v_g', 'ln_v_b', 'w_s', 'b_s', 'w_out_e', 'b_out_e', 'w_in_o', 'b_in_o', 'w_pool', 'pool_scale', 'conv_d_w', 'w_out_o', 'b_out_o', 'ln_g', 'ln_b', 'w_ple', 'w_ple_gate', 'b_ple_gate', 'loss_target', 'm_w_in_e', 'm_b_in_e', 'm_conv_a_w', 'm_conv_a_b', 'm_ln_a_g', 'm_ln_a_b', 'm_ln_v_g', 'm_ln_v_b', 'm_w_s', 'm_b_s', 'm_w_out_e', 'm_b_out_e', 'm_w_in_o', 'm_b_in_o', 'm_w_pool', 'm_pool_scale', 'm_conv_d_w', 'm_w_out_o', 'm_b_out_o', 'm_ln_g', 'm_ln_b', 'm_w_ple', 'm_w_ple_gate', 'm_b_ple_gate', 'v_w_in_e', 'v_b_in_e', 'v_conv_a_w', 'v_conv_a_b', 'v_ln_a_g', 'v_ln_a_b', 'v_ln_v_g', 'v_ln_v_b', 'v_w_s', 'v_b_s', 'v_w_out_e', 'v_b_out_e', 'v_w_in_o', 'v_b_in_o', 'v_w_pool', 'v_pool_scale', 'v_conv_d_w', 'v_w_out_o', 'v_b_out_o', 'v_ln_g', 'v_ln_b', 'v_w_ple', 'v_w_ple_gate', 'v_b_ple_gate']
TWIN_OUTPUTS = ['loss', 'grad_x', 'grad_w_in_e', 'grad_b_in_e', 'grad_conv_a_w', 'grad_conv_a_b', 'grad_ln_a_g', 'grad_ln_a_b', 'grad_ln_v_g', 'grad_ln_v_b', 'grad_w_s', 'grad_b_s', 'grad_w_out_e', 'grad_b_out_e', 'grad_w_in_o', 'grad_b_in_o', 'grad_w_pool', 'grad_pool_scale', 'grad_conv_d_w', 'grad_w_out_o', 'grad_b_out_o', 'grad_ln_g', 'grad_ln_b', 'grad_w_ple', 'grad_w_ple_gate', 'grad_b_ple_gate', 'delta_w_in_e', 'delta_b_in_e', 'delta_conv_a_w', 'delta_conv_a_b', 'delta_ln_a_g', 'delta_ln_a_b', 'delta_ln_v_g', 'delta_ln_v_b', 'delta_w_s', 'delta_b_s', 'delta_w_out_e', 'delta_b_out_e', 'delta_w_in_o', 'delta_b_in_o', 'delta_w_pool', 'delta_pool_scale', 'delta_conv_d_w', 'delta_w_out_o', 'delta_b_out_o', 'delta_ln_g', 'delta_ln_b', 'delta_w_ple', 'delta_w_ple_gate', 'delta_b_ple_gate', 'new_m_w_in_e', 'new_m_b_in_e', 'new_m_conv_a_w', 'new_m_conv_a_b', 'new_m_ln_a_g', 'new_m_ln_a_b', 'new_m_ln_v_g', 'new_m_ln_v_b', 'new_m_w_s', 'new_m_b_s', 'new_m_w_out_e', 'new_m_b_out_e', 'new_m_w_in_o', 'new_m_b_in_o', 'new_m_w_pool', 'new_m_pool_scale', 'new_m_conv_d_w', 'new_m_w_out_o', 'new_m_b_out_o', 'new_m_ln_g', 'new_m_ln_b', 'new_m_w_ple', 'new_m_w_ple_gate', 'new_m_b_ple_gate', 'new_v_w_in_e', 'new_v_b_in_e', 'new_v_conv_a_w', 'new_v_conv_a_b', 'new_v_ln_a_g', 'new_v_ln_a_b', 'new_v_ln_v_g', 'new_v_ln_v_b', 'new_v_w_s', 'new_v_b_s', 'new_v_w_out_e', 'new_v_b_out_e', 'new_v_w_in_o', 'new_v_b_in_o', 'new_v_w_pool', 'new_v_pool_scale', 'new_v_conv_d_w', 'new_v_w_out_o', 'new_v_b_out_o', 'new_v_ln_g', 'new_v_ln_b', 'new_v_w_ple', 'new_v_w_ple_gate', 'new_v_b_ple_gate']
TWIN_LEAF_KINDS = {'loss': 'loss', 'grad_x': 'grad_x', 'grad_w_in_e': 'grad_w', 'grad_b_in_e': 'grad_w', 'grad_conv_a_w': 'grad_w', 'grad_conv_a_b': 'grad_w', 'grad_ln_a_g': 'grad_w', 'grad_ln_a_b': 'grad_w', 'grad_ln_v_g': 'grad_w', 'grad_ln_v_b': 'grad_w', 'grad_w_s': 'grad_w', 'grad_b_s': 'grad_w', 'grad_w_out_e': 'grad_w', 'grad_b_out_e': 'grad_w', 'grad_w_in_o': 'grad_w', 'grad_b_in_o': 'grad_w', 'grad_w_pool': 'grad_w', 'grad_pool_scale': 'grad_w', 'grad_conv_d_w': 'grad_w', 'grad_w_out_o': 'grad_w', 'grad_b_out_o': 'grad_w', 'grad_ln_g': 'grad_w', 'grad_ln_b': 'grad_w', 'grad_w_ple': 'grad_w', 'grad_w_ple_gate': 'grad_w', 'grad_b_ple_gate': 'grad_w', 'delta_w_in_e': 'delta_w', 'delta_b_in_e': 'delta_w', 'delta_conv_a_w': 'delta_w', 'delta_conv_a_b': 'delta_w', 'delta_ln_a_g': 'delta_w', 'delta_ln_a_b': 'delta_w', 'delta_ln_v_g': 'delta_w', 'delta_ln_v_b': 'delta_w', 'delta_w_s': 'delta_w', 'delta_b_s': 'delta_w', 'delta_w_out_e': 'delta_w', 'delta_b_out_e': 'delta_w', 'delta_w_in_o': 'delta_w', 'delta_b_in_o': 'delta_w', 'delta_w_pool': 'delta_w', 'delta_pool_scale': 'delta_w', 'delta_conv_d_w': 'delta_w', 'delta_w_out_o': 'delta_w', 'delta_b_out_o': 'delta_w', 'delta_ln_g': 'delta_w', 'delta_ln_b': 'delta_w', 'delta_w_ple': 'delta_w', 'delta_w_ple_gate': 'delta_w', 'delta_b_ple_gate': 'delta_w', 'new_m_w_in_e': 'new_m', 'new_m_b_in_e': 'new_m', 'new_m_conv_a_w': 'new_m', 'new_m_conv_a_b': 'new_m', 'new_m_ln_a_g': 'new_m', 'new_m_ln_a_b': 'new_m', 'new_m_ln_v_g': 'new_m', 'new_m_ln_v_b': 'new_m', 'new_m_w_s': 'new_m', 'new_m_b_s': 'new_m', 'new_m_w_out_e': 'new_m', 'new_m_b_out_e': 'new_m', 'new_m_w_in_o': 'new_m', 'new_m_b_in_o': 'new_m', 'new_m_w_pool': 'new_m', 'new_m_pool_scale': 'new_m', 'new_m_conv_d_w': 'new_m', 'new_m_w_out_o': 'new_m', 'new_m_b_out_o': 'new_m', 'new_m_ln_g': 'new_m', 'new_m_ln_b': 'new_m', 'new_m_w_ple': 'new_m', 'new_m_w_ple_gate': 'new_m', 'new_m_b_ple_gate': 'new_m', 'new_v_w_in_e': 'new_v', 'new_v_b_in_e': 'new_v', 'new_v_conv_a_w': 'new_v', 'new_v_conv_a_b': 'new_v', 'new_v_ln_a_g': 'new_v', 'new_v_ln_a_b': 'new_v', 'new_v_ln_v_g': 'new_v', 'new_v_ln_v_b': 'new_v', 'new_v_w_s': 'new_v', 'new_v_b_s': 'new_v', 'new_v_w_out_e': 'new_v', 'new_v_b_out_e': 'new_v', 'new_v_w_in_o': 'new_v', 'new_v_b_in_o': 'new_v', 'new_v_w_pool': 'new_v', 'new_v_pool_scale': 'new_v', 'new_v_conv_d_w': 'new_v', 'new_v_w_out_o': 'new_v', 'new_v_b_out_o': 'new_v', 'new_v_ln_g': 'new_v', 'new_v_ln_b': 'new_v', 'new_v_w_ple': 'new_v', 'new_v_w_ple_gate': 'new_v', 'new_v_b_ple_gate': 'new_v'}


def _forward(args):
    return _fwd_reference(*[args[k] for k in FWD_PARAMS])


def _output_shape():
    out = _jax.eval_shape(lambda: _forward(_fwd_setup_inputs(0)))
    return out.shape, out.dtype

N_MICROBATCH = 1
ADAM_LR = 0.001
ADAM_B1 = 0.9
ADAM_B2 = 0.999
ADAM_EPS = 1e-08
ADAM_WD = 0.01
ADAM_STEP = 10
PER_EXAMPLE_BATCH_AXIS = {'x': 0, 'p': 1, 'loss_target': 0}
SHARED_INPUTS = []
_WEIGHT_DTYPES = {'w_in_e': _jnp.float32, 'b_in_e': _jnp.float32, 'conv_a_w': _jnp.float32, 'conv_a_b': _jnp.float32, 'ln_a_g': _jnp.float32, 'ln_a_b': _jnp.float32, 'ln_v_g': _jnp.float32, 'ln_v_b': _jnp.float32, 'w_s': _jnp.float32, 'b_s': _jnp.float32, 'w_out_e': _jnp.float32, 'b_out_e': _jnp.float32, 'w_in_o': _jnp.float32, 'b_in_o': _jnp.float32, 'w_pool': _jnp.float32, 'pool_scale': _jnp.float32, 'conv_d_w': _jnp.float32, 'w_out_o': _jnp.float32, 'b_out_o': _jnp.float32, 'ln_g': _jnp.float32, 'ln_b': _jnp.float32, 'w_ple': _jnp.float32, 'w_ple_gate': _jnp.float32, 'b_ple_gate': _jnp.float32}
MOMENT_SCALE = {'w_in_e': 2.422983e-02, 'b_in_e': 8.218262e-02, 'conv_a_w': 2.460587e-02, 'conv_a_b': 1.312158e-01, 'ln_a_g': 5.847750e-02, 'ln_a_b': 9.235007e-02, 'ln_v_g': 1.033640e-02, 'ln_v_b': 1.001062e-02, 'w_s': 1.438200e-02, 'b_s': 2.034632e-02, 'w_out_e': 1.119001e-01, 'b_out_e': 2.841589e+00, 'w_in_o': 5.739288e-02, 'b_in_o': 5.349713e-02, 'w_pool': 4.333849e-02, 'pool_scale': 4.471718e-02, 'conv_d_w': 6.474501e-02, 'w_out_o': 1.275116e-01, 'b_out_o': 3.077778e+00, 'ln_g': 3.343520e+01, 'ln_b': 5.662818e+00, 'w_ple': 3.344968e-01, 'w_ple_gate': 8.961551e-02, 'b_ple_gate': 3.254538e+00}


def _to_microbatches(a, axis):
    t = _jnp.moveaxis(a, axis, 0)
    t = t.reshape((N_MICROBATCH, t.shape[0] // N_MICROBATCH) + t.shape[1:])
    return _jnp.moveaxis(t, 1, axis + 1)


def setup_inputs(seed: int = 0) -> dict:
    inp = _fwd_setup_inputs(seed)
    key = _jax.random.fold_in(_jax.random.key(seed), 7919)
    shape, _ = _output_shape()
    out = dict(inp)
    out["loss_target"] = _jax.random.normal(_jax.random.fold_in(key, 0), shape, _jnp.float32)
    for i, name in enumerate(TWIN_WEIGHTS):
        w = inp[name].astype(_jnp.float32)
        if MOMENT_SCALE is None:
            s = _jnp.sqrt(_jnp.mean(_jnp.square(w)) + 1e-30)
        else:
            s = MOMENT_SCALE[name]
        km, kv = _jax.random.split(_jax.random.fold_in(key, i + 1))
        out[name] = w
        out["m_" + name] = s * _jax.random.normal(km, w.shape, _jnp.float32)
        out["v_" + name] = (s * s) * _jax.random.uniform(kv, w.shape, _jnp.float32, 0.5, 1.5)
    if N_MICROBATCH > 1:
        for name, axis in PER_EXAMPLE_BATCH_AXIS.items():
            out[name] = _to_microbatches(out[name], axis)
    return {'x': out['x'], 'p': out['p'], 'w_in_e': out['w_in_e'], 'b_in_e': out['b_in_e'], 'conv_a_w': out['conv_a_w'], 'conv_a_b': out['conv_a_b'], 'ln_a_g': out['ln_a_g'], 'ln_a_b': out['ln_a_b'], 'ln_v_g': out['ln_v_g'], 'ln_v_b': out['ln_v_b'], 'w_s': out['w_s'], 'b_s': out['b_s'], 'w_out_e': out['w_out_e'], 'b_out_e': out['b_out_e'], 'w_in_o': out['w_in_o'], 'b_in_o': out['b_in_o'], 'w_pool': out['w_pool'], 'pool_scale': out['pool_scale'], 'conv_d_w': out['conv_d_w'], 'w_out_o': out['w_out_o'], 'b_out_o': out['b_out_o'], 'ln_g': out['ln_g'], 'ln_b': out['ln_b'], 'w_ple': out['w_ple'], 'w_ple_gate': out['w_ple_gate'], 'b_ple_gate': out['b_ple_gate'], 'loss_target': out['loss_target'], 'm_w_in_e': out['m_w_in_e'], 'm_b_in_e': out['m_b_in_e'], 'm_conv_a_w': out['m_conv_a_w'], 'm_conv_a_b': out['m_conv_a_b'], 'm_ln_a_g': out['m_ln_a_g'], 'm_ln_a_b': out['m_ln_a_b'], 'm_ln_v_g': out['m_ln_v_g'], 'm_ln_v_b': out['m_ln_v_b'], 'm_w_s': out['m_w_s'], 'm_b_s': out['m_b_s'], 'm_w_out_e': out['m_w_out_e'], 'm_b_out_e': out['m_b_out_e'], 'm_w_in_o': out['m_w_in_o'], 'm_b_in_o': out['m_b_in_o'], 'm_w_pool': out['m_w_pool'], 'm_pool_scale': out['m_pool_scale'], 'm_conv_d_w': out['m_conv_d_w'], 'm_w_out_o': out['m_w_out_o'], 'm_b_out_o': out['m_b_out_o'], 'm_ln_g': out['m_ln_g'], 'm_ln_b': out['m_ln_b'], 'm_w_ple': out['m_w_ple'], 'm_w_ple_gate': out['m_w_ple_gate'], 'm_b_ple_gate': out['m_b_ple_gate'], 'v_w_in_e': out['v_w_in_e'], 'v_b_in_e': out['v_b_in_e'], 'v_conv_a_w': out['v_conv_a_w'], 'v_conv_a_b': out['v_conv_a_b'], 'v_ln_a_g': out['v_ln_a_g'], 'v_ln_a_b': out['v_ln_a_b'], 'v_ln_v_g': out['v_ln_v_g'], 'v_ln_v_b': out['v_ln_v_b'], 'v_w_s': out['v_w_s'], 'v_b_s': out['v_b_s'], 'v_w_out_e': out['v_w_out_e'], 'v_b_out_e': out['v_b_out_e'], 'v_w_in_o': out['v_w_in_o'], 'v_b_in_o': out['v_b_in_o'], 'v_w_pool': out['v_w_pool'], 'v_pool_scale': out['v_pool_scale'], 'v_conv_d_w': out['v_conv_d_w'], 'v_w_out_o': out['v_w_out_o'], 'v_b_out_o': out['v_b_out_o'], 'v_ln_g': out['v_ln_g'], 'v_ln_b': out['v_ln_b'], 'v_w_ple': out['v_w_ple'], 'v_w_ple_gate': out['v_w_ple_gate'], 'v_b_ple_gate': out['v_b_ple_gate']}


def _loss(weights, diff, rest, loss_target):
    with _jax.named_scope("forward"):
        args = {**rest, TWIN_DIFF_INPUT: diff, **{k: w.astype(_WEIGHT_DTYPES[k]) for k, w in weights.items()}}
        y = _forward(args)
    with _jax.named_scope("loss_head"):
        err = _jnp.square(y.astype(_jnp.float32) - loss_target)
        return 0.5 * _jnp.sum(_jnp.mean(err, axis=-1)) if err.ndim else 0.5 * err


def _adamw(w, g, m, v):
    m = ADAM_B1 * m + (1.0 - ADAM_B1) * g
    v = ADAM_B2 * v + (1.0 - ADAM_B2) * _jnp.square(g)
    m_hat = m / (1.0 - ADAM_B1 ** ADAM_STEP)
    v_hat = v / (1.0 - ADAM_B2 ** ADAM_STEP)
    delta = -ADAM_LR * (m_hat / (_jnp.sqrt(v_hat) + ADAM_EPS) + ADAM_WD * w)
    return delta, m, v


def reference(x, p, w_in_e, b_in_e, conv_a_w, conv_a_b, ln_a_g, ln_a_b, ln_v_g, ln_v_b, w_s, b_s, w_out_e, b_out_e, w_in_o, b_in_o, w_pool, pool_scale, conv_d_w, w_out_o, b_out_o, ln_g, ln_b, w_ple, w_ple_gate, b_ple_gate, loss_target, m_w_in_e, m_b_in_e, m_conv_a_w, m_conv_a_b, m_ln_a_g, m_ln_a_b, m_ln_v_g, m_ln_v_b, m_w_s, m_b_s, m_w_out_e, m_b_out_e, m_w_in_o, m_b_in_o, m_w_pool, m_pool_scale, m_conv_d_w, m_w_out_o, m_b_out_o, m_ln_g, m_ln_b, m_w_ple, m_w_ple_gate, m_b_ple_gate, v_w_in_e, v_b_in_e, v_conv_a_w, v_conv_a_b, v_ln_a_g, v_ln_a_b, v_ln_v_g, v_ln_v_b, v_w_s, v_b_s, v_w_out_e, v_b_out_e, v_w_in_o, v_b_in_o, v_w_pool, v_pool_scale, v_conv_d_w, v_w_out_o, v_b_out_o, v_ln_g, v_ln_b, v_w_ple, v_w_ple_gate, v_b_ple_gate):
    given = dict(x=x, p=p, w_in_e=w_in_e, b_in_e=b_in_e, conv_a_w=conv_a_w, conv_a_b=conv_a_b, ln_a_g=ln_a_g, ln_a_b=ln_a_b, ln_v_g=ln_v_g, ln_v_b=ln_v_b, w_s=w_s, b_s=b_s, w_out_e=w_out_e, b_out_e=b_out_e, w_in_o=w_in_o, b_in_o=b_in_o, w_pool=w_pool, pool_scale=pool_scale, conv_d_w=conv_d_w, w_out_o=w_out_o, b_out_o=b_out_o, ln_g=ln_g, ln_b=ln_b, w_ple=w_ple, w_ple_gate=w_ple_gate, b_ple_gate=b_ple_gate, loss_target=loss_target, m_w_in_e=m_w_in_e, m_b_in_e=m_b_in_e, m_conv_a_w=m_conv_a_w, m_conv_a_b=m_conv_a_b, m_ln_a_g=m_ln_a_g, m_ln_a_b=m_ln_a_b, m_ln_v_g=m_ln_v_g, m_ln_v_b=m_ln_v_b, m_w_s=m_w_s, m_b_s=m_b_s, m_w_out_e=m_w_out_e, m_b_out_e=m_b_out_e, m_w_in_o=m_w_in_o, m_b_in_o=m_b_in_o, m_w_pool=m_w_pool, m_pool_scale=m_pool_scale, m_conv_d_w=m_conv_d_w, m_w_out_o=m_w_out_o, m_b_out_o=m_b_out_o, m_ln_g=m_ln_g, m_ln_b=m_ln_b, m_w_ple=m_w_ple, m_w_ple_gate=m_w_ple_gate, m_b_ple_gate=m_b_ple_gate, v_w_in_e=v_w_in_e, v_b_in_e=v_b_in_e, v_conv_a_w=v_conv_a_w, v_conv_a_b=v_conv_a_b, v_ln_a_g=v_ln_a_g, v_ln_a_b=v_ln_a_b, v_ln_v_g=v_ln_v_g, v_ln_v_b=v_ln_v_b, v_w_s=v_w_s, v_b_s=v_b_s, v_w_out_e=v_w_out_e, v_b_out_e=v_b_out_e, v_w_in_o=v_w_in_o, v_b_in_o=v_b_in_o, v_w_pool=v_w_pool, v_pool_scale=v_pool_scale, v_conv_d_w=v_conv_d_w, v_w_out_o=v_w_out_o, v_b_out_o=v_b_out_o, v_ln_g=v_ln_g, v_ln_b=v_ln_b, v_w_ple=v_w_ple, v_w_ple_gate=v_w_ple_gate, v_b_ple_gate=v_b_ple_gate)
    weights = {n: given[n] for n in TWIN_WEIGHTS}
    shared = {n: given[n] for n in SHARED_INPUTS}
    per_example = {n: given[n] for n in ['x', 'p']}
    grad_fn = _jax.value_and_grad(_loss, argnums=(0, 1))

    def one_microbatch(ex, loss_target):
        ex = dict(ex)
        diff = ex.pop(TWIN_DIFF_INPUT)
        return grad_fn(weights, diff, {**shared, **ex}, loss_target)

    if N_MICROBATCH == 1:
        loss, (grad_w, grad_x) = one_microbatch(per_example, given["loss_target"])
    else:
        def body(carry, xs):
            loss_sum, grad_sum = carry
            l_k, (gw_k, gx_k) = one_microbatch(xs[0], xs[1])
            with _jax.named_scope("update"):
                return (loss_sum + l_k, _jax.tree.map(_jnp.add, grad_sum, gw_k)), gx_k

        init = (_jnp.zeros((), _jnp.float32), _jax.tree.map(_jnp.zeros_like, weights))
        (loss, grad_w), grad_x = _jax.lax.scan(body, init, (per_example, given["loss_target"]))
    with _jax.named_scope("update"):
        delta_w, new_m, new_v = {}, {}, {}
        for n in TWIN_WEIGHTS:
            delta_w[n], new_m[n], new_v[n] = _adamw(weights[n], grad_w[n], given["m_" + n], given["v_" + n])
    return (loss, grad_x, *[grad_w[n] for n in TWIN_WEIGHTS], *[delta_w[n] for n in TWIN_WEIGHTS],
            *[new_m[n] for n in TWIN_WEIGHTS], *[new_v[n] for n in TWIN_WEIGHTS])
```

```python
import functools
import math

import jax
import jax.numpy as jnp
from jax import lax
from jax.experimental import pallas as pl
from jax.experimental.pallas import tpu as pltpu

F32 = jnp.float32
BF16 = jnp.bfloat16

N_DEV = 8
DEPTH = 4
LN_EPS = 1e-5
DEEPNORM_ALPHA = (2.0 * DEPTH) ** 0.25
POOL_WINDOWS = (2, 4, 8, 16)
CONV_A_WIDTH = 31
CONV_D_WIDTH = 3
GELU_C = math.sqrt(2.0 / math.pi)
GELU_K = 0.044715

ADAM_LR = 0.001
ADAM_B1 = 0.9
ADAM_B2 = 0.999
ADAM_EPS = 1e-08
ADAM_WD = 0.01
ADAM_STEP = 10

LANE = 128
SUBLANE = 8
HALO_A = 32
HALO_C = 16
HALO_D = 8
ROW_CHUNK = 32
VMEM_LIMIT = 56 * 2**20

ANY = pl.BlockSpec(memory_space=pl.ANY)
MESH = pl.DeviceIdType.MESH

REP_NAMES = ("b_in_e", "conv_a_b", "ln_a_g", "ln_a_b", "ln_v_g", "ln_v_b", "w_s", "b_s", "b_out_e", "w_pool", "ln_g",
             "ln_b", "b_ple_gate")
SH_NAMES = ("conv_a_w", "conv_d_w", "pool_scale", "b_in_o", "b_out_o")


def _params(n_grid_axes):
    return pltpu.CompilerParams(dimension_semantics=("arbitrary",) * n_grid_axes, vmem_limit_bytes=VMEM_LIMIT)


def _const(shape):
    nd = len(shape)
    return pl.BlockSpec(shape, lambda *_: (0,) * nd)


def _dot(a, b):
    return jnp.dot(a, b, preferred_element_type=F32)


def _dot_nt(a, b):
    return lax.dot_general(a, b, (((1,), (1,)), ((), ())), preferred_element_type=F32)


def _dot_tn(a, b):
    return lax.dot_general(a, b, (((0,), (0,)), ((), ())), preferred_element_type=F32)


def _silu(x):
    return x * jax.nn.sigmoid(x)


def _silu_grad(x):
    s = jax.nn.sigmoid(x)
    return x * s, s * (1.0 + x * (1.0 - s))


def _gelu(x):
    return 0.5 * x * (1.0 + jnp.tanh(GELU_C * (x + GELU_K * x * x * x)))


def _gelu_grad(x):
    x2 = x * x
    t = jnp.tanh(GELU_C * x * (1.0 + GELU_K * x2))
    return 0.5 * x * (1.0 + t), 0.5 * (1.0 + t) + 0.5 * x * (1.0 - t * t) * GELU_C * (1.0 + 3.0 * GELU_K * x2)


def _norm(v):
    mu = jnp.mean(v, axis=-1, keepdims=True)
    d = v - mu
    var = jnp.mean(d * d, axis=-1, keepdims=True)
    rstd = lax.rsqrt(var + LN_EPS)
    return d * rstd, rstd


def _norm_bwd(dxh, xh, rstd):
    return rstd * (dxh - jnp.mean(dxh, axis=-1, keepdims=True) - xh * jnp.mean(dxh * xh, axis=-1, keepdims=True))


def _sum0(v):
    return jnp.sum(v, axis=0, keepdims=True)


def _head_low_mask(n_cols):
    lane = lax.broadcasted_iota(jnp.int32, (LANE, n_cols), 1)
    return (lane & (LANE - 1)) < (LANE // 2)


def _blocks_to_lanes(v, j, nb):
    return jnp.concatenate([v[n * LANE:(n + 1) * LANE, j * LANE:(j + 1) * LANE] for n in range(nb)], axis=1)


def _axis_view(axis, size):
    def view(ref, d):
        idx = [slice(None)] * len(ref.shape)
        idx[axis] = pl.ds(pl.multiple_of(d * size, size), size)
        return ref.at[tuple(idx)]
    return view


def _lead_view(ref, d):
    return ref.at[d]


def _whole_view(ref, d):
    return ref


def _exchange(name, srcs, src_views, dst_shapes, dst_views):
    n = len(srcs)
    peer_bits = (1, 2, 4, 6, 3, 5, 7)

    def body(*refs):
        src = refs[:n]
        dst = refs[n:2 * n]
        send_sems, recv_sems, local_sems = refs[2 * n:]
        x, y, c = lax.axis_index("x"), lax.axis_index("y"), lax.axis_index("c")
        me = 4 * x + 2 * y + c
        local = [pltpu.make_async_copy(src_views[a](src[a], me), dst_views[a](dst[a], me), local_sems.at[a]) for a in range(n)]
        for cp in local:
            cp.start()
        sends, recvs = [], []
        for ki, k in enumerate(peer_bits):
            px, py, pc = x ^ (k >> 2), y ^ ((k >> 1) & 1), c ^ (k & 1)
            peer = 4 * px + 2 * py + pc
            for a in range(n):
                s = a * 7 + ki
                cp = pltpu.make_async_remote_copy(src_ref=src_views[a](src[a], peer), dst_ref=dst_views[a](dst[a], me),
                                                  send_sem=send_sems.at[s], recv_sem=recv_sems.at[s],
                                                  device_id=(px, py, pc), device_id_type=MESH)
                cp.start()
                sends.append(cp)
                recvs.append(pltpu.make_async_remote_copy(src_ref=src_views[a](src[a], peer), dst_ref=dst_views[a](dst[a], peer),
                                                          send_sem=send_sems.at[s], recv_sem=recv_sems.at[s],
                                                          device_id=(px, py, pc), device_id_type=MESH))
        for cp in recvs:
            cp.wait_recv()
        for cp in sends:
            cp.wait_send()
        for cp in local:
            cp.wait()

    return pl.pallas_call(
        body, name=name, out_shape=tuple(dst_shapes), in_specs=[ANY] * n, out_specs=tuple([ANY] * n),
        scratch_shapes=[pltpu.SemaphoreType.DMA((7 * n,)), pltpu.SemaphoreType.DMA((7 * n,)), pltpu.SemaphoreType.DMA((n,))],
    )(*srcs)


def _z_parts(x_ref, w_ref, b_ref, z_ref, width):
    xb = x_ref[...].astype(BF16)

    def part(k):
        cols = slice(k * width, (k + 1) * width)
        zk = (_dot(xb, w_ref[:, cols]) + b_ref[:, cols]).astype(BF16)
        z_ref[:, cols] = zk
        return zk.astype(F32)
    return part


def _even_fwd(x, w_in, b_in, conv_w, conv_b, ln_a_g, ln_a_b, ln_v_g, ln_v_b, wcat, bs_full, *, seq, tm):
    t_len, d_model = x.shape
    w = d_model // 2
    nt, tps, nb = t_len // tm, seq // tm, tm // LANE

    def body(x_ref, w_ref, b_ref, cw_ref, cb_ref, lag_ref, lab_ref, lvg_ref, lvb_ref, wcat_ref, bs_ref,
             z_ref, y_ref, a1_ref, sg_ref, a0_s):
        i = pl.program_id(0)

        @pl.when(i % tps == 0)
        def _():
            a0_s[0:HALO_A, :] = jnp.zeros((HALO_A, w), F32)

        part = _z_parts(x_ref, w_ref, b_ref, z_ref, w)
        a0_s[HALO_A:HALO_A + tm, :] = part(0) * jax.nn.sigmoid(part(1))
        for r in range(0, tm, ROW_CHUNK):
            acc = jnp.zeros((ROW_CHUNK, w), F32) + cb_ref[...]
            for k in range(CONV_A_WIDTH):
                acc = acc + cw_ref[k:k + 1, :] * a0_s[pl.ds(HALO_A - (CONV_A_WIDTH - 1) + k + r, ROW_CHUNK), :]
            a1_ref[r:r + ROW_CHUNK, :] = acc
        a0_s[0:HALO_A, :] = a0_s[tm:tm + HALO_A, :]
        ah, _ = _norm(a1_ref[...])
        a = _silu(ah * lag_ref[...] + lab_ref[...]) * _silu(part(2))
        y_ref[:, 0:w] = a.astype(BF16)

        u = part(3)
        vh, _ = _norm(_gelu(part(4)))
        v2 = vh * lvg_ref[...] + lvb_ref[...]
        low = _head_low_mask(nb * LANE)
        for j in range(w // LANE):
            vt = _blocks_to_lanes(v2, j, nb)
            rhs = jnp.concatenate([jnp.where(low, vt, 0.0), jnp.where(low, 0.0, vt)], axis=0).astype(BF16)
            out = _dot(wcat_ref[j], rhs)
            for n in range(nb):
                sg_ref[n * LANE:(n + 1) * LANE, j * LANE:(j + 1) * LANE] = (
                    out[:, n * LANE:(n + 1) * LANE] + bs_ref[:, j * LANE:(j + 1) * LANE]).astype(BF16)
        g = _gelu(u) * sg_ref[...].astype(F32) * _silu(part(5))
        y_ref[:, w:2 * w] = g.astype(BF16)

    row = lambda cols: pl.BlockSpec((tm, cols), lambda i: (i, 0))
    return pl.pallas_call(
        body, name="even_fwd", grid=(nt,),
        out_shape=(jax.ShapeDtypeStruct((t_len, 6 * w), BF16), jax.ShapeDtypeStruct((t_len, d_model), BF16),
                   jax.ShapeDtypeStruct((t_len, w), F32), jax.ShapeDtypeStruct((t_len, w), BF16)),
        in_specs=[row(d_model), _const(w_in.shape), _const(b_in.shape), _const(conv_w.shape), _const(conv_b.shape),
                  _const(ln_a_g.shape), _const(ln_a_b.shape), _const(ln_v_g.shape), _const(ln_v_b.shape),
                  _const(wcat.shape), _const(bs_full.shape)],
        out_specs=(row(6 * w), row(d_model), row(w), row(w)),
        scratch_shapes=[pltpu.VMEM((tm + HALO_A, w), F32)],
        compiler_params=_params(1),
    )(x, w_in, b_in, conv_w, conv_b, ln_a_g, ln_a_b, ln_v_g, ln_v_b, wcat, bs_full)


def _pool_inverse(tile_index, tm, seq, window):
    row = tile_index * tm + lax.broadcasted_iota(jnp.int32, (tm, 1), 0)
    pos = (row % seq + 1).astype(F32)
    return 1.0 / jnp.minimum(pos, float(window))


def _odd_fwd(x, w_in, b_in, w_pool, pool_scale, conv_w, *, seq, tm):
    t_len, d_model = x.shape
    w = d_model // 2
    nt, tps = t_len // tm, seq // tm

    def body(x_ref, w_ref, b_ref, wp_ref, ps_ref, cw_ref, z_ref, y_ref, pooled_ref, q_ref, cv_s, hc_s):
        i = pl.program_id(0)

        @pl.when(i % tps == 0)
        def _():
            cv_s[0:HALO_C, :] = jnp.zeros((HALO_C, w), F32)
            hc_s[0:HALO_D, :] = jnp.zeros((HALO_D, w), F32)

        part = _z_parts(x_ref, w_ref, b_ref, z_ref, w)
        c_val = part(0)
        c_gate = part(1)
        cv_s[HALO_C:HALO_C + tm, :] = c_val
        for gi, win in enumerate(POOL_WINDOWS):
            cols = slice(gi * LANE, (gi + 1) * LANE)
            s = cv_s[pl.ds(HALO_C, tm), cols]
            for j in range(1, win):
                s = s + cv_s[pl.ds(HALO_C - j, tm), cols]
            pooled = (s * _pool_inverse(i, tm, seq, win) - c_val[:, cols]).astype(BF16)
            pooled_ref[:, cols] = pooled
            c = _dot(pooled, wp_ref[gi]) * ps_ref[:, cols] * _silu(c_gate[:, cols])
            y_ref[:, cols] = c.astype(BF16)
        cv_s[0:HALO_C, :] = cv_s[tm:tm + HALO_C, :]

        d_h = part(2)
        d_b = part(3)
        hc_s[HALO_D:HALO_D + tm, :] = part(4) * d_h
        q = jnp.zeros((tm, w), F32)
        for k in range(CONV_D_WIDTH):
            q = q + cw_ref[k:k + 1, :] * hc_s[pl.ds(HALO_D - (CONV_D_WIDTH - 1) + k, tm), :]
        hc_s[0:HALO_D, :] = hc_s[tm:tm + HALO_D, :]
        qb = q.astype(BF16)
        q_ref[...] = qb
        y_ref[:, w:2 * w] = (d_b * qb.astype(F32) * _silu(part(5))).astype(BF16)

    row = lambda cols: pl.BlockSpec((tm, cols), lambda i: (i, 0))
    return pl.pallas_call(
        body, name="odd_fwd", grid=(nt,),
        out_shape=(jax.ShapeDtypeStruct((t_len, 6 * w), BF16), jax.ShapeDtypeStruct((t_len, d_model), BF16),
                   jax.ShapeDtypeStruct((t_len, w), BF16), jax.ShapeDtypeStruct((t_len, w), BF16)),
        in_specs=[row(d_model), _const(w_in.shape), _const(b_in.shape), _const(w_pool.shape), _const(pool_scale.shape),
                  _const(conv_w.shape)],
        out_specs=(row(6 * w), row(d_model), row(w), row(w)),
        scratch_shapes=[pltpu.VMEM((tm + HALO_C, w), F32), pltpu.VMEM((tm + HALO_D, w), F32)],
        compiler_params=_params(1),
    )(x, w_in, b_in, w_pool, pool_scale, conv_w)


def _post_fwd(y, x, p_all, layer, w_out, b_out, ln_g, ln_b, wg, bg, wp, loss_target, *, tm):
    t_len, d_model = x.shape
    d_ple = p_all.shape[-1]
    nt = t_len // tm
    last = loss_target is not None

    def body(*refs):
        y_ref, x_ref, p_ref, wo_ref, bo_ref, g_ref, b_ref, wg_ref, bg_ref, wp_ref = refs[:10]
        rest = refs[10:]
        if last:
            lt_ref, rest = rest[0], rest[1:]
        xn_ref, rh_ref, h_ref, rstd_ref, gate_ref, e_ref = rest[:6]
        r = DEEPNORM_ALPHA * x_ref[...] + _dot(y_ref[...], wo_ref[...]) + bo_ref[...]
        rh, rstd = _norm(r)
        h = rh * g_ref[...] + b_ref[...]
        hb = h.astype(BF16)
        gate = jax.nn.sigmoid(_dot(hb, wg_ref[...]) + bg_ref[...])
        e = _dot(p_ref[...].astype(BF16), wp_ref[...])
        xn = h + gate * e
        xn_ref[...] = xn
        rh_ref[...] = rh.astype(BF16)
        h_ref[...] = hb
        rstd_ref[...] = jnp.broadcast_to(rstd, (tm, LANE))
        gate_ref[...] = gate.astype(BF16)
        e_ref[...] = e.astype(BF16)
        if last:
            dxn_ref, sse_ref = rest[6:]
            diff = xn - lt_ref[...]
            dxn_ref[...] = diff * (1.0 / d_model)

            @pl.when(pl.program_id(0) == 0)
            def _():
                sse_ref[...] = jnp.zeros_like(sse_ref)
            sse_ref[...] += jnp.sum(_sum0(diff * diff), axis=1, keepdims=True)

    row = lambda cols: pl.BlockSpec((tm, cols), lambda i: (i, 0))
    in_specs = [row(d_model), row(d_model), pl.BlockSpec((None, tm, d_ple), lambda i: (layer, i, 0)),
                _const(w_out.shape), _const(b_out.shape), _const(ln_g.shape), _const(ln_b.shape), _const(wg.shape),
                _const(bg.shape), _const(wp.shape)]
    args = [y, x, p_all, w_out, b_out, ln_g, ln_b, wg, bg, wp]
    out_shape = [jax.ShapeDtypeStruct((t_len, d_model), F32), jax.ShapeDtypeStruct((t_len, d_model), BF16),
                 jax.ShapeDtypeStruct((t_len, d_model), BF16), jax.ShapeDtypeStruct((t_len, LANE), F32),
                 jax.ShapeDtypeStruct((t_len, d_model), BF16), jax.ShapeDtypeStruct((t_len, d_model), BF16)]
    out_specs = [row(d_model), row(d_model), row(d_model), row(LANE), row(d_model), row(d_model)]
    if last:
        in_specs.append(row(d_model))
        args.append(loss_target)
        out_shape += [jax.ShapeDtypeStruct((t_len, d_model), F32), jax.ShapeDtypeStruct((SUBLANE, LANE), F32)]
        out_specs += [row(d_model), _const((SUBLANE, LANE))]
    return pl.pallas_call(body, name="post_fwd_last" if last else "post_fwd", grid=(nt,), out_shape=tuple(out_shape),
                          in_specs=in_specs, out_specs=tuple(out_specs), compiler_params=_params(1))(*args)


def _post_bwd(dxn, gate, e, rh, rstd, ln_g, wg, w_out, *, tm):
    t_len, d_model = dxn.shape
    nt = t_len // tm

    def body(dxn_ref, gate_ref, e_ref, rh_ref, rstd_ref, g_ref, wg_ref, wo_ref, de_ref, dgl_ref, dy_ref, dr_ref, acc_ref):
        @pl.when(pl.program_id(0) == 0)
        def _():
            acc_ref[...] = jnp.zeros_like(acc_ref)

        d = dxn_ref[...]
        gt = gate_ref[...].astype(F32)
        rhat = rh_ref[...].astype(F32)
        de_ref[...] = (d * gt).astype(BF16)
        dgl = d * e_ref[...].astype(F32) * gt * (1.0 - gt)
        dglb = dgl.astype(BF16)
        dgl_ref[...] = dglb
        dh = d + _dot_nt(dglb, wg_ref[...])
        dr = _norm_bwd(dh * g_ref[...], rhat, rstd_ref[:, 0:1])
        dr_ref[...] = dr
        dy_ref[...] = _dot_nt(dr.astype(BF16), wo_ref[...]).astype(BF16)
        acc_ref[0:1, :] += _sum0(dgl)
        acc_ref[1:2, :] += _sum0(dh * rhat)
        acc_ref[2:3, :] += _sum0(dh)
        acc_ref[3:4, :] += _sum0(dr)

    row = lambda cols: pl.BlockSpec((tm, cols), lambda i: (i, 0))
    return pl.pallas_call(
        body, name="post_bwd", grid=(nt,),
        out_shape=(jax.ShapeDtypeStruct((t_len, d_model), BF16), jax.ShapeDtypeStruct((t_len, d_model), BF16),
                   jax.ShapeDtypeStruct((t_len, d_model), BF16), jax.ShapeDtypeStruct((t_len, d_model), F32),
                   jax.ShapeDtypeStruct((SUBLANE, d_model), F32)),
        in_specs=[row(d_model), row(d_model), row(d_model), row(d_model), row(LANE), _const(ln_g.shape), _const(wg.shape),
                  _const(w_out.shape)],
        out_specs=(row(d_model), row(d_model), row(d_model), row(d_model), _const((SUBLANE, d_model))),
        compiler_params=_params(1),
    )(dxn, gate, e, rh, rstd, ln_g, wg, w_out)


def _dz_store(dz_ref, dbin_ref, width):
    def store(k, v):
        cols = slice(k * width, (k + 1) * width)
        vb = v.astype(BF16)
        dz_ref[:, cols] = vb
        dbin_ref[0:1, cols] += _sum0(v)
        return vb
    return store


def _even_bwd(dy, z, a1, sg, dr, w_in, conv_w, ln_a_g, ln_a_b, ln_v_g, ln_v_b, wcat_t, *, seq, tm):
    t_len, d_model = dr.shape
    w = d_model // 2
    nt, tps, nb = t_len // tm, seq // tm, tm // LANE
    n_heads = 2 * (w // LANE)

    def body(dy_ref, z_ref, a1_ref, sg_ref, dr_ref, w_ref, cw_ref, lag_ref, lab_ref, lvg_ref, lvb_ref, wct_ref,
             dz_ref, dx_ref, dcw_ref, vec_ref, dbin_ref, dws_ref, dbs_ref, da1_s, a0_s, da0_s, cw_acc):
        i = pl.program_id(0)
        tile = nt - 1 - i

        @pl.when(i == 0)
        def _():
            vec_ref[...] = jnp.zeros_like(vec_ref)
            dbin_ref[...] = jnp.zeros_like(dbin_ref)
            dws_ref[...] = jnp.zeros_like(dws_ref)
            dbs_ref[...] = jnp.zeros_like(dbs_ref)
            cw_acc[...] = jnp.zeros_like(cw_acc)

        @pl.when((tile + 1) % tps == 0)
        def _():
            da1_s[tm:tm + HALO_A, :] = jnp.zeros((HALO_A, w), F32)

        zp = lambda k: z_ref[:, k * w:(k + 1) * w].astype(F32)
        store = _dz_store(dz_ref, dbin_ref, w)

        da = dy_ref[:, 0:w].astype(F32)
        a_val, a_glu, a_gate = zp(0), zp(1), zp(2)
        s_glu = jax.nn.sigmoid(a_glu)
        a0_s[...] = a_val * s_glu
        ah, rstd_a = _norm(a1_ref[...])
        silu_a2, dsilu_a2 = _silu_grad(ah * lag_ref[...] + lab_ref[...])
        silu_ag, dsilu_ag = _silu_grad(a_gate)
        dzb2 = store(2, da * silu_a2 * dsilu_ag)
        da2 = da * silu_ag * dsilu_a2
        vec_ref[1:2, :] += _sum0(da2 * ah)
        vec_ref[2:3, :] += _sum0(da2)
        da1 = _norm_bwd(da2 * lag_ref[...], ah, rstd_a)
        vec_ref[0:1, :] += _sum0(da1)
        da1_s[0:tm, :] = da1
        for r in range(0, tm, ROW_CHUNK):
            a0c = a0_s[r:r + ROW_CHUNK, :]
            acc = jnp.zeros((ROW_CHUNK, w), F32)
            for k in range(CONV_A_WIDTH):
                d = da1_s[pl.ds(r + (CONV_A_WIDTH - 1) - k, ROW_CHUNK), :]
                acc = acc + cw_ref[k:k + 1, :] * d
                pw = a0c * d
                p8 = pw[0:SUBLANE]
                for q in range(1, ROW_CHUNK // SUBLANE):
                    p8 = p8 + pw[q * SUBLANE:(q + 1) * SUBLANE]
                cw_acc[k * SUBLANE:(k + 1) * SUBLANE, :] += p8
            da0_s[r:r + ROW_CHUNK, :] = acc
        da1_s[tm:tm + HALO_A, :] = da1_s[0:HALO_A, :]
        da0 = da0_s[...]
        dzb0 = store(0, da0 * s_glu)
        dzb1 = store(1, da0 * a_val * s_glu * (1.0 - s_glu))

        dg = dy_ref[:, w:2 * w].astype(F32)
        u, v, gg = zp(3), zp(4), zp(5)
        sgv = sg_ref[...].astype(F32)
        gelu_u, dgelu_u = _gelu_grad(u)
        silu_gg, dsilu_gg = _silu_grad(gg)
        dzb5 = store(5, dg * gelu_u * sgv * dsilu_gg)
        t1 = dg * silu_gg
        dzb3 = store(3, t1 * sgv * dgelu_u)
        dsg = t1 * gelu_u
        gelu_v, dgelu_v = _gelu_grad(v)
        vh, rstd_v = _norm(gelu_v)
        v2 = vh * lvg_ref[...] + lvb_ref[...]
        low = _head_low_mask(nb * LANE)
        for j in range(w // LANE):
            dt = _blocks_to_lanes(dsg, j, nb)
            d_lo = jnp.where(low, dt, 0.0).astype(BF16)
            d_hi = jnp.where(low, 0.0, dt).astype(BF16)
            v2t = _blocks_to_lanes(v2, j, nb).astype(BF16)
            dv2t = _dot(wct_ref[j], jnp.concatenate([d_lo, d_hi], axis=0))
            for n in range(nb):
                da0_s[n * LANE:(n + 1) * LANE, j * LANE:(j + 1) * LANE] = dv2t[:, n * LANE:(n + 1) * LANE]
            dws_ref[2 * j] += _dot_nt(d_lo, v2t)
            dws_ref[2 * j + 1] += _dot_nt(d_hi, v2t)
            bsum = dt[:, 0:LANE]
            for n in range(1, nb):
                bsum = bsum + dt[:, n * LANE:(n + 1) * LANE]
            dbs_ref[:, j * LANE:(j + 1) * LANE] += bsum
        dv2 = da0_s[...]
        vec_ref[3:4, :] += _sum0(dv2 * vh)
        vec_ref[4:5, :] += _sum0(dv2)
        dzb4 = store(4, _norm_bwd(dv2 * lvg_ref[...], vh, rstd_v) * dgelu_v)

        dx = DEEPNORM_ALPHA * dr_ref[...]
        for k, dzb in enumerate((dzb0, dzb1, dzb2, dzb3, dzb4, dzb5)):
            dx = dx + _dot_nt(dzb, w_ref[:, k * w:(k + 1) * w])
        dx_ref[...] = dx

        @pl.when(i == nt - 1)
        def _():
            for k in range(CONV_A_WIDTH):
                dcw_ref[k:k + 1, :] = _sum0(cw_acc[k * SUBLANE:(k + 1) * SUBLANE, :])
            keep = (lax.broadcasted_iota(jnp.int32, (LANE, LANE), 0) >= lax.broadcasted_iota(jnp.int32, (LANE, LANE), 1))
            for hd in range(n_heads):
                dws_ref[hd] = jnp.where(keep, dws_ref[hd], 0.0)

    rev = lambda cols: pl.BlockSpec((tm, cols), lambda i: (nt - 1 - i, 0))
    return pl.pallas_call(
        body, name="even_bwd", grid=(nt,),
        out_shape=(jax.ShapeDtypeStruct((t_len, 6 * w), BF16), jax.ShapeDtypeStruct((t_len, d_model), F32),
                   jax.ShapeDtypeStruct((CONV_A_WIDTH, w), F32), jax.ShapeDtypeStruct((SUBLANE, w), F32),
                   jax.ShapeDtypeStruct((SUBLANE, 6 * w), F32), jax.ShapeDtypeStruct((n_heads, LANE, LANE), F32),
                   jax.ShapeDtypeStruct((LANE, w), F32)),
        in_specs=[rev(d_model), rev(6 * w), rev(w), rev(w), rev(d_model), _const(w_in.shape), _const(conv_w.shape),
                  _const(ln_a_g.shape), _const(ln_a_b.shape), _const(ln_v_g.shape), _const(ln_v_b.shape), _const(wcat_t.shape)],
        out_specs=(rev(6 * w), rev(d_model), _const((CONV_A_WIDTH, w)), _const((SUBLANE, w)), _const((SUBLANE, 6 * w)),
                   _const((n_heads, LANE, LANE)), _const((LANE, w))),
        scratch_shapes=[pltpu.VMEM((tm + HALO_A, w), F32), pltpu.VMEM((tm, w), F32), pltpu.VMEM((tm, w), F32),
                        pltpu.VMEM((CONV_A_WIDTH * SUBLANE, w), F32)],
        compiler_params=_params(1),
    )(dy, z, a1, sg, dr, w_in, conv_w, ln_a_g, ln_a_b, ln_v_g, ln_v_b, wcat_t)


def _odd_bwd(dy, z, pooled, q, dr, w_in, w_pool, pool_scale, conv_w, *, seq, tm):
    t_len, d_model = dr.shape
    w = d_model // 2
    nt, tps = t_len // tm, seq // tm
    n_groups = len(POOL_WINDOWS)

    def body(dy_ref, z_ref, pooled_ref, q_ref, dr_ref, w_ref, wp_ref, ps_ref, cw_ref,
             dz_ref, dx_ref, dwp_ref, vec_ref, dbin_ref, dm_s, dq_s):
        i = pl.program_id(0)
        tile = nt - 1 - i

        @pl.when(i == 0)
        def _():
            dwp_ref[...] = jnp.zeros_like(dwp_ref)
            vec_ref[...] = jnp.zeros_like(vec_ref)
            dbin_ref[...] = jnp.zeros_like(dbin_ref)

        @pl.when((tile + 1) % tps == 0)
        def _():
            dm_s[tm:tm + HALO_C, :] = jnp.zeros((HALO_C, w), F32)
            dq_s[tm:tm + HALO_D, :] = jnp.zeros((HALO_D, w), F32)

        zp = lambda k: z_ref[:, k * w:(k + 1) * w].astype(F32)
        store = _dz_store(dz_ref, dbin_ref, w)

        dc = dy_ref[:, 0:w].astype(F32)
        c_gate = zp(1)
        silu_c, dsilu_c = _silu_grad(c_gate)
        dcs = dc * silu_c
        dvg_parts, dcg_parts = [], []
        for gi, win in enumerate(POOL_WINDOWS):
            cols = slice(gi * LANE, (gi + 1) * LANE)
            pooled_g = pooled_ref[:, cols]
            wp = wp_ref[gi]
            cpre = _dot(pooled_g, wp)
            scale = ps_ref[:, cols]
            vec_ref[0:1, cols] += _sum0(dcs[:, cols] * cpre)
            dcg_parts.append(dc[:, cols] * cpre * scale * dsilu_c[:, cols])
            dcp = (dcs[:, cols] * scale).astype(BF16)
            dwp_ref[gi] += _dot_tn(pooled_g, dcp)
            dpooled = _dot_nt(dcp, wp)
            dm_s[0:tm, cols] = dpooled * _pool_inverse(tile, tm, seq, win)
            s = dm_s[pl.ds(0, tm), cols]
            for j in range(1, win):
                s = s + dm_s[pl.ds(j, tm), cols]
            dvg_parts.append(s - dpooled)
        dm_s[tm:tm + HALO_C, :] = dm_s[0:HALO_C, :]
        dzb0 = store(0, jnp.concatenate(dvg_parts, axis=1))
        dzb1 = store(1, jnp.concatenate(dcg_parts, axis=1))

        dd = dy_ref[:, w:2 * w].astype(F32)
        d_h, d_b, d_c, d_gate = zp(2), zp(3), zp(4), zp(5)
        qv = q_ref[...].astype(F32)
        silu_d, dsilu_d = _silu_grad(d_gate)
        dzb5 = store(5, dd * d_b * qv * dsilu_d)
        dzb3 = store(3, dd * qv * silu_d)
        dq_s[0:tm, :] = dd * d_b * silu_d
        hc = d_c * d_h
        dhc = jnp.zeros((tm, w), F32)
        for k in range(CONV_D_WIDTH):
            d = dq_s[pl.ds((CONV_D_WIDTH - 1) - k, tm), :]
            dhc = dhc + cw_ref[k:k + 1, :] * d
            vec_ref[1 + k:2 + k, :] += _sum0(hc * d)
        dq_s[tm:tm + HALO_D, :] = dq_s[0:HALO_D, :]
        dzb2 = store(2, dhc * d_c)
        dzb4 = store(4, dhc * d_h)

        dx = DEEPNORM_ALPHA * dr_ref[...]
        for k, dzb in enumerate((dzb0, dzb1, dzb2, dzb3, dzb4, dzb5)):
            dx = dx + _dot_nt(dzb, w_ref[:, k * w:(k + 1) * w])
        dx_ref[...] = dx

    rev = lambda cols: pl.BlockSpec((tm, cols), lambda i: (nt - 1 - i, 0))
    return pl.pallas_call(
        body, name="odd_bwd", grid=(nt,),
        out_shape=(jax.ShapeDtypeStruct((t_len, 6 * w), BF16), jax.ShapeDtypeStruct((t_len, d_model), F32),
                   jax.ShapeDtypeStruct((n_groups, LANE, LANE), F32), jax.ShapeDtypeStruct((SUBLANE, w), F32),
                   jax.ShapeDtypeStruct((SUBLANE, 6 * w), F32)),
        in_specs=[rev(d_model), rev(6 * w), rev(w), rev(w), rev(d_model), _const(w_in.shape), _const(w_pool.shape),
                  _const(pool_scale.shape), _const(conv_w.shape)],
        out_specs=(rev(6 * w), rev(d_model), _const((n_groups, LANE, LANE)), _const((SUBLANE, w)), _const((SUBLANE, 6 * w))),
        scratch_shapes=[pltpu.VMEM((tm + HALO_C, w), F32), pltpu.VMEM((tm + HALO_D, w), F32)],
        compiler_params=_params(1),
    )(dy, z, pooled, q, dr, w_in, w_pool, pool_scale, conv_w)


def _weight_grad(name, a, b, slot, n_slots, prev=None, a_layer=None, bn=None):
    if a_layer is None:
        t_len, m = a.shape
        a_spec = lambda tk: pl.BlockSpec((tk, m), lambda n, k: (k, 0))
    else:
        _, t_len, m = a.shape
        a_spec = lambda tk: pl.BlockSpec((None, tk, m), lambda n, k: (a_layer, k, 0))
    n_cols = b.shape[1]
    bn = n_cols if bn is None else bn
    tk = min(t_len, 512)

    def body(*refs):
        a_ref, b_ref = refs[0], refs[1]
        o_ref = refs[-1]

        @pl.when(pl.program_id(1) == 0)
        def _():
            o_ref[...] = jnp.zeros_like(o_ref)
        o_ref[...] += _dot_tn(a_ref[...].astype(BF16), b_ref[...].astype(BF16))

    in_specs = [a_spec(tk), pl.BlockSpec((tk, bn), lambda n, k: (k, n))]
    args = [a, b]
    aliases = {}
    if prev is not None:
        in_specs.append(ANY)
        args.append(prev)
        aliases = {2: 0}
    return pl.pallas_call(
        body, name=name, grid=(n_cols // bn, t_len // tk),
        out_shape=jax.ShapeDtypeStruct((n_slots, m, n_cols), F32),
        in_specs=in_specs, out_specs=pl.BlockSpec((None, m, bn), lambda n, k: (slot, 0, n)),
        input_output_aliases=aliases, compiler_params=_params(2),
    )(*args)


def _adamw_reduce(name, parts, w, m, v, rows_per_block):
    n_rows, n_cols = w.shape
    br = rows_per_block

    def body(p_ref, w_ref, m_ref, v_ref, g_ref, d_ref, nm_ref, nv_ref):
        g = p_ref[0]
        for k in range(1, N_DEV):
            g = g + p_ref[k]
        nm = ADAM_B1 * m_ref[...] + (1.0 - ADAM_B1) * g
        nv = ADAM_B2 * v_ref[...] + (1.0 - ADAM_B2) * (g * g)
        m_hat = nm / (1.0 - ADAM_B1 ** ADAM_STEP)
        v_hat = nv / (1.0 - ADAM_B2 ** ADAM_STEP)
        g_ref[...] = g
        d_ref[...] = -ADAM_LR * (m_hat / (jnp.sqrt(v_hat) + ADAM_EPS) + ADAM_WD * w_ref[...])
        nm_ref[...] = nm
        nv_ref[...] = nv

    blk = pl.BlockSpec((br, n_cols), lambda i: (i, 0))
    shp = jax.ShapeDtypeStruct((n_rows, n_cols), F32)
    return pl.pallas_call(
        body, name=name, grid=(n_rows // br,), out_shape=(shp, shp, shp, shp),
        in_specs=[pl.BlockSpec((N_DEV, br, n_cols), lambda i: (0, i, 0)), blk, blk, blk], out_specs=(blk, blk, blk, blk),
        compiler_params=_params(1),
    )(parts, w, m, v)


def _pack_rows(flat_parts, pad_to=None):
    flat = jnp.concatenate([a.reshape(-1) for a in flat_parts])
    if pad_to is not None and pad_to > flat.shape[0]:
        flat = jnp.concatenate([flat, jnp.zeros((pad_to - flat.shape[0],), flat.dtype)])
    return flat.reshape(-1, LANE)


def _unpack_rows(packed, shapes):
    flat = packed.reshape(-1)
    out, off = [], 0
    for s in shapes:
        n = math.prod(s)
        out.append(flat[off:off + n].reshape(s))
        off += n
    return out


def _to_dest_major(full):
    lead, last = full.shape[:-1], full.shape[-1]
    t = full.reshape(lead + (N_DEV, last // N_DEV))
    return jnp.moveaxis(t, -2, 0).reshape(N_DEV, -1)


def _from_source_major(blocks, shard_shape):
    t = blocks.reshape((N_DEV,) + tuple(shard_shape))
    t = jnp.moveaxis(t, 0, -2)
    return t.reshape(tuple(shard_shape[:-1]) + (N_DEV * shard_shape[-1],))


def _block_rows(n_rows, n_cols, target_elems=96 * 1024):
    best = None
    for br in range(SUBLANE, n_rows + 1, SUBLANE):
        if n_rows % br == 0 and br * n_cols <= target_elems:
            best = br
    return n_rows if best is None else best


def kernel(x, p, w_in_e, b_in_e, conv_a_w, conv_a_b, ln_a_g, ln_a_b, ln_v_g, ln_v_b, w_s, b_s, w_out_e, b_out_e, w_in_o, b_in_o, w_pool, pool_scale, conv_d_w, w_out_o, b_out_o, ln_g, ln_b, w_ple, w_ple_gate, b_ple_gate, loss_target, m_w_in_e, m_b_in_e, m_conv_a_w, m_conv_a_b, m_ln_a_g, m_ln_a_b, m_ln_v_g, m_ln_v_b, m_w_s, m_b_s, m_w_out_e, m_b_out_e, m_w_in_o, m_b_in_o, m_w_pool, m_pool_scale, m_conv_d_w, m_w_out_o, m_b_out_o, m_ln_g, m_ln_b, m_w_ple, m_w_ple_gate, m_b_ple_gate, v_w_in_e, v_b_in_e, v_conv_a_w, v_conv_a_b, v_ln_a_g, v_ln_a_b, v_ln_v_g, v_ln_v_b, v_w_s, v_b_s, v_w_out_e, v_b_out_e, v_w_in_o, v_b_in_o, v_w_pool, v_pool_scale, v_conv_d_w, v_w_out_o, v_b_out_o, v_ln_g, v_ln_b, v_w_ple, v_w_ple_gate, v_b_ple_gate):
    weights = dict(w_in_e=w_in_e, b_in_e=b_in_e, conv_a_w=conv_a_w, conv_a_b=conv_a_b, ln_a_g=ln_a_g, ln_a_b=ln_a_b,
                   ln_v_g=ln_v_g, ln_v_b=ln_v_b, w_s=w_s, b_s=b_s, w_out_e=w_out_e, b_out_e=b_out_e, w_in_o=w_in_o,
                   b_in_o=b_in_o, w_pool=w_pool, pool_scale=pool_scale, conv_d_w=conv_d_w, w_out_o=w_out_o,
                   b_out_o=b_out_o, ln_g=ln_g, ln_b=ln_b, w_ple=w_ple, w_ple_gate=w_ple_gate, b_ple_gate=b_ple_gate)
    mom_m = dict(w_in_e=m_w_in_e, b_in_e=m_b_in_e, conv_a_w=m_conv_a_w, conv_a_b=m_conv_a_b, ln_a_g=m_ln_a_g,
                 ln_a_b=m_ln_a_b, ln_v_g=m_ln_v_g, ln_v_b=m_ln_v_b, w_s=m_w_s, b_s=m_b_s, w_out_e=m_w_out_e,
                 b_out_e=m_b_out_e, w_in_o=m_w_in_o, b_in_o=m_b_in_o, w_pool=m_w_pool, pool_scale=m_pool_scale,
                 conv_d_w=m_conv_d_w, w_out_o=m_w_out_o, b_out_o=m_b_out_o, ln_g=m_ln_g, ln_b=m_ln_b, w_ple=m_w_ple,
                 w_ple_gate=m_w_ple_gate, b_ple_gate=m_b_ple_gate)
    mom_v = dict(w_in_e=v_w_in_e, b_in_e=v_b_in_e, conv_a_w=v_conv_a_w, conv_a_b=v_conv_a_b, ln_a_g=v_ln_a_g,
                 ln_a_b=v_ln_a_b, ln_v_g=v_ln_v_g, ln_v_b=v_ln_v_b, w_s=v_w_s, b_s=v_b_s, w_out_e=v_w_out_e,
                 b_out_e=v_b_out_e, w_in_o=v_w_in_o, b_in_o=v_b_in_o, w_pool=v_w_pool, pool_scale=v_pool_scale,
                 conv_d_w=v_conv_d_w, w_out_o=v_w_out_o, b_out_o=v_b_out_o, ln_g=v_ln_g, ln_b=v_ln_b, w_ple=v_w_ple,
                 w_ple_gate=v_w_ple_gate, b_ple_gate=v_b_ple_gate)
    names = tuple(weights)

    batch, seq, d_model = x.shape
    t_len = batch * seq
    w = d_model // 2
    n_even = w_in_e.shape[0]
    n_odd = w_in_o.shape[0]
    depth = ln_g.shape[0]
    d_ple = p.shape[-1]
    n_heads = w_s.shape[1]
    tm = 512 if seq % 512 == 0 and seq >= 1024 else seq // 2
    in_cols = w_in_e.shape[-1]
    out_rows = w_out_e.shape[1]
    ple_cols = w_ple.shape[-1]
    gate_rows = w_ple_gate.shape[1]

    sh_shapes = [weights[n].shape for n in SH_NAMES]
    sh_len = sum(math.prod(s) for s in sh_shapes)
    sh_pad = -(-sh_len // (SUBLANE * LANE)) * (SUBLANE * LANE)
    sh_rows = sh_pad // LANE
    gathered = _exchange(
        "gather_weights",
        [w_in_e.astype(BF16), w_in_o.astype(BF16), w_out_e.astype(BF16), w_out_o.astype(BF16), w_ple.astype(BF16),
         w_ple_gate.astype(BF16), _pack_rows([weights[n] for n in SH_NAMES], sh_pad)],
        [_whole_view] * 7,
        [jax.ShapeDtypeStruct((n_even, d_model, N_DEV * in_cols), BF16), jax.ShapeDtypeStruct((n_odd, d_model, N_DEV * in_cols), BF16),
         jax.ShapeDtypeStruct((n_even, N_DEV * out_rows, d_model), BF16), jax.ShapeDtypeStruct((n_odd, N_DEV * out_rows, d_model), BF16),
         jax.ShapeDtypeStruct((depth, d_ple, N_DEV * ple_cols), BF16), jax.ShapeDtypeStruct((depth, N_DEV * gate_rows, d_model), BF16),
         jax.ShapeDtypeStruct((N_DEV, sh_rows, LANE), F32)],
        [_axis_view(2, in_cols), _axis_view(2, in_cols), _axis_view(1, out_rows), _axis_view(1, out_rows),
         _axis_view(2, ple_cols), _axis_view(1, gate_rows), _lead_view])
    win_e, win_o, wout_e, wout_o, wple, wgate, sh_all = gathered
    sh_flat = sh_all.reshape(N_DEV, sh_pad)
    full_small, off = {}, 0
    for n, s in zip(SH_NAMES, sh_shapes):
        size = math.prod(s)
        full_small[n] = _from_source_major(sh_flat[:, off:off + size], s)
        off += size

    tril = jnp.tril(jnp.ones((LANE, LANE), dtype=bool))
    ws_m = jnp.where(tril[None, None], w_s, 0.0)
    pair = lambda t: jnp.concatenate([t[:, 0::2], t[:, 1::2]], axis=-1).astype(BF16)
    wcat = pair(ws_m)
    wcat_t = pair(jnp.swapaxes(ws_m, -1, -2))
    bs_full = jnp.repeat(jnp.swapaxes(b_s, -1, -2), w // n_heads, axis=-1)
    row2 = lambda a, j: a[j][None, :]

    x2 = x.reshape(t_len, d_model)
    p3 = p.reshape(depth, t_len, d_ple)
    lt2 = loss_target.reshape(t_len, d_model)

    xs, saved = [x2], []
    dxn = sse = None
    for i in range(depth):
        j = i // 2
        if i % 2 == 0:
            z, y, s1, s2 = _even_fwd(xs[i], win_e[j], row2(b_in_e, j), conv_a_w_full(full_small, j), row2(conv_a_b, j),
                                     row2(ln_a_g, j), row2(ln_a_b, j), row2(ln_v_g, j), row2(ln_v_b, j), wcat[j], bs_full[j],
                                     seq=seq, tm=tm)
            w_out, b_out = wout_e[j], row2(b_out_e, j)
        else:
            z, y, s1, s2 = _odd_fwd(xs[i], win_o[j], row2(full_small["b_in_o"], j), w_pool[j].astype(BF16),
                                    row2(full_small["pool_scale"], j), full_small["conv_d_w"][j], seq=seq, tm=tm)
            w_out, b_out = wout_o[j], row2(full_small["b_out_o"], j)
        last = i == depth - 1
        outs = _post_fwd(y, xs[i], p3, i, w_out, b_out, row2(ln_g, i), row2(ln_b, i), wgate[i], row2(b_ple_gate, i),
                         wple[i], lt2 if last else None, tm=tm)
        xn, rh, h, rstd, gate, e = outs[:6]
        if last:
            dxn, sse = outs[6:]
        xs.append(xn)
        saved.append(dict(z=z, y=y, s1=s1, s2=s2, rh=rh, h=h, rstd=rstd, gate=gate, e=e, w_out=w_out))

    loss = lax.psum((0.5 / d_model) * sse[0, 0], ("x", "y", "c"))

    g_in = {0: None, 1: None}
    g_out = {0: None, 1: None}
    g_ple = g_gate = None
    small = {n: [None] * weights[n].shape[0] for n in REP_NAMES + SH_NAMES}
    for i in reversed(range(depth)):
        j, par = i // 2, i % 2
        sv = saved[i]
        de, dgl, dy, dr, acc = _post_bwd(dxn, sv["gate"], sv["e"], sv["rh"], sv["rstd"], row2(ln_g, i), wgate[i], sv["w_out"], tm=tm)
        small["b_ple_gate"][i], small["ln_g"][i], small["ln_b"][i] = acc[0], acc[1], acc[2]
        small["b_out_o" if par else "b_out_e"][j] = acc[3]
        if par == 0:
            dz, dx, dcw, vec, dbin, dws, dbs = _even_bwd(dy, sv["z"], sv["s1"], sv["s2"], dr, win_e[j], conv_a_w_full(full_small, j),
                                                         row2(ln_a_g, j), row2(ln_a_b, j), row2(ln_v_g, j), row2(ln_v_b, j),
                                                         wcat_t[j], seq=seq, tm=tm)
            small["conv_a_w"][j], small["conv_a_b"][j] = dcw, vec[0]
            small["ln_a_g"][j], small["ln_a_b"][j], small["ln_v_g"][j], small["ln_v_b"][j] = vec[1], vec[2], vec[3], vec[4]
            small["b_in_e"][j], small["w_s"][j] = dbin[0], dws
            small["b_s"][j] = _head_sums(dbs, n_heads)
        else:
            dz, dx, dwp, vec, dbin = _odd_bwd(dy, sv["z"], sv["s1"], sv["s2"], dr, win_o[j], w_pool[j].astype(BF16),
                                              row2(full_small["pool_scale"], j), full_small["conv_d_w"][j], seq=seq, tm=tm)
            small["w_pool"][j], small["pool_scale"][j], small["conv_d_w"][j] = dwp, vec[0], vec[1:1 + CONV_D_WIDTH]
            small["b_in_o"][j] = dbin[0]
        tag = f"_l{i}"
        g_in[par] = _weight_grad("dw_in" + tag, xs[i], dz, j, n_odd if par else n_even, prev=g_in[par], bn=in_cols * N_DEV // 2)
        g_out[par] = _weight_grad("dw_out" + tag, sv["y"], dr, j, n_odd if par else n_even, prev=g_out[par])
        g_gate = _weight_grad("dw_gate" + tag, sv["h"], dgl, i, depth, prev=g_gate)
        g_ple = _weight_grad("dw_ple" + tag, p3, de, i, depth, prev=g_ple, a_layer=i)
        dxn = dx
    grad_x = dxn.reshape(batch, seq, d_model)

    small_full = {n: jnp.stack(small[n]) for n in small}
    rep_part = _pack_rows([small_full[n] for n in REP_NAMES])
    sh_part = jnp.concatenate([_to_dest_major(small_full[n]) for n in SH_NAMES], axis=1)
    sh_part = jnp.concatenate([sh_part, jnp.zeros((N_DEV, sh_pad - sh_len), F32)], axis=1).reshape(N_DEV, sh_rows, LANE)
    rep_rows = rep_part.shape[0]
    recv = _exchange(
        "exchange_grads",
        [g_in[0], g_in[1], g_out[0], g_out[1], g_ple, g_gate, sh_part, rep_part],
        [_axis_view(2, in_cols), _axis_view(2, in_cols), _axis_view(1, out_rows), _axis_view(1, out_rows),
         _axis_view(2, ple_cols), _axis_view(1, gate_rows), _lead_view, _whole_view],
        [jax.ShapeDtypeStruct((N_DEV,) + w_in_e.shape, F32), jax.ShapeDtypeStruct((N_DEV,) + w_in_o.shape, F32),
         jax.ShapeDtypeStruct((N_DEV,) + w_out_e.shape, F32), jax.ShapeDtypeStruct((N_DEV,) + w_out_o.shape, F32),
         jax.ShapeDtypeStruct((N_DEV,) + w_ple.shape, F32), jax.ShapeDtypeStruct((N_DEV,) + w_ple_gate.shape, F32),
         jax.ShapeDtypeStruct((N_DEV, sh_rows, LANE), F32), jax.ShapeDtypeStruct((N_DEV, rep_rows, LANE), F32)],
        [_lead_view] * 8)

    results = {}
    for n, parts in zip(("w_in_e", "w_in_o", "w_out_e", "w_out_o", "w_ple", "w_ple_gate"), recv[:6]):
        shp = weights[n].shape
        rows, cols = math.prod(shp[:-1]), shp[-1]
        two = lambda a: a.reshape(rows, cols)
        outs = _adamw_reduce("adamw_" + n, parts.reshape(N_DEV, rows, cols), two(weights[n]), two(mom_m[n]), two(mom_v[n]),
                             _block_rows(rows, cols))
        results[n] = [o.reshape(shp) for o in outs]
    pack_sh = lambda d: _pack_rows([d[n] for n in SH_NAMES], sh_pad)
    outs = _adamw_reduce("adamw_small_sharded", recv[6], pack_sh(weights), pack_sh(mom_m), pack_sh(mom_v), sh_rows)
    for n, *vals in zip(SH_NAMES, *[_unpack_rows(o, sh_shapes) for o in outs]):
        results[n] = vals
    pack_rep = lambda d: _pack_rows([d[n] for n in REP_NAMES])
    rep_shapes = [weights[n].shape for n in REP_NAMES]
    outs = _adamw_reduce("adamw_replicated", recv[7], pack_rep(weights), pack_rep(mom_m), pack_rep(mom_v),
                         _block_rows(rep_rows, LANE))
    for n, *vals in zip(REP_NAMES, *[_unpack_rows(o, rep_shapes) for o in outs]):
        results[n] = vals

    return (loss, grad_x, *[results[n][0] for n in names], *[results[n][1] for n in names],
            *[results[n][2] for n in names], *[results[n][3] for n in names])


def conv_a_w_full(full_small, j):
    return full_small["conv_a_w"][j]


def _head_sums(dbs, n_heads):
    t, width = dbs.shape
    return jnp.sum(dbs.reshape(t, n_heads, width // n_heads), axis=-1).T
```

```python
import functools
import math

import jax
import jax.numpy as jnp
from jax import lax
from jax.experimental import pallas as pl
from jax.experimental.pallas import tpu as pltpu

F32 = jnp.float32
BF16 = jnp.bfloat16

N_DEV = 8
DEPTH = 4
LN_EPS = 1e-5
DEEPNORM_ALPHA = (2.0 * DEPTH) ** 0.25
POOL_WINDOWS = (2, 4, 8, 16)
CONV_A_WIDTH = 31
CONV_D_WIDTH = 3
GELU_C = math.sqrt(2.0 / math.pi)
GELU_K = 0.044715

ADAM_LR = 0.001
ADAM_B1 = 0.9
ADAM_B2 = 0.999
ADAM_EPS = 1e-08
ADAM_WD = 0.01
ADAM_STEP = 10

LANE = 128
SUBLANE = 8
HALO_A = 32
HALO_C = 16
HALO_D = 8
ROW_CHUNK = 32
VMEM_LIMIT = 56 * 2**20

ANY = pl.BlockSpec(memory_space=pl.ANY)
MESH = pl.DeviceIdType.MESH

REP_NAMES = ("b_in_e", "conv_a_b", "ln_a_g", "ln_a_b", "ln_v_g", "ln_v_b", "w_s", "b_s", "b_out_e", "w_pool", "ln_g",
             "ln_b", "b_ple_gate")
SH_NAMES = ("conv_a_w", "conv_d_w", "pool_scale", "b_in_o", "b_out_o")


def _params(n_grid_axes):
    return pltpu.CompilerParams(dimension_semantics=("arbitrary",) * n_grid_axes, vmem_limit_bytes=VMEM_LIMIT)


def _const(shape):
    nd = len(shape)
    return pl.BlockSpec(shape, lambda *_: (0,) * nd)


def _dot(a, b):
    return jnp.dot(a, b, preferred_element_type=F32)


def _dot_nt(a, b):
    return lax.dot_general(a, b, (((1,), (1,)), ((), ())), preferred_element_type=F32)


def _dot_tn(a, b):
    return lax.dot_general(a, b, (((0,), (0,)), ((), ())), preferred_element_type=F32)


def _silu(x):
    return x * jax.nn.sigmoid(x)


def _silu_grad(x):
    s = jax.nn.sigmoid(x)
    return x * s, s * (1.0 + x * (1.0 - s))


def _gelu(x):
    return 0.5 * x * (1.0 + jnp.tanh(GELU_C * (x + GELU_K * x * x * x)))


def _gelu_grad(x):
    x2 = x * x
    t = jnp.tanh(GELU_C * x * (1.0 + GELU_K * x2))
    return 0.5 * x * (1.0 + t), 0.5 * (1.0 + t) + 0.5 * x * (1.0 - t * t) * GELU_C * (1.0 + 3.0 * GELU_K * x2)


def _norm(v):
    mu = jnp.mean(v, axis=-1, keepdims=True)
    d = v - mu
    var = jnp.mean(d * d, axis=-1, keepdims=True)
    rstd = lax.rsqrt(var + LN_EPS)
    return d * rstd, rstd


def _norm_bwd(dxh, xh, rstd):
    return rstd * (dxh - jnp.mean(dxh, axis=-1, keepdims=True) - xh * jnp.mean(dxh * xh, axis=-1, keepdims=True))


def _sum0(v):
    return jnp.sum(v, axis=0, keepdims=True)


def _head_low_mask(n_cols):
    lane = lax.broadcasted_iota(jnp.int32, (LANE, n_cols), 1)
    return (lane & (LANE - 1)) < (LANE // 2)


def _blocks_to_lanes(v, j, nb):
    return jnp.concatenate([v[n * LANE:(n + 1) * LANE, j * LANE:(j + 1) * LANE] for n in range(nb)], axis=1)


def _axis_view(axis, size):
    def view(ref, d):
        idx = [slice(None)] * len(ref.shape)
        idx[axis] = pl.ds(pl.multiple_of(d * size, size), size)
        return ref.at[tuple(idx)]
    return view


def _slot_view(*slot):
    return lambda ref, d: ref.at[(d,) + slot]


def _whole_view(ref, d):
    return ref


PEER_BITS = (1, 2, 4, 6, 3, 5, 7)


class _Rider:
    def __init__(self, entries):
        self.n = len(entries)
        self.srcs = [e[0] for e in entries]
        self.src_views = [e[1] for e in entries]
        self.dsts = [e[2] for e in entries]
        self.dst_views = [e[3] for e in entries]
        self.passed = [a for a, d in enumerate(self.dsts) if not isinstance(d, jax.ShapeDtypeStruct)]

    def operands(self):
        return self.srcs + [self.dsts[a] for a in self.passed]

    def out_shape(self):
        return [jax.ShapeDtypeStruct(d.shape, d.dtype) for d in self.dsts]

    def scratch(self):
        return [pltpu.SemaphoreType.DMA((7 * self.n,)), pltpu.SemaphoreType.DMA((7 * self.n,)), pltpu.SemaphoreType.DMA((self.n,))]

    def aliases(self, n_in_before, n_out_before):
        return {n_in_before + self.n + q: n_out_before + a for q, a in enumerate(self.passed)}

    def _copies(self, src, dst, sems):
        send_sems, recv_sems, local_sems = sems
        x, y, c = lax.axis_index("x"), lax.axis_index("y"), lax.axis_index("c")
        me = 4 * x + 2 * y + c
        local = [pltpu.make_async_copy(self.src_views[a](src[a], me), self.dst_views[a](dst[a], me), local_sems.at[a])
                 for a in range(self.n)]
        sends, recvs = [], []
        for ki, k in enumerate(PEER_BITS):
            px, py, pc = x ^ (k >> 2), y ^ ((k >> 1) & 1), c ^ (k & 1)
            peer = 4 * px + 2 * py + pc
            for a in range(self.n):
                s = a * 7 + ki
                mk = lambda landing: pltpu.make_async_remote_copy(
                    src_ref=self.src_views[a](src[a], peer), dst_ref=self.dst_views[a](dst[a], landing),
                    send_sem=send_sems.at[s], recv_sem=recv_sems.at[s], device_id=(px, py, pc), device_id_type=MESH)
                sends.append(mk(me))
                recvs.append(mk(peer))
        return local, sends, recvs

    def start(self, src, dst, sems):
        local, sends, _ = self._copies(src, dst, sems)
        for cp in local + sends:
            cp.start()

    def wait(self, src, dst, sems):
        local, sends, recvs = self._copies(src, dst, sems)
        for cp in recvs:
            cp.wait_recv()
        for cp in sends:
            cp.wait_send()
        for cp in local:
            cp.wait()


def _call(body, *, name, grid, in_specs, args, out_shape, out_specs, scratch=(), rider=None, aliases=None):
    n_in, n_out, n_scr = len(args), len(out_shape), len(scratch)
    aliases = dict(aliases or {})
    if rider is None:
        return pl.pallas_call(body, name=name, grid=grid, out_shape=tuple(out_shape), in_specs=list(in_specs),
                              out_specs=tuple(out_specs), scratch_shapes=list(scratch), input_output_aliases=aliases,
                              compiler_params=_params(len(grid)))(*args)
    r_ops = rider.operands()

    def full_body(*refs):
        ins, refs = refs[:n_in], refs[n_in:]
        r_src, refs = refs[:rider.n], refs[len(r_ops):]
        outs, refs = refs[:n_out], refs[n_out:]
        r_dst, refs = refs[:rider.n], refs[rider.n:]
        scr, sems = refs[:n_scr], refs[n_scr:]
        if grid:
            @pl.when(pl.program_id(0) == 0)
            def _():
                rider.start(r_src, r_dst, sems)
            body(*ins, *outs, *scr)

            @pl.when(pl.program_id(0) == grid[0] - 1)
            def _():
                rider.wait(r_src, r_dst, sems)
        else:
            rider.start(r_src, r_dst, sems)
            rider.wait(r_src, r_dst, sems)

    aliases.update(rider.aliases(n_in, n_out))
    kw = dict(compiler_params=_params(len(grid))) if grid else {}
    if grid:
        kw["grid"] = grid
    return pl.pallas_call(
        full_body, name=name, out_shape=tuple(out_shape) + tuple(rider.out_shape()),
        in_specs=list(in_specs) + [ANY] * len(r_ops), out_specs=tuple(out_specs) + tuple([ANY] * rider.n),
        scratch_shapes=list(scratch) + rider.scratch(), input_output_aliases=aliases, **kw)(*args, *r_ops)


def _exchange(name, entries):
    return _call(None, name=name, grid=(), in_specs=[], args=[], out_shape=[], out_specs=[], rider=_Rider(entries))


def _z_parts(x_ref, w_ref, b_ref, z_ref, width):
    xb = x_ref[...].astype(BF16)

    def part(k):
        cols = slice(k * width, (k + 1) * width)
        zk = (_dot(xb, w_ref[:, cols]) + b_ref[:, cols]).astype(BF16)
        z_ref[:, cols] = zk
        return zk.astype(F32)
    return part


def _even_fwd(x, w_in, b_in, conv_w, conv_b, ln_a_g, ln_a_b, ln_v_g, ln_v_b, wcat, bs_full, *, seq, tm, rider=None):
    t_len, d_model = x.shape
    w = d_model // 2
    nt, tps, nb = t_len // tm, seq // tm, tm // LANE

    def body(x_ref, w_ref, b_ref, cw_ref, cb_ref, lag_ref, lab_ref, lvg_ref, lvb_ref, wcat_ref, bs_ref,
             z_ref, y_ref, a1_ref, sg_ref, a0_s):
        i = pl.program_id(0)

        @pl.when(i % tps == 0)
        def _():
            a0_s[0:HALO_A, :] = jnp.zeros((HALO_A, w), F32)

        part = _z_parts(x_ref, w_ref, b_ref, z_ref, w)
        a0_s[HALO_A:HALO_A + tm, :] = part(0) * jax.nn.sigmoid(part(1))
        for r in range(0, tm, ROW_CHUNK):
            acc = jnp.zeros((ROW_CHUNK, w), F32) + cb_ref[...]
            for k in range(CONV_A_WIDTH):
                acc = acc + cw_ref[k:k + 1, :] * a0_s[pl.ds(HALO_A - (CONV_A_WIDTH - 1) + k + r, ROW_CHUNK), :]
            a1_ref[r:r + ROW_CHUNK, :] = acc
        a0_s[0:HALO_A, :] = a0_s[tm:tm + HALO_A, :]
        ah, _ = _norm(a1_ref[...])
        a = _silu(ah * lag_ref[...] + lab_ref[...]) * _silu(part(2))
        y_ref[:, 0:w] = a.astype(BF16)

        u = part(3)
        vh, _ = _norm(_gelu(part(4)))
        v2 = vh * lvg_ref[...] + lvb_ref[...]
        low = _head_low_mask(nb * LANE)
        for j in range(w // LANE):
            vt = _blocks_to_lanes(v2, j, nb)
            rhs = jnp.concatenate([jnp.where(low, vt, 0.0), jnp.where(low, 0.0, vt)], axis=0).astype(BF16)
            out = _dot(wcat_ref[j], rhs)
            for n in range(nb):
                sg_ref[n * LANE:(n + 1) * LANE, j * LANE:(j + 1) * LANE] = (
                    out[:, n * LANE:(n + 1) * LANE] + bs_ref[:, j * LANE:(j + 1) * LANE]).astype(BF16)
        g = _gelu(u) * sg_ref[...].astype(F32) * _silu(part(5))
        y_ref[:, w:2 * w] = g.astype(BF16)

    row = lambda cols: pl.BlockSpec((tm, cols), lambda i: (i, 0))
    return _call(
        body, name="even_fwd", grid=(nt,), rider=rider,
        out_shape=(jax.ShapeDtypeStruct((t_len, 6 * w), BF16), jax.ShapeDtypeStruct((t_len, d_model), BF16),
                   jax.ShapeDtypeStruct((t_len, w), F32), jax.ShapeDtypeStruct((t_len, w), BF16)),
        in_specs=[row(d_model), _const(w_in.shape), _const(b_in.shape), _const(conv_w.shape), _const(conv_b.shape),
                  _const(ln_a_g.shape), _const(ln_a_b.shape), _const(ln_v_g.shape), _const(ln_v_b.shape),
                  _const(wcat.shape), _const(bs_full.shape)],
        out_specs=(row(6 * w), row(d_model), row(w), row(w)),
        scratch=[pltpu.VMEM((tm + HALO_A, w), F32)],
        args=[x, w_in, b_in, conv_w, conv_b, ln_a_g, ln_a_b, ln_v_g, ln_v_b, wcat, bs_full])


def _pool_inverse(tile_index, tm, seq, window):
    row = tile_index * tm + lax.broadcasted_iota(jnp.int32, (tm, 1), 0)
    pos = (row % seq + 1).astype(F32)
    return 1.0 / jnp.minimum(pos, float(window))


def _odd_fwd(x, w_in, b_in, w_pool, pool_scale, conv_w, *, seq, tm, rider=None):
    t_len, d_model = x.shape
    w = d_model // 2
    nt, tps = t_len // tm, seq // tm

    def body(x_ref, w_ref, b_ref, wp_ref, ps_ref, cw_ref, z_ref, y_ref, pooled_ref, q_ref, cv_s, hc_s):
        i = pl.program_id(0)

        @pl.when(i % tps == 0)
        def _():
            cv_s[0:HALO_C, :] = jnp.zeros((HALO_C, w), F32)
            hc_s[0:HALO_D, :] = jnp.zeros((HALO_D, w), F32)

        part = _z_parts(x_ref, w_ref, b_ref, z_ref, w)
        c_val = part(0)
        c_gate = part(1)
        cv_s[HALO_C:HALO_C + tm, :] = c_val
        for gi, win in enumerate(POOL_WINDOWS):
            cols = slice(gi * LANE, (gi + 1) * LANE)
            s = cv_s[pl.ds(HALO_C, tm), cols]
            for j in range(1, win):
                s = s + cv_s[pl.ds(HALO_C - j, tm), cols]
            pooled = (s * _pool_inverse(i, tm, seq, win) - c_val[:, cols]).astype(BF16)
            pooled_ref[:, cols] = pooled
            c = _dot(pooled, wp_ref[gi]) * ps_ref[:, cols] * _silu(c_gate[:, cols])
            y_ref[:, cols] = c.astype(BF16)
        cv_s[0:HALO_C, :] = cv_s[tm:tm + HALO_C, :]

        d_h = part(2)
        d_b = part(3)
        hc_s[HALO_D:HALO_D + tm, :] = part(4) * d_h
        q = jnp.zeros((tm, w), F32)
        for k in range(CONV_D_WIDTH):
            q = q + cw_ref[k:k + 1, :] * hc_s[pl.ds(HALO_D - (CONV_D_WIDTH - 1) + k, tm), :]
        hc_s[0:HALO_D, :] = hc_s[tm:tm + HALO_D, :]
        qb = q.astype(BF16)
        q_ref[...] = qb
        y_ref[:, w:2 * w] = (d_b * qb.astype(F32) * _silu(part(5))).astype(BF16)

    row = lambda cols: pl.BlockSpec((tm, cols), lambda i: (i, 0))
    return _call(
        body, name="odd_fwd", grid=(nt,), rider=rider,
        out_shape=(jax.ShapeDtypeStruct((t_len, 6 * w), BF16), jax.ShapeDtypeStruct((t_len, d_model), BF16),
                   jax.ShapeDtypeStruct((t_len, w), BF16), jax.ShapeDtypeStruct((t_len, w), BF16)),
        in_specs=[row(d_model), _const(w_in.shape), _const(b_in.shape), _const(w_pool.shape), _const(pool_scale.shape),
                  _const(conv_w.shape)],
        out_specs=(row(6 * w), row(d_model), row(w), row(w)),
        scratch=[pltpu.VMEM((tm + HALO_C, w), F32), pltpu.VMEM((tm + HALO_D, w), F32)],
        args=[x, w_in, b_in, w_pool, pool_scale, conv_w])


def _post_fwd(y, x, p_all, layer, w_out, b_out, ln_g, ln_b, wg, bg, wp, loss_target, *, tm):
    t_len, d_model = x.shape
    d_ple = p_all.shape[-1]
    nt = t_len // tm
    last = loss_target is not None

    def body(*refs):
        y_ref, x_ref, p_ref, wo_ref, bo_ref, g_ref, b_ref, wg_ref, bg_ref, wp_ref = refs[:10]
        rest = refs[10:]
        if last:
            lt_ref, rest = rest[0], rest[1:]
        xn_ref, rh_ref, h_ref, rstd_ref, gate_ref, e_ref = rest[:6]
        r = DEEPNORM_ALPHA * x_ref[...] + _dot(y_ref[...], wo_ref[...]) + bo_ref[...]
        rh, rstd = _norm(r)
        h = rh * g_ref[...] + b_ref[...]
        hb = h.astype(BF16)
        gate = jax.nn.sigmoid(_dot(hb, wg_ref[...]) + bg_ref[...])
        e = _dot(p_ref[...].astype(BF16), wp_ref[...])
        xn = h + gate * e
        xn_ref[...] = xn
        rh_ref[...] = rh.astype(BF16)
        h_ref[...] = hb
        rstd_ref[...] = jnp.broadcast_to(rstd, (tm, LANE))
        gate_ref[...] = gate.astype(BF16)
        e_ref[...] = e.astype(BF16)
        if last:
            dxn_ref, sse_ref = rest[6:]
            diff = xn - lt_ref[...]
            dxn_ref[...] = diff * (1.0 / d_model)

            @pl.when(pl.program_id(0) == 0)
            def _():
                sse_ref[...] = jnp.zeros_like(sse_ref)
            sse_ref[...] += jnp.sum(_sum0(diff * diff), axis=1, keepdims=True)

    row = lambda cols: pl.BlockSpec((tm, cols), lambda i: (i, 0))
    in_specs = [row(d_model), row(d_model), pl.BlockSpec((None, tm, d_ple), lambda i: (layer, i, 0)),
                _const(w_out.shape), _const(b_out.shape), _const(ln_g.shape), _const(ln_b.shape), _const(wg.shape),
                _const(bg.shape), _const(wp.shape)]
    args = [y, x, p_all, w_out, b_out, ln_g, ln_b, wg, bg, wp]
    out_shape = [jax.ShapeDtypeStruct((t_len, d_model), F32), jax.ShapeDtypeStruct((t_len, d_model), BF16),
                 jax.ShapeDtypeStruct((t_len, d_model), BF16), jax.ShapeDtypeStruct((t_len, LANE), F32),
                 jax.ShapeDtypeStruct((t_len, d_model), BF16), jax.ShapeDtypeStruct((t_len, d_model), BF16)]
    out_specs = [row(d_model), row(d_model), row(d_model), row(LANE), row(d_model), row(d_model)]
    if last:
        in_specs.append(row(d_model))
        args.append(loss_target)
        out_shape += [jax.ShapeDtypeStruct((t_len, d_model), F32), jax.ShapeDtypeStruct((SUBLANE, LANE), F32)]
        out_specs += [row(d_model), _const((SUBLANE, LANE))]
    return pl.pallas_call(body, name="post_fwd_last" if last else "post_fwd", grid=(nt,), out_shape=tuple(out_shape),
                          in_specs=in_specs, out_specs=tuple(out_specs), compiler_params=_params(1))(*args)


def _post_bwd(dxn, gate, e, rh, rstd, ln_g, wg, w_out, *, tm):
    t_len, d_model = dxn.shape
    nt = t_len // tm

    def body(dxn_ref, gate_ref, e_ref, rh_ref, rstd_ref, g_ref, wg_ref, wo_ref, de_ref, dgl_ref, dy_ref, dr_ref, acc_ref):
        @pl.when(pl.program_id(0) == 0)
        def _():
            acc_ref[...] = jnp.zeros_like(acc_ref)

        d = dxn_ref[...]
        gt = gate_ref[...].astype(F32)
        rhat = rh_ref[...].astype(F32)
        de_ref[...] = (d * gt).astype(BF16)
        dgl = d * e_ref[...].astype(F32) * gt * (1.0 - gt)
        dglb = dgl.astype(BF16)
        dgl_ref[...] = dglb
        dh = d + _dot_nt(dglb, wg_ref[...])
        dr = _norm_bwd(dh * g_ref[...], rhat, rstd_ref[:, 0:1])
        dr_ref[...] = dr
        dy_ref[...] = _dot_nt(dr.astype(BF16), wo_ref[...]).astype(BF16)
        acc_ref[0:1, :] += _sum0(dgl)
        acc_ref[1:2, :] += _sum0(dh * rhat)
        acc_ref[2:3, :] += _sum0(dh)
        acc_ref[3:4, :] += _sum0(dr)

    row = lambda cols: pl.BlockSpec((tm, cols), lambda i: (i, 0))
    return pl.pallas_call(
        body, name="post_bwd", grid=(nt,),
        out_shape=(jax.ShapeDtypeStruct((t_len, d_model), BF16), jax.ShapeDtypeStruct((t_len, d_model), BF16),
                   jax.ShapeDtypeStruct((t_len, d_model), BF16), jax.ShapeDtypeStruct((t_len, d_model), F32),
                   jax.ShapeDtypeStruct((SUBLANE, d_model), F32)),
        in_specs=[row(d_model), row(d_model), row(d_model), row(d_model), row(LANE), _const(ln_g.shape), _const(wg.shape),
                  _const(w_out.shape)],
        out_specs=(row(d_model), row(d_model), row(d_model), row(d_model), _const((SUBLANE, d_model))),
        compiler_params=_params(1),
    )(dxn, gate, e, rh, rstd, ln_g, wg, w_out)


def _dz_store(dz_ref, dbin_ref, width):
    def store(k, v):
        cols = slice(k * width, (k + 1) * width)
        vb = v.astype(BF16)
        dz_ref[:, cols] = vb
        dbin_ref[0:1, cols] += _sum0(v)
        return vb
    return store


def _even_bwd(dy, z, a1, sg, dr, w_in, conv_w, ln_a_g, ln_a_b, ln_v_g, ln_v_b, wcat_t, *, seq, tm, rider=None):
    t_len, d_model = dr.shape
    w = d_model // 2
    nt, tps, nb = t_len // tm, seq // tm, tm // LANE
    n_heads = 2 * (w // LANE)

    def body(dy_ref, z_ref, a1_ref, sg_ref, dr_ref, w_ref, cw_ref, lag_ref, lab_ref, lvg_ref, lvb_ref, wct_ref,
             dz_ref, dx_ref, dcw_ref, vec_ref, dbin_ref, dws_ref, dbs_ref, da1_s, a0_s, da0_s, cw_acc):
        i = pl.program_id(0)
        tile = nt - 1 - i

        @pl.when(i == 0)
        def _():
            vec_ref[...] = jnp.zeros_like(vec_ref)
            dbin_ref[...] = jnp.zeros_like(dbin_ref)
            dws_ref[...] = jnp.zeros_like(dws_ref)
            dbs_ref[...] = jnp.zeros_like(dbs_ref)
            cw_acc[...] = jnp.zeros_like(cw_acc)

        @pl.when((tile + 1) % tps == 0)
        def _():
            da1_s[tm:tm + HALO_A, :] = jnp.zeros((HALO_A, w), F32)

        zp = lambda k: z_ref[:, k * w:(k + 1) * w].astype(F32)
        store = _dz_store(dz_ref, dbin_ref, w)

        da = dy_ref[:, 0:w].astype(F32)
        a_val, a_glu, a_gate = zp(0), zp(1), zp(2)
        s_glu = jax.nn.sigmoid(a_glu)
        a0_s[...] = a_val * s_glu
        ah, rstd_a = _norm(a1_ref[...])
        silu_a2, dsilu_a2 = _silu_grad(ah * lag_ref[...] + lab_ref[...])
        silu_ag, dsilu_ag = _silu_grad(a_gate)
        dzb2 = store(2, da * silu_a2 * dsilu_ag)
        da2 = da * silu_ag * dsilu_a2
        vec_ref[1:2, :] += _sum0(da2 * ah)
        vec_ref[2:3, :] += _sum0(da2)
        da1 = _norm_bwd(da2 * lag_ref[...], ah, rstd_a)
        vec_ref[0:1, :] += _sum0(da1)
        da1_s[0:tm, :] = da1
        for r in range(0, tm, ROW_CHUNK):
            a0c = a0_s[r:r + ROW_CHUNK, :]
            acc = jnp.zeros((ROW_CHUNK, w), F32)
            for k in range(CONV_A_WIDTH):
                d = da1_s[pl.ds(r + (CONV_A_WIDTH - 1) - k, ROW_CHUNK), :]
                acc = acc + cw_ref[k:k + 1, :] * d
                pw = a0c * d
                p8 = pw[0:SUBLANE]
                for q in range(1, ROW_CHUNK // SUBLANE):
                    p8 = p8 + pw[q * SUBLANE:(q + 1) * SUBLANE]
                cw_acc[k * SUBLANE:(k + 1) * SUBLANE, :] += p8
            da0_s[r:r + ROW_CHUNK, :] = acc
        da1_s[tm:tm + HALO_A, :] = da1_s[0:HALO_A, :]
        da0 = da0_s[...]
        dzb0 = store(0, da0 * s_glu)
        dzb1 = store(1, da0 * a_val * s_glu * (1.0 - s_glu))

        dg = dy_ref[:, w:2 * w].astype(F32)
        u, v, gg = zp(3), zp(4), zp(5)
        sgv = sg_ref[...].astype(F32)
        gelu_u, dgelu_u = _gelu_grad(u)
        silu_gg, dsilu_gg = _silu_grad(gg)
        dzb5 = store(5, dg * gelu_u * sgv * dsilu_gg)
        t1 = dg * silu_gg
        dzb3 = store(3, t1 * sgv * dgelu_u)
        dsg = t1 * gelu_u
        gelu_v, dgelu_v = _gelu_grad(v)
        vh, rstd_v = _norm(gelu_v)
        v2 = vh * lvg_ref[...] + lvb_ref[...]
        low = _head_low_mask(nb * LANE)
        for j in range(w // LANE):
            dt = _blocks_to_lanes(dsg, j, nb)
            d_lo = jnp.where(low, dt, 0.0).astype(BF16)
            d_hi = jnp.where(low, 0.0, dt).astype(BF16)
            v2t = _blocks_to_lanes(v2, j, nb).astype(BF16)
            dv2t = _dot(wct_ref[j], jnp.concatenate([d_lo, d_hi], axis=0))
            for n in range(nb):
                da0_s[n * LANE:(n + 1) * LANE, j * LANE:(j + 1) * LANE] = dv2t[:, n * LANE:(n + 1) * LANE]
            dws_ref[2 * j] += _dot_nt(d_lo, v2t)
            dws_ref[2 * j + 1] += _dot_nt(d_hi, v2t)
            bsum = dt[:, 0:LANE]
            for n in range(1, nb):
                bsum = bsum + dt[:, n * LANE:(n + 1) * LANE]
            dbs_ref[:, j * LANE:(j + 1) * LANE] += bsum
        dv2 = da0_s[...]
        vec_ref[3:4, :] += _sum0(dv2 * vh)
        vec_ref[4:5, :] += _sum0(dv2)
        dzb4 = store(4, _norm_bwd(dv2 * lvg_ref[...], vh, rstd_v) * dgelu_v)

        dx = DEEPNORM_ALPHA * dr_ref[...]
        for k, dzb in enumerate((dzb0, dzb1, dzb2, dzb3, dzb4, dzb5)):
            dx = dx + _dot_nt(dzb, w_ref[:, k * w:(k + 1) * w])
        dx_ref[...] = dx

        @pl.when(i == nt - 1)
        def _():
            for k in range(CONV_A_WIDTH):
                dcw_ref[k:k + 1, :] = _sum0(cw_acc[k * SUBLANE:(k + 1) * SUBLANE, :])
            keep = (lax.broadcasted_iota(jnp.int32, (LANE, LANE), 0) >= lax.broadcasted_iota(jnp.int32, (LANE, LANE), 1))
            for hd in range(n_heads):
                dws_ref[hd] = jnp.where(keep, dws_ref[hd], 0.0)

    rev = lambda cols: pl.BlockSpec((tm, cols), lambda i: (nt - 1 - i, 0))
    return _call(
        body, name="even_bwd", grid=(nt,), rider=rider,
        out_shape=(jax.ShapeDtypeStruct((t_len, 6 * w), BF16), jax.ShapeDtypeStruct((t_len, d_model), F32),
                   jax.ShapeDtypeStruct((CONV_A_WIDTH, w), F32), jax.ShapeDtypeStruct((SUBLANE, w), F32),
                   jax.ShapeDtypeStruct((SUBLANE, 6 * w), F32), jax.ShapeDtypeStruct((n_heads, LANE, LANE), F32),
                   jax.ShapeDtypeStruct((LANE, w), F32)),
        in_specs=[rev(d_model), rev(6 * w), rev(w), rev(w), rev(d_model), _const(w_in.shape), _const(conv_w.shape),
                  _const(ln_a_g.shape), _const(ln_a_b.shape), _const(ln_v_g.shape), _const(ln_v_b.shape), _const(wcat_t.shape)],
        out_specs=(rev(6 * w), rev(d_model), _const((CONV_A_WIDTH, w)), _const((SUBLANE, w)), _const((SUBLANE, 6 * w)),
                   _const((n_heads, LANE, LANE)), _const((LANE, w))),
        scratch=[pltpu.VMEM((tm + HALO_A, w), F32), pltpu.VMEM((tm, w), F32), pltpu.VMEM((tm, w), F32),
                 pltpu.VMEM((CONV_A_WIDTH * SUBLANE, w), F32)],
        args=[dy, z, a1, sg, dr, w_in, conv_w, ln_a_g, ln_a_b, ln_v_g, ln_v_b, wcat_t])


def _odd_bwd(dy, z, pooled, q, dr, w_in, w_pool, pool_scale, conv_w, *, seq, tm, rider=None):
    t_len, d_model = dr.shape
    w = d_model // 2
    nt, tps = t_len // tm, seq // tm
    n_groups = len(POOL_WINDOWS)

    def body(dy_ref, z_ref, pooled_ref, q_ref, dr_ref, w_ref, wp_ref, ps_ref, cw_ref,
             dz_ref, dx_ref, dwp_ref, vec_ref, dbin_ref, dm_s, dq_s):
        i = pl.program_id(0)
        tile = nt - 1 - i

        @pl.when(i == 0)
        def _():
            dwp_ref[...] = jnp.zeros_like(dwp_ref)
            vec_ref[...] = jnp.zeros_like(vec_ref)
            dbin_ref[...] = jnp.zeros_like(dbin_ref)

        @pl.when((tile + 1) % tps == 0)
        def _():
            dm_s[tm:tm + HALO_C, :] = jnp.zeros((HALO_C, w), F32)
            dq_s[tm:tm + HALO_D, :] = jnp.zeros((HALO_D, w), F32)

        zp = lambda k: z_ref[:, k * w:(k + 1) * w].astype(F32)
        store = _dz_store(dz_ref, dbin_ref, w)

        dc = dy_ref[:, 0:w].astype(F32)
        c_gate = zp(1)
        silu_c, dsilu_c = _silu_grad(c_gate)
        dcs = dc * silu_c
        dvg_parts, dcg_parts = [], []
        for gi, win in enumerate(POOL_WINDOWS):
            cols = slice(gi * LANE, (gi + 1) * LANE)
            pooled_g = pooled_ref[:, cols]
            wp = wp_ref[gi]
            cpre = _dot(pooled_g, wp)
            scale = ps_ref[:, cols]
            vec_ref[0:1, cols] += _sum0(dcs[:, cols] * cpre)
            dcg_parts.append(dc[:, cols] * cpre * scale * dsilu_c[:, cols])
            dcp = (dcs[:, cols] * scale).astype(BF16)
            dwp_ref[gi] += _dot_tn(pooled_g, dcp)
            dpooled = _dot_nt(dcp, wp)
            dm_s[0:tm, cols] = dpooled * _pool_inverse(tile, tm, seq, win)
            s = dm_s[pl.ds(0, tm), cols]
            for j in range(1, win):
                s = s + dm_s[pl.ds(j, tm), cols]
            dvg_parts.append(s - dpooled)
        dm_s[tm:tm + HALO_C, :] = dm_s[0:HALO_C, :]
        dzb0 = store(0, jnp.concatenate(dvg_parts, axis=1))
        dzb1 = store(1, jnp.concatenate(dcg_parts, axis=1))

        dd = dy_ref[:, w:2 * w].astype(F32)
        d_h, d_b, d_c, d_gate = zp(2), zp(3), zp(4), zp(5)
        qv = q_ref[...].astype(F32)
        silu_d, dsilu_d = _silu_grad(d_gate)
        dzb5 = store(5, dd * d_b * qv * dsilu_d)
        dzb3 = store(3, dd * qv * silu_d)
        dq_s[0:tm, :] = dd * d_b * silu_d
        hc = d_c * d_h
        dhc = jnp.zeros((tm, w), F32)
        for k in range(CONV_D_WIDTH):
            d = dq_s[pl.ds((CONV_D_WIDTH - 1) - k, tm), :]
            dhc = dhc + cw_ref[k:k + 1, :] * d
            vec_ref[1 + k:2 + k, :] += _sum0(hc * d)
        dq_s[tm:tm + HALO_D, :] = dq_s[0:HALO_D, :]
        dzb2 = store(2, dhc * d_c)
        dzb4 = store(4, dhc * d_h)

        dx = DEEPNORM_ALPHA * dr_ref[...]
        for k, dzb in enumerate((dzb0, dzb1, dzb2, dzb3, dzb4, dzb5)):
            dx = dx + _dot_nt(dzb, w_ref[:, k * w:(k + 1) * w])
        dx_ref[...] = dx

    rev = lambda cols: pl.BlockSpec((tm, cols), lambda i: (nt - 1 - i, 0))
    return _call(
        body, name="odd_bwd", grid=(nt,), rider=rider,
        out_shape=(jax.ShapeDtypeStruct((t_len, 6 * w), BF16), jax.ShapeDtypeStruct((t_len, d_model), F32),
                   jax.ShapeDtypeStruct((n_groups, LANE, LANE), F32), jax.ShapeDtypeStruct((SUBLANE, w), F32),
                   jax.ShapeDtypeStruct((SUBLANE, 6 * w), F32)),
        in_specs=[rev(d_model), rev(6 * w), rev(w), rev(w), rev(d_model), _const(w_in.shape), _const(w_pool.shape),
                  _const(pool_scale.shape), _const(conv_w.shape)],
        out_specs=(rev(6 * w), rev(d_model), _const((n_groups, LANE, LANE)), _const((SUBLANE, w)), _const((SUBLANE, 6 * w))),
        scratch=[pltpu.VMEM((tm + HALO_C, w), F32), pltpu.VMEM((tm + HALO_D, w), F32)],
        args=[dy, z, pooled, q, dr, w_in, w_pool, pool_scale, conv_w])


def _weight_grad(name, a, b, a_layer=None, bn=None):
    if a_layer is None:
        t_len, m = a.shape
        a_spec = lambda tk: pl.BlockSpec((tk, m), lambda n, k: (k, 0))
    else:
        _, t_len, m = a.shape
        a_spec = lambda tk: pl.BlockSpec((None, tk, m), lambda n, k: (a_layer, k, 0))
    n_cols = b.shape[1]
    bn = n_cols if bn is None else bn
    tk = min(t_len, 512)
    n_k = t_len // tk

    def body(a_ref, b_ref, o_ref, acc):
        k = pl.program_id(1)

        @pl.when(k == 0)
        def _():
            acc[...] = jnp.zeros_like(acc)
        acc[...] += _dot_tn(a_ref[...].astype(BF16), b_ref[...].astype(BF16))

        @pl.when(k == n_k - 1)
        def _():
            o_ref[...] = acc[...].astype(BF16)

    return pl.pallas_call(
        body, name=name, grid=(n_cols // bn, n_k), out_shape=jax.ShapeDtypeStruct((m, n_cols), BF16),
        in_specs=[a_spec(tk), pl.BlockSpec((tk, bn), lambda n, k: (k, n))],
        out_specs=pl.BlockSpec((m, bn), lambda n, k: (0, n)), scratch_shapes=[pltpu.VMEM((m, bn), F32)],
        compiler_params=_params(2),
    )(a, b)


def _adamw_reduce(name, parts, w, m, v, rows_per_block):
    n_rows, n_cols = w.shape
    br = rows_per_block

    def body(p_ref, w_ref, m_ref, v_ref, g_ref, d_ref, nm_ref, nv_ref):
        g = p_ref[0].astype(F32)
        for k in range(1, N_DEV):
            g = g + p_ref[k].astype(F32)
        nm = ADAM_B1 * m_ref[...] + (1.0 - ADAM_B1) * g
        nv = ADAM_B2 * v_ref[...] + (1.0 - ADAM_B2) * (g * g)
        m_hat = nm / (1.0 - ADAM_B1 ** ADAM_STEP)
        v_hat = nv / (1.0 - ADAM_B2 ** ADAM_STEP)
        g_ref[...] = g
        d_ref[...] = -ADAM_LR * (m_hat / (jnp.sqrt(v_hat) + ADAM_EPS) + ADAM_WD * w_ref[...])
        nm_ref[...] = nm
        nv_ref[...] = nv

    blk = pl.BlockSpec((br, n_cols), lambda i: (i, 0))
    shp = jax.ShapeDtypeStruct((n_rows, n_cols), F32)
    return pl.pallas_call(
        body, name=name, grid=(n_rows // br,), out_shape=(shp, shp, shp, shp),
        in_specs=[pl.BlockSpec((N_DEV, br, n_cols), lambda i: (0, i, 0)), blk, blk, blk], out_specs=(blk, blk, blk, blk),
        compiler_params=_params(1),
    )(parts, w, m, v)


def _pack_rows(flat_parts, pad_to=None, width=LANE):
    flat = jnp.concatenate([a.reshape(-1) for a in flat_parts])
    if pad_to is not None and pad_to > flat.shape[0]:
        flat = jnp.concatenate([flat, jnp.zeros((pad_to - flat.shape[0],), flat.dtype)])
    return flat.reshape(-1, width)


def _unpack_rows(packed, shapes):
    flat = packed.reshape(-1)
    out, off = [], 0
    for s in shapes:
        n = math.prod(s)
        out.append(flat[off:off + n].reshape(s))
        off += n
    return out


def _to_dest_major(full):
    lead, last = full.shape[:-1], full.shape[-1]
    t = full.reshape(lead + (N_DEV, last // N_DEV))
    return jnp.moveaxis(t, -2, 0).reshape(N_DEV, -1)


def _from_source_major(blocks, shard_shape):
    t = blocks.reshape((N_DEV,) + tuple(shard_shape))
    t = jnp.moveaxis(t, 0, -2)
    return t.reshape(tuple(shard_shape[:-1]) + (N_DEV * shard_shape[-1],))


def _block_rows(n_rows, n_cols, target_elems=96 * 1024):
    best = None
    for br in range(SUBLANE, n_rows + 1, SUBLANE):
        if n_rows % br == 0 and br * n_cols <= target_elems:
            best = br
    return n_rows if best is None else best


def kernel(x, p, w_in_e, b_in_e, conv_a_w, conv_a_b, ln_a_g, ln_a_b, ln_v_g, ln_v_b, w_s, b_s, w_out_e, b_out_e, w_in_o, b_in_o, w_pool, pool_scale, conv_d_w, w_out_o, b_out_o, ln_g, ln_b, w_ple, w_ple_gate, b_ple_gate, loss_target, m_w_in_e, m_b_in_e, m_conv_a_w, m_conv_a_b, m_ln_a_g, m_ln_a_b, m_ln_v_g, m_ln_v_b, m_w_s, m_b_s, m_w_out_e, m_b_out_e, m_w_in_o, m_b_in_o, m_w_pool, m_pool_scale, m_conv_d_w, m_w_out_o, m_b_out_o, m_ln_g, m_ln_b, m_w_ple, m_w_ple_gate, m_b_ple_gate, v_w_in_e, v_b_in_e, v_conv_a_w, v_conv_a_b, v_ln_a_g, v_ln_a_b, v_ln_v_g, v_ln_v_b, v_w_s, v_b_s, v_w_out_e, v_b_out_e, v_w_in_o, v_b_in_o, v_w_pool, v_pool_scale, v_conv_d_w, v_w_out_o, v_b_out_o, v_ln_g, v_ln_b, v_w_ple, v_w_ple_gate, v_b_ple_gate):
    weights = dict(w_in_e=w_in_e, b_in_e=b_in_e, conv_a_w=conv_a_w, conv_a_b=conv_a_b, ln_a_g=ln_a_g, ln_a_b=ln_a_b,
                   ln_v_g=ln_v_g, ln_v_b=ln_v_b, w_s=w_s, b_s=b_s, w_out_e=w_out_e, b_out_e=b_out_e, w_in_o=w_in_o,
                   b_in_o=b_in_o, w_pool=w_pool, pool_scale=pool_scale, conv_d_w=conv_d_w, w_out_o=w_out_o,
                   b_out_o=b_out_o, ln_g=ln_g, ln_b=ln_b, w_ple=w_ple, w_ple_gate=w_ple_gate, b_ple_gate=b_ple_gate)
    mom_m = dict(w_in_e=m_w_in_e, b_in_e=m_b_in_e, conv_a_w=m_conv_a_w, conv_a_b=m_conv_a_b, ln_a_g=m_ln_a_g,
                 ln_a_b=m_ln_a_b, ln_v_g=m_ln_v_g, ln_v_b=m_ln_v_b, w_s=m_w_s, b_s=m_b_s, w_out_e=m_w_out_e,
                 b_out_e=m_b_out_e, w_in_o=m_w_in_o, b_in_o=m_b_in_o, w_pool=m_w_pool, pool_scale=m_pool_scale,
                 conv_d_w=m_conv_d_w, w_out_o=m_w_out_o, b_out_o=m_b_out_o, ln_g=m_ln_g, ln_b=m_ln_b, w_ple=m_w_ple,
                 w_ple_gate=m_w_ple_gate, b_ple_gate=m_b_ple_gate)
    mom_v = dict(w_in_e=v_w_in_e, b_in_e=v_b_in_e, conv_a_w=v_conv_a_w, conv_a_b=v_conv_a_b, ln_a_g=v_ln_a_g,
                 ln_a_b=v_ln_a_b, ln_v_g=v_ln_v_g, ln_v_b=v_ln_v_b, w_s=v_w_s, b_s=v_b_s, w_out_e=v_w_out_e,
                 b_out_e=v_b_out_e, w_in_o=v_w_in_o, b_in_o=v_b_in_o, w_pool=v_w_pool, pool_scale=v_pool_scale,
                 conv_d_w=v_conv_d_w, w_out_o=v_w_out_o, b_out_o=v_b_out_o, ln_g=v_ln_g, ln_b=v_ln_b, w_ple=v_w_ple,
                 w_ple_gate=v_w_ple_gate, b_ple_gate=v_b_ple_gate)
    names = tuple(weights)

    batch, seq, d_model = x.shape
    t_len = batch * seq
    w = d_model // 2
    n_even = w_in_e.shape[0]
    n_odd = w_in_o.shape[0]
    depth = ln_g.shape[0]
    d_ple = p.shape[-1]
    n_heads = w_s.shape[1]
    tm = 512 if seq % 512 == 0 and seq >= 1024 else seq // 2
    in_cols = w_in_e.shape[-1]
    out_rows = w_out_e.shape[1]
    ple_cols = w_ple.shape[-1]
    gate_rows = w_ple_gate.shape[1]

    sh_shapes = [weights[n].shape for n in SH_NAMES]
    sh_len = sum(math.prod(s) for s in sh_shapes)
    sh_pad = -(-sh_len // (SUBLANE * LANE)) * (SUBLANE * LANE)
    sh_rows = sh_pad // LANE
    sds = jax.ShapeDtypeStruct
    w_in16 = (w_in_e.astype(BF16), w_in_o.astype(BF16))
    w_out16 = (w_out_e.astype(BF16), w_out_o.astype(BF16))
    w_ple16, w_gate16 = w_ple.astype(BF16), w_ple_gate.astype(BF16)

    def weight_entries(i):
        j, par = i // 2, i % 2
        return [(w_in16[par][j], _whole_view, sds((d_model, N_DEV * in_cols), BF16), _axis_view(1, in_cols)),
                (w_out16[par][j], _whole_view, sds((N_DEV * out_rows, d_model), BF16), _axis_view(0, out_rows)),
                (w_ple16[i], _whole_view, sds((d_ple, N_DEV * ple_cols), BF16), _axis_view(1, ple_cols)),
                (w_gate16[i], _whole_view, sds((N_DEV * gate_rows, d_model), BF16), _axis_view(0, gate_rows))]

    first = _exchange("gather_first", weight_entries(0) + [
        (_pack_rows([weights[n] for n in SH_NAMES], sh_pad), _whole_view, sds((N_DEV, sh_rows, LANE), F32), _slot_view())])
    layer_w = {0: first[:4]}
    sh_flat = first[4].reshape(N_DEV, sh_pad)
    full_small, off = {}, 0
    for n, s in zip(SH_NAMES, sh_shapes):
        size = math.prod(s)
        full_small[n] = _from_source_major(sh_flat[:, off:off + size], s)
        off += size

    tril = jnp.tril(jnp.ones((LANE, LANE), dtype=bool))
    ws_m = jnp.where(tril[None, None], w_s, 0.0)
    pair = lambda t: jnp.concatenate([t[:, 0::2], t[:, 1::2]], axis=-1).astype(BF16)
    wcat = pair(ws_m)
    wcat_t = pair(jnp.swapaxes(ws_m, -1, -2))
    bs_full = jnp.repeat(jnp.swapaxes(b_s, -1, -2), w // n_heads, axis=-1)
    row2 = lambda a, j: a[j][None, :]

    x2 = x.reshape(t_len, d_model)
    p3 = p.reshape(depth, t_len, d_ple)
    lt2 = loss_target.reshape(t_len, d_model)

    xs, saved = [x2], []
    dxn = sse = None
    for i in range(depth):
        j = i // 2
        last = i == depth - 1
        w_in, w_out, wple, wgate = layer_w[i]
        rider = None if last else _Rider(weight_entries(i + 1))
        if i % 2 == 0:
            outs = _even_fwd(xs[i], w_in, row2(b_in_e, j), full_small["conv_a_w"][j], row2(conv_a_b, j),
                             row2(ln_a_g, j), row2(ln_a_b, j), row2(ln_v_g, j), row2(ln_v_b, j), wcat[j], bs_full[j],
                             seq=seq, tm=tm, rider=rider)
            b_out = row2(b_out_e, j)
        else:
            outs = _odd_fwd(xs[i], w_in, row2(full_small["b_in_o"], j), w_pool[j].astype(BF16),
                            row2(full_small["pool_scale"], j), full_small["conv_d_w"][j], seq=seq, tm=tm, rider=rider)
            b_out = row2(full_small["b_out_o"], j)
        z, y, s1, s2 = outs[:4]
        if not last:
            layer_w[i + 1] = outs[4:]
        outs = _post_fwd(y, xs[i], p3, i, w_out, b_out, row2(ln_g, i), row2(ln_b, i), wgate, row2(b_ple_gate, i),
                         wple, lt2 if last else None, tm=tm)
        xn, rh, h, rstd, gate, e = outs[:6]
        if last:
            dxn, sse = outs[6:]
        xs.append(xn)
        saved.append(dict(z=z, y=y, s1=s1, s2=s2, rh=rh, h=h, rstd=rstd, gate=gate, e=e))

    loss = lax.psum((0.5 / d_model) * sse[0, 0], ("x", "y", "c"))

    recv = {"w_in_e": None, "w_in_o": None, "w_out_e": None, "w_out_o": None, "w_ple": None, "w_ple_gate": None}

    def grad_entries(i, g_in, g_out, g_ple, g_gate):
        j, par = i // 2, i % 2
        sfx = "_o" if par else "_e"

        def entry(g, src_view, name, slot):
            dst = recv[name] if recv[name] is not None else sds((N_DEV,) + weights[name].shape, BF16)
            return (g, src_view, dst, _slot_view(slot))
        return (("w_in" + sfx, "w_out" + sfx, "w_ple", "w_ple_gate"),
                [entry(g_in, _axis_view(1, in_cols), "w_in" + sfx, j), entry(g_out, _axis_view(0, out_rows), "w_out" + sfx, j),
                 entry(g_ple, _axis_view(1, ple_cols), "w_ple", i), entry(g_gate, _axis_view(0, gate_rows), "w_ple_gate", i)])

    small = {n: [None] * weights[n].shape[0] for n in REP_NAMES + SH_NAMES}
    pending = None
    for i in reversed(range(depth)):
        j, par = i // 2, i % 2
        sv = saved[i]
        w_in, w_out, wple, wgate = layer_w[i]
        de, dgl, dy, dr, acc = _post_bwd(dxn, sv["gate"], sv["e"], sv["rh"], sv["rstd"], row2(ln_g, i), wgate, w_out, tm=tm)
        small["b_ple_gate"][i], small["ln_g"][i], small["ln_b"][i] = acc[0], acc[1], acc[2]
        small["b_out_o" if par else "b_out_e"][j] = acc[3]
        rider_names, rider = (None, None) if pending is None else (pending[0], _Rider(pending[1]))
        if par == 0:
            outs = _even_bwd(dy, sv["z"], sv["s1"], sv["s2"], dr, w_in, full_small["conv_a_w"][j],
                             row2(ln_a_g, j), row2(ln_a_b, j), row2(ln_v_g, j), row2(ln_v_b, j), wcat_t[j], seq=seq, tm=tm,
                             rider=rider)
            dz, dx, dcw, vec, dbin, dws, dbs = outs[:7]
            landed = outs[7:]
            small["conv_a_w"][j], small["conv_a_b"][j] = dcw, vec[0]
            small["ln_a_g"][j], small["ln_a_b"][j], small["ln_v_g"][j], small["ln_v_b"][j] = vec[1], vec[2], vec[3], vec[4]
            small["b_in_e"][j], small["w_s"][j] = dbin[0], dws
            small["b_s"][j] = _head_sums(dbs, n_heads)
        else:
            outs = _odd_bwd(dy, sv["z"], sv["s1"], sv["s2"], dr, w_in, w_pool[j].astype(BF16),
                            row2(full_small["pool_scale"], j), full_small["conv_d_w"][j], seq=seq, tm=tm, rider=rider)
            dz, dx, dwp, vec, dbin = outs[:5]
            landed = outs[5:]
            small["w_pool"][j], small["pool_scale"][j], small["conv_d_w"][j] = dwp, vec[0], vec[1:1 + CONV_D_WIDTH]
            small["b_in_o"][j] = dbin[0]
        if rider is not None:
            recv.update(zip(rider_names, landed))
        tag = f"_l{i}"
        pending = grad_entries(i, _weight_grad("dw_in" + tag, xs[i], dz, bn=in_cols * N_DEV // 2),
                               _weight_grad("dw_out" + tag, sv["y"], dr), _weight_grad("dw_ple" + tag, p3, de, a_layer=i),
                               _weight_grad("dw_gate" + tag, sv["h"], dgl))
        dxn = dx
    grad_x = dxn.reshape(batch, seq, d_model)

    small_full = {n: jnp.stack(small[n]) for n in small}
    rep_width = 4 * LANE
    rep_len = sum(math.prod(weights[n].shape) for n in REP_NAMES)
    rep_block = 64
    rep_pad = -(-rep_len // (rep_block * rep_width)) * (rep_block * rep_width)
    rep_rows = rep_pad // rep_width
    rep_part = _pack_rows([small_full[n] for n in REP_NAMES], rep_pad, rep_width)
    sh_part = jnp.concatenate([_to_dest_major(small_full[n]) for n in SH_NAMES], axis=1)
    sh_part = jnp.concatenate([sh_part, jnp.zeros((N_DEV, sh_pad - sh_len), F32)], axis=1).reshape(N_DEV, sh_rows, LANE)
    landed = _exchange("exchange_last", pending[1] + [
        (sh_part, _slot_view(), sds((N_DEV, sh_rows, LANE), F32), _slot_view()),
        (rep_part, _whole_view, sds((N_DEV, rep_rows, rep_width), F32), _slot_view())])
    recv.update(zip(pending[0], landed[:4]))

    results = {}
    for n, parts in recv.items():
        shp = weights[n].shape
        rows, cols = math.prod(shp[:-1]), shp[-1]
        two = lambda a: a.reshape(rows, cols)
        outs = _adamw_reduce("adamw_" + n, parts.reshape(N_DEV, rows, cols), two(weights[n]), two(mom_m[n]), two(mom_v[n]),
                             _block_rows(rows, cols))
        results[n] = [o.reshape(shp) for o in outs]
    pack_sh = lambda d: _pack_rows([d[n] for n in SH_NAMES], sh_pad)
    outs = _adamw_reduce("adamw_small_sharded", landed[4], pack_sh(weights), pack_sh(mom_m), pack_sh(mom_v), sh_rows)
    for n, *vals in zip(SH_NAMES, *[_unpack_rows(o, sh_shapes) for o in outs]):
        results[n] = vals
    pack_rep = lambda d: _pack_rows([d[n] for n in REP_NAMES], rep_pad, rep_width)
    rep_shapes = [weights[n].shape for n in REP_NAMES]
    outs = _adamw_reduce("adamw_replicated", landed[5], pack_rep(weights), pack_rep(mom_m), pack_rep(mom_v), rep_block)
    for n, *vals in zip(REP_NAMES, *[_unpack_rows(o, rep_shapes) for o in outs]):
        results[n] = vals

    return (loss, grad_x, *[results[n][0] for n in names], *[results[n][1] for n in names],
            *[results[n][2] for n in names], *[results[n][3] for n in names])


def _head_sums(dbs, n_heads):
    t, width = dbs.shape
    return jnp.sum(dbs.reshape(t, n_heads, width // n_heads), axis=-1).T
```

```python
import functools
import math

import jax
import jax.numpy as jnp
from jax import lax
from jax.experimental import pallas as pl
from jax.experimental.pallas import tpu as pltpu

F32 = jnp.float32
BF16 = jnp.bfloat16

N_DEV = 8
DEPTH = 4
LN_EPS = 1e-5
DEEPNORM_ALPHA = (2.0 * DEPTH) ** 0.25
POOL_WINDOWS = (2, 4, 8, 16)
CONV_A_WIDTH = 31
CONV_D_WIDTH = 3
GELU_C = math.sqrt(2.0 / math.pi)
GELU_K = 0.044715

ADAM_LR = 0.001
ADAM_B1 = 0.9
ADAM_B2 = 0.999
ADAM_EPS = 1e-08
ADAM_WD = 0.01
ADAM_STEP = 10

LANE = 128
SUBLANE = 8
HALO_A = 32
HALO_C = 16
HALO_D = 8
ROW_CHUNK = 32
VMEM_LIMIT = 56 * 2**20

ANY = pl.BlockSpec(memory_space=pl.ANY)
MESH = pl.DeviceIdType.MESH

REP_NAMES = ("b_in_e", "conv_a_b", "ln_a_g", "ln_a_b", "ln_v_g", "ln_v_b", "w_s", "b_s", "b_out_e", "w_pool", "ln_g",
             "ln_b", "b_ple_gate")
SH_NAMES = ("conv_a_w", "conv_d_w", "pool_scale", "b_in_o", "b_out_o")


def _params(n_grid_axes):
    return pltpu.CompilerParams(dimension_semantics=("arbitrary",) * n_grid_axes, vmem_limit_bytes=VMEM_LIMIT)


def _const(shape):
    nd = len(shape)
    return pl.BlockSpec(shape, lambda *_: (0,) * nd)


def _dot(a, b):
    return jnp.dot(a, b, preferred_element_type=F32)


def _dot_nt(a, b):
    return lax.dot_general(a, b, (((1,), (1,)), ((), ())), preferred_element_type=F32)


def _dot_tn(a, b):
    return lax.dot_general(a, b, (((0,), (0,)), ((), ())), preferred_element_type=F32)


def _silu(x):
    return x * jax.nn.sigmoid(x)


def _silu_grad(x):
    s = jax.nn.sigmoid(x)
    return x * s, s * (1.0 + x * (1.0 - s))


def _gelu(x):
    return 0.5 * x * (1.0 + jnp.tanh(GELU_C * (x + GELU_K * x * x * x)))


def _gelu_grad(x):
    x2 = x * x
    t = jnp.tanh(GELU_C * x * (1.0 + GELU_K * x2))
    return 0.5 * x * (1.0 + t), 0.5 * (1.0 + t) + 0.5 * x * (1.0 - t * t) * GELU_C * (1.0 + 3.0 * GELU_K * x2)


def _norm(v):
    mu = jnp.mean(v, axis=-1, keepdims=True)
    d = v - mu
    var = jnp.mean(d * d, axis=-1, keepdims=True)
    rstd = lax.rsqrt(var + LN_EPS)
    return d * rstd, rstd


def _norm_bwd(dxh, xh, rstd):
    return rstd * (dxh - jnp.mean(dxh, axis=-1, keepdims=True) - xh * jnp.mean(dxh * xh, axis=-1, keepdims=True))


def _sum0(v):
    return jnp.sum(v, axis=0, keepdims=True)


def _head_low_mask(n_cols):
    lane = lax.broadcasted_iota(jnp.int32, (LANE, n_cols), 1)
    return (lane & (LANE - 1)) < (LANE // 2)


def _blocks_to_lanes(v, j, nb):
    return jnp.concatenate([v[n * LANE:(n + 1) * LANE, j * LANE:(j + 1) * LANE] for n in range(nb)], axis=1)


def _axis_view(axis, size):
    def view(ref, d):
        idx = [slice(None)] * len(ref.shape)
        idx[axis] = pl.ds(pl.multiple_of(d * size, size), size)
        return ref.at[tuple(idx)]
    return view


def _slot_view(*slot):
    return lambda ref, d: ref.at[(d,) + slot]


def _whole_view(ref, d):
    return ref


PEER_BITS = (1, 2, 4, 6, 3, 5, 7)


class _Rider:
    def __init__(self, entries):
        self.n = len(entries)
        self.srcs = [e[0] for e in entries]
        self.src_views = [e[1] for e in entries]
        self.dsts = [e[2] for e in entries]
        self.dst_views = [e[3] for e in entries]
        self.passed = [a for a, d in enumerate(self.dsts) if not isinstance(d, jax.ShapeDtypeStruct)]

    def operands(self):
        return self.srcs + [self.dsts[a] for a in self.passed]

    def out_shape(self):
        return [jax.ShapeDtypeStruct(d.shape, d.dtype) for d in self.dsts]

    def scratch(self):
        return [pltpu.SemaphoreType.DMA((7 * self.n,)), pltpu.SemaphoreType.DMA((7 * self.n,)), pltpu.SemaphoreType.DMA((self.n,))]

    def aliases(self, n_in_before, n_out_before):
        return {n_in_before + self.n + q: n_out_before + a for q, a in enumerate(self.passed)}

    def _copies(self, src, dst, sems):
        send_sems, recv_sems, local_sems = sems
        x, y, c = lax.axis_index("x"), lax.axis_index("y"), lax.axis_index("c")
        me = 4 * x + 2 * y + c
        local = [pltpu.make_async_copy(self.src_views[a](src[a], me), self.dst_views[a](dst[a], me), local_sems.at[a])
                 for a in range(self.n)]
        sends, recvs = [], []
        for ki, k in enumerate(PEER_BITS):
            px, py, pc = x ^ (k >> 2), y ^ ((k >> 1) & 1), c ^ (k & 1)
            peer = 4 * px + 2 * py + pc
            for a in range(self.n):
                s = a * 7 + ki
                mk = lambda landing: pltpu.make_async_remote_copy(
                    src_ref=self.src_views[a](src[a], peer), dst_ref=self.dst_views[a](dst[a], landing),
                    send_sem=send_sems.at[s], recv_sem=recv_sems.at[s], device_id=(px, py, pc), device_id_type=MESH)
                sends.append(mk(me))
                recvs.append(mk(peer))
        return local, sends, recvs

    def start(self, src, dst, sems):
        local, sends, _ = self._copies(src, dst, sems)
        for cp in local + sends:
            cp.start()

    def wait(self, src, dst, sems):
        local, sends, recvs = self._copies(src, dst, sems)
        for cp in recvs:
            cp.wait_recv()
        for cp in sends:
            cp.wait_send()
        for cp in local:
            cp.wait()


def _call(body, *, name, grid, in_specs, args, out_shape, out_specs, scratch=(), rider=None, aliases=None):
    n_in, n_out, n_scr = len(args), len(out_shape), len(scratch)
    aliases = dict(aliases or {})
    if rider is None:
        return pl.pallas_call(body, name=name, grid=grid, out_shape=tuple(out_shape), in_specs=list(in_specs),
                              out_specs=tuple(out_specs), scratch_shapes=list(scratch), input_output_aliases=aliases,
                              compiler_params=_params(len(grid)))(*args)
    r_ops = rider.operands()

    def full_body(*refs):
        ins, refs = refs[:n_in], refs[n_in:]
        r_src, refs = refs[:rider.n], refs[len(r_ops):]
        outs, refs = refs[:n_out], refs[n_out:]
        r_dst, refs = refs[:rider.n], refs[rider.n:]
        scr, sems = refs[:n_scr], refs[n_scr:]
        if grid:
            first = last = None
            for axis, size in enumerate(grid):
                at_start, at_end = pl.program_id(axis) == 0, pl.program_id(axis) == size - 1
                first = at_start if first is None else jnp.logical_and(first, at_start)
                last = at_end if last is None else jnp.logical_and(last, at_end)

            @pl.when(first)
            def _():
                rider.start(r_src, r_dst, sems)
            body(*ins, *outs, *scr)

            @pl.when(last)
            def _():
                rider.wait(r_src, r_dst, sems)
        else:
            rider.start(r_src, r_dst, sems)
            rider.wait(r_src, r_dst, sems)

    aliases.update(rider.aliases(n_in, n_out))
    kw = dict(compiler_params=_params(len(grid))) if grid else {}
    if grid:
        kw["grid"] = grid
    return pl.pallas_call(
        full_body, name=name, out_shape=tuple(out_shape) + tuple(rider.out_shape()),
        in_specs=list(in_specs) + [ANY] * len(r_ops), out_specs=tuple(out_specs) + tuple([ANY] * rider.n),
        scratch_shapes=list(scratch) + rider.scratch(), input_output_aliases=aliases, **kw)(*args, *r_ops)


def _exchange(name, entries):
    return _call(None, name=name, grid=(), in_specs=[], args=[], out_shape=[], out_specs=[], rider=_Rider(entries))


def _allreduce_rows(name, part):
    n_rows, n_cols = part.shape
    sl = n_rows // N_DEV

    def body(p_ref, o_ref, recv_v, sum_v, send1, recv1, send2, recv2, local_sems):
        x, y, c = lax.axis_index("x"), lax.axis_index("y"), lax.axis_index("c")
        me = 4 * x + 2 * y + c
        rows_of = lambda d: pl.ds(pl.multiple_of(d * sl, SUBLANE), sl)
        peers = []
        for ki, k in enumerate(PEER_BITS):
            px, py, pc = x ^ (k >> 2), y ^ ((k >> 1) & 1), c ^ (k & 1)
            peers.append((ki, (px, py, pc), 4 * px + 2 * py + pc))

        def scatter(ki, dev, peer, landing):
            return pltpu.make_async_remote_copy(src_ref=p_ref.at[rows_of(peer)], dst_ref=recv_v.at[landing],
                                                send_sem=send1.at[ki], recv_sem=recv1.at[ki], device_id=dev, device_id_type=MESH)

        def gather(ki, dev, landing):
            return pltpu.make_async_remote_copy(src_ref=sum_v, dst_ref=o_ref.at[rows_of(landing)],
                                                send_sem=send2.at[ki], recv_sem=recv2.at[ki], device_id=dev, device_id_type=MESH)

        own = pltpu.make_async_copy(p_ref.at[rows_of(me)], recv_v.at[me], local_sems.at[0])
        own.start()
        for ki, dev, peer in peers:
            scatter(ki, dev, peer, me).start()
        for ki, dev, peer in peers:
            scatter(ki, dev, peer, peer).wait_recv()
        own.wait()
        total = recv_v[0]
        for d in range(1, N_DEV):
            total = total + recv_v[d]
        sum_v[...] = total
        own = pltpu.make_async_copy(sum_v, o_ref.at[rows_of(me)], local_sems.at[1])
        own.start()
        for ki, dev, peer in peers:
            gather(ki, dev, me).start()
        for ki, dev, peer in peers:
            gather(ki, dev, peer).wait_recv()
        for ki, dev, peer in peers:
            scatter(ki, dev, peer, me).wait_send()
            gather(ki, dev, me).wait_send()
        own.wait()

    return pl.pallas_call(
        body, name=name, out_shape=jax.ShapeDtypeStruct((n_rows, n_cols), F32), in_specs=[ANY], out_specs=ANY,
        scratch_shapes=[pltpu.VMEM((N_DEV, sl, n_cols), F32), pltpu.VMEM((sl, n_cols), F32),
                        pltpu.SemaphoreType.DMA((7,)), pltpu.SemaphoreType.DMA((7,)), pltpu.SemaphoreType.DMA((7,)),
                        pltpu.SemaphoreType.DMA((7,)), pltpu.SemaphoreType.DMA((2,))],
    )(part)


def _shifted_windows(buf, shift_s, n_rows):
    for o in range(1, SUBLANE):
        shift_s[o - 1, 0:n_rows - SUBLANE, :] = buf[pl.ds(o, n_rows - SUBLANE), :]

    def window(s):
        q, o = divmod(s, SUBLANE)
        src = buf if o == 0 else shift_s.at[o - 1]
        return src[pl.ds(q * SUBLANE, ROW_CHUNK), :]
    return window


def _z_parts(x_ref, w_ref, b_ref, z_ref, width):
    xb = x_ref[...].astype(BF16)

    def part(k):
        cols = slice(k * width, (k + 1) * width)
        zk = (_dot(xb, w_ref[:, cols]) + b_ref[:, cols]).astype(BF16)
        z_ref[:, cols] = zk
        return zk.astype(F32)
    return part


def _even_fwd(x, w_in, b_in, conv_w, conv_b, ln_a_g, ln_a_b, ln_v_g, ln_v_b, wcat, bs_full, *, seq, tm, rider=None):
    t_len, d_model = x.shape
    w = d_model // 2
    nt, tps, nb = t_len // tm, seq // tm, tm // LANE

    def body(x_ref, w_ref, b_ref, cw_ref, cb_ref, lag_ref, lab_ref, lvg_ref, lvb_ref, wcat_ref, bs_ref,
             z_ref, y_ref, a1_ref, sg_ref, a0_s, shift_s):
        i = pl.program_id(0)

        @pl.when(i % tps == 0)
        def _():
            a0_s[0:HALO_A, :] = jnp.zeros((HALO_A, w), F32)

        part = _z_parts(x_ref, w_ref, b_ref, z_ref, w)
        a0_s[HALO_A:HALO_A + tm, :] = part(0) * jax.nn.sigmoid(part(1))
        window = _shifted_windows(a0_s, shift_s, tm + HALO_A)
        for r in range(0, tm, ROW_CHUNK):
            acc = jnp.zeros((ROW_CHUNK, w), F32) + cb_ref[...]
            for k in range(CONV_A_WIDTH):
                acc = acc + cw_ref[k:k + 1, :] * window(HALO_A - (CONV_A_WIDTH - 1) + k + r)
            a1_ref[r:r + ROW_CHUNK, :] = acc
        a0_s[0:HALO_A, :] = a0_s[tm:tm + HALO_A, :]
        ah, _ = _norm(a1_ref[...])
        a = _silu(ah * lag_ref[...] + lab_ref[...]) * _silu(part(2))
        y_ref[:, 0:w] = a.astype(BF16)

        u = part(3)
        vh, _ = _norm(_gelu(part(4)))
        v2 = vh * lvg_ref[...] + lvb_ref[...]
        low = _head_low_mask(nb * LANE)
        for j in range(w // LANE):
            vt = _blocks_to_lanes(v2, j, nb)
            rhs = jnp.concatenate([jnp.where(low, vt, 0.0), jnp.where(low, 0.0, vt)], axis=0).astype(BF16)
            out = _dot(wcat_ref[j], rhs)
            for n in range(nb):
                sg_ref[n * LANE:(n + 1) * LANE, j * LANE:(j + 1) * LANE] = (
                    out[:, n * LANE:(n + 1) * LANE] + bs_ref[:, j * LANE:(j + 1) * LANE]).astype(BF16)
        g = _gelu(u) * sg_ref[...].astype(F32) * _silu(part(5))
        y_ref[:, w:2 * w] = g.astype(BF16)

    row = lambda cols: pl.BlockSpec((tm, cols), lambda i: (i, 0))
    return _call(
        body, name="even_fwd", grid=(nt,), rider=rider,
        out_shape=(jax.ShapeDtypeStruct((t_len, 6 * w), BF16), jax.ShapeDtypeStruct((t_len, d_model), BF16),
                   jax.ShapeDtypeStruct((t_len, w), F32), jax.ShapeDtypeStruct((t_len, w), BF16)),
        in_specs=[row(d_model), _const(w_in.shape), _const(b_in.shape), _const(conv_w.shape), _const(conv_b.shape),
                  _const(ln_a_g.shape), _const(ln_a_b.shape), _const(ln_v_g.shape), _const(ln_v_b.shape),
                  _const(wcat.shape), _const(bs_full.shape)],
        out_specs=(row(6 * w), row(d_model), row(w), row(w)),
        scratch=[pltpu.VMEM((tm + HALO_A, w), F32), pltpu.VMEM((SUBLANE - 1, tm + HALO_A, w), F32)],
        args=[x, w_in, b_in, conv_w, conv_b, ln_a_g, ln_a_b, ln_v_g, ln_v_b, wcat, bs_full])


def _pool_inverse(tile_index, tm, seq, window):
    row = tile_index * tm + lax.broadcasted_iota(jnp.int32, (tm, 1), 0)
    pos = (row % seq + 1).astype(F32)
    return 1.0 / jnp.minimum(pos, float(window))


def _odd_fwd(x, w_in, b_in, w_pool, pool_scale, conv_w, *, seq, tm, rider=None):
    t_len, d_model = x.shape
    w = d_model // 2
    nt, tps = t_len // tm, seq // tm

    def body(x_ref, w_ref, b_ref, wp_ref, ps_ref, cw_ref, z_ref, y_ref, pooled_ref, q_ref, cv_s, hc_s):
        i = pl.program_id(0)

        @pl.when(i % tps == 0)
        def _():
            cv_s[0:HALO_C, :] = jnp.zeros((HALO_C, w), F32)
            hc_s[0:HALO_D, :] = jnp.zeros((HALO_D, w), F32)

        part = _z_parts(x_ref, w_ref, b_ref, z_ref, w)
        c_val = part(0)
        c_gate = part(1)
        cv_s[HALO_C:HALO_C + tm, :] = c_val
        for gi, win in enumerate(POOL_WINDOWS):
            cols = slice(gi * LANE, (gi + 1) * LANE)
            s = cv_s[pl.ds(HALO_C, tm), cols]
            for j in range(1, win):
                s = s + cv_s[pl.ds(HALO_C - j, tm), cols]
            pooled = (s * _pool_inverse(i, tm, seq, win) - c_val[:, cols]).astype(BF16)
            pooled_ref[:, cols] = pooled
            c = _dot(pooled, wp_ref[gi]) * ps_ref[:, cols] * _silu(c_gate[:, cols])
            y_ref[:, cols] = c.astype(BF16)
        cv_s[0:HALO_C, :] = cv_s[tm:tm + HALO_C, :]

        d_h = part(2)
        d_b = part(3)
        hc_s[HALO_D:HALO_D + tm, :] = part(4) * d_h
        q = jnp.zeros((tm, w), F32)
        for k in range(CONV_D_WIDTH):
            q = q + cw_ref[k:k + 1, :] * hc_s[pl.ds(HALO_D - (CONV_D_WIDTH - 1) + k, tm), :]
        hc_s[0:HALO_D, :] = hc_s[tm:tm + HALO_D, :]
        qb = q.astype(BF16)
        q_ref[...] = qb
        y_ref[:, w:2 * w] = (d_b * qb.astype(F32) * _silu(part(5))).astype(BF16)

    row = lambda cols: pl.BlockSpec((tm, cols), lambda i: (i, 0))
    return _call(
        body, name="odd_fwd", grid=(nt,), rider=rider,
        out_shape=(jax.ShapeDtypeStruct((t_len, 6 * w), BF16), jax.ShapeDtypeStruct((t_len, d_model), BF16),
                   jax.ShapeDtypeStruct((t_len, w), BF16), jax.ShapeDtypeStruct((t_len, w), BF16)),
        in_specs=[row(d_model), _const(w_in.shape), _const(b_in.shape), _const(w_pool.shape), _const(pool_scale.shape),
                  _const(conv_w.shape)],
        out_specs=(row(6 * w), row(d_model), row(w), row(w)),
        scratch=[pltpu.VMEM((tm + HALO_C, w), F32), pltpu.VMEM((tm + HALO_D, w), F32)],
        args=[x, w_in, b_in, w_pool, pool_scale, conv_w])


def _post_fwd(y, x, p_all, layer, w_out, b_out, ln_g, ln_b, wg, bg, wp, loss_target, *, tm, rider=None):
    t_len, d_model = x.shape
    d_ple = p_all.shape[-1]
    nt = t_len // tm
    last = loss_target is not None

    def body(*refs):
        y_ref, x_ref, p_ref, wo_ref, bo_ref, g_ref, b_ref, wg_ref, bg_ref, wp_ref = refs[:10]
        rest = refs[10:]
        if last:
            lt_ref, rest = rest[0], rest[1:]
        xn_ref, rh_ref, h_ref, rstd_ref, gate_ref, e_ref = rest[:6]
        r = DEEPNORM_ALPHA * x_ref[...] + _dot(y_ref[...], wo_ref[...]) + bo_ref[...]
        rh, rstd = _norm(r)
        h = rh * g_ref[...] + b_ref[...]
        hb = h.astype(BF16)
        gate = jax.nn.sigmoid(_dot(hb, wg_ref[...]) + bg_ref[...])
        e = _dot(p_ref[...].astype(BF16), wp_ref[...])
        xn = h + gate * e
        xn_ref[...] = xn
        rh_ref[...] = rh.astype(BF16)
        h_ref[...] = hb
        rstd_ref[...] = jnp.broadcast_to(rstd, (tm, LANE))
        gate_ref[...] = gate.astype(BF16)
        e_ref[...] = e.astype(BF16)
        if last:
            dxn_ref, sse_ref = rest[6:]
            diff = xn - lt_ref[...]
            dxn_ref[...] = diff * (1.0 / d_model)

            @pl.when(pl.program_id(0) == 0)
            def _():
                sse_ref[...] = jnp.zeros_like(sse_ref)
            sse_ref[...] += jnp.sum(_sum0(diff * diff), axis=1, keepdims=True)

    row = lambda cols: pl.BlockSpec((tm, cols), lambda i: (i, 0))
    in_specs = [row(d_model), row(d_model), pl.BlockSpec((None, tm, d_ple), lambda i: (layer, i, 0)),
                _const(w_out.shape), _const(b_out.shape), _const(ln_g.shape), _const(ln_b.shape), _const(wg.shape),
                _const(bg.shape), _const(wp.shape)]
    args = [y, x, p_all, w_out, b_out, ln_g, ln_b, wg, bg, wp]
    out_shape = [jax.ShapeDtypeStruct((t_len, d_model), F32), jax.ShapeDtypeStruct((t_len, d_model), BF16),
                 jax.ShapeDtypeStruct((t_len, d_model), BF16), jax.ShapeDtypeStruct((t_len, LANE), F32),
                 jax.ShapeDtypeStruct((t_len, d_model), BF16), jax.ShapeDtypeStruct((t_len, d_model), BF16)]
    out_specs = [row(d_model), row(d_model), row(d_model), row(LANE), row(d_model), row(d_model)]
    if last:
        in_specs.append(row(d_model))
        args.append(loss_target)
        out_shape += [jax.ShapeDtypeStruct((t_len, d_model), F32), jax.ShapeDtypeStruct((SUBLANE, LANE), F32)]
        out_specs += [row(d_model), _const((SUBLANE, LANE))]
    return _call(body, name="post_fwd_last" if last else "post_fwd", grid=(nt,), out_shape=out_shape, in_specs=in_specs,
                 out_specs=out_specs, args=args, rider=rider)


def _post_bwd(dxn, gate, e, rh, rstd, ln_g, wg, w_out, *, tm, rider=None):
    t_len, d_model = dxn.shape
    nt = t_len // tm

    def body(dxn_ref, gate_ref, e_ref, rh_ref, rstd_ref, g_ref, wg_ref, wo_ref, de_ref, dgl_ref, dy_ref, dr_ref, acc_ref):
        @pl.when(pl.program_id(0) == 0)
        def _():
            acc_ref[...] = jnp.zeros_like(acc_ref)

        d = dxn_ref[...]
        gt = gate_ref[...].astype(F32)
        rhat = rh_ref[...].astype(F32)
        de_ref[...] = (d * gt).astype(BF16)
        dgl = d * e_ref[...].astype(F32) * gt * (1.0 - gt)
        dglb = dgl.astype(BF16)
        dgl_ref[...] = dglb
        dh = d + _dot_nt(dglb, wg_ref[...])
        dr = _norm_bwd(dh * g_ref[...], rhat, rstd_ref[:, 0:1])
        dr_ref[...] = dr
        dy_ref[...] = _dot_nt(dr.astype(BF16), wo_ref[...]).astype(BF16)
        acc_ref[0:1, :] += _sum0(dgl)
        acc_ref[1:2, :] += _sum0(dh * rhat)
        acc_ref[2:3, :] += _sum0(dh)
        acc_ref[3:4, :] += _sum0(dr)

    row = lambda cols: pl.BlockSpec((tm, cols), lambda i: (i, 0))
    return _call(
        body, name="post_bwd", grid=(nt,), rider=rider,
        out_shape=(jax.ShapeDtypeStruct((t_len, d_model), BF16), jax.ShapeDtypeStruct((t_len, d_model), BF16),
                   jax.ShapeDtypeStruct((t_len, d_model), BF16), jax.ShapeDtypeStruct((t_len, d_model), F32),
                   jax.ShapeDtypeStruct((SUBLANE, d_model), F32)),
        in_specs=[row(d_model), row(d_model), row(d_model), row(d_model), row(LANE), _const(ln_g.shape), _const(wg.shape),
                  _const(w_out.shape)],
        out_specs=(row(d_model), row(d_model), row(d_model), row(d_model), _const((SUBLANE, d_model))),
        args=[dxn, gate, e, rh, rstd, ln_g, wg, w_out])


def _dz_store(dz_ref, dbin_ref, width):
    def store(k, v):
        cols = slice(k * width, (k + 1) * width)
        vb = v.astype(BF16)
        dz_ref[:, cols] = vb
        dbin_ref[0:1, cols] += _sum0(v)
        return vb
    return store


def _even_bwd(dy, z, a1, sg, dr, w_in, conv_w, ln_a_g, ln_a_b, ln_v_g, ln_v_b, wcat_t, *, seq, tm, rider=None):
    t_len, d_model = dr.shape
    w = d_model // 2
    nt, tps, nb = t_len // tm, seq // tm, tm // LANE
    n_heads = 2 * (w // LANE)

    def body(dy_ref, z_ref, a1_ref, sg_ref, dr_ref, w_ref, cw_ref, lag_ref, lab_ref, lvg_ref, lvb_ref, wct_ref,
             dz_ref, dx_ref, dcw_ref, vec_ref, dbin_ref, dws_ref, dbs_ref, da1_s, a0_s, da0_s, cw_acc, shift_s):
        i = pl.program_id(0)
        tile = nt - 1 - i

        @pl.when(i == 0)
        def _():
            vec_ref[...] = jnp.zeros_like(vec_ref)
            dbin_ref[...] = jnp.zeros_like(dbin_ref)
            dws_ref[...] = jnp.zeros_like(dws_ref)
            dbs_ref[...] = jnp.zeros_like(dbs_ref)
            cw_acc[...] = jnp.zeros_like(cw_acc)

        @pl.when((tile + 1) % tps == 0)
        def _():
            da1_s[tm:tm + HALO_A, :] = jnp.zeros((HALO_A, w), F32)

        zp = lambda k: z_ref[:, k * w:(k + 1) * w].astype(F32)
        store = _dz_store(dz_ref, dbin_ref, w)

        da = dy_ref[:, 0:w].astype(F32)
        a_val, a_glu, a_gate = zp(0), zp(1), zp(2)
        s_glu = jax.nn.sigmoid(a_glu)
        a0_s[...] = a_val * s_glu
        ah, rstd_a = _norm(a1_ref[...])
        silu_a2, dsilu_a2 = _silu_grad(ah * lag_ref[...] + lab_ref[...])
        silu_ag, dsilu_ag = _silu_grad(a_gate)
        dzb2 = store(2, da * silu_a2 * dsilu_ag)
        da2 = da * silu_ag * dsilu_a2
        vec_ref[1:2, :] += _sum0(da2 * ah)
        vec_ref[2:3, :] += _sum0(da2)
        da1 = _norm_bwd(da2 * lag_ref[...], ah, rstd_a)
        vec_ref[0:1, :] += _sum0(da1)
        da1_s[0:tm, :] = da1
        window = _shifted_windows(da1_s, shift_s, tm + HALO_A)
        for r in range(0, tm, ROW_CHUNK):
            a0c = a0_s[r:r + ROW_CHUNK, :]
            acc = jnp.zeros((ROW_CHUNK, w), F32)
            for k in range(CONV_A_WIDTH):
                d = window(r + (CONV_A_WIDTH - 1) - k)
                acc = acc + cw_ref[k:k + 1, :] * d
                pw = a0c * d
                p8 = pw[0:SUBLANE]
                for q in range(1, ROW_CHUNK // SUBLANE):
                    p8 = p8 + pw[q * SUBLANE:(q + 1) * SUBLANE]
                cw_acc[k * SUBLANE:(k + 1) * SUBLANE, :] += p8
            da0_s[r:r + ROW_CHUNK, :] = acc
        da1_s[tm:tm + HALO_A, :] = da1_s[0:HALO_A, :]
        da0 = da0_s[...]
        dzb0 = store(0, da0 * s_glu)
        dzb1 = store(1, da0 * a_val * s_glu * (1.0 - s_glu))

        dg = dy_ref[:, w:2 * w].astype(F32)
        u, v, gg = zp(3), zp(4), zp(5)
        sgv = sg_ref[...].astype(F32)
        gelu_u, dgelu_u = _gelu_grad(u)
        silu_gg, dsilu_gg = _silu_grad(gg)
        dzb5 = store(5, dg * gelu_u * sgv * dsilu_gg)
        t1 = dg * silu_gg
        dzb3 = store(3, t1 * sgv * dgelu_u)
        dsg = t1 * gelu_u
        gelu_v, dgelu_v = _gelu_grad(v)
        vh, rstd_v = _norm(gelu_v)
        v2 = vh * lvg_ref[...] + lvb_ref[...]
        low = _head_low_mask(nb * LANE)
        for j in range(w // LANE):
            dt = _blocks_to_lanes(dsg, j, nb)
            d_lo = jnp.where(low, dt, 0.0).astype(BF16)
            d_hi = jnp.where(low, 0.0, dt).astype(BF16)
            v2t = _blocks_to_lanes(v2, j, nb).astype(BF16)
            dv2t = _dot(wct_ref[j], jnp.concatenate([d_lo, d_hi], axis=0))
            for n in range(nb):
                da0_s[n * LANE:(n + 1) * LANE, j * LANE:(j + 1) * LANE] = dv2t[:, n * LANE:(n + 1) * LANE]
            dws_ref[2 * j] += _dot_nt(d_lo, v2t)
            dws_ref[2 * j + 1] += _dot_nt(d_hi, v2t)
            bsum = dt[:, 0:LANE]
            for n in range(1, nb):
                bsum = bsum + dt[:, n * LANE:(n + 1) * LANE]
            dbs_ref[:, j * LANE:(j + 1) * LANE] += bsum
        dv2 = da0_s[...]
        vec_ref[3:4, :] += _sum0(dv2 * vh)
        vec_ref[4:5, :] += _sum0(dv2)
        dzb4 = store(4, _norm_bwd(dv2 * lvg_ref[...], vh, rstd_v) * dgelu_v)

        dx = DEEPNORM_ALPHA * dr_ref[...]
        for k, dzb in enumerate((dzb0, dzb1, dzb2, dzb3, dzb4, dzb5)):
            dx = dx + _dot_nt(dzb, w_ref[:, k * w:(k + 1) * w])
        dx_ref[...] = dx

        @pl.when(i == nt - 1)
        def _():
            for k in range(CONV_A_WIDTH):
                dcw_ref[k:k + 1, :] = _sum0(cw_acc[k * SUBLANE:(k + 1) * SUBLANE, :])
            keep = (lax.broadcasted_iota(jnp.int32, (LANE, LANE), 0) >= lax.broadcasted_iota(jnp.int32, (LANE, LANE), 1))
            for hd in range(n_heads):
                dws_ref[hd] = jnp.where(keep, dws_ref[hd], 0.0)

    rev = lambda cols: pl.BlockSpec((tm, cols), lambda i: (nt - 1 - i, 0))
    return _call(
        body, name="even_bwd", grid=(nt,), rider=rider,
        out_shape=(jax.ShapeDtypeStruct((t_len, 6 * w), BF16), jax.ShapeDtypeStruct((t_len, d_model), F32),
                   jax.ShapeDtypeStruct((CONV_A_WIDTH, w), F32), jax.ShapeDtypeStruct((SUBLANE, w), F32),
                   jax.ShapeDtypeStruct((SUBLANE, 6 * w), F32), jax.ShapeDtypeStruct((n_heads, LANE, LANE), F32),
                   jax.ShapeDtypeStruct((LANE, w), F32)),
        in_specs=[rev(d_model), rev(6 * w), rev(w), rev(w), rev(d_model), _const(w_in.shape), _const(conv_w.shape),
                  _const(ln_a_g.shape), _const(ln_a_b.shape), _const(ln_v_g.shape), _const(ln_v_b.shape), _const(wcat_t.shape)],
        out_specs=(rev(6 * w), rev(d_model), _const((CONV_A_WIDTH, w)), _const((SUBLANE, w)), _const((SUBLANE, 6 * w)),
                   _const((n_heads, LANE, LANE)), _const((LANE, w))),
        scratch=[pltpu.VMEM((tm + HALO_A, w), F32), pltpu.VMEM((tm, w), F32), pltpu.VMEM((tm, w), F32),
                 pltpu.VMEM((CONV_A_WIDTH * SUBLANE, w), F32), pltpu.VMEM((SUBLANE - 1, tm + HALO_A, w), F32)],
        args=[dy, z, a1, sg, dr, w_in, conv_w, ln_a_g, ln_a_b, ln_v_g, ln_v_b, wcat_t])


def _odd_bwd(dy, z, pooled, q, dr, w_in, w_pool, pool_scale, conv_w, *, seq, tm, rider=None):
    t_len, d_model = dr.shape
    w = d_model // 2
    nt, tps = t_len // tm, seq // tm
    n_groups = len(POOL_WINDOWS)

    def body(dy_ref, z_ref, pooled_ref, q_ref, dr_ref, w_ref, wp_ref, ps_ref, cw_ref,
             dz_ref, dx_ref, dwp_ref, vec_ref, dbin_ref, dm_s, dq_s):
        i = pl.program_id(0)
        tile = nt - 1 - i

        @pl.when(i == 0)
        def _():
            dwp_ref[...] = jnp.zeros_like(dwp_ref)
            vec_ref[...] = jnp.zeros_like(vec_ref)
            dbin_ref[...] = jnp.zeros_like(dbin_ref)

        @pl.when((tile + 1) % tps == 0)
        def _():
            dm_s[tm:tm + HALO_C, :] = jnp.zeros((HALO_C, w), F32)
            dq_s[tm:tm + HALO_D, :] = jnp.zeros((HALO_D, w), F32)

        zp = lambda k: z_ref[:, k * w:(k + 1) * w].astype(F32)
        store = _dz_store(dz_ref, dbin_ref, w)

        dc = dy_ref[:, 0:w].astype(F32)
        c_gate = zp(1)
        silu_c, dsilu_c = _silu_grad(c_gate)
        dcs = dc * silu_c
        dvg_parts, dcg_parts = [], []
        for gi, win in enumerate(POOL_WINDOWS):
            cols = slice(gi * LANE, (gi + 1) * LANE)
            pooled_g = pooled_ref[:, cols]
            wp = wp_ref[gi]
            cpre = _dot(pooled_g, wp)
            scale = ps_ref[:, cols]
            vec_ref[0:1, cols] += _sum0(dcs[:, cols] * cpre)
            dcg_parts.append(dc[:, cols] * cpre * scale * dsilu_c[:, cols])
            dcp = (dcs[:, cols] * scale).astype(BF16)
            dwp_ref[gi] += _dot_tn(pooled_g, dcp)
            dpooled = _dot_nt(dcp, wp)
            dm_s[0:tm, cols] = dpooled * _pool_inverse(tile, tm, seq, win)
            s = dm_s[pl.ds(0, tm), cols]
            for j in range(1, win):
                s = s + dm_s[pl.ds(j, tm), cols]
            dvg_parts.append(s - dpooled)
        dm_s[tm:tm + HALO_C, :] = dm_s[0:HALO_C, :]
        dzb0 = store(0, jnp.concatenate(dvg_parts, axis=1))
        dzb1 = store(1, jnp.concatenate(dcg_parts, axis=1))

        dd = dy_ref[:, w:2 * w].astype(F32)
        d_h, d_b, d_c, d_gate = zp(2), zp(3), zp(4), zp(5)
        qv = q_ref[...].astype(F32)
        silu_d, dsilu_d = _silu_grad(d_gate)
        dzb5 = store(5, dd * d_b * qv * dsilu_d)
        dzb3 = store(3, dd * qv * silu_d)
        dq_s[0:tm, :] = dd * d_b * silu_d
        hc = d_c * d_h
        dhc = jnp.zeros((tm, w), F32)
        for k in range(CONV_D_WIDTH):
            d = dq_s[pl.ds((CONV_D_WIDTH - 1) - k, tm), :]
            dhc = dhc + cw_ref[k:k + 1, :] * d
            vec_ref[1 + k:2 + k, :] += _sum0(hc * d)
        dq_s[tm:tm + HALO_D, :] = dq_s[0:HALO_D, :]
        dzb2 = store(2, dhc * d_c)
        dzb4 = store(4, dhc * d_h)

        dx = DEEPNORM_ALPHA * dr_ref[...]
        for k, dzb in enumerate((dzb0, dzb1, dzb2, dzb3, dzb4, dzb5)):
            dx = dx + _dot_nt(dzb, w_ref[:, k * w:(k + 1) * w])
        dx_ref[...] = dx

    rev = lambda cols: pl.BlockSpec((tm, cols), lambda i: (nt - 1 - i, 0))
    return _call(
        body, name="odd_bwd", grid=(nt,), rider=rider,
        out_shape=(jax.ShapeDtypeStruct((t_len, 6 * w), BF16), jax.ShapeDtypeStruct((t_len, d_model), F32),
                   jax.ShapeDtypeStruct((n_groups, LANE, LANE), F32), jax.ShapeDtypeStruct((SUBLANE, w), F32),
                   jax.ShapeDtypeStruct((SUBLANE, 6 * w), F32)),
        in_specs=[rev(d_model), rev(6 * w), rev(w), rev(w), rev(d_model), _const(w_in.shape), _const(w_pool.shape),
                  _const(pool_scale.shape), _const(conv_w.shape)],
        out_specs=(rev(6 * w), rev(d_model), _const((n_groups, LANE, LANE)), _const((SUBLANE, w)), _const((SUBLANE, 6 * w))),
        scratch=[pltpu.VMEM((tm + HALO_C, w), F32), pltpu.VMEM((tm + HALO_D, w), F32)],
        args=[dy, z, pooled, q, dr, w_in, w_pool, pool_scale, conv_w])


def _weight_grad(name, a, b, a_layer=None, bn=None, a_cols=None, rider=None):
    if a_layer is None:
        t_len, m = a.shape
        col = 0
        if a_cols is not None:
            col, m = a_cols
        a_spec = lambda tk: pl.BlockSpec((tk, m), lambda n, k: (k, col))
    else:
        _, t_len, m = a.shape
        a_spec = lambda tk: pl.BlockSpec((None, tk, m), lambda n, k: (a_layer, k, 0))
    n_cols = b.shape[1]
    bn = n_cols if bn is None else bn
    tk = min(t_len, 512)
    n_k = t_len // tk

    def body(a_ref, b_ref, o_ref, acc):
        k = pl.program_id(1)

        @pl.when(k == 0)
        def _():
            acc[...] = jnp.zeros_like(acc)
        acc[...] += _dot_tn(a_ref[...].astype(BF16), b_ref[...].astype(BF16))

        @pl.when(k == n_k - 1)
        def _():
            o_ref[...] = acc[...].astype(BF16)

    outs = _call(
        body, name=name, grid=(n_cols // bn, n_k), out_shape=[jax.ShapeDtypeStruct((m, n_cols), BF16)],
        in_specs=[a_spec(tk), pl.BlockSpec((tk, bn), lambda n, k: (k, n))],
        out_specs=[pl.BlockSpec((m, bn), lambda n, k: (0, n))], scratch=[pltpu.VMEM((m, bn), F32)],
        args=[a, b], rider=rider)
    return outs[0] if rider is None else outs


def _adamw_reduce(name, parts, w, m, v, rows_per_block):
    n_rows, n_cols = w.shape
    br = rows_per_block
    n_parts = parts.shape[0]

    def body(p_ref, w_ref, m_ref, v_ref, g_ref, d_ref, nm_ref, nv_ref):
        g = p_ref[0].astype(F32)
        for k in range(1, n_parts):
            g = g + p_ref[k].astype(F32)
        nm = ADAM_B1 * m_ref[...] + (1.0 - ADAM_B1) * g
        nv = ADAM_B2 * v_ref[...] + (1.0 - ADAM_B2) * (g * g)
        m_hat = nm / (1.0 - ADAM_B1 ** ADAM_STEP)
        v_hat = nv / (1.0 - ADAM_B2 ** ADAM_STEP)
        g_ref[...] = g
        d_ref[...] = -ADAM_LR * (m_hat / (jnp.sqrt(v_hat) + ADAM_EPS) + ADAM_WD * w_ref[...])
        nm_ref[...] = nm
        nv_ref[...] = nv

    blk = pl.BlockSpec((br, n_cols), lambda i: (i, 0))
    shp = jax.ShapeDtypeStruct((n_rows, n_cols), F32)
    return pl.pallas_call(
        body, name=name, grid=(n_rows // br,), out_shape=(shp, shp, shp, shp),
        in_specs=[pl.BlockSpec((n_parts, br, n_cols), lambda i: (0, i, 0)), blk, blk, blk], out_specs=(blk, blk, blk, blk),
        compiler_params=_params(1),
    )(parts, w, m, v)


def _pack_rows(flat_parts, pad_to=None, width=LANE):
    flat = jnp.concatenate([a.reshape(-1) for a in flat_parts])
    if pad_to is not None and pad_to > flat.shape[0]:
        flat = jnp.concatenate([flat, jnp.zeros((pad_to - flat.shape[0],), flat.dtype)])
    return flat.reshape(-1, width)


def _unpack_rows(packed, shapes):
    flat = packed.reshape(-1)
    out, off = [], 0
    for s in shapes:
        n = math.prod(s)
        out.append(flat[off:off + n].reshape(s))
        off += n
    return out


def _to_dest_major(full):
    lead, last = full.shape[:-1], full.shape[-1]
    t = full.reshape(lead + (N_DEV, last // N_DEV))
    return jnp.moveaxis(t, -2, 0).reshape(N_DEV, -1)


def _from_source_major(blocks, shard_shape):
    t = blocks.reshape((N_DEV,) + tuple(shard_shape))
    t = jnp.moveaxis(t, 0, -2)
    return t.reshape(tuple(shard_shape[:-1]) + (N_DEV * shard_shape[-1],))


def _block_rows(n_rows, n_cols, target_elems=96 * 1024):
    best = None
    for br in range(SUBLANE, n_rows + 1, SUBLANE):
        if n_rows % br == 0 and br * n_cols <= target_elems:
            best = br
    return n_rows if best is None else best


def kernel(x, p, w_in_e, b_in_e, conv_a_w, conv_a_b, ln_a_g, ln_a_b, ln_v_g, ln_v_b, w_s, b_s, w_out_e, b_out_e, w_in_o, b_in_o, w_pool, pool_scale, conv_d_w, w_out_o, b_out_o, ln_g, ln_b, w_ple, w_ple_gate, b_ple_gate, loss_target, m_w_in_e, m_b_in_e, m_conv_a_w, m_conv_a_b, m_ln_a_g, m_ln_a_b, m_ln_v_g, m_ln_v_b, m_w_s, m_b_s, m_w_out_e, m_b_out_e, m_w_in_o, m_b_in_o, m_w_pool, m_pool_scale, m_conv_d_w, m_w_out_o, m_b_out_o, m_ln_g, m_ln_b, m_w_ple, m_w_ple_gate, m_b_ple_gate, v_w_in_e, v_b_in_e, v_conv_a_w, v_conv_a_b, v_ln_a_g, v_ln_a_b, v_ln_v_g, v_ln_v_b, v_w_s, v_b_s, v_w_out_e, v_b_out_e, v_w_in_o, v_b_in_o, v_w_pool, v_pool_scale, v_conv_d_w, v_w_out_o, v_b_out_o, v_ln_g, v_ln_b, v_w_ple, v_w_ple_gate, v_b_ple_gate):
    weights = dict(w_in_e=w_in_e, b_in_e=b_in_e, conv_a_w=conv_a_w, conv_a_b=conv_a_b, ln_a_g=ln_a_g, ln_a_b=ln_a_b,
                   ln_v_g=ln_v_g, ln_v_b=ln_v_b, w_s=w_s, b_s=b_s, w_out_e=w_out_e, b_out_e=b_out_e, w_in_o=w_in_o,
                   b_in_o=b_in_o, w_pool=w_pool, pool_scale=pool_scale, conv_d_w=conv_d_w, w_out_o=w_out_o,
                   b_out_o=b_out_o, ln_g=ln_g, ln_b=ln_b, w_ple=w_ple, w_ple_gate=w_ple_gate, b_ple_gate=b_ple_gate)
    mom_m = dict(w_in_e=m_w_in_e, b_in_e=m_b_in_e, conv_a_w=m_conv_a_w, conv_a_b=m_conv_a_b, ln_a_g=m_ln_a_g,
                 ln_a_b=m_ln_a_b, ln_v_g=m_ln_v_g, ln_v_b=m_ln_v_b, w_s=m_w_s, b_s=m_b_s, w_out_e=m_w_out_e,
                 b_out_e=m_b_out_e, w_in_o=m_w_in_o, b_in_o=m_b_in_o, w_pool=m_w_pool, pool_scale=m_pool_scale,
                 conv_d_w=m_conv_d_w, w_out_o=m_w_out_o, b_out_o=m_b_out_o, ln_g=m_ln_g, ln_b=m_ln_b, w_ple=m_w_ple,
                 w_ple_gate=m_w_ple_gate, b_ple_gate=m_b_ple_gate)
    mom_v = dict(w_in_e=v_w_in_e, b_in_e=v_b_in_e, conv_a_w=v_conv_a_w, conv_a_b=v_conv_a_b, ln_a_g=v_ln_a_g,
                 ln_a_b=v_ln_a_b, ln_v_g=v_ln_v_g, ln_v_b=v_ln_v_b, w_s=v_w_s, b_s=v_b_s, w_out_e=v_w_out_e,
                 b_out_e=v_b_out_e, w_in_o=v_w_in_o, b_in_o=v_b_in_o, w_pool=v_w_pool, pool_scale=v_pool_scale,
                 conv_d_w=v_conv_d_w, w_out_o=v_w_out_o, b_out_o=v_b_out_o, ln_g=v_ln_g, ln_b=v_ln_b, w_ple=v_w_ple,
                 w_ple_gate=v_w_ple_gate, b_ple_gate=v_b_ple_gate)
    names = tuple(weights)

    batch, seq, d_model = x.shape
    t_len = batch * seq
    w = d_model // 2
    n_even = w_in_e.shape[0]
    n_odd = w_in_o.shape[0]
    depth = ln_g.shape[0]
    d_ple = p.shape[-1]
    n_heads = w_s.shape[1]
    tm = 512 if seq % 512 == 0 and seq >= 1024 else seq // 2
    in_cols = w_in_e.shape[-1]
    out_rows = w_out_e.shape[1]
    ple_cols = w_ple.shape[-1]
    gate_rows = w_ple_gate.shape[1]

    sh_shapes = [weights[n].shape for n in SH_NAMES]
    sh_len = sum(math.prod(s) for s in sh_shapes)
    sh_pad = -(-sh_len // (SUBLANE * LANE)) * (SUBLANE * LANE)
    sh_rows = sh_pad // LANE
    sds = jax.ShapeDtypeStruct
    w_in16 = (w_in_e.astype(BF16), w_in_o.astype(BF16))
    w_out16 = (w_out_e.astype(BF16), w_out_o.astype(BF16))
    w_ple16, w_gate16 = w_ple.astype(BF16), w_ple_gate.astype(BF16)

    kinds = ("in", "out", "ple", "gate")

    def weight_entries(i, which=kinds):
        j, par = i // 2, i % 2
        all_four = {"in": (w_in16[par][j], _whole_view, sds((d_model, N_DEV * in_cols), BF16), _axis_view(1, in_cols)),
                    "out": (w_out16[par][j], _whole_view, sds((N_DEV * out_rows, d_model), BF16), _axis_view(0, out_rows)),
                    "ple": (w_ple16[i], _whole_view, sds((d_ple, N_DEV * ple_cols), BF16), _axis_view(1, ple_cols)),
                    "gate": (w_gate16[i], _whole_view, sds((N_DEV * gate_rows, d_model), BF16), _axis_view(0, gate_rows))}
        return [((i, k), all_four[k]) for k in which]

    fwd_riders = {("mixer", 0): weight_entries(0, kinds[1:]) + weight_entries(1)}
    for i in range(1, depth - 1):
        if i % 2:
            fwd_riders[("mixer", i)] = weight_entries(i + 1, kinds[:1])
            fwd_riders[("post", i)] = weight_entries(i + 1, kinds[1:])
        else:
            fwd_riders[("mixer", i)] = weight_entries(i + 1)
    layer_w = {}

    def carried(where):
        tagged = fwd_riders.get(where)
        if tagged is None:
            return None, lambda landed: None
        return _Rider([e for _, e in tagged]), lambda landed: layer_w.update(zip([t for t, _ in tagged], landed))

    first = _exchange("gather_first", [e for _, e in weight_entries(0, kinds[:1])] + [
        (_pack_rows([weights[n] for n in SH_NAMES], sh_pad), _whole_view, sds((N_DEV, sh_rows, LANE), F32), _slot_view())])
    layer_w[(0, "in")] = first[0]
    sh_flat = first[1].reshape(N_DEV, sh_pad)
    full_small, off = {}, 0
    for n, s in zip(SH_NAMES, sh_shapes):
        size = math.prod(s)
        full_small[n] = _from_source_major(sh_flat[:, off:off + size], s)
        off += size

    tril = jnp.tril(jnp.ones((LANE, LANE), dtype=bool))
    ws_m = jnp.where(tril[None, None], w_s, 0.0)
    pair = lambda t: jnp.concatenate([t[:, 0::2], t[:, 1::2]], axis=-1).astype(BF16)
    wcat = pair(ws_m)
    wcat_t = pair(jnp.swapaxes(ws_m, -1, -2))
    bs_full = jnp.repeat(jnp.swapaxes(b_s, -1, -2), w // n_heads, axis=-1)
    row2 = lambda a, j: a[j][None, :]

    x2 = x.reshape(t_len, d_model)
    p3 = p.reshape(depth, t_len, d_ple)
    lt2 = loss_target.reshape(t_len, d_model)

    xs, saved = [x2], []
    dxn = sse = None
    for i in range(depth):
        j = i // 2
        last = i == depth - 1
        rider, file_weights = carried(("mixer", i))
        if i % 2 == 0:
            outs = _even_fwd(xs[i], layer_w[(i, "in")], row2(b_in_e, j), full_small["conv_a_w"][j], row2(conv_a_b, j),
                             row2(ln_a_g, j), row2(ln_a_b, j), row2(ln_v_g, j), row2(ln_v_b, j), wcat[j], bs_full[j],
                             seq=seq, tm=tm, rider=rider)
            b_out = row2(b_out_e, j)
        else:
            outs = _odd_fwd(xs[i], layer_w[(i, "in")], row2(full_small["b_in_o"], j), w_pool[j].astype(BF16),
                            row2(full_small["pool_scale"], j), full_small["conv_d_w"][j], seq=seq, tm=tm, rider=rider)
            b_out = row2(full_small["b_out_o"], j)
        z, y, s1, s2 = outs[:4]
        file_weights(outs[4:])
        rider, file_weights = carried(("post", i))
        outs = _post_fwd(y, xs[i], p3, i, layer_w[(i, "out")], b_out, row2(ln_g, i), row2(ln_b, i), layer_w[(i, "gate")],
                         row2(b_ple_gate, i), layer_w[(i, "ple")], lt2 if last else None, tm=tm, rider=rider)
        xn, rh, h, rstd, gate, e = outs[:6]
        if last:
            dxn, sse = outs[6:8]
        file_weights(outs[8 if last else 6:])
        xs.append(xn)
        saved.append(dict(z=z, y=y, s1=s1, s2=s2, rh=rh, h=h, rstd=rstd, gate=gate, e=e))

    loss = lax.psum((0.5 / d_model) * sse[0, 0], ("x", "y", "c"))

    recv = {"w_in_e": None, "w_in_o": None, "w_out_e": None, "w_out_o": None, "w_ple": None, "w_ple_gate": None}

    def grad_entry(i, kind, g, half=None):
        j, par = i // 2, i % 2
        sfx = "_o" if par else "_e"
        name, src_view, slot = {"in": ("w_in" + sfx, _axis_view(1, in_cols), j), "out": ("w_out" + sfx, _axis_view(0, out_rows), j),
                                "ple": ("w_ple", _axis_view(1, ple_cols), i), "gate": ("w_ple_gate", _axis_view(0, gate_rows), i)}[kind]
        dst = recv[name] if recv[name] is not None else sds((N_DEV,) + weights[name].shape, BF16)
        if half is None:
            dst_view = _slot_view(slot)
        else:
            dst_view = lambda ref, d: ref.at[d, slot, pl.ds(half[0] * half[1], half[1]), :]
        return name, (g, src_view, dst, dst_view)

    def ride(tagged):
        if not tagged:
            return None, lambda landed: None
        return _Rider([e for _, e in tagged]), lambda landed: recv.update(zip([n for n, _ in tagged], landed))

    small = {n: [None] * weights[n].shape[0] for n in REP_NAMES + SH_NAMES}
    grads = {}
    for i in reversed(range(depth)):
        j, par = i // 2, i % 2
        sv = saved[i]
        w_in = layer_w[(i, "in")]
        split = par == 1 and (i + 1, "in") in grads
        rider, file_landed = ride([grad_entry(i + 1, k, grads.pop((i + 1, k))) for k in kinds[1:]] if split else [])
        outs = _post_bwd(dxn, sv["gate"], sv["e"], sv["rh"], sv["rstd"], row2(ln_g, i), layer_w[(i, "gate")],
                         layer_w[(i, "out")], tm=tm, rider=rider)
        de, dgl, dy, dr, acc = outs[:5]
        file_landed(outs[5:])
        small["b_ple_gate"][i], small["ln_g"][i], small["ln_b"][i] = acc[0], acc[1], acc[2]
        small["b_out_o" if par else "b_out_e"][j] = acc[3]
        rider, file_landed = ride([grad_entry(i + 1, k, grads.pop((i + 1, k))) for k in kinds if (i + 1, k) in grads])
        if par == 0:
            outs = _even_bwd(dy, sv["z"], sv["s1"], sv["s2"], dr, w_in, full_small["conv_a_w"][j],
                             row2(ln_a_g, j), row2(ln_a_b, j), row2(ln_v_g, j), row2(ln_v_b, j), wcat_t[j], seq=seq, tm=tm,
                             rider=rider)
            dz, dx, dcw, vec, dbin, dws, dbs = outs[:7]
            file_landed(outs[7:])
            small["conv_a_w"][j], small["conv_a_b"][j] = dcw, vec[0]
            small["ln_a_g"][j], small["ln_a_b"][j], small["ln_v_g"][j], small["ln_v_b"][j] = vec[1], vec[2], vec[3], vec[4]
            small["b_in_e"][j], small["w_s"][j] = dbin[0], dws
            small["b_s"][j] = _head_sums(dbs, n_heads)
        else:
            outs = _odd_bwd(dy, sv["z"], sv["s1"], sv["s2"], dr, w_in, w_pool[j].astype(BF16),
                            row2(full_small["pool_scale"], j), full_small["conv_d_w"][j], seq=seq, tm=tm, rider=rider)
            dz, dx, dwp, vec, dbin = outs[:5]
            file_landed(outs[5:])
            small["w_pool"][j], small["pool_scale"][j], small["conv_d_w"][j] = dwp, vec[0], vec[1:1 + CONV_D_WIDTH]
            small["b_in_o"][j] = dbin[0]
        tag = f"_l{i}"
        grads[(i, "out")] = _weight_grad("dw_out" + tag, sv["y"], dr)
        grads[(i, "ple")] = _weight_grad("dw_ple" + tag, p3, de, a_layer=i)
        grads[(i, "gate")] = _weight_grad("dw_gate" + tag, sv["h"], dgl)
        if i > 0:
            grads[(i, "in")] = _weight_grad("dw_in" + tag, xs[i], dz, bn=in_cols * N_DEV // 2)
        dxn = dx
    grad_x = dxn.reshape(batch, seq, d_model)

    half_rows = d_model // 2
    rider, file_landed = ride([grad_entry(0, k, grads.pop((0, k))) for k in kinds[1:]])
    outs = _weight_grad("dw_in_l0_top", xs[0], dz, a_cols=(0, half_rows), rider=rider)
    file_landed(outs[1:])
    rider, file_landed = ride([grad_entry(0, "in", outs[0], half=(0, half_rows))])
    outs = _weight_grad("dw_in_l0_bottom", xs[0], dz, a_cols=(1, half_rows), rider=rider)
    file_landed(outs[1:])

    small_full = {n: jnp.stack(small[n]) for n in small}
    sh_part = jnp.concatenate([_to_dest_major(small_full[n]) for n in SH_NAMES], axis=1)
    sh_part = jnp.concatenate([sh_part, jnp.zeros((N_DEV, sh_pad - sh_len), F32)], axis=1).reshape(N_DEV, sh_rows, LANE)
    name, entry = grad_entry(0, "in", outs[0], half=(1, half_rows))
    landed = _exchange("exchange_last", [entry, (sh_part, _slot_view(), sds((N_DEV, sh_rows, LANE), F32), _slot_view())])
    recv[name] = landed[0]

    rep_width = 4 * LANE
    rep_len = sum(math.prod(weights[n].shape) for n in REP_NAMES)
    rep_block = N_DEV * SUBLANE
    rep_pad = -(-rep_len // (rep_block * rep_width)) * (rep_block * rep_width)
    rep_sum = _allreduce_rows("allreduce_replicated", _pack_rows([small_full[n] for n in REP_NAMES], rep_pad, rep_width))

    results = {}
    for n, parts in recv.items():
        shp = weights[n].shape
        rows, cols = math.prod(shp[:-1]), shp[-1]
        two = lambda a: a.reshape(rows, cols)
        outs = _adamw_reduce("adamw_" + n, parts.reshape(N_DEV, rows, cols), two(weights[n]), two(mom_m[n]), two(mom_v[n]),
                             _block_rows(rows, cols))
        results[n] = [o.reshape(shp) for o in outs]
    pack_sh = lambda d: _pack_rows([d[n] for n in SH_NAMES], sh_pad)
    outs = _adamw_reduce("adamw_small_sharded", landed[1], pack_sh(weights), pack_sh(mom_m), pack_sh(mom_v), sh_rows)
    for n, *vals in zip(SH_NAMES, *[_unpack_rows(o, sh_shapes) for o in outs]):
        results[n] = vals
    pack_rep = lambda d: _pack_rows([d[n] for n in REP_NAMES], rep_pad, rep_width)
    rep_shapes = [weights[n].shape for n in REP_NAMES]
    outs = _adamw_reduce("adamw_replicated", rep_sum[None], pack_rep(weights), pack_rep(mom_m), pack_rep(mom_v), rep_block)
    for n, *vals in zip(REP_NAMES, *[_unpack_rows(o, rep_shapes) for o in outs]):
        results[n] = vals

    return (loss, grad_x, *[results[n][0] for n in names], *[results[n][1] for n in names],
            *[results[n][2] for n in names], *[results[n][3] for n in names])


def _head_sums(dbs, n_heads):
    t, width = dbs.shape
    return jnp.sum(dbs.reshape(t, n_heads, width // n_heads), axis=-1).T
```

```python
import functools
import math

import jax
import jax.numpy as jnp
from jax import lax
from jax.experimental import pallas as pl
from jax.experimental.pallas import tpu as pltpu

F32 = jnp.float32
BF16 = jnp.bfloat16

N_DEV = 8
DEPTH = 4
LN_EPS = 1e-5
DEEPNORM_ALPHA = (2.0 * DEPTH) ** 0.25
POOL_WINDOWS = (2, 4, 8, 16)
CONV_A_WIDTH = 31
CONV_D_WIDTH = 3
GELU_C = math.sqrt(2.0 / math.pi)
GELU_K = 0.044715

ADAM_LR = 0.001
ADAM_B1 = 0.9
ADAM_B2 = 0.999
ADAM_EPS = 1e-08
ADAM_WD = 0.01
ADAM_STEP = 10

LANE = 128
SUBLANE = 8
HALO_A = 32
HALO_C = 16
HALO_D = 8
ROW_CHUNK = 32
VMEM_LIMIT = 56 * 2**20

ANY = pl.BlockSpec(memory_space=pl.ANY)
MESH = pl.DeviceIdType.MESH

REP_NAMES = ("b_in_e", "conv_a_b", "ln_a_g", "ln_a_b", "ln_v_g", "ln_v_b", "w_s", "b_s", "b_out_e", "w_pool", "ln_g",
             "ln_b", "b_ple_gate")
SH_NAMES = ("conv_a_w", "conv_d_w", "pool_scale", "b_in_o", "b_out_o")


def _params(n_grid_axes):
    return pltpu.CompilerParams(dimension_semantics=("arbitrary",) * n_grid_axes, vmem_limit_bytes=VMEM_LIMIT)


def _const(shape):
    nd = len(shape)
    return pl.BlockSpec(shape, lambda *_: (0,) * nd)


def _dot(a, b):
    return jnp.dot(a, b, preferred_element_type=F32)


def _dot_nt(a, b):
    return lax.dot_general(a, b, (((1,), (1,)), ((), ())), preferred_element_type=F32)


def _dot_tn(a, b):
    return lax.dot_general(a, b, (((0,), (0,)), ((), ())), preferred_element_type=F32)


def _silu(x):
    return x * jax.nn.sigmoid(x)


def _silu_grad(x):
    s = jax.nn.sigmoid(x)
    return x * s, s * (1.0 + x * (1.0 - s))


def _gelu(x):
    return 0.5 * x * (1.0 + jnp.tanh(GELU_C * (x + GELU_K * x * x * x)))


def _gelu_grad(x):
    x2 = x * x
    t = jnp.tanh(GELU_C * x * (1.0 + GELU_K * x2))
    return 0.5 * x * (1.0 + t), 0.5 * (1.0 + t) + 0.5 * x * (1.0 - t * t) * GELU_C * (1.0 + 3.0 * GELU_K * x2)


def _norm(v):
    mu = jnp.mean(v, axis=-1, keepdims=True)
    d = v - mu
    var = jnp.mean(d * d, axis=-1, keepdims=True)
    rstd = lax.rsqrt(var + LN_EPS)
    return d * rstd, rstd


def _norm_bwd(dxh, xh, rstd):
    return rstd * (dxh - jnp.mean(dxh, axis=-1, keepdims=True) - xh * jnp.mean(dxh * xh, axis=-1, keepdims=True))


def _sum0(v):
    return jnp.sum(v, axis=0, keepdims=True)


def _head_low_mask(n_cols):
    lane = lax.broadcasted_iota(jnp.int32, (LANE, n_cols), 1)
    return (lane & (LANE - 1)) < (LANE // 2)


def _blocks_to_lanes(v, j, nb):
    return jnp.concatenate([v[n * LANE:(n + 1) * LANE, j * LANE:(j + 1) * LANE] for n in range(nb)], axis=1)


def _axis_view(axis, size):
    def view(ref, d):
        idx = [slice(None)] * len(ref.shape)
        idx[axis] = pl.ds(pl.multiple_of(d * size, size), size)
        return ref.at[tuple(idx)]
    return view


def _slot_view(*slot):
    return lambda ref, d: ref.at[(d,) + slot]


def _whole_view(ref, d):
    return ref


PEER_BITS = (1, 2, 4, 6, 3, 5, 7)


class _Rider:
    def __init__(self, entries):
        self.n = len(entries)
        self.srcs = [e[0] for e in entries]
        self.src_views = [e[1] for e in entries]
        self.dsts = [e[2] for e in entries]
        self.dst_views = [e[3] for e in entries]
        self.passed = [a for a, d in enumerate(self.dsts) if not isinstance(d, jax.ShapeDtypeStruct)]

    def operands(self):
        return self.srcs + [self.dsts[a] for a in self.passed]

    def out_shape(self):
        return [jax.ShapeDtypeStruct(d.shape, d.dtype) for d in self.dsts]

    def scratch(self):
        return [pltpu.SemaphoreType.DMA((7 * self.n,)), pltpu.SemaphoreType.DMA((7 * self.n,)), pltpu.SemaphoreType.DMA((self.n,))]

    def aliases(self, n_in_before, n_out_before):
        return {n_in_before + self.n + q: n_out_before + a for q, a in enumerate(self.passed)}

    def _copies(self, src, dst, sems):
        send_sems, recv_sems, local_sems = sems
        x, y, c = lax.axis_index("x"), lax.axis_index("y"), lax.axis_index("c")
        me = 4 * x + 2 * y + c
        local = [pltpu.make_async_copy(self.src_views[a](src[a], me), self.dst_views[a](dst[a], me), local_sems.at[a])
                 for a in range(self.n)]
        sends, recvs = [], []
        for ki, k in enumerate(PEER_BITS):
            px, py, pc = x ^ (k >> 2), y ^ ((k >> 1) & 1), c ^ (k & 1)
            peer = 4 * px + 2 * py + pc
            for a in range(self.n):
                s = a * 7 + ki
                mk = lambda landing: pltpu.make_async_remote_copy(
                    src_ref=self.src_views[a](src[a], peer), dst_ref=self.dst_views[a](dst[a], landing),
                    send_sem=send_sems.at[s], recv_sem=recv_sems.at[s], device_id=(px, py, pc), device_id_type=MESH)
                sends.append(mk(me))
                recvs.append(mk(peer))
        return local, sends, recvs

    def start(self, src, dst, sems):
        local, sends, _ = self._copies(src, dst, sems)
        for cp in local + sends:
            cp.start()

    def wait(self, src, dst, sems):
        local, sends, recvs = self._copies(src, dst, sems)
        for cp in recvs:
            cp.wait_recv()
        for cp in sends:
            cp.wait_send()
        for cp in local:
            cp.wait()


def _call(body, *, name, grid, in_specs, args, out_shape, out_specs, scratch=(), rider=None, aliases=None):
    n_in, n_out, n_scr = len(args), len(out_shape), len(scratch)
    aliases = dict(aliases or {})
    if rider is None:
        return pl.pallas_call(body, name=name, grid=grid, out_shape=tuple(out_shape), in_specs=list(in_specs),
                              out_specs=tuple(out_specs), scratch_shapes=list(scratch), input_output_aliases=aliases,
                              compiler_params=_params(len(grid)))(*args)
    r_ops = rider.operands()

    def full_body(*refs):
        ins, refs = refs[:n_in], refs[n_in:]
        r_src, refs = refs[:rider.n], refs[len(r_ops):]
        outs, refs = refs[:n_out], refs[n_out:]
        r_dst, refs = refs[:rider.n], refs[rider.n:]
        scr, sems = refs[:n_scr], refs[n_scr:]
        if grid:
            first = last = None
            for axis, size in enumerate(grid):
                at_start, at_end = pl.program_id(axis) == 0, pl.program_id(axis) == size - 1
                first = at_start if first is None else jnp.logical_and(first, at_start)
                last = at_end if last is None else jnp.logical_and(last, at_end)

            @pl.when(first)
            def _():
                rider.start(r_src, r_dst, sems)
            body(*ins, *outs, *scr)

            @pl.when(last)
            def _():
                rider.wait(r_src, r_dst, sems)
        else:
            rider.start(r_src, r_dst, sems)
            rider.wait(r_src, r_dst, sems)

    aliases.update(rider.aliases(n_in, n_out))
    kw = dict(compiler_params=_params(len(grid))) if grid else {}
    if grid:
        kw["grid"] = grid
    return pl.pallas_call(
        full_body, name=name, out_shape=tuple(out_shape) + tuple(rider.out_shape()),
        in_specs=list(in_specs) + [ANY] * len(r_ops), out_specs=tuple(out_specs) + tuple([ANY] * rider.n),
        scratch_shapes=list(scratch) + rider.scratch(), input_output_aliases=aliases, **kw)(*args, *r_ops)


def _exchange(name, entries):
    return _call(None, name=name, grid=(), in_specs=[], args=[], out_shape=[], out_specs=[], rider=_Rider(entries))


def _allreduce_rows(name, part):
    n_rows, n_cols = part.shape
    sl = n_rows // N_DEV

    def body(p_ref, o_ref, recv_v, sum_v, send1, recv1, send2, recv2, local_sems):
        x, y, c = lax.axis_index("x"), lax.axis_index("y"), lax.axis_index("c")
        me = 4 * x + 2 * y + c
        rows_of = lambda d: pl.ds(pl.multiple_of(d * sl, SUBLANE), sl)
        peers = []
        for ki, k in enumerate(PEER_BITS):
            px, py, pc = x ^ (k >> 2), y ^ ((k >> 1) & 1), c ^ (k & 1)
            peers.append((ki, (px, py, pc), 4 * px + 2 * py + pc))

        def scatter(ki, dev, peer, landing):
            return pltpu.make_async_remote_copy(src_ref=p_ref.at[rows_of(peer)], dst_ref=recv_v.at[landing],
                                                send_sem=send1.at[ki], recv_sem=recv1.at[ki], device_id=dev, device_id_type=MESH)

        def gather(ki, dev, landing):
            return pltpu.make_async_remote_copy(src_ref=sum_v, dst_ref=o_ref.at[rows_of(landing)],
                                                send_sem=send2.at[ki], recv_sem=recv2.at[ki], device_id=dev, device_id_type=MESH)

        own = pltpu.make_async_copy(p_ref.at[rows_of(me)], recv_v.at[me], local_sems.at[0])
        own.start()
        for ki, dev, peer in peers:
            scatter(ki, dev, peer, me).start()
        for ki, dev, peer in peers:
            scatter(ki, dev, peer, peer).wait_recv()
        own.wait()
        total = recv_v[0]
        for d in range(1, N_DEV):
            total = total + recv_v[d]
        sum_v[...] = total
        own = pltpu.make_async_copy(sum_v, o_ref.at[rows_of(me)], local_sems.at[1])
        own.start()
        for ki, dev, peer in peers:
            gather(ki, dev, me).start()
        for ki, dev, peer in peers:
            gather(ki, dev, peer).wait_recv()
        for ki, dev, peer in peers:
            scatter(ki, dev, peer, me).wait_send()
            gather(ki, dev, me).wait_send()
        own.wait()

    return pl.pallas_call(
        body, name=name, out_shape=jax.ShapeDtypeStruct((n_rows, n_cols), F32), in_specs=[ANY], out_specs=ANY,
        scratch_shapes=[pltpu.VMEM((N_DEV, sl, n_cols), F32), pltpu.VMEM((sl, n_cols), F32),
                        pltpu.SemaphoreType.DMA((7,)), pltpu.SemaphoreType.DMA((7,)), pltpu.SemaphoreType.DMA((7,)),
                        pltpu.SemaphoreType.DMA((7,)), pltpu.SemaphoreType.DMA((2,))],
    )(part)


def _shifted_windows(buf, shift_s, n_rows):
    for o in range(1, SUBLANE):
        shift_s[o - 1, 0:n_rows - SUBLANE, :] = buf[pl.ds(o, n_rows - SUBLANE), :]

    def window(s):
        q, o = divmod(s, SUBLANE)
        src = buf if o == 0 else shift_s.at[o - 1]
        return src[pl.ds(q * SUBLANE, ROW_CHUNK), :]
    return window


def _z_parts(x_ref, w_ref, b_ref, z_ref, width):
    xb = x_ref[...].astype(BF16)

    def part(k):
        cols = slice(k * width, (k + 1) * width)
        zk = (_dot(xb, w_ref[:, cols]) + b_ref[:, cols]).astype(BF16)
        z_ref[:, cols] = zk
        return zk.astype(F32)
    return part


def _even_fwd(x, w_in, b_in, conv_w, conv_b, ln_a_g, ln_a_b, ln_v_g, ln_v_b, wcat, bs_full, *, seq, tm, rider=None):
    t_len, d_model = x.shape
    w = d_model // 2
    nt, tps, nb = t_len // tm, seq // tm, tm // LANE

    def body(x_ref, w_ref, b_ref, cw_ref, cb_ref, lag_ref, lab_ref, lvg_ref, lvb_ref, wcat_ref, bs_ref,
             z_ref, y_ref, a1_ref, sg_ref, a0_s, shift_s):
        i = pl.program_id(0)

        @pl.when(i % tps == 0)
        def _():
            a0_s[0:HALO_A, :] = jnp.zeros((HALO_A, w), F32)

        part = _z_parts(x_ref, w_ref, b_ref, z_ref, w)
        a0_s[HALO_A:HALO_A + tm, :] = part(0) * jax.nn.sigmoid(part(1))
        window = _shifted_windows(a0_s, shift_s, tm + HALO_A)
        for r in range(0, tm, ROW_CHUNK):
            acc = jnp.zeros((ROW_CHUNK, w), F32) + cb_ref[...]
            for k in range(CONV_A_WIDTH):
                acc = acc + cw_ref[k:k + 1, :] * window(HALO_A - (CONV_A_WIDTH - 1) + k + r)
            a1_ref[r:r + ROW_CHUNK, :] = acc
        a0_s[0:HALO_A, :] = a0_s[tm:tm + HALO_A, :]
        ah, _ = _norm(a1_ref[...])
        a = _silu(ah * lag_ref[...] + lab_ref[...]) * _silu(part(2))
        y_ref[:, 0:w] = a.astype(BF16)

        u = part(3)
        vh, _ = _norm(_gelu(part(4)))
        v2 = vh * lvg_ref[...] + lvb_ref[...]
        low = _head_low_mask(nb * LANE)
        for j in range(w // LANE):
            vt = _blocks_to_lanes(v2, j, nb)
            rhs = jnp.concatenate([jnp.where(low, vt, 0.0), jnp.where(low, 0.0, vt)], axis=0).astype(BF16)
            out = _dot(wcat_ref[j], rhs)
            for n in range(nb):
                sg_ref[n * LANE:(n + 1) * LANE, j * LANE:(j + 1) * LANE] = (
                    out[:, n * LANE:(n + 1) * LANE] + bs_ref[:, j * LANE:(j + 1) * LANE]).astype(BF16)
        g = _gelu(u) * sg_ref[...].astype(F32) * _silu(part(5))
        y_ref[:, w:2 * w] = g.astype(BF16)

    row = lambda cols: pl.BlockSpec((tm, cols), lambda i: (i, 0))
    return _call(
        body, name="even_fwd", grid=(nt,), rider=rider,
        out_shape=(jax.ShapeDtypeStruct((t_len, 6 * w), BF16), jax.ShapeDtypeStruct((t_len, d_model), BF16),
                   jax.ShapeDtypeStruct((t_len, w), F32), jax.ShapeDtypeStruct((t_len, w), BF16)),
        in_specs=[row(d_model), _const(w_in.shape), _const(b_in.shape), _const(conv_w.shape), _const(conv_b.shape),
                  _const(ln_a_g.shape), _const(ln_a_b.shape), _const(ln_v_g.shape), _const(ln_v_b.shape),
                  _const(wcat.shape), _const(bs_full.shape)],
        out_specs=(row(6 * w), row(d_model), row(w), row(w)),
        scratch=[pltpu.VMEM((tm + HALO_A, w), F32), pltpu.VMEM((SUBLANE - 1, tm + HALO_A, w), F32)],
        args=[x, w_in, b_in, conv_w, conv_b, ln_a_g, ln_a_b, ln_v_g, ln_v_b, wcat, bs_full])


def _pool_inverse(tile_index, tm, seq, window):
    row = tile_index * tm + lax.broadcasted_iota(jnp.int32, (tm, 1), 0)
    pos = (row % seq + 1).astype(F32)
    return 1.0 / jnp.minimum(pos, float(window))


def _odd_fwd(x, w_in, b_in, w_pool, pool_scale, conv_w, *, seq, tm, rider=None):
    t_len, d_model = x.shape
    w = d_model // 2
    nt, tps = t_len // tm, seq // tm

    def body(x_ref, w_ref, b_ref, wp_ref, ps_ref, cw_ref, z_ref, y_ref, pooled_ref, q_ref, cv_s, hc_s):
        i = pl.program_id(0)

        @pl.when(i % tps == 0)
        def _():
            cv_s[0:HALO_C, :] = jnp.zeros((HALO_C, w), F32)
            hc_s[0:HALO_D, :] = jnp.zeros((HALO_D, w), F32)

        part = _z_parts(x_ref, w_ref, b_ref, z_ref, w)
        c_val = part(0)
        c_gate = part(1)
        cv_s[HALO_C:HALO_C + tm, :] = c_val
        for gi, win in enumerate(POOL_WINDOWS):
            cols = slice(gi * LANE, (gi + 1) * LANE)
            s = cv_s[pl.ds(HALO_C, tm), cols]
            for j in range(1, win):
                s = s + cv_s[pl.ds(HALO_C - j, tm), cols]
            pooled = (s * _pool_inverse(i, tm, seq, win) - c_val[:, cols]).astype(BF16)
            pooled_ref[:, cols] = pooled
            c = _dot(pooled, wp_ref[gi]) * ps_ref[:, cols] * _silu(c_gate[:, cols])
            y_ref[:, cols] = c.astype(BF16)
        cv_s[0:HALO_C, :] = cv_s[tm:tm + HALO_C, :]

        d_h = part(2)
        d_b = part(3)
        hc_s[HALO_D:HALO_D + tm, :] = part(4) * d_h
        q = jnp.zeros((tm, w), F32)
        for k in range(CONV_D_WIDTH):
            q = q + cw_ref[k:k + 1, :] * hc_s[pl.ds(HALO_D - (CONV_D_WIDTH - 1) + k, tm), :]
        hc_s[0:HALO_D, :] = hc_s[tm:tm + HALO_D, :]
        qb = q.astype(BF16)
        q_ref[...] = qb
        y_ref[:, w:2 * w] = (d_b * qb.astype(F32) * _silu(part(5))).astype(BF16)

    row = lambda cols: pl.BlockSpec((tm, cols), lambda i: (i, 0))
    return _call(
        body, name="odd_fwd", grid=(nt,), rider=rider,
        out_shape=(jax.ShapeDtypeStruct((t_len, 6 * w), BF16), jax.ShapeDtypeStruct((t_len, d_model), BF16),
                   jax.ShapeDtypeStruct((t_len, w), BF16), jax.ShapeDtypeStruct((t_len, w), BF16)),
        in_specs=[row(d_model), _const(w_in.shape), _const(b_in.shape), _const(w_pool.shape), _const(pool_scale.shape),
                  _const(conv_w.shape)],
        out_specs=(row(6 * w), row(d_model), row(w), row(w)),
        scratch=[pltpu.VMEM((tm + HALO_C, w), F32), pltpu.VMEM((tm + HALO_D, w), F32)],
        args=[x, w_in, b_in, w_pool, pool_scale, conv_w])


def _post_fwd(y, x, p_all, layer, w_out, b_out, ln_g, ln_b, wg, bg, wp, loss_target, *, tm, rider=None):
    t_len, d_model = x.shape
    d_ple = p_all.shape[-1]
    nt = t_len // tm
    last = loss_target is not None

    def body(*refs):
        y_ref, x_ref, p_ref, wo_ref, bo_ref, g_ref, b_ref, wg_ref, bg_ref, wp_ref = refs[:10]
        rest = refs[10:]
        if last:
            lt_ref, rest = rest[0], rest[1:]
        xn_ref, rh_ref, h_ref, rstd_ref, gate_ref, e_ref = rest[:6]
        r = DEEPNORM_ALPHA * x_ref[...] + _dot(y_ref[...], wo_ref[...]) + bo_ref[...]
        rh, rstd = _norm(r)
        h = rh * g_ref[...] + b_ref[...]
        hb = h.astype(BF16)
        gate = jax.nn.sigmoid(_dot(hb, wg_ref[...]) + bg_ref[...])
        e = _dot(p_ref[...].astype(BF16), wp_ref[...])
        xn = h + gate * e
        xn_ref[...] = xn
        rh_ref[...] = rh.astype(BF16)
        h_ref[...] = hb
        rstd_ref[...] = jnp.broadcast_to(rstd, (tm, LANE))
        gate_ref[...] = gate.astype(BF16)
        e_ref[...] = e.astype(BF16)
        if last:
            dxn_ref, sse_ref = rest[6:]
            diff = xn - lt_ref[...]
            dxn_ref[...] = diff * (1.0 / d_model)

            @pl.when(pl.program_id(0) == 0)
            def _():
                sse_ref[...] = jnp.zeros_like(sse_ref)
            sse_ref[...] += jnp.sum(_sum0(diff * diff), axis=1, keepdims=True)

    row = lambda cols: pl.BlockSpec((tm, cols), lambda i: (i, 0))
    in_specs = [row(d_model), row(d_model), pl.BlockSpec((None, tm, d_ple), lambda i: (layer, i, 0)),
                _const(w_out.shape), _const(b_out.shape), _const(ln_g.shape), _const(ln_b.shape), _const(wg.shape),
                _const(bg.shape), _const(wp.shape)]
    args = [y, x, p_all, w_out, b_out, ln_g, ln_b, wg, bg, wp]
    out_shape = [jax.ShapeDtypeStruct((t_len, d_model), F32), jax.ShapeDtypeStruct((t_len, d_model), BF16),
                 jax.ShapeDtypeStruct((t_len, d_model), BF16), jax.ShapeDtypeStruct((t_len, LANE), F32),
                 jax.ShapeDtypeStruct((t_len, d_model), BF16), jax.ShapeDtypeStruct((t_len, d_model), BF16)]
    out_specs = [row(d_model), row(d_model), row(d_model), row(LANE), row(d_model), row(d_model)]
    if last:
        in_specs.append(row(d_model))
        args.append(loss_target)
        out_shape += [jax.ShapeDtypeStruct((t_len, d_model), F32), jax.ShapeDtypeStruct((SUBLANE, LANE), F32)]
        out_specs += [row(d_model), _const((SUBLANE, LANE))]
    return _call(body, name="post_fwd_last" if last else "post_fwd", grid=(nt,), out_shape=out_shape, in_specs=in_specs,
                 out_specs=out_specs, args=args, rider=rider)


def _post_bwd(dxn, gate, e, rh, rstd, ln_g, wg, w_out, *, tm, rider=None):
    t_len, d_model = dxn.shape
    nt = t_len // tm

    def body(dxn_ref, gate_ref, e_ref, rh_ref, rstd_ref, g_ref, wg_ref, wo_ref, de_ref, dgl_ref, dy_ref, dr_ref, acc_ref):
        @pl.when(pl.program_id(0) == 0)
        def _():
            acc_ref[...] = jnp.zeros_like(acc_ref)

        d = dxn_ref[...]
        gt = gate_ref[...].astype(F32)
        rhat = rh_ref[...].astype(F32)
        de_ref[...] = (d * gt).astype(BF16)
        dgl = d * e_ref[...].astype(F32) * gt * (1.0 - gt)
        dglb = dgl.astype(BF16)
        dgl_ref[...] = dglb
        dh = d + _dot_nt(dglb, wg_ref[...])
        dr = _norm_bwd(dh * g_ref[...], rhat, rstd_ref[:, 0:1])
        dr_ref[...] = dr
        dy_ref[...] = _dot_nt(dr.astype(BF16), wo_ref[...]).astype(BF16)
        acc_ref[0:1, :] += _sum0(dgl)
        acc_ref[1:2, :] += _sum0(dh * rhat)
        acc_ref[2:3, :] += _sum0(dh)
        acc_ref[3:4, :] += _sum0(dr)

    row = lambda cols: pl.BlockSpec((tm, cols), lambda i: (i, 0))
    return _call(
        body, name="post_bwd", grid=(nt,), rider=rider,
        out_shape=(jax.ShapeDtypeStruct((t_len, d_model), BF16), jax.ShapeDtypeStruct((t_len, d_model), BF16),
                   jax.ShapeDtypeStruct((t_len, d_model), BF16), jax.ShapeDtypeStruct((t_len, d_model), F32),
                   jax.ShapeDtypeStruct((SUBLANE, d_model), F32)),
        in_specs=[row(d_model), row(d_model), row(d_model), row(d_model), row(LANE), _const(ln_g.shape), _const(wg.shape),
                  _const(w_out.shape)],
        out_specs=(row(d_model), row(d_model), row(d_model), row(d_model), _const((SUBLANE, d_model))),
        args=[dxn, gate, e, rh, rstd, ln_g, wg, w_out])


def _dz_store(dz_ref, dbin_ref, width):
    def store(k, v):
        cols = slice(k * width, (k + 1) * width)
        vb = v.astype(BF16)
        dz_ref[:, cols] = vb
        dbin_ref[0:1, cols] += _sum0(v)
        return vb
    return store


def _even_bwd(dy, z, a1, sg, dr, w_in, conv_w, ln_a_g, ln_a_b, ln_v_g, ln_v_b, wcat_t, *, seq, tm, rider=None):
    t_len, d_model = dr.shape
    w = d_model // 2
    nt, tps, nb = t_len // tm, seq // tm, tm // LANE
    n_heads = 2 * (w // LANE)

    def body(dy_ref, z_ref, a1_ref, sg_ref, dr_ref, w_ref, cw_ref, lag_ref, lab_ref, lvg_ref, lvb_ref, wct_ref,
             dz_ref, dx_ref, dcw_ref, vec_ref, dbin_ref, dws_ref, dbs_ref, da1_s, a0_s, da0_s, cw_acc, shift_s):
        i = pl.program_id(0)
        tile = nt - 1 - i

        @pl.when(i == 0)
        def _():
            vec_ref[...] = jnp.zeros_like(vec_ref)
            dbin_ref[...] = jnp.zeros_like(dbin_ref)
            dws_ref[...] = jnp.zeros_like(dws_ref)
            dbs_ref[...] = jnp.zeros_like(dbs_ref)
            cw_acc[...] = jnp.zeros_like(cw_acc)

        @pl.when((tile + 1) % tps == 0)
        def _():
            da1_s[tm:tm + HALO_A, :] = jnp.zeros((HALO_A, w), F32)

        zp = lambda k: z_ref[:, k * w:(k + 1) * w].astype(F32)
        store = _dz_store(dz_ref, dbin_ref, w)

        da = dy_ref[:, 0:w].astype(F32)
        a_val, a_glu, a_gate = zp(0), zp(1), zp(2)
        s_glu = jax.nn.sigmoid(a_glu)
        a0_s[...] = a_val * s_glu
        ah, rstd_a = _norm(a1_ref[...])
        silu_a2, dsilu_a2 = _silu_grad(ah * lag_ref[...] + lab_ref[...])
        silu_ag, dsilu_ag = _silu_grad(a_gate)
        dzb2 = store(2, da * silu_a2 * dsilu_ag)
        da2 = da * silu_ag * dsilu_a2
        vec_ref[1:2, :] += _sum0(da2 * ah)
        vec_ref[2:3, :] += _sum0(da2)
        da1 = _norm_bwd(da2 * lag_ref[...], ah, rstd_a)
        vec_ref[0:1, :] += _sum0(da1)
        da1_s[0:tm, :] = da1
        window = _shifted_windows(da1_s, shift_s, tm + HALO_A)
        for r in range(0, tm, ROW_CHUNK):
            a0c = a0_s[r:r + ROW_CHUNK, :]
            acc = jnp.zeros((ROW_CHUNK, w), F32)
            for k in range(CONV_A_WIDTH):
                d = window(r + (CONV_A_WIDTH - 1) - k)
                acc = acc + cw_ref[k:k + 1, :] * d
                pw = a0c * d
                p8 = pw[0:SUBLANE]
                for q in range(1, ROW_CHUNK // SUBLANE):
                    p8 = p8 + pw[q * SUBLANE:(q + 1) * SUBLANE]
                cw_acc[k * SUBLANE:(k + 1) * SUBLANE, :] += p8
            da0_s[r:r + ROW_CHUNK, :] = acc
        da1_s[tm:tm + HALO_A, :] = da1_s[0:HALO_A, :]
        da0 = da0_s[...]
        dzb0 = store(0, da0 * s_glu)
        dzb1 = store(1, da0 * a_val * s_glu * (1.0 - s_glu))

        dg = dy_ref[:, w:2 * w].astype(F32)
        u, v, gg = zp(3), zp(4), zp(5)
        sgv = sg_ref[...].astype(F32)
        gelu_u, dgelu_u = _gelu_grad(u)
        silu_gg, dsilu_gg = _silu_grad(gg)
        dzb5 = store(5, dg * gelu_u * sgv * dsilu_gg)
        t1 = dg * silu_gg
        dzb3 = store(3, t1 * sgv * dgelu_u)
        dsg = t1 * gelu_u
        gelu_v, dgelu_v = _gelu_grad(v)
        vh, rstd_v = _norm(gelu_v)
        v2 = vh * lvg_ref[...] + lvb_ref[...]
        low = _head_low_mask(nb * LANE)
        for j in range(w // LANE):
            dt = _blocks_to_lanes(dsg, j, nb)
            d_lo = jnp.where(low, dt, 0.0).astype(BF16)
            d_hi = jnp.where(low, 0.0, dt).astype(BF16)
            v2t = _blocks_to_lanes(v2, j, nb).astype(BF16)
            dv2t = _dot(wct_ref[j], jnp.concatenate([d_lo, d_hi], axis=0))
            for n in range(nb):
                da0_s[n * LANE:(n + 1) * LANE, j * LANE:(j + 1) * LANE] = dv2t[:, n * LANE:(n + 1) * LANE]
            dws_ref[2 * j] += _dot_nt(d_lo, v2t)
            dws_ref[2 * j + 1] += _dot_nt(d_hi, v2t)
            bsum = dt[:, 0:LANE]
            for n in range(1, nb):
                bsum = bsum + dt[:, n * LANE:(n + 1) * LANE]
            dbs_ref[:, j * LANE:(j + 1) * LANE] += bsum
        dv2 = da0_s[...]
        vec_ref[3:4, :] += _sum0(dv2 * vh)
        vec_ref[4:5, :] += _sum0(dv2)
        dzb4 = store(4, _norm_bwd(dv2 * lvg_ref[...], vh, rstd_v) * dgelu_v)

        dx = DEEPNORM_ALPHA * dr_ref[...]
        for k, dzb in enumerate((dzb0, dzb1, dzb2, dzb3, dzb4, dzb5)):
            dx = dx + _dot_nt(dzb, w_ref[:, k * w:(k + 1) * w])
        dx_ref[...] = dx

        @pl.when(i == nt - 1)
        def _():
            for k in range(CONV_A_WIDTH):
                dcw_ref[k:k + 1, :] = _sum0(cw_acc[k * SUBLANE:(k + 1) * SUBLANE, :])
            keep = (lax.broadcasted_iota(jnp.int32, (LANE, LANE), 0) >= lax.broadcasted_iota(jnp.int32, (LANE, LANE), 1))
            for hd in range(n_heads):
                dws_ref[hd] = jnp.where(keep, dws_ref[hd], 0.0)

    rev = lambda cols: pl.BlockSpec((tm, cols), lambda i: (nt - 1 - i, 0))
    return _call(
        body, name="even_bwd", grid=(nt,), rider=rider,
        out_shape=(jax.ShapeDtypeStruct((t_len, 6 * w), BF16), jax.ShapeDtypeStruct((t_len, d_model), F32),
                   jax.ShapeDtypeStruct((CONV_A_WIDTH, w), F32), jax.ShapeDtypeStruct((SUBLANE, w), F32),
                   jax.ShapeDtypeStruct((SUBLANE, 6 * w), F32), jax.ShapeDtypeStruct((n_heads, LANE, LANE), F32),
                   jax.ShapeDtypeStruct((LANE, w), F32)),
        in_specs=[rev(d_model), rev(6 * w), rev(w), rev(w), rev(d_model), _const(w_in.shape), _const(conv_w.shape),
                  _const(ln_a_g.shape), _const(ln_a_b.shape), _const(ln_v_g.shape), _const(ln_v_b.shape), _const(wcat_t.shape)],
        out_specs=(rev(6 * w), rev(d_model), _const((CONV_A_WIDTH, w)), _const((SUBLANE, w)), _const((SUBLANE, 6 * w)),
                   _const((n_heads, LANE, LANE)), _const((LANE, w))),
        scratch=[pltpu.VMEM((tm + HALO_A, w), F32), pltpu.VMEM((tm, w), F32), pltpu.VMEM((tm, w), F32),
                 pltpu.VMEM((CONV_A_WIDTH * SUBLANE, w), F32), pltpu.VMEM((SUBLANE - 1, tm + HALO_A, w), F32)],
        args=[dy, z, a1, sg, dr, w_in, conv_w, ln_a_g, ln_a_b, ln_v_g, ln_v_b, wcat_t])


def _odd_bwd(dy, z, pooled, q, dr, w_in, w_pool, pool_scale, conv_w, *, seq, tm, rider=None):
    t_len, d_model = dr.shape
    w = d_model // 2
    nt, tps = t_len // tm, seq // tm
    n_groups = len(POOL_WINDOWS)

    def body(dy_ref, z_ref, pooled_ref, q_ref, dr_ref, w_ref, wp_ref, ps_ref, cw_ref,
             dz_ref, dx_ref, dwp_ref, vec_ref, dbin_ref, dm_s, dq_s):
        i = pl.program_id(0)
        tile = nt - 1 - i

        @pl.when(i == 0)
        def _():
            dwp_ref[...] = jnp.zeros_like(dwp_ref)
            vec_ref[...] = jnp.zeros_like(vec_ref)
            dbin_ref[...] = jnp.zeros_like(dbin_ref)

        @pl.when((tile + 1) % tps == 0)
        def _():
            dm_s[tm:tm + HALO_C, :] = jnp.zeros((HALO_C, w), F32)
            dq_s[tm:tm + HALO_D, :] = jnp.zeros((HALO_D, w), F32)

        zp = lambda k: z_ref[:, k * w:(k + 1) * w].astype(F32)
        store = _dz_store(dz_ref, dbin_ref, w)

        dc = dy_ref[:, 0:w].astype(F32)
        c_gate = zp(1)
        silu_c, dsilu_c = _silu_grad(c_gate)
        dcs = dc * silu_c
        dvg_parts, dcg_parts = [], []
        for gi, win in enumerate(POOL_WINDOWS):
            cols = slice(gi * LANE, (gi + 1) * LANE)
            pooled_g = pooled_ref[:, cols]
            wp = wp_ref[gi]
            cpre = _dot(pooled_g, wp)
            scale = ps_ref[:, cols]
            vec_ref[0:1, cols] += _sum0(dcs[:, cols] * cpre)
            dcg_parts.append(dc[:, cols] * cpre * scale * dsilu_c[:, cols])
            dcp = (dcs[:, cols] * scale).astype(BF16)
            dwp_ref[gi] += _dot_tn(pooled_g, dcp)
            dpooled = _dot_nt(dcp, wp)
            dm_s[0:tm, cols] = dpooled * _pool_inverse(tile, tm, seq, win)
            s = dm_s[pl.ds(0, tm), cols]
            for j in range(1, win):
                s = s + dm_s[pl.ds(j, tm), cols]
            dvg_parts.append(s - dpooled)
        dm_s[tm:tm + HALO_C, :] = dm_s[0:HALO_C, :]
        dzb0 = store(0, jnp.concatenate(dvg_parts, axis=1))
        dzb1 = store(1, jnp.concatenate(dcg_parts, axis=1))

        dd = dy_ref[:, w:2 * w].astype(F32)
        d_h, d_b, d_c, d_gate = zp(2), zp(3), zp(4), zp(5)
        qv = q_ref[...].astype(F32)
        silu_d, dsilu_d = _silu_grad(d_gate)
        dzb5 = store(5, dd * d_b * qv * dsilu_d)
        dzb3 = store(3, dd * qv * silu_d)
        dq_s[0:tm, :] = dd * d_b * silu_d
        hc = d_c * d_h
        dhc = jnp.zeros((tm, w), F32)
        for k in range(CONV_D_WIDTH):
            d = dq_s[pl.ds((CONV_D_WIDTH - 1) - k, tm), :]
            dhc = dhc + cw_ref[k:k + 1, :] * d
            vec_ref[1 + k:2 + k, :] += _sum0(hc * d)
        dq_s[tm:tm + HALO_D, :] = dq_s[0:HALO_D, :]
        dzb2 = store(2, dhc * d_c)
        dzb4 = store(4, dhc * d_h)

        dx = DEEPNORM_ALPHA * dr_ref[...]
        for k, dzb in enumerate((dzb0, dzb1, dzb2, dzb3, dzb4, dzb5)):
            dx = dx + _dot_nt(dzb, w_ref[:, k * w:(k + 1) * w])
        dx_ref[...] = dx

    rev = lambda cols: pl.BlockSpec((tm, cols), lambda i: (nt - 1 - i, 0))
    return _call(
        body, name="odd_bwd", grid=(nt,), rider=rider,
        out_shape=(jax.ShapeDtypeStruct((t_len, 6 * w), BF16), jax.ShapeDtypeStruct((t_len, d_model), F32),
                   jax.ShapeDtypeStruct((n_groups, LANE, LANE), F32), jax.ShapeDtypeStruct((SUBLANE, w), F32),
                   jax.ShapeDtypeStruct((SUBLANE, 6 * w), F32)),
        in_specs=[rev(d_model), rev(6 * w), rev(w), rev(w), rev(d_model), _const(w_in.shape), _const(w_pool.shape),
                  _const(pool_scale.shape), _const(conv_w.shape)],
        out_specs=(rev(6 * w), rev(d_model), _const((n_groups, LANE, LANE)), _const((SUBLANE, w)), _const((SUBLANE, 6 * w))),
        scratch=[pltpu.VMEM((tm + HALO_C, w), F32), pltpu.VMEM((tm + HALO_D, w), F32)],
        args=[dy, z, pooled, q, dr, w_in, w_pool, pool_scale, conv_w])


def _weight_grad(name, a, b, a_layer=None, bn=None, a_cols=None, rider=None):
    if a_layer is None:
        t_len, m = a.shape
        col = 0
        if a_cols is not None:
            col, m = a_cols
        a_spec = lambda tk: pl.BlockSpec((tk, m), lambda n, k: (k, col))
    else:
        _, t_len, m = a.shape
        a_spec = lambda tk: pl.BlockSpec((None, tk, m), lambda n, k: (a_layer, k, 0))
    n_cols = b.shape[1]
    bn = n_cols if bn is None else bn
    tk = min(t_len, 1024)
    n_k = t_len // tk

    def body(a_ref, b_ref, o_ref, acc):
        k = pl.program_id(1)

        @pl.when(k == 0)
        def _():
            acc[...] = jnp.zeros_like(acc)
        acc[...] += _dot_tn(a_ref[...].astype(BF16), b_ref[...].astype(BF16))

        @pl.when(k == n_k - 1)
        def _():
            o_ref[...] = acc[...].astype(BF16)

    outs = _call(
        body, name=name, grid=(n_cols // bn, n_k), out_shape=[jax.ShapeDtypeStruct((m, n_cols), BF16)],
        in_specs=[a_spec(tk), pl.BlockSpec((tk, bn), lambda n, k: (k, n))],
        out_specs=[pl.BlockSpec((m, bn), lambda n, k: (0, n))], scratch=[pltpu.VMEM((m, bn), F32)],
        args=[a, b], rider=rider)
    return outs[0] if rider is None else outs


def _adamw_reduce(name, parts, w, m, v, rows_per_block):
    n_rows, n_cols = w.shape
    br = rows_per_block
    n_parts = parts.shape[0]

    def body(p_ref, w_ref, m_ref, v_ref, g_ref, d_ref, nm_ref, nv_ref):
        g = p_ref[0].astype(F32)
        for k in range(1, n_parts):
            g = g + p_ref[k].astype(F32)
        nm = ADAM_B1 * m_ref[...] + (1.0 - ADAM_B1) * g
        nv = ADAM_B2 * v_ref[...] + (1.0 - ADAM_B2) * (g * g)
        m_hat = nm / (1.0 - ADAM_B1 ** ADAM_STEP)
        v_hat = nv / (1.0 - ADAM_B2 ** ADAM_STEP)
        g_ref[...] = g
        d_ref[...] = -ADAM_LR * (m_hat / (jnp.sqrt(v_hat) + ADAM_EPS) + ADAM_WD * w_ref[...])
        nm_ref[...] = nm
        nv_ref[...] = nv

    blk = pl.BlockSpec((br, n_cols), lambda i: (i, 0))
    shp = jax.ShapeDtypeStruct((n_rows, n_cols), F32)
    return pl.pallas_call(
        body, name=name, grid=(n_rows // br,), out_shape=(shp, shp, shp, shp),
        in_specs=[pl.BlockSpec((n_parts, br, n_cols), lambda i: (0, i, 0)), blk, blk, blk], out_specs=(blk, blk, blk, blk),
        compiler_params=_params(1),
    )(parts, w, m, v)


def _pack_rows(flat_parts, pad_to=None, width=LANE):
    flat = jnp.concatenate([a.reshape(-1) for a in flat_parts])
    if pad_to is not None and pad_to > flat.shape[0]:
        flat = jnp.concatenate([flat, jnp.zeros((pad_to - flat.shape[0],), flat.dtype)])
    return flat.reshape(-1, width)


def _unpack_rows(packed, shapes):
    flat = packed.reshape(-1)
    out, off = [], 0
    for s in shapes:
        n = math.prod(s)
        out.append(flat[off:off + n].reshape(s))
        off += n
    return out


def _to_dest_major(full):
    lead, last = full.shape[:-1], full.shape[-1]
    t = full.reshape(lead + (N_DEV, last // N_DEV))
    return jnp.moveaxis(t, -2, 0).reshape(N_DEV, -1)


def _from_source_major(blocks, shard_shape):
    t = blocks.reshape((N_DEV,) + tuple(shard_shape))
    t = jnp.moveaxis(t, 0, -2)
    return t.reshape(tuple(shard_shape[:-1]) + (N_DEV * shard_shape[-1],))


def _block_rows(n_rows, n_cols, target_elems=96 * 1024):
    best = None
    for br in range(SUBLANE, n_rows + 1, SUBLANE):
        if n_rows % br == 0 and br * n_cols <= target_elems:
            best = br
    return n_rows if best is None else best


def kernel(x, p, w_in_e, b_in_e, conv_a_w, conv_a_b, ln_a_g, ln_a_b, ln_v_g, ln_v_b, w_s, b_s, w_out_e, b_out_e, w_in_o, b_in_o, w_pool, pool_scale, conv_d_w, w_out_o, b_out_o, ln_g, ln_b, w_ple, w_ple_gate, b_ple_gate, loss_target, m_w_in_e, m_b_in_e, m_conv_a_w, m_conv_a_b, m_ln_a_g, m_ln_a_b, m_ln_v_g, m_ln_v_b, m_w_s, m_b_s, m_w_out_e, m_b_out_e, m_w_in_o, m_b_in_o, m_w_pool, m_pool_scale, m_conv_d_w, m_w_out_o, m_b_out_o, m_ln_g, m_ln_b, m_w_ple, m_w_ple_gate, m_b_ple_gate, v_w_in_e, v_b_in_e, v_conv_a_w, v_conv_a_b, v_ln_a_g, v_ln_a_b, v_ln_v_g, v_ln_v_b, v_w_s, v_b_s, v_w_out_e, v_b_out_e, v_w_in_o, v_b_in_o, v_w_pool, v_pool_scale, v_conv_d_w, v_w_out_o, v_b_out_o, v_ln_g, v_ln_b, v_w_ple, v_w_ple_gate, v_b_ple_gate):
    weights = dict(w_in_e=w_in_e, b_in_e=b_in_e, conv_a_w=conv_a_w, conv_a_b=conv_a_b, ln_a_g=ln_a_g, ln_a_b=ln_a_b,
                   ln_v_g=ln_v_g, ln_v_b=ln_v_b, w_s=w_s, b_s=b_s, w_out_e=w_out_e, b_out_e=b_out_e, w_in_o=w_in_o,
                   b_in_o=b_in_o, w_pool=w_pool, pool_scale=pool_scale, conv_d_w=conv_d_w, w_out_o=w_out_o,
                   b_out_o=b_out_o, ln_g=ln_g, ln_b=ln_b, w_ple=w_ple, w_ple_gate=w_ple_gate, b_ple_gate=b_ple_gate)
    mom_m = dict(w_in_e=m_w_in_e, b_in_e=m_b_in_e, conv_a_w=m_conv_a_w, conv_a_b=m_conv_a_b, ln_a_g=m_ln_a_g,
                 ln_a_b=m_ln_a_b, ln_v_g=m_ln_v_g, ln_v_b=m_ln_v_b, w_s=m_w_s, b_s=m_b_s, w_out_e=m_w_out_e,
                 b_out_e=m_b_out_e, w_in_o=m_w_in_o, b_in_o=m_b_in_o, w_pool=m_w_pool, pool_scale=m_pool_scale,
                 conv_d_w=m_conv_d_w, w_out_o=m_w_out_o, b_out_o=m_b_out_o, ln_g=m_ln_g, ln_b=m_ln_b, w_ple=m_w_ple,
                 w_ple_gate=m_w_ple_gate, b_ple_gate=m_b_ple_gate)
    mom_v = dict(w_in_e=v_w_in_e, b_in_e=v_b_in_e, conv_a_w=v_conv_a_w, conv_a_b=v_conv_a_b, ln_a_g=v_ln_a_g,
                 ln_a_b=v_ln_a_b, ln_v_g=v_ln_v_g, ln_v_b=v_ln_v_b, w_s=v_w_s, b_s=v_b_s, w_out_e=v_w_out_e,
                 b_out_e=v_b_out_e, w_in_o=v_w_in_o, b_in_o=v_b_in_o, w_pool=v_w_pool, pool_scale=v_pool_scale,
                 conv_d_w=v_conv_d_w, w_out_o=v_w_out_o, b_out_o=v_b_out_o, ln_g=v_ln_g, ln_b=v_ln_b, w_ple=v_w_ple,
                 w_ple_gate=v_w_ple_gate, b_ple_gate=v_b_ple_gate)
    names = tuple(weights)

    batch, seq, d_model = x.shape
    t_len = batch * seq
    w = d_model // 2
    n_even = w_in_e.shape[0]
    n_odd = w_in_o.shape[0]
    depth = ln_g.shape[0]
    d_ple = p.shape[-1]
    n_heads = w_s.shape[1]
    tm = 512 if seq % 512 == 0 and seq >= 1024 else seq // 2
    in_cols = w_in_e.shape[-1]
    out_rows = w_out_e.shape[1]
    ple_cols = w_ple.shape[-1]
    gate_rows = w_ple_gate.shape[1]

    sh_shapes = [weights[n].shape for n in SH_NAMES]
    sh_len = sum(math.prod(s) for s in sh_shapes)
    sh_pad = -(-sh_len // (SUBLANE * LANE)) * (SUBLANE * LANE)
    sh_rows = sh_pad // LANE
    sds = jax.ShapeDtypeStruct
    w_in16 = (w_in_e.astype(BF16), w_in_o.astype(BF16))
    w_out16 = (w_out_e.astype(BF16), w_out_o.astype(BF16))
    w_ple16, w_gate16 = w_ple.astype(BF16), w_ple_gate.astype(BF16)

    kinds = ("in", "out", "ple", "gate")

    def weight_entries(i, which=kinds):
        j, par = i // 2, i % 2
        all_four = {"in": (w_in16[par][j], _whole_view, sds((d_model, N_DEV * in_cols), BF16), _axis_view(1, in_cols)),
                    "out": (w_out16[par][j], _whole_view, sds((N_DEV * out_rows, d_model), BF16), _axis_view(0, out_rows)),
                    "ple": (w_ple16[i], _whole_view, sds((d_ple, N_DEV * ple_cols), BF16), _axis_view(1, ple_cols)),
                    "gate": (w_gate16[i], _whole_view, sds((N_DEV * gate_rows, d_model), BF16), _axis_view(0, gate_rows))}
        return [((i, k), all_four[k]) for k in which]

    fwd_riders = {("mixer", 0): weight_entries(0, kinds[1:]) + weight_entries(1, kinds[:1]),
                  ("post", 0): weight_entries(1, kinds[1:])}
    for i in range(1, depth - 1):
        if i % 2:
            fwd_riders[("mixer", i)] = weight_entries(i + 1, kinds[:1])
            fwd_riders[("post", i)] = weight_entries(i + 1, kinds[1:])
        else:
            fwd_riders[("mixer", i)] = weight_entries(i + 1)
    layer_w = {}

    def carried(where):
        tagged = fwd_riders.get(where)
        if tagged is None:
            return None, lambda landed: None
        return _Rider([e for _, e in tagged]), lambda landed: layer_w.update(zip([t for t, _ in tagged], landed))

    first = _exchange("gather_first", [e for _, e in weight_entries(0, kinds[:1])] + [
        (_pack_rows([weights[n] for n in SH_NAMES], sh_pad), _whole_view, sds((N_DEV, sh_rows, LANE), F32), _slot_view())])
    layer_w[(0, "in")] = first[0]
    sh_flat = first[1].reshape(N_DEV, sh_pad)
    full_small, off = {}, 0
    for n, s in zip(SH_NAMES, sh_shapes):
        size = math.prod(s)
        full_small[n] = _from_source_major(sh_flat[:, off:off + size], s)
        off += size

    tril = jnp.tril(jnp.ones((LANE, LANE), dtype=bool))
    ws_m = jnp.where(tril[None, None], w_s, 0.0)
    pair = lambda t: jnp.concatenate([t[:, 0::2], t[:, 1::2]], axis=-1).astype(BF16)
    wcat = pair(ws_m)
    wcat_t = pair(jnp.swapaxes(ws_m, -1, -2))
    bs_full = jnp.repeat(jnp.swapaxes(b_s, -1, -2), w // n_heads, axis=-1)
    row2 = lambda a, j: a[j][None, :]

    x2 = x.reshape(t_len, d_model)
    p3 = p.reshape(depth, t_len, d_ple)
    lt2 = loss_target.reshape(t_len, d_model)

    xs, saved = [x2], []
    dxn = sse = None
    for i in range(depth):
        j = i // 2
        last = i == depth - 1
        rider, file_weights = carried(("mixer", i))
        if i % 2 == 0:
            outs = _even_fwd(xs[i], layer_w[(i, "in")], row2(b_in_e, j), full_small["conv_a_w"][j], row2(conv_a_b, j),
                             row2(ln_a_g, j), row2(ln_a_b, j), row2(ln_v_g, j), row2(ln_v_b, j), wcat[j], bs_full[j],
                             seq=seq, tm=tm, rider=rider)
            b_out = row2(b_out_e, j)
        else:
            outs = _odd_fwd(xs[i], layer_w[(i, "in")], row2(full_small["b_in_o"], j), w_pool[j].astype(BF16),
                            row2(full_small["pool_scale"], j), full_small["conv_d_w"][j], seq=seq, tm=tm, rider=rider)
            b_out = row2(full_small["b_out_o"], j)
        z, y, s1, s2 = outs[:4]
        file_weights(outs[4:])
        rider, file_weights = carried(("post", i))
        outs = _post_fwd(y, xs[i], p3, i, layer_w[(i, "out")], b_out, row2(ln_g, i), row2(ln_b, i), layer_w[(i, "gate")],
                         row2(b_ple_gate, i), layer_w[(i, "ple")], lt2 if last else None, tm=tm, rider=rider)
        xn, rh, h, rstd, gate, e = outs[:6]
        if last:
            dxn, sse = outs[6:8]
        file_weights(outs[8 if last else 6:])
        xs.append(xn)
        saved.append(dict(z=z, y=y, s1=s1, s2=s2, rh=rh, h=h, rstd=rstd, gate=gate, e=e))

    loss = lax.psum((0.5 / d_model) * sse[0, 0], ("x", "y", "c"))

    recv = {"w_in_e": None, "w_in_o": None, "w_out_e": None, "w_out_o": None, "w_ple": None, "w_ple_gate": None}

    def grad_entry(i, kind, g, half=None):
        j, par = i // 2, i % 2
        sfx = "_o" if par else "_e"
        name, src_view, slot = {"in": ("w_in" + sfx, _axis_view(1, in_cols), j), "out": ("w_out" + sfx, _axis_view(0, out_rows), j),
                                "ple": ("w_ple", _axis_view(1, ple_cols), i), "gate": ("w_ple_gate", _axis_view(0, gate_rows), i)}[kind]
        dst = recv[name] if recv[name] is not None else sds((N_DEV,) + weights[name].shape, BF16)
        if half is None:
            dst_view = _slot_view(slot)
        else:
            dst_view = lambda ref, d: ref.at[d, slot, pl.ds(half[0] * half[1], half[1]), :]
        return name, (g, src_view, dst, dst_view)

    def ride(tagged):
        if not tagged:
            return None, lambda landed: None
        return _Rider([e for _, e in tagged]), lambda landed: recv.update(zip([n for n, _ in tagged], landed))

    small = {n: [None] * weights[n].shape[0] for n in REP_NAMES + SH_NAMES}
    grads = {}
    for i in reversed(range(depth)):
        j, par = i // 2, i % 2
        sv = saved[i]
        w_in = layer_w[(i, "in")]
        split = par == 1 and (i + 1, "in") in grads
        rider, file_landed = ride([grad_entry(i + 1, k, grads.pop((i + 1, k))) for k in kinds[1:]] if split else [])
        outs = _post_bwd(dxn, sv["gate"], sv["e"], sv["rh"], sv["rstd"], row2(ln_g, i), layer_w[(i, "gate")],
                         layer_w[(i, "out")], tm=tm, rider=rider)
        de, dgl, dy, dr, acc = outs[:5]
        file_landed(outs[5:])
        small["b_ple_gate"][i], small["ln_g"][i], small["ln_b"][i] = acc[0], acc[1], acc[2]
        small["b_out_o" if par else "b_out_e"][j] = acc[3]
        rider, file_landed = ride([grad_entry(i + 1, k, grads.pop((i + 1, k))) for k in kinds if (i + 1, k) in grads])
        if par == 0:
            outs = _even_bwd(dy, sv["z"], sv["s1"], sv["s2"], dr, w_in, full_small["conv_a_w"][j],
                             row2(ln_a_g, j), row2(ln_a_b, j), row2(ln_v_g, j), row2(ln_v_b, j), wcat_t[j], seq=seq, tm=tm,
                             rider=rider)
            dz, dx, dcw, vec, dbin, dws, dbs = outs[:7]
            file_landed(outs[7:])
            small["conv_a_w"][j], small["conv_a_b"][j] = dcw, vec[0]
            small["ln_a_g"][j], small["ln_a_b"][j], small["ln_v_g"][j], small["ln_v_b"][j] = vec[1], vec[2], vec[3], vec[4]
            small["b_in_e"][j], small["w_s"][j] = dbin[0], dws
            small["b_s"][j] = _head_sums(dbs, n_heads)
        else:
            outs = _odd_bwd(dy, sv["z"], sv["s1"], sv["s2"], dr, w_in, w_pool[j].astype(BF16),
                            row2(full_small["pool_scale"], j), full_small["conv_d_w"][j], seq=seq, tm=tm, rider=rider)
            dz, dx, dwp, vec, dbin = outs[:5]
            file_landed(outs[5:])
            small["w_pool"][j], small["pool_scale"][j], small["conv_d_w"][j] = dwp, vec[0], vec[1:1 + CONV_D_WIDTH]
            small["b_in_o"][j] = dbin[0]
        tag = f"_l{i}"
        grads[(i, "out")] = _weight_grad("dw_out" + tag, sv["y"], dr)
        grads[(i, "ple")] = _weight_grad("dw_ple" + tag, p3, de, a_layer=i)
        grads[(i, "gate")] = _weight_grad("dw_gate" + tag, sv["h"], dgl)
        if i > 0:
            grads[(i, "in")] = _weight_grad("dw_in" + tag, xs[i], dz, bn=in_cols * N_DEV // 2)
        dxn = dx
    grad_x = dxn.reshape(batch, seq, d_model)

    half_rows = d_model // 2
    rider, file_landed = ride([grad_entry(0, k, grads.pop((0, k))) for k in kinds[1:]])
    outs = _weight_grad("dw_in_l0_top", xs[0], dz, a_cols=(0, half_rows), rider=rider)
    file_landed(outs[1:])
    rider, file_landed = ride([grad_entry(0, "in", outs[0], half=(0, half_rows))])
    outs = _weight_grad("dw_in_l0_bottom", xs[0], dz, a_cols=(1, half_rows), rider=rider)
    file_landed(outs[1:])

    small_full = {n: jnp.stack(small[n]) for n in small}
    sh_part = jnp.concatenate([_to_dest_major(small_full[n]) for n in SH_NAMES], axis=1)
    sh_part = jnp.concatenate([sh_part, jnp.zeros((N_DEV, sh_pad - sh_len), F32)], axis=1).reshape(N_DEV, sh_rows, LANE)
    name, entry = grad_entry(0, "in", outs[0], half=(1, half_rows))
    landed = _exchange("exchange_last", [entry, (sh_part, _slot_view(), sds((N_DEV, sh_rows, LANE), F32), _slot_view())])
    recv[name] = landed[0]

    rep_width = 4 * LANE
    rep_len = sum(math.prod(weights[n].shape) for n in REP_NAMES)
    rep_block = N_DEV * SUBLANE
    rep_pad = -(-rep_len // (rep_block * rep_width)) * (rep_block * rep_width)
    rep_sum = _allreduce_rows("allreduce_replicated", _pack_rows([small_full[n] for n in REP_NAMES], rep_pad, rep_width))

    results = {}
    for n, parts in recv.items():
        shp = weights[n].shape
        rows, cols = math.prod(shp[:-1]), shp[-1]
        two = lambda a: a.reshape(rows, cols)
        outs = _adamw_reduce("adamw_" + n, parts.reshape(N_DEV, rows, cols), two(weights[n]), two(mom_m[n]), two(mom_v[n]),
                             _block_rows(rows, cols))
        results[n] = [o.reshape(shp) for o in outs]
    pack_sh = lambda d: _pack_rows([d[n] for n in SH_NAMES], sh_pad)
    outs = _adamw_reduce("adamw_small_sharded", landed[1], pack_sh(weights), pack_sh(mom_m), pack_sh(mom_v), sh_rows)
    for n, *vals in zip(SH_NAMES, *[_unpack_rows(o, sh_shapes) for o in outs]):
        results[n] = vals
    pack_rep = lambda d: _pack_rows([d[n] for n in REP_NAMES], rep_pad, rep_width)
    rep_shapes = [weights[n].shape for n in REP_NAMES]
    outs = _adamw_reduce("adamw_replicated", rep_sum[None], pack_rep(weights), pack_rep(mom_m), pack_rep(mom_v), rep_block)
    for n, *vals in zip(REP_NAMES, *[_unpack_rows(o, rep_shapes) for o in outs]):
        results[n] = vals

    return (loss, grad_x, *[results[n][0] for n in names], *[results[n][1] for n in names],
            *[results[n][2] for n in names], *[results[n][3] for n in names])


def _head_sums(dbs, n_heads):
    t, width = dbs.shape
    return jnp.sum(dbs.reshape(t, n_heads, width // n_heads), axis=-1).T
```

```python
import functools
import math

import jax
import jax.numpy as jnp
from jax import lax
from jax.experimental import pallas as pl
from jax.experimental.pallas import tpu as pltpu

F32 = jnp.float32
BF16 = jnp.bfloat16

N_DEV = 8
DEPTH = 4
LN_EPS = 1e-5
DEEPNORM_ALPHA = (2.0 * DEPTH) ** 0.25
POOL_WINDOWS = (2, 4, 8, 16)
CONV_A_WIDTH = 31
CONV_D_WIDTH = 3
GELU_C = math.sqrt(2.0 / math.pi)
GELU_K = 0.044715

ADAM_LR = 0.001
ADAM_B1 = 0.9
ADAM_B2 = 0.999
ADAM_EPS = 1e-08
ADAM_WD = 0.01
ADAM_STEP = 10

LANE = 128
SUBLANE = 8
HALO_A = 32
HALO_C = 16
HALO_D = 8
ROW_CHUNK = 32
VMEM_LIMIT = 56 * 2**20

ANY = pl.BlockSpec(memory_space=pl.ANY)
MESH = pl.DeviceIdType.MESH

REP_NAMES = ("b_in_e", "conv_a_b", "ln_a_g", "ln_a_b", "ln_v_g", "ln_v_b", "w_s", "b_s", "b_out_e", "w_pool", "ln_g",
             "ln_b", "b_ple_gate")
SH_NAMES = ("conv_a_w", "conv_d_w", "pool_scale", "b_in_o", "b_out_o")


def _params(n_grid_axes):
    return pltpu.CompilerParams(dimension_semantics=("arbitrary",) * n_grid_axes, vmem_limit_bytes=VMEM_LIMIT)


def _const(shape):
    nd = len(shape)
    return pl.BlockSpec(shape, lambda *_: (0,) * nd)


def _dot(a, b):
    return jnp.dot(a, b, preferred_element_type=F32)


def _dot_nt(a, b):
    return lax.dot_general(a, b, (((1,), (1,)), ((), ())), preferred_element_type=F32)


def _dot_tn(a, b):
    return lax.dot_general(a, b, (((0,), (0,)), ((), ())), preferred_element_type=F32)


def _silu(x):
    return x * jax.nn.sigmoid(x)


def _silu_grad(x):
    s = jax.nn.sigmoid(x)
    return x * s, s * (1.0 + x * (1.0 - s))


def _gelu(x):
    return 0.5 * x * (1.0 + jnp.tanh(GELU_C * (x + GELU_K * x * x * x)))


def _gelu_grad(x):
    x2 = x * x
    t = jnp.tanh(GELU_C * x * (1.0 + GELU_K * x2))
    return 0.5 * x * (1.0 + t), 0.5 * (1.0 + t) + 0.5 * x * (1.0 - t * t) * GELU_C * (1.0 + 3.0 * GELU_K * x2)


def _norm(v):
    mu = jnp.mean(v, axis=-1, keepdims=True)
    d = v - mu
    var = jnp.mean(d * d, axis=-1, keepdims=True)
    rstd = lax.rsqrt(var + LN_EPS)
    return d * rstd, rstd


def _norm_bwd(dxh, xh, rstd):
    return rstd * (dxh - jnp.mean(dxh, axis=-1, keepdims=True) - xh * jnp.mean(dxh * xh, axis=-1, keepdims=True))


def _sum0(v):
    return jnp.sum(v, axis=0, keepdims=True)


def _head_low_mask(n_cols):
    lane = lax.broadcasted_iota(jnp.int32, (LANE, n_cols), 1)
    return (lane & (LANE - 1)) < (LANE // 2)


def _blocks_to_lanes(v, j, nb):
    return jnp.concatenate([v[n * LANE:(n + 1) * LANE, j * LANE:(j + 1) * LANE] for n in range(nb)], axis=1)


def _axis_view(axis, size):
    def view(ref, d):
        idx = [slice(None)] * len(ref.shape)
        idx[axis] = pl.ds(pl.multiple_of(d * size, size), size)
        return ref.at[tuple(idx)]
    return view


def _slot_view(*slot):
    return lambda ref, d: ref.at[(d,) + slot]


def _whole_view(ref, d):
    return ref


PEER_BITS = (1, 2, 4, 6, 3, 5, 7)


class _Rider:
    def __init__(self, entries):
        self.n = len(entries)
        self.srcs = [e[0] for e in entries]
        self.src_views = [e[1] for e in entries]
        self.dsts = [e[2] for e in entries]
        self.dst_views = [e[3] for e in entries]
        self.passed = [a for a, d in enumerate(self.dsts) if not isinstance(d, jax.ShapeDtypeStruct)]

    def operands(self):
        return self.srcs + [self.dsts[a] for a in self.passed]

    def out_shape(self):
        return [jax.ShapeDtypeStruct(d.shape, d.dtype) for d in self.dsts]

    def scratch(self):
        return [pltpu.SemaphoreType.DMA((7 * self.n,)), pltpu.SemaphoreType.DMA((7 * self.n,)), pltpu.SemaphoreType.DMA((self.n,))]

    def aliases(self, n_in_before, n_out_before):
        return {n_in_before + self.n + q: n_out_before + a for q, a in enumerate(self.passed)}

    def _copies(self, src, dst, sems):
        send_sems, recv_sems, local_sems = sems
        x, y, c = lax.axis_index("x"), lax.axis_index("y"), lax.axis_index("c")
        me = 4 * x + 2 * y + c
        local = [pltpu.make_async_copy(self.src_views[a](src[a], me), self.dst_views[a](dst[a], me), local_sems.at[a])
                 for a in range(self.n)]
        sends, recvs = [], []
        for ki, k in enumerate(PEER_BITS):
            px, py, pc = x ^ (k >> 2), y ^ ((k >> 1) & 1), c ^ (k & 1)
            peer = 4 * px + 2 * py + pc
            for a in range(self.n):
                s = a * 7 + ki
                mk = lambda landing: pltpu.make_async_remote_copy(
                    src_ref=self.src_views[a](src[a], peer), dst_ref=self.dst_views[a](dst[a], landing),
                    send_sem=send_sems.at[s], recv_sem=recv_sems.at[s], device_id=(px, py, pc), device_id_type=MESH)
                sends.append(mk(me))
                recvs.append(mk(peer))
        return local, sends, recvs

    def start(self, src, dst, sems):
        local, sends, _ = self._copies(src, dst, sems)
        for cp in local + sends:
            cp.start()

    def wait(self, src, dst, sems):
        local, sends, recvs = self._copies(src, dst, sems)
        for cp in recvs:
            cp.wait_recv()
        for cp in sends:
            cp.wait_send()
        for cp in local:
            cp.wait()


def _call(body, *, name, grid, in_specs, args, out_shape, out_specs, scratch=(), rider=None, aliases=None):
    n_in, n_out, n_scr = len(args), len(out_shape), len(scratch)
    aliases = dict(aliases or {})
    if rider is None:
        return pl.pallas_call(body, name=name, grid=grid, out_shape=tuple(out_shape), in_specs=list(in_specs),
                              out_specs=tuple(out_specs), scratch_shapes=list(scratch), input_output_aliases=aliases,
                              compiler_params=_params(len(grid)))(*args)
    r_ops = rider.operands()

    def full_body(*refs):
        ins, refs = refs[:n_in], refs[n_in:]
        r_src, refs = refs[:rider.n], refs[len(r_ops):]
        outs, refs = refs[:n_out], refs[n_out:]
        r_dst, refs = refs[:rider.n], refs[rider.n:]
        scr, sems = refs[:n_scr], refs[n_scr:]
        if grid:
            first = last = None
            for axis, size in enumerate(grid):
                at_start, at_end = pl.program_id(axis) == 0, pl.program_id(axis) == size - 1
                first = at_start if first is None else jnp.logical_and(first, at_start)
                last = at_end if last is None else jnp.logical_and(last, at_end)

            @pl.when(first)
            def _():
                rider.start(r_src, r_dst, sems)
            body(*ins, *outs, *scr)

            @pl.when(last)
            def _():
                rider.wait(r_src, r_dst, sems)
        else:
            rider.start(r_src, r_dst, sems)
            rider.wait(r_src, r_dst, sems)

    aliases.update(rider.aliases(n_in, n_out))
    kw = dict(compiler_params=_params(len(grid))) if grid else {}
    if grid:
        kw["grid"] = grid
    return pl.pallas_call(
        full_body, name=name, out_shape=tuple(out_shape) + tuple(rider.out_shape()),
        in_specs=list(in_specs) + [ANY] * len(r_ops), out_specs=tuple(out_specs) + tuple([ANY] * rider.n),
        scratch_shapes=list(scratch) + rider.scratch(), input_output_aliases=aliases, **kw)(*args, *r_ops)


def _exchange(name, entries):
    return _call(None, name=name, grid=(), in_specs=[], args=[], out_shape=[], out_specs=[], rider=_Rider(entries))


def _allreduce_rows(name, part):
    n_rows, n_cols = part.shape
    sl = n_rows // N_DEV

    def body(p_ref, o_ref, recv_v, sum_v, send1, recv1, send2, recv2, local_sems):
        x, y, c = lax.axis_index("x"), lax.axis_index("y"), lax.axis_index("c")
        me = 4 * x + 2 * y + c
        rows_of = lambda d: pl.ds(pl.multiple_of(d * sl, SUBLANE), sl)
        peers = []
        for ki, k in enumerate(PEER_BITS):
            px, py, pc = x ^ (k >> 2), y ^ ((k >> 1) & 1), c ^ (k & 1)
            peers.append((ki, (px, py, pc), 4 * px + 2 * py + pc))

        def scatter(ki, dev, peer, landing):
            return pltpu.make_async_remote_copy(src_ref=p_ref.at[rows_of(peer)], dst_ref=recv_v.at[landing],
                                                send_sem=send1.at[ki], recv_sem=recv1.at[ki], device_id=dev, device_id_type=MESH)

        def gather(ki, dev, landing):
            return pltpu.make_async_remote_copy(src_ref=sum_v, dst_ref=o_ref.at[rows_of(landing)],
                                                send_sem=send2.at[ki], recv_sem=recv2.at[ki], device_id=dev, device_id_type=MESH)

        own = pltpu.make_async_copy(p_ref.at[rows_of(me)], recv_v.at[me], local_sems.at[0])
        own.start()
        for ki, dev, peer in peers:
            scatter(ki, dev, peer, me).start()
        for ki, dev, peer in peers:
            scatter(ki, dev, peer, peer).wait_recv()
        own.wait()
        total = recv_v[0]
        for d in range(1, N_DEV):
            total = total + recv_v[d]
        sum_v[...] = total
        own = pltpu.make_async_copy(sum_v, o_ref.at[rows_of(me)], local_sems.at[1])
        own.start()
        for ki, dev, peer in peers:
            gather(ki, dev, me).start()
        for ki, dev, peer in peers:
            gather(ki, dev, peer).wait_recv()
        for ki, dev, peer in peers:
            scatter(ki, dev, peer, me).wait_send()
            gather(ki, dev, me).wait_send()
        own.wait()

    return pl.pallas_call(
        body, name=name, out_shape=jax.ShapeDtypeStruct((n_rows, n_cols), F32), in_specs=[ANY], out_specs=ANY,
        scratch_shapes=[pltpu.VMEM((N_DEV, sl, n_cols), F32), pltpu.VMEM((sl, n_cols), F32),
                        pltpu.SemaphoreType.DMA((7,)), pltpu.SemaphoreType.DMA((7,)), pltpu.SemaphoreType.DMA((7,)),
                        pltpu.SemaphoreType.DMA((7,)), pltpu.SemaphoreType.DMA((2,))],
    )(part)


def _shifted_windows(buf, shift_s, n_rows):
    for o in range(1, SUBLANE):
        shift_s[o - 1, 0:n_rows - SUBLANE, :] = buf[pl.ds(o, n_rows - SUBLANE), :]

    def window(s):
        q, o = divmod(s, SUBLANE)
        src = buf if o == 0 else shift_s.at[o - 1]
        return src[pl.ds(q * SUBLANE, ROW_CHUNK), :]
    return window


def _z_parts(x_ref, w_ref, b_ref, z_ref, width):
    xb = x_ref[...].astype(BF16)

    def part(k, keep=True, half=None):
        cols = slice(k * width, (k + 1) * width)
        if half is not None:
            cols = slice(k * width + half * (width // 2), k * width + (half + 1) * (width // 2))
        zk = (_dot(xb, w_ref[:, cols]) + b_ref[:, cols]).astype(BF16)
        z_ref[:, cols] = zk
        return zk.astype(F32) if keep else None
    return part


def _even_fwd(x, w_in, b_in, conv_w, conv_b, ln_a_g, ln_a_b, ln_v_g, ln_v_b, wcat, bs_full, *, seq, tm, rider=None):
    t_len, d_model = x.shape
    w = d_model // 2
    nt, tps, nb = t_len // tm, seq // tm, tm // LANE

    def body(x_ref, w_ref, b_ref, cw_ref, cb_ref, lag_ref, lab_ref, lvg_ref, lvb_ref, wcat_ref, bs_ref,
             z_ref, y_ref, a1_ref, sg_ref, a0_s, shift_s):
        i = pl.program_id(0)

        @pl.when(i % tps == 0)
        def _():
            a0_s[0:HALO_A, :] = jnp.zeros((HALO_A, w), F32)

        part = _z_parts(x_ref, w_ref, b_ref, z_ref, w)
        a0_s[HALO_A:HALO_A + tm, :] = part(0) * jax.nn.sigmoid(part(1))
        window = _shifted_windows(a0_s, shift_s, tm + HALO_A)
        n_chunks = tm // ROW_CHUNK
        pieces = [(k, h) for k in range(2, 6) for h in range(2)]
        later = dict(zip(range(n_chunks - 1, -1, -max(1, n_chunks // len(pieces))), reversed(pieces)))
        for c, r in enumerate(range(0, tm, ROW_CHUNK)):
            acc = jnp.zeros((ROW_CHUNK, w), F32) + cb_ref[...]
            for k in range(CONV_A_WIDTH):
                acc = acc + cw_ref[k:k + 1, :] * window(HALO_A - (CONV_A_WIDTH - 1) + k + r)
            a1_ref[r:r + ROW_CHUNK, :] = acc
            if c in later:
                part(later[c][0], keep=False, half=later[c][1])
        for piece in pieces:
            if piece not in later.values():
                part(piece[0], keep=False, half=piece[1])
        zp = lambda k: z_ref[:, k * w:(k + 1) * w].astype(F32)
        a0_s[0:HALO_A, :] = a0_s[tm:tm + HALO_A, :]
        ah, _ = _norm(a1_ref[...])
        a = _silu(ah * lag_ref[...] + lab_ref[...]) * _silu(zp(2))
        y_ref[:, 0:w] = a.astype(BF16)

        u = zp(3)
        vh, _ = _norm(_gelu(zp(4)))
        v2 = vh * lvg_ref[...] + lvb_ref[...]
        low = _head_low_mask(nb * LANE)
        for j in range(w // LANE):
            vt = _blocks_to_lanes(v2, j, nb)
            rhs = jnp.concatenate([jnp.where(low, vt, 0.0), jnp.where(low, 0.0, vt)], axis=0).astype(BF16)
            out = _dot(wcat_ref[j], rhs)
            for n in range(nb):
                sg_ref[n * LANE:(n + 1) * LANE, j * LANE:(j + 1) * LANE] = (
                    out[:, n * LANE:(n + 1) * LANE] + bs_ref[:, j * LANE:(j + 1) * LANE]).astype(BF16)
        g = _gelu(u) * sg_ref[...].astype(F32) * _silu(zp(5))
        y_ref[:, w:2 * w] = g.astype(BF16)

    row = lambda cols: pl.BlockSpec((tm, cols), lambda i: (i, 0))
    return _call(
        body, name="even_fwd", grid=(nt,), rider=rider,
        out_shape=(jax.ShapeDtypeStruct((t_len, 6 * w), BF16), jax.ShapeDtypeStruct((t_len, d_model), BF16),
                   jax.ShapeDtypeStruct((t_len, w), F32), jax.ShapeDtypeStruct((t_len, w), BF16)),
        in_specs=[row(d_model), _const(w_in.shape), _const(b_in.shape), _const(conv_w.shape), _const(conv_b.shape),
                  _const(ln_a_g.shape), _const(ln_a_b.shape), _const(ln_v_g.shape), _const(ln_v_b.shape),
                  _const(wcat.shape), _const(bs_full.shape)],
        out_specs=(row(6 * w), row(d_model), row(w), row(w)),
        scratch=[pltpu.VMEM((tm + HALO_A, w), F32), pltpu.VMEM((SUBLANE - 1, tm + HALO_A, w), F32)],
        args=[x, w_in, b_in, conv_w, conv_b, ln_a_g, ln_a_b, ln_v_g, ln_v_b, wcat, bs_full])


def _pool_inverse(tile_index, tm, seq, window):
    row = tile_index * tm + lax.broadcasted_iota(jnp.int32, (tm, 1), 0)
    pos = (row % seq + 1).astype(F32)
    return 1.0 / jnp.minimum(pos, float(window))


def _odd_fwd(x, w_in, b_in, w_pool, pool_scale, conv_w, *, seq, tm, rider=None):
    t_len, d_model = x.shape
    w = d_model // 2
    nt, tps = t_len // tm, seq // tm

    def body(x_ref, w_ref, b_ref, wp_ref, ps_ref, cw_ref, z_ref, y_ref, pooled_ref, q_ref, cv_s, hc_s):
        i = pl.program_id(0)

        @pl.when(i % tps == 0)
        def _():
            cv_s[0:HALO_C, :] = jnp.zeros((HALO_C, w), F32)
            hc_s[0:HALO_D, :] = jnp.zeros((HALO_D, w), F32)

        part = _z_parts(x_ref, w_ref, b_ref, z_ref, w)
        c_val = part(0)
        c_gate = part(1)
        cv_s[HALO_C:HALO_C + tm, :] = c_val
        for gi, win in enumerate(POOL_WINDOWS):
            cols = slice(gi * LANE, (gi + 1) * LANE)
            s = cv_s[pl.ds(HALO_C, tm), cols]
            for j in range(1, win):
                s = s + cv_s[pl.ds(HALO_C - j, tm), cols]
            pooled = (s * _pool_inverse(i, tm, seq, win) - c_val[:, cols]).astype(BF16)
            pooled_ref[:, cols] = pooled
            c = _dot(pooled, wp_ref[gi]) * ps_ref[:, cols] * _silu(c_gate[:, cols])
            y_ref[:, cols] = c.astype(BF16)
        cv_s[0:HALO_C, :] = cv_s[tm:tm + HALO_C, :]

        d_h = part(2)
        d_b = part(3)
        hc_s[HALO_D:HALO_D + tm, :] = part(4) * d_h
        q = jnp.zeros((tm, w), F32)
        for k in range(CONV_D_WIDTH):
            q = q + cw_ref[k:k + 1, :] * hc_s[pl.ds(HALO_D - (CONV_D_WIDTH - 1) + k, tm), :]
        hc_s[0:HALO_D, :] = hc_s[tm:tm + HALO_D, :]
        qb = q.astype(BF16)
        q_ref[...] = qb
        y_ref[:, w:2 * w] = (d_b * qb.astype(F32) * _silu(part(5))).astype(BF16)

    row = lambda cols: pl.BlockSpec((tm, cols), lambda i: (i, 0))
    return _call(
        body, name="odd_fwd", grid=(nt,), rider=rider,
        out_shape=(jax.ShapeDtypeStruct((t_len, 6 * w), BF16), jax.ShapeDtypeStruct((t_len, d_model), BF16),
                   jax.ShapeDtypeStruct((t_len, w), BF16), jax.ShapeDtypeStruct((t_len, w), BF16)),
        in_specs=[row(d_model), _const(w_in.shape), _const(b_in.shape), _const(w_pool.shape), _const(pool_scale.shape),
                  _const(conv_w.shape)],
        out_specs=(row(6 * w), row(d_model), row(w), row(w)),
        scratch=[pltpu.VMEM((tm + HALO_C, w), F32), pltpu.VMEM((tm + HALO_D, w), F32)],
        args=[x, w_in, b_in, w_pool, pool_scale, conv_w])


def _post_fwd(y, x, p_all, layer, w_out, b_out, ln_g, ln_b, wg, bg, wp, loss_target, *, tm, rider=None):
    t_len, d_model = x.shape
    d_ple = p_all.shape[-1]
    nt = t_len // tm
    last = loss_target is not None

    def body(*refs):
        y_ref, x_ref, p_ref, wo_ref, bo_ref, g_ref, b_ref, wg_ref, bg_ref, wp_ref = refs[:10]
        rest = refs[10:]
        if last:
            lt_ref, rest = rest[0], rest[1:]
        xn_ref, rh_ref, h_ref, rstd_ref, gate_ref, e_ref = rest[:6]
        r = DEEPNORM_ALPHA * x_ref[...] + _dot(y_ref[...], wo_ref[...]) + bo_ref[...]
        rh, rstd = _norm(r)
        h = rh * g_ref[...] + b_ref[...]
        hb = h.astype(BF16)
        gate = jax.nn.sigmoid(_dot(hb, wg_ref[...]) + bg_ref[...])
        e = _dot(p_ref[...].astype(BF16), wp_ref[...])
        xn = h + gate * e
        xn_ref[...] = xn
        rh_ref[...] = rh.astype(BF16)
        h_ref[...] = hb
        rstd_ref[...] = jnp.broadcast_to(rstd, (tm, LANE))
        gate_ref[...] = gate.astype(BF16)
        e_ref[...] = e.astype(BF16)
        if last:
            dxn_ref, sse_ref = rest[6:]
            diff = xn - lt_ref[...]
            dxn_ref[...] = diff * (1.0 / d_model)

            @pl.when(pl.program_id(0) == 0)
            def _():
                sse_ref[...] = jnp.zeros_like(sse_ref)
            sse_ref[...] += jnp.sum(_sum0(diff * diff), axis=1, keepdims=True)

    row = lambda cols: pl.BlockSpec((tm, cols), lambda i: (i, 0))
    in_specs = [row(d_model), row(d_model), pl.BlockSpec((None, tm, d_ple), lambda i: (layer, i, 0)),
                _const(w_out.shape), _const(b_out.shape), _const(ln_g.shape), _const(ln_b.shape), _const(wg.shape),
                _const(bg.shape), _const(wp.shape)]
    args = [y, x, p_all, w_out, b_out, ln_g, ln_b, wg, bg, wp]
    out_shape = [jax.ShapeDtypeStruct((t_len, d_model), F32), jax.ShapeDtypeStruct((t_len, d_model), BF16),
                 jax.ShapeDtypeStruct((t_len, d_model), BF16), jax.ShapeDtypeStruct((t_len, LANE), F32),
                 jax.ShapeDtypeStruct((t_len, d_model), BF16), jax.ShapeDtypeStruct((t_len, d_model), BF16)]
    out_specs = [row(d_model), row(d_model), row(d_model), row(LANE), row(d_model), row(d_model)]
    if last:
        in_specs.append(row(d_model))
        args.append(loss_target)
        out_shape += [jax.ShapeDtypeStruct((t_len, d_model), F32), jax.ShapeDtypeStruct((SUBLANE, LANE), F32)]
        out_specs += [row(d_model), _const((SUBLANE, LANE))]
    return _call(body, name="post_fwd_last" if last else "post_fwd", grid=(nt,), out_shape=out_shape, in_specs=in_specs,
                 out_specs=out_specs, args=args, rider=rider)


def _post_bwd(dxn, gate, e, rh, rstd, ln_g, wg, w_out, *, tm, rider=None):
    t_len, d_model = dxn.shape
    nt = t_len // tm

    def body(dxn_ref, gate_ref, e_ref, rh_ref, rstd_ref, g_ref, wg_ref, wo_ref, de_ref, dgl_ref, dy_ref, dr_ref, acc_ref):
        @pl.when(pl.program_id(0) == 0)
        def _():
            acc_ref[...] = jnp.zeros_like(acc_ref)

        d = dxn_ref[...]
        gt = gate_ref[...].astype(F32)
        rhat = rh_ref[...].astype(F32)
        de_ref[...] = (d * gt).astype(BF16)
        dgl = d * e_ref[...].astype(F32) * gt * (1.0 - gt)
        dglb = dgl.astype(BF16)
        dgl_ref[...] = dglb
        dh = d + _dot_nt(dglb, wg_ref[...])
        dr = _norm_bwd(dh * g_ref[...], rhat, rstd_ref[:, 0:1])
        dr_ref[...] = dr
        dy_ref[...] = _dot_nt(dr.astype(BF16), wo_ref[...]).astype(BF16)
        acc_ref[0:1, :] += _sum0(dgl)
        acc_ref[1:2, :] += _sum0(dh * rhat)
        acc_ref[2:3, :] += _sum0(dh)
        acc_ref[3:4, :] += _sum0(dr)

    row = lambda cols: pl.BlockSpec((tm, cols), lambda i: (i, 0))
    return _call(
        body, name="post_bwd", grid=(nt,), rider=rider,
        out_shape=(jax.ShapeDtypeStruct((t_len, d_model), BF16), jax.ShapeDtypeStruct((t_len, d_model), BF16),
                   jax.ShapeDtypeStruct((t_len, d_model), BF16), jax.ShapeDtypeStruct((t_len, d_model), F32),
                   jax.ShapeDtypeStruct((SUBLANE, d_model), F32)),
        in_specs=[row(d_model), row(d_model), row(d_model), row(d_model), row(LANE), _const(ln_g.shape), _const(wg.shape),
                  _const(w_out.shape)],
        out_specs=(row(d_model), row(d_model), row(d_model), row(d_model), _const((SUBLANE, d_model))),
        args=[dxn, gate, e, rh, rstd, ln_g, wg, w_out])


def _dz_store(dz_ref, dbin_ref, width):
    def store(k, v):
        cols = slice(k * width, (k + 1) * width)
        vb = v.astype(BF16)
        dz_ref[:, cols] = vb
        dbin_ref[0:1, cols] += _sum0(v)
        return vb
    return store


def _even_bwd(dy, z, a1, sg, dr, w_in, conv_w, ln_a_g, ln_a_b, ln_v_g, ln_v_b, wcat_t, *, seq, tm, rider=None):
    t_len, d_model = dr.shape
    w = d_model // 2
    nt, tps, nb = t_len // tm, seq // tm, tm // LANE
    n_heads = 2 * (w // LANE)

    def body(dy_ref, z_ref, a1_ref, sg_ref, dr_ref, w_ref, cw_ref, lag_ref, lab_ref, lvg_ref, lvb_ref, wct_ref,
             dz_ref, dx_ref, dcw_ref, vec_ref, dbin_ref, dws_ref, dbs_ref, da1_s, a0_s, da0_s, cw_acc, shift_s):
        i = pl.program_id(0)
        tile = nt - 1 - i

        @pl.when(i == 0)
        def _():
            vec_ref[...] = jnp.zeros_like(vec_ref)
            dbin_ref[...] = jnp.zeros_like(dbin_ref)
            dws_ref[...] = jnp.zeros_like(dws_ref)
            dbs_ref[...] = jnp.zeros_like(dbs_ref)
            cw_acc[...] = jnp.zeros_like(cw_acc)

        @pl.when((tile + 1) % tps == 0)
        def _():
            da1_s[tm:tm + HALO_A, :] = jnp.zeros((HALO_A, w), F32)

        zp = lambda k: z_ref[:, k * w:(k + 1) * w].astype(F32)
        store = _dz_store(dz_ref, dbin_ref, w)

        dg = dy_ref[:, w:2 * w].astype(F32)
        u, v, gg = zp(3), zp(4), zp(5)
        sgv = sg_ref[...].astype(F32)
        gelu_u, dgelu_u = _gelu_grad(u)
        silu_gg, dsilu_gg = _silu_grad(gg)
        dzb5 = store(5, dg * gelu_u * sgv * dsilu_gg)
        t1 = dg * silu_gg
        dzb3 = store(3, t1 * sgv * dgelu_u)
        dsg = t1 * gelu_u
        gelu_v, dgelu_v = _gelu_grad(v)
        vh, rstd_v = _norm(gelu_v)
        v2 = vh * lvg_ref[...] + lvb_ref[...]
        low = _head_low_mask(nb * LANE)
        for j in range(w // LANE):
            dt = _blocks_to_lanes(dsg, j, nb)
            d_lo = jnp.where(low, dt, 0.0).astype(BF16)
            d_hi = jnp.where(low, 0.0, dt).astype(BF16)
            v2t = _blocks_to_lanes(v2, j, nb).astype(BF16)
            dv2t = _dot(wct_ref[j], jnp.concatenate([d_lo, d_hi], axis=0))
            for n in range(nb):
                da0_s[n * LANE:(n + 1) * LANE, j * LANE:(j + 1) * LANE] = dv2t[:, n * LANE:(n + 1) * LANE]
            dws_ref[2 * j] += _dot_nt(d_lo, v2t)
            dws_ref[2 * j + 1] += _dot_nt(d_hi, v2t)
            bsum = dt[:, 0:LANE]
            for n in range(1, nb):
                bsum = bsum + dt[:, n * LANE:(n + 1) * LANE]
            dbs_ref[:, j * LANE:(j + 1) * LANE] += bsum
        dv2 = da0_s[...]
        vec_ref[3:4, :] += _sum0(dv2 * vh)
        vec_ref[4:5, :] += _sum0(dv2)
        dzb4 = store(4, _norm_bwd(dv2 * lvg_ref[...], vh, rstd_v) * dgelu_v)
        dx_b = DEEPNORM_ALPHA * dr_ref[...]
        for k, dzb in ((3, dzb3), (4, dzb4), (5, dzb5)):
            dx_b = dx_b + _dot_nt(dzb, w_ref[:, k * w:(k + 1) * w])
        dx_ref[...] = dx_b

        da = dy_ref[:, 0:w].astype(F32)
        a_val, a_glu, a_gate = zp(0), zp(1), zp(2)
        s_glu = jax.nn.sigmoid(a_glu)
        a0_s[...] = a_val * s_glu
        ah, rstd_a = _norm(a1_ref[...])
        silu_a2, dsilu_a2 = _silu_grad(ah * lag_ref[...] + lab_ref[...])
        silu_ag, dsilu_ag = _silu_grad(a_gate)
        dzb2 = store(2, da * silu_a2 * dsilu_ag)
        da2 = da * silu_ag * dsilu_a2
        vec_ref[1:2, :] += _sum0(da2 * ah)
        vec_ref[2:3, :] += _sum0(da2)
        da1 = _norm_bwd(da2 * lag_ref[...], ah, rstd_a)
        vec_ref[0:1, :] += _sum0(da1)
        da1_s[0:tm, :] = da1
        window = _shifted_windows(da1_s, shift_s, tm + HALO_A)
        for r in range(0, tm, ROW_CHUNK):
            a0c = a0_s[r:r + ROW_CHUNK, :]
            acc = jnp.zeros((ROW_CHUNK, w), F32)
            for k in range(CONV_A_WIDTH):
                d = window(r + (CONV_A_WIDTH - 1) - k)
                acc = acc + cw_ref[k:k + 1, :] * d
                pw = a0c * d
                p8 = pw[0:SUBLANE]
                for q in range(1, ROW_CHUNK // SUBLANE):
                    p8 = p8 + pw[q * SUBLANE:(q + 1) * SUBLANE]
                cw_acc[k * SUBLANE:(k + 1) * SUBLANE, :] += p8
            da0_s[r:r + ROW_CHUNK, :] = acc
        da1_s[tm:tm + HALO_A, :] = da1_s[0:HALO_A, :]
        da0 = da0_s[...]
        dzb0 = store(0, da0 * s_glu)
        dzb1 = store(1, da0 * a_val * s_glu * (1.0 - s_glu))

        dx = dx_ref[...]
        for k, dzb in ((0, dzb0), (1, dzb1), (2, dzb2)):
            dx = dx + _dot_nt(dzb, w_ref[:, k * w:(k + 1) * w])
        dx_ref[...] = dx

        @pl.when(i == nt - 1)
        def _():
            for k in range(CONV_A_WIDTH):
                dcw_ref[k:k + 1, :] = _sum0(cw_acc[k * SUBLANE:(k + 1) * SUBLANE, :])
            keep = (lax.broadcasted_iota(jnp.int32, (LANE, LANE), 0) >= lax.broadcasted_iota(jnp.int32, (LANE, LANE), 1))
            for hd in range(n_heads):
                dws_ref[hd] = jnp.where(keep, dws_ref[hd], 0.0)

    rev = lambda cols: pl.BlockSpec((tm, cols), lambda i: (nt - 1 - i, 0))
    return _call(
        body, name="even_bwd", grid=(nt,), rider=rider,
        out_shape=(jax.ShapeDtypeStruct((t_len, 6 * w), BF16), jax.ShapeDtypeStruct((t_len, d_model), F32),
                   jax.ShapeDtypeStruct((CONV_A_WIDTH, w), F32), jax.ShapeDtypeStruct((SUBLANE, w), F32),
                   jax.ShapeDtypeStruct((SUBLANE, 6 * w), F32), jax.ShapeDtypeStruct((n_heads, LANE, LANE), F32),
                   jax.ShapeDtypeStruct((LANE, w), F32)),
        in_specs=[rev(d_model), rev(6 * w), rev(w), rev(w), rev(d_model), _const(w_in.shape), _const(conv_w.shape),
                  _const(ln_a_g.shape), _const(ln_a_b.shape), _const(ln_v_g.shape), _const(ln_v_b.shape), _const(wcat_t.shape)],
        out_specs=(rev(6 * w), rev(d_model), _const((CONV_A_WIDTH, w)), _const((SUBLANE, w)), _const((SUBLANE, 6 * w)),
                   _const((n_heads, LANE, LANE)), _const((LANE, w))),
        scratch=[pltpu.VMEM((tm + HALO_A, w), F32), pltpu.VMEM((tm, w), F32), pltpu.VMEM((tm, w), F32),
                 pltpu.VMEM((CONV_A_WIDTH * SUBLANE, w), F32), pltpu.VMEM((SUBLANE - 1, tm + HALO_A, w), F32)],
        args=[dy, z, a1, sg, dr, w_in, conv_w, ln_a_g, ln_a_b, ln_v_g, ln_v_b, wcat_t])


def _odd_bwd(dy, z, pooled, q, dr, w_in, w_pool, pool_scale, conv_w, *, seq, tm, rider=None):
    t_len, d_model = dr.shape
    w = d_model // 2
    nt, tps = t_len // tm, seq // tm
    n_groups = len(POOL_WINDOWS)

    def body(dy_ref, z_ref, pooled_ref, q_ref, dr_ref, w_ref, wp_ref, ps_ref, cw_ref,
             dz_ref, dx_ref, dwp_ref, vec_ref, dbin_ref, dm_s, dq_s):
        i = pl.program_id(0)
        tile = nt - 1 - i

        @pl.when(i == 0)
        def _():
            dwp_ref[...] = jnp.zeros_like(dwp_ref)
            vec_ref[...] = jnp.zeros_like(vec_ref)
            dbin_ref[...] = jnp.zeros_like(dbin_ref)

        @pl.when((tile + 1) % tps == 0)
        def _():
            dm_s[tm:tm + HALO_C, :] = jnp.zeros((HALO_C, w), F32)
            dq_s[tm:tm + HALO_D, :] = jnp.zeros((HALO_D, w), F32)

        zp = lambda k: z_ref[:, k * w:(k + 1) * w].astype(F32)
        store = _dz_store(dz_ref, dbin_ref, w)

        dc = dy_ref[:, 0:w].astype(F32)
        c_gate = zp(1)
        silu_c, dsilu_c = _silu_grad(c_gate)
        dcs = dc * silu_c
        dvg_parts, dcg_parts = [], []
        for gi, win in enumerate(POOL_WINDOWS):
            cols = slice(gi * LANE, (gi + 1) * LANE)
            pooled_g = pooled_ref[:, cols]
            wp = wp_ref[gi]
            cpre = _dot(pooled_g, wp)
            scale = ps_ref[:, cols]
            vec_ref[0:1, cols] += _sum0(dcs[:, cols] * cpre)
            dcg_parts.append(dc[:, cols] * cpre * scale * dsilu_c[:, cols])
            dcp = (dcs[:, cols] * scale).astype(BF16)
            dwp_ref[gi] += _dot_tn(pooled_g, dcp)
            dpooled = _dot_nt(dcp, wp)
            dm_s[0:tm, cols] = dpooled * _pool_inverse(tile, tm, seq, win)
            s = dm_s[pl.ds(0, tm), cols]
            for j in range(1, win):
                s = s + dm_s[pl.ds(j, tm), cols]
            dvg_parts.append(s - dpooled)
        dm_s[tm:tm + HALO_C, :] = dm_s[0:HALO_C, :]
        dzb0 = store(0, jnp.concatenate(dvg_parts, axis=1))
        dzb1 = store(1, jnp.concatenate(dcg_parts, axis=1))

        dd = dy_ref[:, w:2 * w].astype(F32)
        d_h, d_b, d_c, d_gate = zp(2), zp(3), zp(4), zp(5)
        qv = q_ref[...].astype(F32)
        silu_d, dsilu_d = _silu_grad(d_gate)
        dzb5 = store(5, dd * d_b * qv * dsilu_d)
        dzb3 = store(3, dd * qv * silu_d)
        dq_s[0:tm, :] = dd * d_b * silu_d
        hc = d_c * d_h
        dhc = jnp.zeros((tm, w), F32)
        for k in range(CONV_D_WIDTH):
            d = dq_s[pl.ds((CONV_D_WIDTH - 1) - k, tm), :]
            dhc = dhc + cw_ref[k:k + 1, :] * d
            vec_ref[1 + k:2 + k, :] += _sum0(hc * d)
        dq_s[tm:tm + HALO_D, :] = dq_s[0:HALO_D, :]
        dzb2 = store(2, dhc * d_c)
        dzb4 = store(4, dhc * d_h)

        dx = DEEPNORM_ALPHA * dr_ref[...]
        for k, dzb in enumerate((dzb0, dzb1, dzb2, dzb3, dzb4, dzb5)):
            dx = dx + _dot_nt(dzb, w_ref[:, k * w:(k + 1) * w])
        dx_ref[...] = dx

    rev = lambda cols: pl.BlockSpec((tm, cols), lambda i: (nt - 1 - i, 0))
    return _call(
        body, name="odd_bwd", grid=(nt,), rider=rider,
        out_shape=(jax.ShapeDtypeStruct((t_len, 6 * w), BF16), jax.ShapeDtypeStruct((t_len, d_model), F32),
                   jax.ShapeDtypeStruct((n_groups, LANE, LANE), F32), jax.ShapeDtypeStruct((SUBLANE, w), F32),
                   jax.ShapeDtypeStruct((SUBLANE, 6 * w), F32)),
        in_specs=[rev(d_model), rev(6 * w), rev(w), rev(w), rev(d_model), _const(w_in.shape), _const(w_pool.shape),
                  _const(pool_scale.shape), _const(conv_w.shape)],
        out_specs=(rev(6 * w), rev(d_model), _const((n_groups, LANE, LANE)), _const((SUBLANE, w)), _const((SUBLANE, 6 * w))),
        scratch=[pltpu.VMEM((tm + HALO_C, w), F32), pltpu.VMEM((tm + HALO_D, w), F32)],
        args=[dy, z, pooled, q, dr, w_in, w_pool, pool_scale, conv_w])


def _weight_grad(name, a, b, a_layer=None, bn=None, a_cols=None, rider=None):
    if a_layer is None:
        t_len, m = a.shape
        col = 0
        if a_cols is not None:
            col, m = a_cols
        a_spec = lambda tk: pl.BlockSpec((tk, m), lambda n, k: (k, col))
    else:
        _, t_len, m = a.shape
        a_spec = lambda tk: pl.BlockSpec((None, tk, m), lambda n, k: (a_layer, k, 0))
    n_cols = b.shape[1]
    bn = n_cols if bn is None else bn
    tk = min(t_len, 1024)
    n_k = t_len // tk

    def body(a_ref, b_ref, o_ref, acc):
        k = pl.program_id(1)

        @pl.when(k == 0)
        def _():
            acc[...] = jnp.zeros_like(acc)
        acc[...] += _dot_tn(a_ref[...].astype(BF16), b_ref[...].astype(BF16))

        @pl.when(k == n_k - 1)
        def _():
            o_ref[...] = acc[...].astype(BF16)

    outs = _call(
        body, name=name, grid=(n_cols // bn, n_k), out_shape=[jax.ShapeDtypeStruct((m, n_cols), BF16)],
        in_specs=[a_spec(tk), pl.BlockSpec((tk, bn), lambda n, k: (k, n))],
        out_specs=[pl.BlockSpec((m, bn), lambda n, k: (0, n))], scratch=[pltpu.VMEM((m, bn), F32)],
        args=[a, b], rider=rider)
    return outs[0] if rider is None else outs


def _weight_grads_post(name, y, dr, h, dgl, p_all, layer, de):
    t_len, d_model = y.shape
    d_ple = p_all.shape[-1]
    tk = min(t_len, 1024)
    n_k = t_len // tk

    def body(y_ref, dr_ref, h_ref, dgl_ref, p_ref, de_ref, o_out, o_gate, o_ple, acc_out, acc_gate, acc_ple):
        k = pl.program_id(0)

        @pl.when(k == 0)
        def _():
            acc_out[...] = jnp.zeros_like(acc_out)
            acc_gate[...] = jnp.zeros_like(acc_gate)
            acc_ple[...] = jnp.zeros_like(acc_ple)
        acc_out[...] += _dot_tn(y_ref[...], dr_ref[...].astype(BF16))
        acc_gate[...] += _dot_tn(h_ref[...], dgl_ref[...])
        acc_ple[...] += _dot_tn(p_ref[...].astype(BF16), de_ref[...])

        @pl.when(k == n_k - 1)
        def _():
            o_out[...] = acc_out[...].astype(BF16)
            o_gate[...] = acc_gate[...].astype(BF16)
            o_ple[...] = acc_ple[...].astype(BF16)

    row = lambda cols: pl.BlockSpec((tk, cols), lambda k: (k, 0))
    return pl.pallas_call(
        body, name=name, grid=(n_k,),
        out_shape=(jax.ShapeDtypeStruct((d_model, d_model), BF16), jax.ShapeDtypeStruct((d_model, d_model), BF16),
                   jax.ShapeDtypeStruct((d_ple, d_model), BF16)),
        in_specs=[row(d_model), row(d_model), row(d_model), row(d_model),
                  pl.BlockSpec((None, tk, d_ple), lambda k: (layer, k, 0)), row(d_model)],
        out_specs=(_const((d_model, d_model)), _const((d_model, d_model)), _const((d_ple, d_model))),
        scratch_shapes=[pltpu.VMEM((d_model, d_model), F32), pltpu.VMEM((d_model, d_model), F32),
                        pltpu.VMEM((d_ple, d_model), F32)],
        compiler_params=_params(1),
    )(y, dr, h, dgl, p_all, de)


def _adamw_reduce(name, parts, w, m, v, rows_per_block):
    n_rows, n_cols = w.shape
    br = rows_per_block
    n_parts = parts.shape[0]

    def body(p_ref, w_ref, m_ref, v_ref, g_ref, d_ref, nm_ref, nv_ref):
        g = p_ref[0].astype(F32)
        for k in range(1, n_parts):
            g = g + p_ref[k].astype(F32)
        nm = ADAM_B1 * m_ref[...] + (1.0 - ADAM_B1) * g
        nv = ADAM_B2 * v_ref[...] + (1.0 - ADAM_B2) * (g * g)
        m_hat = nm / (1.0 - ADAM_B1 ** ADAM_STEP)
        v_hat = nv / (1.0 - ADAM_B2 ** ADAM_STEP)
        g_ref[...] = g
        d_ref[...] = -ADAM_LR * (m_hat / (jnp.sqrt(v_hat) + ADAM_EPS) + ADAM_WD * w_ref[...])
        nm_ref[...] = nm
        nv_ref[...] = nv

    blk = pl.BlockSpec((br, n_cols), lambda i: (i, 0))
    shp = jax.ShapeDtypeStruct((n_rows, n_cols), F32)
    return pl.pallas_call(
        body, name=name, grid=(n_rows // br,), out_shape=(shp, shp, shp, shp),
        in_specs=[pl.BlockSpec((n_parts, br, n_cols), lambda i: (0, i, 0)), blk, blk, blk], out_specs=(blk, blk, blk, blk),
        compiler_params=_params(1),
    )(parts, w, m, v)


def _pack_rows(flat_parts, pad_to=None, width=LANE):
    flat = jnp.concatenate([a.reshape(-1) for a in flat_parts])
    if pad_to is not None and pad_to > flat.shape[0]:
        flat = jnp.concatenate([flat, jnp.zeros((pad_to - flat.shape[0],), flat.dtype)])
    return flat.reshape(-1, width)


def _unpack_rows(packed, shapes):
    flat = packed.reshape(-1)
    out, off = [], 0
    for s in shapes:
        n = math.prod(s)
        out.append(flat[off:off + n].reshape(s))
        off += n
    return out


def _to_dest_major(full):
    lead, last = full.shape[:-1], full.shape[-1]
    t = full.reshape(lead + (N_DEV, last // N_DEV))
    return jnp.moveaxis(t, -2, 0).reshape(N_DEV, -1)


def _from_source_major(blocks, shard_shape):
    t = blocks.reshape((N_DEV,) + tuple(shard_shape))
    t = jnp.moveaxis(t, 0, -2)
    return t.reshape(tuple(shard_shape[:-1]) + (N_DEV * shard_shape[-1],))


def _block_rows(n_rows, n_cols, target_elems=96 * 1024):
    best = None
    for br in range(SUBLANE, n_rows + 1, SUBLANE):
        if n_rows % br == 0 and br * n_cols <= target_elems:
            best = br
    return n_rows if best is None else best


def kernel(x, p, w_in_e, b_in_e, conv_a_w, conv_a_b, ln_a_g, ln_a_b, ln_v_g, ln_v_b, w_s, b_s, w_out_e, b_out_e, w_in_o, b_in_o, w_pool, pool_scale, conv_d_w, w_out_o, b_out_o, ln_g, ln_b, w_ple, w_ple_gate, b_ple_gate, loss_target, m_w_in_e, m_b_in_e, m_conv_a_w, m_conv_a_b, m_ln_a_g, m_ln_a_b, m_ln_v_g, m_ln_v_b, m_w_s, m_b_s, m_w_out_e, m_b_out_e, m_w_in_o, m_b_in_o, m_w_pool, m_pool_scale, m_conv_d_w, m_w_out_o, m_b_out_o, m_ln_g, m_ln_b, m_w_ple, m_w_ple_gate, m_b_ple_gate, v_w_in_e, v_b_in_e, v_conv_a_w, v_conv_a_b, v_ln_a_g, v_ln_a_b, v_ln_v_g, v_ln_v_b, v_w_s, v_b_s, v_w_out_e, v_b_out_e, v_w_in_o, v_b_in_o, v_w_pool, v_pool_scale, v_conv_d_w, v_w_out_o, v_b_out_o, v_ln_g, v_ln_b, v_w_ple, v_w_ple_gate, v_b_ple_gate):
    weights = dict(w_in_e=w_in_e, b_in_e=b_in_e, conv_a_w=conv_a_w, conv_a_b=conv_a_b, ln_a_g=ln_a_g, ln_a_b=ln_a_b,
                   ln_v_g=ln_v_g, ln_v_b=ln_v_b, w_s=w_s, b_s=b_s, w_out_e=w_out_e, b_out_e=b_out_e, w_in_o=w_in_o,
                   b_in_o=b_in_o, w_pool=w_pool, pool_scale=pool_scale, conv_d_w=conv_d_w, w_out_o=w_out_o,
                   b_out_o=b_out_o, ln_g=ln_g, ln_b=ln_b, w_ple=w_ple, w_ple_gate=w_ple_gate, b_ple_gate=b_ple_gate)
    mom_m = dict(w_in_e=m_w_in_e, b_in_e=m_b_in_e, conv_a_w=m_conv_a_w, conv_a_b=m_conv_a_b, ln_a_g=m_ln_a_g,
                 ln_a_b=m_ln_a_b, ln_v_g=m_ln_v_g, ln_v_b=m_ln_v_b, w_s=m_w_s, b_s=m_b_s, w_out_e=m_w_out_e,
                 b_out_e=m_b_out_e, w_in_o=m_w_in_o, b_in_o=m_b_in_o, w_pool=m_w_pool, pool_scale=m_pool_scale,
                 conv_d_w=m_conv_d_w, w_out_o=m_w_out_o, b_out_o=m_b_out_o, ln_g=m_ln_g, ln_b=m_ln_b, w_ple=m_w_ple,
                 w_ple_gate=m_w_ple_gate, b_ple_gate=m_b_ple_gate)
    mom_v = dict(w_in_e=v_w_in_e, b_in_e=v_b_in_e, conv_a_w=v_conv_a_w, conv_a_b=v_conv_a_b, ln_a_g=v_ln_a_g,
                 ln_a_b=v_ln_a_b, ln_v_g=v_ln_v_g, ln_v_b=v_ln_v_b, w_s=v_w_s, b_s=v_b_s, w_out_e=v_w_out_e,
                 b_out_e=v_b_out_e, w_in_o=v_w_in_o, b_in_o=v_b_in_o, w_pool=v_w_pool, pool_scale=v_pool_scale,
                 conv_d_w=v_conv_d_w, w_out_o=v_w_out_o, b_out_o=v_b_out_o, ln_g=v_ln_g, ln_b=v_ln_b, w_ple=v_w_ple,
                 w_ple_gate=v_w_ple_gate, b_ple_gate=v_b_ple_gate)
    names = tuple(weights)

    batch, seq, d_model = x.shape
    t_len = batch * seq
    w = d_model // 2
    n_even = w_in_e.shape[0]
    n_odd = w_in_o.shape[0]
    depth = ln_g.shape[0]
    d_ple = p.shape[-1]
    n_heads = w_s.shape[1]
    tm = 512 if seq % 512 == 0 and seq >= 1024 else seq // 2
    in_cols = w_in_e.shape[-1]
    out_rows = w_out_e.shape[1]
    ple_cols = w_ple.shape[-1]
    gate_rows = w_ple_gate.shape[1]

    sh_shapes = [weights[n].shape for n in SH_NAMES]
    sh_len = sum(math.prod(s) for s in sh_shapes)
    sh_pad = -(-sh_len // (SUBLANE * LANE)) * (SUBLANE * LANE)
    sh_rows = sh_pad // LANE
    sds = jax.ShapeDtypeStruct
    w_in16 = (w_in_e.astype(BF16), w_in_o.astype(BF16))
    w_out16 = (w_out_e.astype(BF16), w_out_o.astype(BF16))
    w_ple16, w_gate16 = w_ple.astype(BF16), w_ple_gate.astype(BF16)

    kinds = ("in", "out", "ple", "gate")

    def weight_entries(i, which=kinds):
        j, par = i // 2, i % 2
        all_four = {"in": (w_in16[par][j], _whole_view, sds((d_model, N_DEV * in_cols), BF16), _axis_view(1, in_cols)),
                    "out": (w_out16[par][j], _whole_view, sds((N_DEV * out_rows, d_model), BF16), _axis_view(0, out_rows)),
                    "ple": (w_ple16[i], _whole_view, sds((d_ple, N_DEV * ple_cols), BF16), _axis_view(1, ple_cols)),
                    "gate": (w_gate16[i], _whole_view, sds((N_DEV * gate_rows, d_model), BF16), _axis_view(0, gate_rows))}
        return [((i, k), all_four[k]) for k in which]

    fwd_riders = {("mixer", 0): weight_entries(0, kinds[1:]) + weight_entries(1, kinds[:1]),
                  ("post", 0): weight_entries(1, kinds[1:])}
    for i in range(1, depth - 1):
        if i % 2:
            fwd_riders[("mixer", i)] = weight_entries(i + 1, kinds[:1])
            fwd_riders[("post", i)] = weight_entries(i + 1, kinds[1:])
        else:
            fwd_riders[("mixer", i)] = weight_entries(i + 1)
    layer_w = {}

    def carried(where):
        tagged = fwd_riders.get(where)
        if tagged is None:
            return None, lambda landed: None
        return _Rider([e for _, e in tagged]), lambda landed: layer_w.update(zip([t for t, _ in tagged], landed))

    first = _exchange("gather_first", [e for _, e in weight_entries(0, kinds[:1])] + [
        (_pack_rows([weights[n] for n in SH_NAMES], sh_pad), _whole_view, sds((N_DEV, sh_rows, LANE), F32), _slot_view())])
    layer_w[(0, "in")] = first[0]
    sh_flat = first[1].reshape(N_DEV, sh_pad)
    full_small, off = {}, 0
    for n, s in zip(SH_NAMES, sh_shapes):
        size = math.prod(s)
        full_small[n] = _from_source_major(sh_flat[:, off:off + size], s)
        off += size

    tril = jnp.tril(jnp.ones((LANE, LANE), dtype=bool))
    ws_m = jnp.where(tril[None, None], w_s, 0.0)
    pair = lambda t: jnp.concatenate([t[:, 0::2], t[:, 1::2]], axis=-1).astype(BF16)
    wcat = pair(ws_m)
    wcat_t = pair(jnp.swapaxes(ws_m, -1, -2))
    bs_full = jnp.repeat(jnp.swapaxes(b_s, -1, -2), w // n_heads, axis=-1)
    row2 = lambda a, j: a[j][None, :]

    x2 = x.reshape(t_len, d_model)
    p3 = p.reshape(depth, t_len, d_ple)
    lt2 = loss_target.reshape(t_len, d_model)

    xs, saved = [x2], []
    dxn = sse = None
    for i in range(depth):
        j = i // 2
        last = i == depth - 1
        rider, file_weights = carried(("mixer", i))
        if i % 2 == 0:
            outs = _even_fwd(xs[i], layer_w[(i, "in")], row2(b_in_e, j), full_small["conv_a_w"][j], row2(conv_a_b, j),
                             row2(ln_a_g, j), row2(ln_a_b, j), row2(ln_v_g, j), row2(ln_v_b, j), wcat[j], bs_full[j],
                             seq=seq, tm=tm, rider=rider)
            b_out = row2(b_out_e, j)
        else:
            outs = _odd_fwd(xs[i], layer_w[(i, "in")], row2(full_small["b_in_o"], j), w_pool[j].astype(BF16),
                            row2(full_small["pool_scale"], j), full_small["conv_d_w"][j], seq=seq, tm=tm, rider=rider)
            b_out = row2(full_small["b_out_o"], j)
        z, y, s1, s2 = outs[:4]
        file_weights(outs[4:])
        rider, file_weights = carried(("post", i))
        outs = _post_fwd(y, xs[i], p3, i, layer_w[(i, "out")], b_out, row2(ln_g, i), row2(ln_b, i), layer_w[(i, "gate")],
                         row2(b_ple_gate, i), layer_w[(i, "ple")], lt2 if last else None, tm=tm, rider=rider)
        xn, rh, h, rstd, gate, e = outs[:6]
        if last:
            dxn, sse = outs[6:8]
        file_weights(outs[8 if last else 6:])
        xs.append(xn)
        saved.append(dict(z=z, y=y, s1=s1, s2=s2, rh=rh, h=h, rstd=rstd, gate=gate, e=e))


    recv = {"w_in_e": None, "w_in_o": None, "w_out_e": None, "w_out_o": None, "w_ple": None, "w_ple_gate": None}

    def grad_entry(i, kind, g, half=None):
        j, par = i // 2, i % 2
        sfx = "_o" if par else "_e"
        name, src_view, slot = {"in": ("w_in" + sfx, _axis_view(1, in_cols), j), "out": ("w_out" + sfx, _axis_view(0, out_rows), j),
                                "ple": ("w_ple", _axis_view(1, ple_cols), i), "gate": ("w_ple_gate", _axis_view(0, gate_rows), i)}[kind]
        dst = recv[name] if recv[name] is not None else sds((N_DEV,) + weights[name].shape, BF16)
        if half is None:
            dst_view = _slot_view(slot)
        else:
            dst_view = lambda ref, d: ref.at[d, slot, pl.ds(half[0] * half[1], half[1]), :]
        return name, (g, src_view, dst, dst_view)

    def ride(tagged):
        if not tagged:
            return None, lambda landed: None
        return _Rider([e for _, e in tagged]), lambda landed: recv.update(zip([n for n, _ in tagged], landed))

    small = {n: [None] * weights[n].shape[0] for n in REP_NAMES + SH_NAMES}
    grads = {}
    for i in reversed(range(depth)):
        j, par = i // 2, i % 2
        sv = saved[i]
        w_in = layer_w[(i, "in")]
        split = par == 1 and (i + 1, "in") in grads
        rider, file_landed = ride([grad_entry(i + 1, k, grads.pop((i + 1, k))) for k in kinds[1:]] if split else [])
        outs = _post_bwd(dxn, sv["gate"], sv["e"], sv["rh"], sv["rstd"], row2(ln_g, i), layer_w[(i, "gate")],
                         layer_w[(i, "out")], tm=tm, rider=rider)
        de, dgl, dy, dr, acc = outs[:5]
        file_landed(outs[5:])
        small["b_ple_gate"][i], small["ln_g"][i], small["ln_b"][i] = acc[0], acc[1], acc[2]
        small["b_out_o" if par else "b_out_e"][j] = acc[3]
        rider, file_landed = ride([grad_entry(i + 1, k, grads.pop((i + 1, k))) for k in kinds if (i + 1, k) in grads])
        if par == 0:
            outs = _even_bwd(dy, sv["z"], sv["s1"], sv["s2"], dr, w_in, full_small["conv_a_w"][j],
                             row2(ln_a_g, j), row2(ln_a_b, j), row2(ln_v_g, j), row2(ln_v_b, j), wcat_t[j], seq=seq, tm=tm,
                             rider=rider)
            dz, dx, dcw, vec, dbin, dws, dbs = outs[:7]
            file_landed(outs[7:])
            small["conv_a_w"][j], small["conv_a_b"][j] = dcw, vec[0]
            small["ln_a_g"][j], small["ln_a_b"][j], small["ln_v_g"][j], small["ln_v_b"][j] = vec[1], vec[2], vec[3], vec[4]
            small["b_in_e"][j], small["w_s"][j] = dbin[0], dws
            small["b_s"][j] = _head_sums(dbs, n_heads)
        else:
            outs = _odd_bwd(dy, sv["z"], sv["s1"], sv["s2"], dr, w_in, w_pool[j].astype(BF16),
                            row2(full_small["pool_scale"], j), full_small["conv_d_w"][j], seq=seq, tm=tm, rider=rider)
            dz, dx, dwp, vec, dbin = outs[:5]
            file_landed(outs[5:])
            small["w_pool"][j], small["pool_scale"][j], small["conv_d_w"][j] = dwp, vec[0], vec[1:1 + CONV_D_WIDTH]
            small["b_in_o"][j] = dbin[0]
        tag = f"_l{i}"
        grads[(i, "out")], grads[(i, "gate")], grads[(i, "ple")] = _weight_grads_post(
            "dw_post" + tag, sv["y"], dr, sv["h"], dgl, p3, i, de)
        if i > 0:
            grads[(i, "in")] = _weight_grad("dw_in" + tag, xs[i], dz, bn=in_cols * N_DEV // 2)
        dxn = dx
    grad_x = dxn.reshape(batch, seq, d_model)

    half_rows = d_model // 2
    rider, file_landed = ride([grad_entry(0, k, grads.pop((0, k))) for k in kinds[1:]])
    outs = _weight_grad("dw_in_l0_top", xs[0], dz, a_cols=(0, half_rows), rider=rider)
    file_landed(outs[1:])
    rider, file_landed = ride([grad_entry(0, "in", outs[0], half=(0, half_rows))])
    outs = _weight_grad("dw_in_l0_bottom", xs[0], dz, a_cols=(1, half_rows), rider=rider)
    file_landed(outs[1:])

    small_full = {n: jnp.stack(small[n]) for n in small}
    sh_part = jnp.concatenate([_to_dest_major(small_full[n]) for n in SH_NAMES], axis=1)
    sh_part = jnp.concatenate([sh_part, jnp.zeros((N_DEV, sh_pad - sh_len), F32)], axis=1).reshape(N_DEV, sh_rows, LANE)
    name, entry = grad_entry(0, "in", outs[0], half=(1, half_rows))
    landed = _exchange("exchange_last", [entry, (sh_part, _slot_view(), sds((N_DEV, sh_rows, LANE), F32), _slot_view())])
    recv[name] = landed[0]

    rep_width = 4 * LANE
    rep_len = sum(math.prod(weights[n].shape) for n in REP_NAMES)
    rep_block = N_DEV * SUBLANE
    rep_pad = -(-(rep_len + 1) // (rep_block * rep_width)) * (rep_block * rep_width)
    rep_sum = _allreduce_rows("allreduce_replicated",
                              _pack_rows([small_full[n] for n in REP_NAMES] + [sse[0:1, 0]], rep_pad, rep_width))
    loss = (0.5 / d_model) * rep_sum.reshape(-1)[rep_len]

    results = {}
    for n, parts in recv.items():
        shp = weights[n].shape
        rows, cols = math.prod(shp[:-1]), shp[-1]
        two = lambda a: a.reshape(rows, cols)
        outs = _adamw_reduce("adamw_" + n, parts.reshape(N_DEV, rows, cols), two(weights[n]), two(mom_m[n]), two(mom_v[n]),
                             _block_rows(rows, cols))
        results[n] = [o.reshape(shp) for o in outs]
    pack_sh = lambda d: _pack_rows([d[n] for n in SH_NAMES], sh_pad)
    outs = _adamw_reduce("adamw_small_sharded", landed[1], pack_sh(weights), pack_sh(mom_m), pack_sh(mom_v), sh_rows)
    for n, *vals in zip(SH_NAMES, *[_unpack_rows(o, sh_shapes) for o in outs]):
        results[n] = vals
    pack_rep = lambda d: _pack_rows([d[n] for n in REP_NAMES], rep_pad, rep_width)
    rep_shapes = [weights[n].shape for n in REP_NAMES]
    outs = _adamw_reduce("adamw_replicated", rep_sum[None], pack_rep(weights), pack_rep(mom_m), pack_rep(mom_v), rep_block)
    for n, *vals in zip(REP_NAMES, *[_unpack_rows(o, rep_shapes) for o in outs]):
        results[n] = vals

    return (loss, grad_x, *[results[n][0] for n in names], *[results[n][1] for n in names],
            *[results[n][2] for n in names], *[results[n][3] for n in names])


def _head_sums(dbs, n_heads):
    t, width = dbs.shape
    return jnp.sum(dbs.reshape(t, n_heads, width // n_heads), axis=-1).T
```

```python
import functools
import math

import jax
import jax.numpy as jnp
from jax import lax
from jax.experimental import pallas as pl
from jax.experimental.pallas import tpu as pltpu

F32 = jnp.float32
BF16 = jnp.bfloat16

N_DEV = 8
DEPTH = 4
LN_EPS = 1e-5
DEEPNORM_ALPHA = (2.0 * DEPTH) ** 0.25
POOL_WINDOWS = (2, 4, 8, 16)
CONV_A_WIDTH = 31
CONV_D_WIDTH = 3
GELU_C = math.sqrt(2.0 / math.pi)
GELU_K = 0.044715

ADAM_LR = 0.001
ADAM_B1 = 0.9
ADAM_B2 = 0.999
ADAM_EPS = 1e-08
ADAM_WD = 0.01
ADAM_STEP = 10

LANE = 128
SUBLANE = 8
HALO_A = 32
HALO_C = 16
HALO_D = 8
ROW_CHUNK = 32
VMEM_LIMIT = 56 * 2**20

ANY = pl.BlockSpec(memory_space=pl.ANY)
MESH = pl.DeviceIdType.MESH

REP_NAMES = ("b_in_e", "conv_a_b", "ln_a_g", "ln_a_b", "ln_v_g", "ln_v_b", "w_s", "b_s", "b_out_e", "w_pool", "ln_g",
             "ln_b", "b_ple_gate")
SH_NAMES = ("conv_a_w", "conv_d_w", "pool_scale", "b_in_o", "b_out_o")


def _params(n_grid_axes):
    return pltpu.CompilerParams(dimension_semantics=("arbitrary",) * n_grid_axes, vmem_limit_bytes=VMEM_LIMIT)


def _const(shape):
    nd = len(shape)
    return pl.BlockSpec(shape, lambda *_: (0,) * nd)


def _dot(a, b):
    return jnp.dot(a, b, preferred_element_type=F32)


def _dot_nt(a, b):
    return lax.dot_general(a, b, (((1,), (1,)), ((), ())), preferred_element_type=F32)


def _dot_tn(a, b):
    return lax.dot_general(a, b, (((0,), (0,)), ((), ())), preferred_element_type=F32)


def _sigmoid(x):
    return jax.nn.sigmoid(x)


def _silu(x):
    return x * _sigmoid(x)


def _silu_grad(x):
    s = _sigmoid(x)
    return x * s, s * (1.0 + x * (1.0 - s))


def _gelu(x):
    return 0.5 * x * (1.0 + jnp.tanh(GELU_C * (x + GELU_K * x * x * x)))


def _gelu_grad(x):
    x2 = x * x
    t = jnp.tanh(GELU_C * x * (1.0 + GELU_K * x2))
    return 0.5 * x * (1.0 + t), 0.5 * (1.0 + t) + 0.5 * x * (1.0 - t * t) * GELU_C * (1.0 + 3.0 * GELU_K * x2)


def _norm(v):
    mu = jnp.mean(v, axis=-1, keepdims=True)
    d = v - mu
    var = jnp.mean(d * d, axis=-1, keepdims=True)
    rstd = lax.rsqrt(var + LN_EPS)
    return d * rstd, rstd


def _norm_bwd(dxh, xh, rstd):
    return rstd * (dxh - jnp.mean(dxh, axis=-1, keepdims=True) - xh * jnp.mean(dxh * xh, axis=-1, keepdims=True))


def _sum0(v):
    return jnp.sum(v, axis=0, keepdims=True)


def _head_low_mask(n_cols):
    lane = lax.broadcasted_iota(jnp.int32, (LANE, n_cols), 1)
    return (lane & (LANE - 1)) < (LANE // 2)


def _blocks_to_lanes(v, j, nb):
    return jnp.concatenate([v[n * LANE:(n + 1) * LANE, j * LANE:(j + 1) * LANE] for n in range(nb)], axis=1)


def _axis_view(axis, size):
    def view(ref, d):
        idx = [slice(None)] * len(ref.shape)
        idx[axis] = pl.ds(pl.multiple_of(d * size, size), size)
        return ref.at[tuple(idx)]
    return view


def _slot_view(*slot):
    return lambda ref, d: ref.at[(d,) + slot]


def _whole_view(ref, d):
    return ref


PEER_BITS = (1, 2, 4, 6, 3, 5, 7)


class _Rider:
    def __init__(self, entries):
        self.n = len(entries)
        self.srcs = [e[0] for e in entries]
        self.src_views = [e[1] for e in entries]
        self.dsts = [e[2] for e in entries]
        self.dst_views = [e[3] for e in entries]
        self.passed = [a for a, d in enumerate(self.dsts) if not isinstance(d, jax.ShapeDtypeStruct)]

    def operands(self):
        return self.srcs + [self.dsts[a] for a in self.passed]

    def out_shape(self):
        return [jax.ShapeDtypeStruct(d.shape, d.dtype) for d in self.dsts]

    def scratch(self):
        return [pltpu.SemaphoreType.DMA((7 * self.n,)), pltpu.SemaphoreType.DMA((7 * self.n,)), pltpu.SemaphoreType.DMA((self.n,))]

    def aliases(self, n_in_before, n_out_before):
        return {n_in_before + self.n + q: n_out_before + a for q, a in enumerate(self.passed)}

    def _copies(self, src, dst, sems):
        send_sems, recv_sems, local_sems = sems
        x, y, c = lax.axis_index("x"), lax.axis_index("y"), lax.axis_index("c")
        me = 4 * x + 2 * y + c
        local = [pltpu.make_async_copy(self.src_views[a](src[a], me), self.dst_views[a](dst[a], me), local_sems.at[a])
                 for a in range(self.n)]
        sends, recvs = [], []
        for ki, k in enumerate(PEER_BITS):
            px, py, pc = x ^ (k >> 2), y ^ ((k >> 1) & 1), c ^ (k & 1)
            peer = 4 * px + 2 * py + pc
            for a in range(self.n):
                s = a * 7 + ki
                mk = lambda landing: pltpu.make_async_remote_copy(
                    src_ref=self.src_views[a](src[a], peer), dst_ref=self.dst_views[a](dst[a], landing),
                    send_sem=send_sems.at[s], recv_sem=recv_sems.at[s], device_id=(px, py, pc), device_id_type=MESH)
                sends.append(mk(me))
                recvs.append(mk(peer))
        return local, sends, recvs

    def start(self, src, dst, sems):
        local, sends, _ = self._copies(src, dst, sems)
        for cp in local + sends:
            cp.start()

    def wait(self, src, dst, sems):
        local, sends, recvs = self._copies(src, dst, sems)
        for cp in recvs:
            cp.wait_recv()
        for cp in sends:
            cp.wait_send()
        for cp in local:
            cp.wait()


def _call(body, *, name, grid, in_specs, args, out_shape, out_specs, scratch=(), rider=None, aliases=None):
    n_in, n_out, n_scr = len(args), len(out_shape), len(scratch)
    aliases = dict(aliases or {})
    if rider is None:
        return pl.pallas_call(body, name=name, grid=grid, out_shape=tuple(out_shape), in_specs=list(in_specs),
                              out_specs=tuple(out_specs), scratch_shapes=list(scratch), input_output_aliases=aliases,
                              compiler_params=_params(len(grid)))(*args)
    r_ops = rider.operands()

    def full_body(*refs):
        ins, refs = refs[:n_in], refs[n_in:]
        r_src, refs = refs[:rider.n], refs[len(r_ops):]
        outs, refs = refs[:n_out], refs[n_out:]
        r_dst, refs = refs[:rider.n], refs[rider.n:]
        scr, sems = refs[:n_scr], refs[n_scr:]
        if grid:
            first = last = None
            for axis, size in enumerate(grid):
                at_start, at_end = pl.program_id(axis) == 0, pl.program_id(axis) == size - 1
                first = at_start if first is None else jnp.logical_and(first, at_start)
                last = at_end if last is None else jnp.logical_and(last, at_end)

            @pl.when(first)
            def _():
                rider.start(r_src, r_dst, sems)
            body(*ins, *outs, *scr)

            @pl.when(last)
            def _():
                rider.wait(r_src, r_dst, sems)
        else:
            rider.start(r_src, r_dst, sems)
            rider.wait(r_src, r_dst, sems)

    aliases.update(rider.aliases(n_in, n_out))
    kw = dict(compiler_params=_params(len(grid))) if grid else {}
    if grid:
        kw["grid"] = grid
    return pl.pallas_call(
        full_body, name=name, out_shape=tuple(out_shape) + tuple(rider.out_shape()),
        in_specs=list(in_specs) + [ANY] * len(r_ops), out_specs=tuple(out_specs) + tuple([ANY] * rider.n),
        scratch_shapes=list(scratch) + rider.scratch(), input_output_aliases=aliases, **kw)(*args, *r_ops)


def _exchange(name, entries):
    return _call(None, name=name, grid=(), in_specs=[], args=[], out_shape=[], out_specs=[], rider=_Rider(entries))


def _gather_two_level(name, shards, dst_shapes, dst_views):
    n = len(shards)

    def body(*refs):
        src, dst = refs[:n], refs[n:2 * n]
        send_sems, recv_sems, local_sems = refs[2 * n:]
        x, y, c = lax.axis_index("x"), lax.axis_index("y"), lax.axis_index("c")
        flat = lambda dev: 4 * dev[0] + 2 * dev[1] + dev[2]
        me, sibling = (x, y, c), (x, y, 1 - c)
        chips = [(1 - x, y), (x, 1 - y), (1 - x, 1 - y)]

        def copy(a, k, block, to, from_shard=False):
            place = dst_views[a](dst[a], flat(block))
            return pltpu.make_async_remote_copy(src_ref=src[a] if from_shard else place, dst_ref=place,
                                                send_sem=send_sems.at[7 * a + k], recv_sem=recv_sems.at[7 * a + k],
                                                device_id=to, device_id_type=MESH)

        mine = [pltpu.make_async_copy(src[a], dst_views[a](dst[a], flat(me)), local_sems.at[a]) for a in range(n)]
        first = [copy(a, 0, me, sibling, True) for a in range(n)]
        first += [copy(a, 1 + j, me, (*chip, c), True) for j, chip in enumerate(chips) for a in range(n)]
        for cp in mine + first:
            cp.start()
        passed = []
        for j, chip in enumerate(chips):
            for a in range(n):
                copy(a, 1 + j, (*chip, c), me).wait_recv()
                passed.append(copy(a, 4 + j, (*chip, c), sibling))
                passed[-1].start()
        for a in range(n):
            copy(a, 0, sibling, me).wait_recv()
            for j, chip in enumerate(chips):
                copy(a, 4 + j, (*chip, 1 - c), me).wait_recv()
        for cp in first + passed:
            cp.wait_send()
        for cp in mine:
            cp.wait()

    return pl.pallas_call(
        body, name=name, out_shape=tuple(dst_shapes), in_specs=[ANY] * n, out_specs=tuple([ANY] * n),
        scratch_shapes=[pltpu.SemaphoreType.DMA((7 * n,)), pltpu.SemaphoreType.DMA((7 * n,)), pltpu.SemaphoreType.DMA((n,))],
    )(*shards)


def _allreduce_rows(name, part):
    n_rows, n_cols = part.shape
    sl = n_rows // N_DEV

    def body(p_ref, o_ref, recv_v, sum_v, send1, recv1, send2, recv2, local_sems):
        x, y, c = lax.axis_index("x"), lax.axis_index("y"), lax.axis_index("c")
        me = 4 * x + 2 * y + c
        rows_of = lambda d: pl.ds(pl.multiple_of(d * sl, SUBLANE), sl)
        peers = []
        for ki, k in enumerate(PEER_BITS):
            px, py, pc = x ^ (k >> 2), y ^ ((k >> 1) & 1), c ^ (k & 1)
            peers.append((ki, (px, py, pc), 4 * px + 2 * py + pc))

        def scatter(ki, dev, peer, landing):
            return pltpu.make_async_remote_copy(src_ref=p_ref.at[rows_of(peer)], dst_ref=recv_v.at[landing],
                                                send_sem=send1.at[ki], recv_sem=recv1.at[ki], device_id=dev, device_id_type=MESH)

        def gather(ki, dev, landing):
            return pltpu.make_async_remote_copy(src_ref=sum_v, dst_ref=o_ref.at[rows_of(landing)],
                                                send_sem=send2.at[ki], recv_sem=recv2.at[ki], device_id=dev, device_id_type=MESH)

        own = pltpu.make_async_copy(p_ref.at[rows_of(me)], recv_v.at[me], local_sems.at[0])
        own.start()
        for ki, dev, peer in peers:
            scatter(ki, dev, peer, me).start()
        for ki, dev, peer in peers:
            scatter(ki, dev, peer, peer).wait_recv()
        own.wait()
        total = recv_v[0]
        for d in range(1, N_DEV):
            total = total + recv_v[d]
        sum_v[...] = total
        own = pltpu.make_async_copy(sum_v, o_ref.at[rows_of(me)], local_sems.at[1])
        own.start()
        for ki, dev, peer in peers:
            gather(ki, dev, me).start()
        for ki, dev, peer in peers:
            gather(ki, dev, peer).wait_recv()
        for ki, dev, peer in peers:
            scatter(ki, dev, peer, me).wait_send()
            gather(ki, dev, me).wait_send()
        own.wait()

    return pl.pallas_call(
        body, name=name, out_shape=jax.ShapeDtypeStruct((n_rows, n_cols), F32), in_specs=[ANY], out_specs=ANY,
        scratch_shapes=[pltpu.VMEM((N_DEV, sl, n_cols), F32), pltpu.VMEM((sl, n_cols), F32),
                        pltpu.SemaphoreType.DMA((7,)), pltpu.SemaphoreType.DMA((7,)), pltpu.SemaphoreType.DMA((7,)),
                        pltpu.SemaphoreType.DMA((7,)), pltpu.SemaphoreType.DMA((2,))],
    )(part)


def _shifted_windows(buf, shift_s, n_rows):
    for o in range(1, SUBLANE):
        shift_s[o - 1, 0:n_rows - SUBLANE, :] = buf[pl.ds(o, n_rows - SUBLANE), :]

    def window(s):
        q, o = divmod(s, SUBLANE)
        src = buf if o == 0 else shift_s.at[o - 1]
        return src[pl.ds(q * SUBLANE, ROW_CHUNK), :]
    return window


def _z_parts(x_ref, w_ref, b_ref, z_ref, width):
    xb = x_ref[...].astype(BF16)

    def part(k, keep=True, half=None):
        cols = slice(k * width, (k + 1) * width)
        if half is not None:
            cols = slice(k * width + half * (width // 2), k * width + (half + 1) * (width // 2))
        zk = (_dot(xb, w_ref[:, cols]) + b_ref[:, cols]).astype(BF16)
        z_ref[:, cols] = zk
        return zk.astype(F32) if keep else None
    return part


def _even_fwd(x, w_in, b_in, conv_w, conv_b, ln_a_g, ln_a_b, ln_v_g, ln_v_b, wcat, bs_full, *, seq, tm, rider=None):
    t_len, d_model = x.shape
    w = d_model // 2
    nt, tps, nb = t_len // tm, seq // tm, tm // LANE

    def body(x_ref, w_ref, b_ref, cw_ref, cb_ref, lag_ref, lab_ref, lvg_ref, lvb_ref, wcat_ref, bs_ref,
             z_ref, y_ref, a1_ref, sg_ref, a0_s, shift_s):
        i = pl.program_id(0)

        @pl.when(i % tps == 0)
        def _():
            a0_s[0:HALO_A, :] = jnp.zeros((HALO_A, w), F32)

        part = _z_parts(x_ref, w_ref, b_ref, z_ref, w)
        a0_s[HALO_A:HALO_A + tm, :] = part(0) * _sigmoid(part(1))
        window = _shifted_windows(a0_s, shift_s, tm + HALO_A)
        n_chunks = tm // ROW_CHUNK
        pieces = [(k, h) for k in range(2, 6) for h in range(2)]
        later = dict(zip(range(n_chunks - 1, -1, -max(1, n_chunks // len(pieces))), reversed(pieces)))
        for c, r in enumerate(range(0, tm, ROW_CHUNK)):
            acc = jnp.zeros((ROW_CHUNK, w), F32) + cb_ref[...]
            for k in range(CONV_A_WIDTH):
                acc = acc + cw_ref[k:k + 1, :] * window(HALO_A - (CONV_A_WIDTH - 1) + k + r)
            a1_ref[r:r + ROW_CHUNK, :] = acc
            if c in later:
                part(later[c][0], keep=False, half=later[c][1])
        for piece in pieces:
            if piece not in later.values():
                part(piece[0], keep=False, half=piece[1])
        zp = lambda k: z_ref[:, k * w:(k + 1) * w].astype(F32)
        a0_s[0:HALO_A, :] = a0_s[tm:tm + HALO_A, :]
        ah, _ = _norm(a1_ref[...])
        a = _silu(ah * lag_ref[...] + lab_ref[...]) * _silu(zp(2))
        y_ref[:, 0:w] = a.astype(BF16)

        u = zp(3)
        vh, _ = _norm(_gelu(zp(4)))
        v2 = vh * lvg_ref[...] + lvb_ref[...]
        low = _head_low_mask(nb * LANE)
        for j in range(w // LANE):
            vt = _blocks_to_lanes(v2, j, nb)
            rhs = jnp.concatenate([jnp.where(low, vt, 0.0), jnp.where(low, 0.0, vt)], axis=0).astype(BF16)
            out = _dot(wcat_ref[j], rhs)
            for n in range(nb):
                sg_ref[n * LANE:(n + 1) * LANE, j * LANE:(j + 1) * LANE] = (
                    out[:, n * LANE:(n + 1) * LANE] + bs_ref[:, j * LANE:(j + 1) * LANE]).astype(BF16)
        g = _gelu(u) * sg_ref[...].astype(F32) * _silu(zp(5))
        y_ref[:, w:2 * w] = g.astype(BF16)

    row = lambda cols: pl.BlockSpec((tm, cols), lambda i: (i, 0))
    return _call(
        body, name="even_fwd", grid=(nt,), rider=rider,
        out_shape=(jax.ShapeDtypeStruct((t_len, 6 * w), BF16), jax.ShapeDtypeStruct((t_len, d_model), BF16),
                   jax.ShapeDtypeStruct((t_len, w), F32), jax.ShapeDtypeStruct((t_len, w), BF16)),
        in_specs=[row(d_model), _const(w_in.shape), _const(b_in.shape), _const(conv_w.shape), _const(conv_b.shape),
                  _const(ln_a_g.shape), _const(ln_a_b.shape), _const(ln_v_g.shape), _const(ln_v_b.shape),
                  _const(wcat.shape), _const(bs_full.shape)],
        out_specs=(row(6 * w), row(d_model), row(w), row(w)),
        scratch=[pltpu.VMEM((tm + HALO_A, w), F32), pltpu.VMEM((SUBLANE - 1, tm + HALO_A, w), F32)],
        args=[x, w_in, b_in, conv_w, conv_b, ln_a_g, ln_a_b, ln_v_g, ln_v_b, wcat, bs_full])


def _doubling_sums(src, cols, lv_s, win, lo, n, step):
    read = lambda off: src[pl.ds(lo + off, n), cols]
    shift, level = 1, 0
    while shift < win:
        dst = lv_s.at[level % 2]
        dst[pl.ds(lo, n), :] = read(0) + read(step * shift)
        read = lambda off, d=dst: d[pl.ds(lo + off, n), :]
        shift, level = 2 * shift, level + 1
    final = lv_s.at[(level - 1) % 2]
    return lambda start, rows: final[pl.ds(start, rows), :]


def _pool_inverse(tile_index, tm, seq, window):
    row = tile_index * tm + lax.broadcasted_iota(jnp.int32, (tm, 1), 0)
    pos = (row % seq + 1).astype(F32)
    return 1.0 / jnp.minimum(pos, float(window))


def _odd_fwd(x, w_in, b_in, w_pool, pool_scale, conv_w, *, seq, tm, rider=None):
    t_len, d_model = x.shape
    w = d_model // 2
    nt, tps = t_len // tm, seq // tm

    def body(x_ref, w_ref, b_ref, wp_ref, ps_ref, cw_ref, z_ref, y_ref, pooled_ref, q_ref, cv_s, hc_s, lv_s):
        i = pl.program_id(0)
        first = SUBLANE + HALO_C

        @pl.when(i == 0)
        def _():
            cv_s[0:SUBLANE, :] = jnp.zeros((SUBLANE, w), F32)
            lv_s[:, 0:SUBLANE, :] = jnp.zeros((2, SUBLANE, LANE), F32)

        @pl.when(i % tps == 0)
        def _():
            cv_s[SUBLANE:first, :] = jnp.zeros((HALO_C, w), F32)
            hc_s[0:HALO_D, :] = jnp.zeros((HALO_D, w), F32)

        part = _z_parts(x_ref, w_ref, b_ref, z_ref, w)
        c_val = part(0)
        c_gate = part(1)
        cv_s[first:first + tm, :] = c_val
        for gi, win in enumerate(POOL_WINDOWS):
            cols = slice(gi * LANE, (gi + 1) * LANE)
            s = _doubling_sums(cv_s, cols, lv_s, win, SUBLANE, HALO_C + tm, -1)(first, tm)
            pooled = (s * _pool_inverse(i, tm, seq, win) - c_val[:, cols]).astype(BF16)
            pooled_ref[:, cols] = pooled
            c = _dot(pooled, wp_ref[gi]) * ps_ref[:, cols] * _silu(c_gate[:, cols])
            y_ref[:, cols] = c.astype(BF16)
        cv_s[SUBLANE:first, :] = cv_s[tm + SUBLANE:tm + first, :]

        d_h = part(2)
        d_b = part(3)
        hc_s[HALO_D:HALO_D + tm, :] = part(4) * d_h
        q = jnp.zeros((tm, w), F32)
        for k in range(CONV_D_WIDTH):
            q = q + cw_ref[k:k + 1, :] * hc_s[pl.ds(HALO_D - (CONV_D_WIDTH - 1) + k, tm), :]
        hc_s[0:HALO_D, :] = hc_s[tm:tm + HALO_D, :]
        qb = q.astype(BF16)
        q_ref[...] = qb
        y_ref[:, w:2 * w] = (d_b * qb.astype(F32) * _silu(part(5))).astype(BF16)

    row = lambda cols: pl.BlockSpec((tm, cols), lambda i: (i, 0))
    return _call(
        body, name="odd_fwd", grid=(nt,), rider=rider,
        out_shape=(jax.ShapeDtypeStruct((t_len, 6 * w), BF16), jax.ShapeDtypeStruct((t_len, d_model), BF16),
                   jax.ShapeDtypeStruct((t_len, w), BF16), jax.ShapeDtypeStruct((t_len, w), BF16)),
        in_specs=[row(d_model), _const(w_in.shape), _const(b_in.shape), _const(w_pool.shape), _const(pool_scale.shape),
                  _const(conv_w.shape)],
        out_specs=(row(6 * w), row(d_model), row(w), row(w)),
        scratch=[pltpu.VMEM((tm + HALO_C + SUBLANE, w), F32), pltpu.VMEM((tm + HALO_D, w), F32),
                 pltpu.VMEM((2, tm + HALO_C + SUBLANE, LANE), F32)],
        args=[x, w_in, b_in, w_pool, pool_scale, conv_w])


def _post_fwd(y, x, p_all, layer, w_out, b_out, ln_g, ln_b, wg, bg, wp, loss_target, *, tm, rider=None):
    t_len, d_model = x.shape
    d_ple = p_all.shape[-1]
    nt = t_len // tm
    last = loss_target is not None

    def body(*refs):
        y_ref, x_ref, p_ref, wo_ref, bo_ref, g_ref, b_ref, wg_ref, bg_ref, wp_ref = refs[:10]
        rest = refs[10:]
        if last:
            lt_ref, rest = rest[0], rest[1:]
        xn_ref, rh_ref, h_ref, rstd_ref, gate_ref, e_ref = rest[:6]
        r = DEEPNORM_ALPHA * x_ref[...] + _dot(y_ref[...], wo_ref[...]) + bo_ref[...]
        rh, rstd = _norm(r)
        h = rh * g_ref[...] + b_ref[...]
        hb = h.astype(BF16)
        gate = _sigmoid(_dot(hb, wg_ref[...]) + bg_ref[...])
        e = _dot(p_ref[...].astype(BF16), wp_ref[...])
        xn = h + gate * e
        xn_ref[...] = xn
        rh_ref[...] = rh.astype(BF16)
        h_ref[...] = hb
        rstd_ref[...] = jnp.broadcast_to(rstd, (tm, LANE))
        gate_ref[...] = gate.astype(BF16)
        e_ref[...] = e.astype(BF16)
        if last:
            dxn_ref, sse_ref = rest[6:]
            diff = xn - lt_ref[...]
            dxn_ref[...] = diff * (1.0 / d_model)

            @pl.when(pl.program_id(0) == 0)
            def _():
                sse_ref[...] = jnp.zeros_like(sse_ref)
            sse_ref[...] += jnp.sum(_sum0(diff * diff), axis=1, keepdims=True)

    row = lambda cols: pl.BlockSpec((tm, cols), lambda i: (i, 0))
    in_specs = [row(d_model), row(d_model), pl.BlockSpec((None, tm, d_ple), lambda i: (layer, i, 0)),
                _const(w_out.shape), _const(b_out.shape), _const(ln_g.shape), _const(ln_b.shape), _const(wg.shape),
                _const(bg.shape), _const(wp.shape)]
    args = [y, x, p_all, w_out, b_out, ln_g, ln_b, wg, bg, wp]
    out_shape = [jax.ShapeDtypeStruct((t_len, d_model), F32), jax.ShapeDtypeStruct((t_len, d_model), BF16),
                 jax.ShapeDtypeStruct((t_len, d_model), BF16), jax.ShapeDtypeStruct((t_len, LANE), F32),
                 jax.ShapeDtypeStruct((t_len, d_model), BF16), jax.ShapeDtypeStruct((t_len, d_model), BF16)]
    out_specs = [row(d_model), row(d_model), row(d_model), row(LANE), row(d_model), row(d_model)]
    if last:
        in_specs.append(row(d_model))
        args.append(loss_target)
        out_shape += [jax.ShapeDtypeStruct((t_len, d_model), F32), jax.ShapeDtypeStruct((SUBLANE, LANE), F32)]
        out_specs += [row(d_model), _const((SUBLANE, LANE))]
    return _call(body, name="post_fwd_last" if last else "post_fwd", grid=(nt,), out_shape=out_shape, in_specs=in_specs,
                 out_specs=out_specs, args=args, rider=rider)


def _post_bwd(dxn, gate, e, rh, rstd, ln_g, wg, w_out, *, tm, rider=None):
    t_len, d_model = dxn.shape
    nt = t_len // tm

    def body(dxn_ref, gate_ref, e_ref, rh_ref, rstd_ref, g_ref, wg_ref, wo_ref, de_ref, dgl_ref, dy_ref, dr_ref, acc_ref):
        @pl.when(pl.program_id(0) == 0)
        def _():
            acc_ref[...] = jnp.zeros_like(acc_ref)

        d = dxn_ref[...]
        gt = gate_ref[...].astype(F32)
        rhat = rh_ref[...].astype(F32)
        de_ref[...] = (d * gt).astype(BF16)
        dgl = d * e_ref[...].astype(F32) * gt * (1.0 - gt)
        dglb = dgl.astype(BF16)
        dgl_ref[...] = dglb
        dh = d + _dot_nt(dglb, wg_ref[...])
        dr = _norm_bwd(dh * g_ref[...], rhat, rstd_ref[:, 0:1])
        dr_ref[...] = dr
        dy_ref[...] = _dot_nt(dr.astype(BF16), wo_ref[...]).astype(BF16)
        acc_ref[0:1, :] += _sum0(dgl)
        acc_ref[1:2, :] += _sum0(dh * rhat)
        acc_ref[2:3, :] += _sum0(dh)
        acc_ref[3:4, :] += _sum0(dr)

    row = lambda cols: pl.BlockSpec((tm, cols), lambda i: (i, 0))
    return _call(
        body, name="post_bwd", grid=(nt,), rider=rider,
        out_shape=(jax.ShapeDtypeStruct((t_len, d_model), BF16), jax.ShapeDtypeStruct((t_len, d_model), BF16),
                   jax.ShapeDtypeStruct((t_len, d_model), BF16), jax.ShapeDtypeStruct((t_len, d_model), F32),
                   jax.ShapeDtypeStruct((SUBLANE, d_model), F32)),
        in_specs=[row(d_model), row(d_model), row(d_model), row(d_model), row(LANE), _const(ln_g.shape), _const(wg.shape),
                  _const(w_out.shape)],
        out_specs=(row(d_model), row(d_model), row(d_model), row(d_model), _const((SUBLANE, d_model))),
        args=[dxn, gate, e, rh, rstd, ln_g, wg, w_out])


def _dz_store(dz_ref, dbin_ref, width):
    def store(k, v):
        cols = slice(k * width, (k + 1) * width)
        vb = v.astype(BF16)
        dz_ref[:, cols] = vb
        dbin_ref[0:1, cols] += _sum0(v)
        return vb
    return store


def _even_bwd(dy, z, a1, sg, dr, w_in, conv_w, ln_a_g, ln_a_b, ln_v_g, ln_v_b, wcat_t, *, seq, tm, rider=None):
    t_len, d_model = dr.shape
    w = d_model // 2
    nt, tps, nb = t_len // tm, seq // tm, tm // LANE
    n_heads = 2 * (w // LANE)

    def body(dy_ref, z_ref, a1_ref, sg_ref, dr_ref, w_ref, cw_ref, lag_ref, lab_ref, lvg_ref, lvb_ref, wct_ref,
             dz_ref, dx_ref, dcw_ref, vec_ref, dbin_ref, dws_ref, dbs_ref, da1_s, a0_s, da0_s, cw_acc, shift_s):
        i = pl.program_id(0)
        tile = nt - 1 - i

        @pl.when(i == 0)
        def _():
            vec_ref[...] = jnp.zeros_like(vec_ref)
            dbin_ref[...] = jnp.zeros_like(dbin_ref)
            dws_ref[...] = jnp.zeros_like(dws_ref)
            dbs_ref[...] = jnp.zeros_like(dbs_ref)
            cw_acc[...] = jnp.zeros_like(cw_acc)

        @pl.when((tile + 1) % tps == 0)
        def _():
            da1_s[tm:tm + HALO_A, :] = jnp.zeros((HALO_A, w), F32)

        zp = lambda k: z_ref[:, k * w:(k + 1) * w].astype(F32)
        store = _dz_store(dz_ref, dbin_ref, w)

        dg = dy_ref[:, w:2 * w].astype(F32)
        u, v, gg = zp(3), zp(4), zp(5)
        sgv = sg_ref[...].astype(F32)
        gelu_u, dgelu_u = _gelu_grad(u)
        silu_gg, dsilu_gg = _silu_grad(gg)
        dzb5 = store(5, dg * gelu_u * sgv * dsilu_gg)
        t1 = dg * silu_gg
        dzb3 = store(3, t1 * sgv * dgelu_u)
        dsg = t1 * gelu_u
        gelu_v, dgelu_v = _gelu_grad(v)
        vh, rstd_v = _norm(gelu_v)
        v2 = vh * lvg_ref[...] + lvb_ref[...]
        low = _head_low_mask(nb * LANE)
        for j in range(w // LANE):
            dt = _blocks_to_lanes(dsg, j, nb)
            d_lo = jnp.where(low, dt, 0.0).astype(BF16)
            d_hi = jnp.where(low, 0.0, dt).astype(BF16)
            v2t = _blocks_to_lanes(v2, j, nb).astype(BF16)
            dv2t = _dot(wct_ref[j], jnp.concatenate([d_lo, d_hi], axis=0))
            for n in range(nb):
                da0_s[n * LANE:(n + 1) * LANE, j * LANE:(j + 1) * LANE] = dv2t[:, n * LANE:(n + 1) * LANE]
            dws_ref[2 * j] += _dot_nt(d_lo, v2t)
            dws_ref[2 * j + 1] += _dot_nt(d_hi, v2t)
            bsum = dt[:, 0:LANE]
            for n in range(1, nb):
                bsum = bsum + dt[:, n * LANE:(n + 1) * LANE]
            dbs_ref[:, j * LANE:(j + 1) * LANE] += bsum
        dv2 = da0_s[...]
        vec_ref[3:4, :] += _sum0(dv2 * vh)
        vec_ref[4:5, :] += _sum0(dv2)
        dzb4 = store(4, _norm_bwd(dv2 * lvg_ref[...], vh, rstd_v) * dgelu_v)
        dx_b = DEEPNORM_ALPHA * dr_ref[...]
        for k, dzb in ((3, dzb3), (4, dzb4), (5, dzb5)):
            dx_b = dx_b + _dot_nt(dzb, w_ref[:, k * w:(k + 1) * w])
        dx_ref[...] = dx_b

        da = dy_ref[:, 0:w].astype(F32)
        a_val, a_glu, a_gate = zp(0), zp(1), zp(2)
        s_glu = _sigmoid(a_glu)
        a0_s[...] = a_val * s_glu
        ah, rstd_a = _norm(a1_ref[...])
        silu_a2, dsilu_a2 = _silu_grad(ah * lag_ref[...] + lab_ref[...])
        silu_ag, dsilu_ag = _silu_grad(a_gate)
        dzb2 = store(2, da * silu_a2 * dsilu_ag)
        da2 = da * silu_ag * dsilu_a2
        vec_ref[1:2, :] += _sum0(da2 * ah)
        vec_ref[2:3, :] += _sum0(da2)
        da1 = _norm_bwd(da2 * lag_ref[...], ah, rstd_a)
        vec_ref[0:1, :] += _sum0(da1)
        da1_s[0:tm, :] = da1
        window = _shifted_windows(da1_s, shift_s, tm + HALO_A)
        for r in range(0, tm, ROW_CHUNK):
            a0c = a0_s[r:r + ROW_CHUNK, :]
            acc = jnp.zeros((ROW_CHUNK, w), F32)
            for k in range(CONV_A_WIDTH):
                d = window(r + (CONV_A_WIDTH - 1) - k)
                acc = acc + cw_ref[k:k + 1, :] * d
                pw = a0c * d
                p8 = pw[0:SUBLANE]
                for q in range(1, ROW_CHUNK // SUBLANE):
                    p8 = p8 + pw[q * SUBLANE:(q + 1) * SUBLANE]
                cw_acc[k * SUBLANE:(k + 1) * SUBLANE, :] += p8
            da0_s[r:r + ROW_CHUNK, :] = acc
        da1_s[tm:tm + HALO_A, :] = da1_s[0:HALO_A, :]
        da0 = da0_s[...]
        dzb0 = store(0, da0 * s_glu)
        dzb1 = store(1, da0 * a_val * s_glu * (1.0 - s_glu))

        dx = dx_ref[...]
        for k, dzb in ((0, dzb0), (1, dzb1), (2, dzb2)):
            dx = dx + _dot_nt(dzb, w_ref[:, k * w:(k + 1) * w])
        dx_ref[...] = dx

        @pl.when(i == nt - 1)
        def _():
            for k in range(CONV_A_WIDTH):
                dcw_ref[k:k + 1, :] = _sum0(cw_acc[k * SUBLANE:(k + 1) * SUBLANE, :])
            keep = (lax.broadcasted_iota(jnp.int32, (LANE, LANE), 0) >= lax.broadcasted_iota(jnp.int32, (LANE, LANE), 1))
            for hd in range(n_heads):
                dws_ref[hd] = jnp.where(keep, dws_ref[hd], 0.0)

    rev = lambda cols: pl.BlockSpec((tm, cols), lambda i: (nt - 1 - i, 0))
    return _call(
        body, name="even_bwd", grid=(nt,), rider=rider,
        out_shape=(jax.ShapeDtypeStruct((t_len, 6 * w), BF16), jax.ShapeDtypeStruct((t_len, d_model), F32),
                   jax.ShapeDtypeStruct((CONV_A_WIDTH, w), F32), jax.ShapeDtypeStruct((SUBLANE, w), F32),
                   jax.ShapeDtypeStruct((SUBLANE, 6 * w), F32), jax.ShapeDtypeStruct((n_heads, LANE, LANE), F32),
                   jax.ShapeDtypeStruct((LANE, w), F32)),
        in_specs=[rev(d_model), rev(6 * w), rev(w), rev(w), rev(d_model), _const(w_in.shape), _const(conv_w.shape),
                  _const(ln_a_g.shape), _const(ln_a_b.shape), _const(ln_v_g.shape), _const(ln_v_b.shape), _const(wcat_t.shape)],
        out_specs=(rev(6 * w), rev(d_model), _const((CONV_A_WIDTH, w)), _const((SUBLANE, w)), _const((SUBLANE, 6 * w)),
                   _const((n_heads, LANE, LANE)), _const((LANE, w))),
        scratch=[pltpu.VMEM((tm + HALO_A, w), F32), pltpu.VMEM((tm, w), F32), pltpu.VMEM((tm, w), F32),
                 pltpu.VMEM((CONV_A_WIDTH * SUBLANE, w), F32), pltpu.VMEM((SUBLANE - 1, tm + HALO_A, w), F32)],
        args=[dy, z, a1, sg, dr, w_in, conv_w, ln_a_g, ln_a_b, ln_v_g, ln_v_b, wcat_t])


def _odd_bwd(dy, z, pooled, q, dr, w_in, w_pool, pool_scale, conv_w, *, seq, tm, rider=None):
    t_len, d_model = dr.shape
    w = d_model // 2
    nt, tps = t_len // tm, seq // tm
    n_groups = len(POOL_WINDOWS)

    def body(dy_ref, z_ref, pooled_ref, q_ref, dr_ref, w_ref, wp_ref, ps_ref, cw_ref,
             dz_ref, dx_ref, dwp_ref, vec_ref, dbin_ref, dm_s, dq_s, lv_s):
        i = pl.program_id(0)
        tile = nt - 1 - i

        @pl.when(i == 0)
        def _():
            dwp_ref[...] = jnp.zeros_like(dwp_ref)
            vec_ref[...] = jnp.zeros_like(vec_ref)
            dbin_ref[...] = jnp.zeros_like(dbin_ref)
            dm_s[tm + HALO_C:tm + HALO_C + SUBLANE, :] = jnp.zeros((SUBLANE, w), F32)
            lv_s[:, tm + HALO_C:tm + HALO_C + SUBLANE, :] = jnp.zeros((2, SUBLANE, LANE), F32)

        @pl.when((tile + 1) % tps == 0)
        def _():
            dm_s[tm:tm + HALO_C, :] = jnp.zeros((HALO_C, w), F32)
            dq_s[tm:tm + HALO_D, :] = jnp.zeros((HALO_D, w), F32)

        zp = lambda k: z_ref[:, k * w:(k + 1) * w].astype(F32)
        store = _dz_store(dz_ref, dbin_ref, w)

        dc = dy_ref[:, 0:w].astype(F32)
        c_gate = zp(1)
        silu_c, dsilu_c = _silu_grad(c_gate)
        dcs = dc * silu_c
        dvg_parts, dcg_parts = [], []
        for gi, win in enumerate(POOL_WINDOWS):
            cols = slice(gi * LANE, (gi + 1) * LANE)
            pooled_g = pooled_ref[:, cols]
            wp = wp_ref[gi]
            cpre = _dot(pooled_g, wp)
            scale = ps_ref[:, cols]
            vec_ref[0:1, cols] += _sum0(dcs[:, cols] * cpre)
            dcg_parts.append(dc[:, cols] * cpre * scale * dsilu_c[:, cols])
            dcp = (dcs[:, cols] * scale).astype(BF16)
            dwp_ref[gi] += _dot_tn(pooled_g, dcp)
            dpooled = _dot_nt(dcp, wp)
            dm_s[0:tm, cols] = dpooled * _pool_inverse(tile, tm, seq, win)
            s = _doubling_sums(dm_s, cols, lv_s, win, 0, tm + HALO_C, 1)(0, tm)
            dvg_parts.append(s - dpooled)
        dm_s[tm:tm + HALO_C, :] = dm_s[0:HALO_C, :]
        dzb0 = store(0, jnp.concatenate(dvg_parts, axis=1))
        dzb1 = store(1, jnp.concatenate(dcg_parts, axis=1))

        dd = dy_ref[:, w:2 * w].astype(F32)
        d_h, d_b, d_c, d_gate = zp(2), zp(3), zp(4), zp(5)
        qv = q_ref[...].astype(F32)
        silu_d, dsilu_d = _silu_grad(d_gate)
        dzb5 = store(5, dd * d_b * qv * dsilu_d)
        dzb3 = store(3, dd * qv * silu_d)
        dq_s[0:tm, :] = dd * d_b * silu_d
        hc = d_c * d_h
        dhc = jnp.zeros((tm, w), F32)
        for k in range(CONV_D_WIDTH):
            d = dq_s[pl.ds((CONV_D_WIDTH - 1) - k, tm), :]
            dhc = dhc + cw_ref[k:k + 1, :] * d
            vec_ref[1 + k:2 + k, :] += _sum0(hc * d)
        dq_s[tm:tm + HALO_D, :] = dq_s[0:HALO_D, :]
        dzb2 = store(2, dhc * d_c)
        dzb4 = store(4, dhc * d_h)

        dx = DEEPNORM_ALPHA * dr_ref[...]
        for k, dzb in enumerate((dzb0, dzb1, dzb2, dzb3, dzb4, dzb5)):
            dx = dx + _dot_nt(dzb, w_ref[:, k * w:(k + 1) * w])
        dx_ref[...] = dx

    rev = lambda cols: pl.BlockSpec((tm, cols), lambda i: (nt - 1 - i, 0))
    return _call(
        body, name="odd_bwd", grid=(nt,), rider=rider,
        out_shape=(jax.ShapeDtypeStruct((t_len, 6 * w), BF16), jax.ShapeDtypeStruct((t_len, d_model), F32),
                   jax.ShapeDtypeStruct((n_groups, LANE, LANE), F32), jax.ShapeDtypeStruct((SUBLANE, w), F32),
                   jax.ShapeDtypeStruct((SUBLANE, 6 * w), F32)),
        in_specs=[rev(d_model), rev(6 * w), rev(w), rev(w), rev(d_model), _const(w_in.shape), _const(w_pool.shape),
                  _const(pool_scale.shape), _const(conv_w.shape)],
        out_specs=(rev(6 * w), rev(d_model), _const((n_groups, LANE, LANE)), _const((SUBLANE, w)), _const((SUBLANE, 6 * w))),
        scratch=[pltpu.VMEM((tm + HALO_C + SUBLANE, w), F32), pltpu.VMEM((tm + HALO_D, w), F32),
                 pltpu.VMEM((2, tm + HALO_C + SUBLANE, LANE), F32)],
        args=[dy, z, pooled, q, dr, w_in, w_pool, pool_scale, conv_w])


def _weight_grad(name, a, b, a_layer=None, bn=None, a_cols=None, rider=None):
    if a_layer is None:
        t_len, m = a.shape
        col = 0
        if a_cols is not None:
            col, m = a_cols
        a_spec = lambda tk: pl.BlockSpec((tk, m), lambda n, k: (k, col))
    else:
        _, t_len, m = a.shape
        a_spec = lambda tk: pl.BlockSpec((None, tk, m), lambda n, k: (a_layer, k, 0))
    n_cols = b.shape[1]
    bn = n_cols if bn is None else bn
    tk = min(t_len, 1024)
    n_k = t_len // tk

    def body(a_ref, b_ref, o_ref, acc):
        k = pl.program_id(1)

        @pl.when(k == 0)
        def _():
            acc[...] = jnp.zeros_like(acc)
        acc[...] += _dot_tn(a_ref[...].astype(BF16), b_ref[...].astype(BF16))

        @pl.when(k == n_k - 1)
        def _():
            o_ref[...] = acc[...].astype(BF16)

    outs = _call(
        body, name=name, grid=(n_cols // bn, n_k), out_shape=[jax.ShapeDtypeStruct((m, n_cols), BF16)],
        in_specs=[a_spec(tk), pl.BlockSpec((tk, bn), lambda n, k: (k, n))],
        out_specs=[pl.BlockSpec((m, bn), lambda n, k: (0, n))], scratch=[pltpu.VMEM((m, bn), F32)],
        args=[a, b], rider=rider)
    return outs[0] if rider is None else outs


def _weight_grads_post(name, y, dr, h, dgl, p_all, layer, de):
    t_len, d_model = y.shape
    d_ple = p_all.shape[-1]
    tk = min(t_len, 1024)
    n_k = t_len // tk

    def body(y_ref, dr_ref, h_ref, dgl_ref, p_ref, de_ref, o_out, o_gate, o_ple, acc_out, acc_gate, acc_ple):
        k = pl.program_id(0)

        @pl.when(k == 0)
        def _():
            acc_out[...] = jnp.zeros_like(acc_out)
            acc_gate[...] = jnp.zeros_like(acc_gate)
            acc_ple[...] = jnp.zeros_like(acc_ple)
        acc_out[...] += _dot_tn(y_ref[...], dr_ref[...].astype(BF16))
        acc_gate[...] += _dot_tn(h_ref[...], dgl_ref[...])
        acc_ple[...] += _dot_tn(p_ref[...].astype(BF16), de_ref[...])

        @pl.when(k == n_k - 1)
        def _():
            o_out[...] = acc_out[...].astype(BF16)
            o_gate[...] = acc_gate[...].astype(BF16)
            o_ple[...] = acc_ple[...].astype(BF16)

    row = lambda cols: pl.BlockSpec((tk, cols), lambda k: (k, 0))
    return pl.pallas_call(
        body, name=name, grid=(n_k,),
        out_shape=(jax.ShapeDtypeStruct((d_model, d_model), BF16), jax.ShapeDtypeStruct((d_model, d_model), BF16),
                   jax.ShapeDtypeStruct((d_ple, d_model), BF16)),
        in_specs=[row(d_model), row(d_model), row(d_model), row(d_model),
                  pl.BlockSpec((None, tk, d_ple), lambda k: (layer, k, 0)), row(d_model)],
        out_specs=(_const((d_model, d_model)), _const((d_model, d_model)), _const((d_ple, d_model))),
        scratch_shapes=[pltpu.VMEM((d_model, d_model), F32), pltpu.VMEM((d_model, d_model), F32),
                        pltpu.VMEM((d_ple, d_model), F32)],
        compiler_params=_params(1),
    )(y, dr, h, dgl, p_all, de)


def _adamw_reduce(name, parts, w, m, v, rows_per_block):
    n_rows, n_cols = w.shape
    br = rows_per_block
    n_parts = parts.shape[0]

    def body(p_ref, w_ref, m_ref, v_ref, g_ref, d_ref, nm_ref, nv_ref):
        g = p_ref[0].astype(F32)
        for k in range(1, n_parts):
            g = g + p_ref[k].astype(F32)
        nm = ADAM_B1 * m_ref[...] + (1.0 - ADAM_B1) * g
        nv = ADAM_B2 * v_ref[...] + (1.0 - ADAM_B2) * (g * g)
        m_hat = nm / (1.0 - ADAM_B1 ** ADAM_STEP)
        v_hat = nv / (1.0 - ADAM_B2 ** ADAM_STEP)
        g_ref[...] = g
        d_ref[...] = -ADAM_LR * (m_hat / (jnp.sqrt(v_hat) + ADAM_EPS) + ADAM_WD * w_ref[...])
        nm_ref[...] = nm
        nv_ref[...] = nv

    blk = pl.BlockSpec((br, n_cols), lambda i: (i, 0))
    shp = jax.ShapeDtypeStruct((n_rows, n_cols), F32)
    return pl.pallas_call(
        body, name=name, grid=(n_rows // br,), out_shape=(shp, shp, shp, shp),
        in_specs=[pl.BlockSpec((n_parts, br, n_cols), lambda i: (0, i, 0)), blk, blk, blk], out_specs=(blk, blk, blk, blk),
        compiler_params=_params(1),
    )(parts, w, m, v)


def _pack_rows(flat_parts, pad_to=None, width=LANE):
    flat = jnp.concatenate([a.reshape(-1) for a in flat_parts])
    if pad_to is not None and pad_to > flat.shape[0]:
        flat = jnp.concatenate([flat, jnp.zeros((pad_to - flat.shape[0],), flat.dtype)])
    return flat.reshape(-1, width)


def _unpack_rows(packed, shapes):
    flat = packed.reshape(-1)
    out, off = [], 0
    for s in shapes:
        n = math.prod(s)
        out.append(flat[off:off + n].reshape(s))
        off += n
    return out


def _to_dest_major(full):
    lead, last = full.shape[:-1], full.shape[-1]
    t = full.reshape(lead + (N_DEV, last // N_DEV))
    return jnp.moveaxis(t, -2, 0).reshape(N_DEV, -1)


def _from_source_major(blocks, shard_shape):
    t = blocks.reshape((N_DEV,) + tuple(shard_shape))
    t = jnp.moveaxis(t, 0, -2)
    return t.reshape(tuple(shard_shape[:-1]) + (N_DEV * shard_shape[-1],))


def _block_rows(n_rows, n_cols, target_elems=96 * 1024):
    best = None
    for br in range(SUBLANE, n_rows + 1, SUBLANE):
        if n_rows % br == 0 and br * n_cols <= target_elems:
            best = br
    return n_rows if best is None else best


def kernel(x, p, w_in_e, b_in_e, conv_a_w, conv_a_b, ln_a_g, ln_a_b, ln_v_g, ln_v_b, w_s, b_s, w_out_e, b_out_e, w_in_o, b_in_o, w_pool, pool_scale, conv_d_w, w_out_o, b_out_o, ln_g, ln_b, w_ple, w_ple_gate, b_ple_gate, loss_target, m_w_in_e, m_b_in_e, m_conv_a_w, m_conv_a_b, m_ln_a_g, m_ln_a_b, m_ln_v_g, m_ln_v_b, m_w_s, m_b_s, m_w_out_e, m_b_out_e, m_w_in_o, m_b_in_o, m_w_pool, m_pool_scale, m_conv_d_w, m_w_out_o, m_b_out_o, m_ln_g, m_ln_b, m_w_ple, m_w_ple_gate, m_b_ple_gate, v_w_in_e, v_b_in_e, v_conv_a_w, v_conv_a_b, v_ln_a_g, v_ln_a_b, v_ln_v_g, v_ln_v_b, v_w_s, v_b_s, v_w_out_e, v_b_out_e, v_w_in_o, v_b_in_o, v_w_pool, v_pool_scale, v_conv_d_w, v_w_out_o, v_b_out_o, v_ln_g, v_ln_b, v_w_ple, v_w_ple_gate, v_b_ple_gate):
    weights = dict(w_in_e=w_in_e, b_in_e=b_in_e, conv_a_w=conv_a_w, conv_a_b=conv_a_b, ln_a_g=ln_a_g, ln_a_b=ln_a_b,
                   ln_v_g=ln_v_g, ln_v_b=ln_v_b, w_s=w_s, b_s=b_s, w_out_e=w_out_e, b_out_e=b_out_e, w_in_o=w_in_o,
                   b_in_o=b_in_o, w_pool=w_pool, pool_scale=pool_scale, conv_d_w=conv_d_w, w_out_o=w_out_o,
                   b_out_o=b_out_o, ln_g=ln_g, ln_b=ln_b, w_ple=w_ple, w_ple_gate=w_ple_gate, b_ple_gate=b_ple_gate)
    mom_m = dict(w_in_e=m_w_in_e, b_in_e=m_b_in_e, conv_a_w=m_conv_a_w, conv_a_b=m_conv_a_b, ln_a_g=m_ln_a_g,
                 ln_a_b=m_ln_a_b, ln_v_g=m_ln_v_g, ln_v_b=m_ln_v_b, w_s=m_w_s, b_s=m_b_s, w_out_e=m_w_out_e,
                 b_out_e=m_b_out_e, w_in_o=m_w_in_o, b_in_o=m_b_in_o, w_pool=m_w_pool, pool_scale=m_pool_scale,
                 conv_d_w=m_conv_d_w, w_out_o=m_w_out_o, b_out_o=m_b_out_o, ln_g=m_ln_g, ln_b=m_ln_b, w_ple=m_w_ple,
                 w_ple_gate=m_w_ple_gate, b_ple_gate=m_b_ple_gate)
    mom_v = dict(w_in_e=v_w_in_e, b_in_e=v_b_in_e, conv_a_w=v_conv_a_w, conv_a_b=v_conv_a_b, ln_a_g=v_ln_a_g,
                 ln_a_b=v_ln_a_b, ln_v_g=v_ln_v_g, ln_v_b=v_ln_v_b, w_s=v_w_s, b_s=v_b_s, w_out_e=v_w_out_e,
                 b_out_e=v_b_out_e, w_in_o=v_w_in_o, b_in_o=v_b_in_o, w_pool=v_w_pool, pool_scale=v_pool_scale,
                 conv_d_w=v_conv_d_w, w_out_o=v_w_out_o, b_out_o=v_b_out_o, ln_g=v_ln_g, ln_b=v_ln_b, w_ple=v_w_ple,
                 w_ple_gate=v_w_ple_gate, b_ple_gate=v_b_ple_gate)
    names = tuple(weights)

    batch, seq, d_model = x.shape
    t_len = batch * seq
    w = d_model // 2
    n_even = w_in_e.shape[0]
    n_odd = w_in_o.shape[0]
    depth = ln_g.shape[0]
    d_ple = p.shape[-1]
    n_heads = w_s.shape[1]
    tm = 512 if seq % 512 == 0 and seq >= 1024 else seq // 2
    in_cols = w_in_e.shape[-1]
    out_rows = w_out_e.shape[1]
    ple_cols = w_ple.shape[-1]
    gate_rows = w_ple_gate.shape[1]

    sh_shapes = [weights[n].shape for n in SH_NAMES]
    sh_len = sum(math.prod(s) for s in sh_shapes)
    sh_pad = -(-sh_len // (SUBLANE * LANE)) * (SUBLANE * LANE)
    sh_rows = sh_pad // LANE
    sds = jax.ShapeDtypeStruct
    w_in16 = (w_in_e.astype(BF16), w_in_o.astype(BF16))
    w_out16 = (w_out_e.astype(BF16), w_out_o.astype(BF16))
    w_ple16, w_gate16 = w_ple.astype(BF16), w_ple_gate.astype(BF16)

    kinds = ("in", "out", "ple", "gate")

    def weight_entries(i, which=kinds):
        j, par = i // 2, i % 2
        all_four = {"in": (w_in16[par][j], _whole_view, sds((d_model, N_DEV * in_cols), BF16), _axis_view(1, in_cols)),
                    "out": (w_out16[par][j], _whole_view, sds((N_DEV * out_rows, d_model), BF16), _axis_view(0, out_rows)),
                    "ple": (w_ple16[i], _whole_view, sds((d_ple, N_DEV * ple_cols), BF16), _axis_view(1, ple_cols)),
                    "gate": (w_gate16[i], _whole_view, sds((N_DEV * gate_rows, d_model), BF16), _axis_view(0, gate_rows))}
        return [((i, k), all_four[k]) for k in which]

    fwd_riders = {("mixer", 0): weight_entries(0, kinds[1:]) + weight_entries(1, kinds[:1]),
                  ("post", 0): weight_entries(1, kinds[1:])}
    for i in range(1, depth - 1):
        if i % 2:
            fwd_riders[("mixer", i)] = weight_entries(i + 1, kinds[:1])
            fwd_riders[("post", i)] = weight_entries(i + 1, kinds[1:])
        else:
            fwd_riders[("mixer", i)] = weight_entries(i + 1)
    layer_w = {}

    def carried(where):
        tagged = fwd_riders.get(where)
        if tagged is None:
            return None, lambda landed: None
        return _Rider([e for _, e in tagged]), lambda landed: layer_w.update(zip([t for t, _ in tagged], landed))

    w_in_first = weight_entries(0, kinds[:1])[0][1]
    first = _gather_two_level("gather_first", [w_in_first[0], _pack_rows([weights[n] for n in SH_NAMES], sh_pad)],
                              [w_in_first[2], sds((N_DEV, sh_rows, LANE), F32)], [w_in_first[3], _slot_view()])
    layer_w[(0, "in")] = first[0]
    sh_flat = first[1].reshape(N_DEV, sh_pad)
    full_small, off = {}, 0
    for n, s in zip(SH_NAMES, sh_shapes):
        size = math.prod(s)
        full_small[n] = _from_source_major(sh_flat[:, off:off + size], s)
        off += size

    tril = jnp.tril(jnp.ones((LANE, LANE), dtype=bool))
    ws_m = jnp.where(tril[None, None], w_s, 0.0)
    pair = lambda t: jnp.concatenate([t[:, 0::2], t[:, 1::2]], axis=-1).astype(BF16)
    wcat = pair(ws_m)
    wcat_t = pair(jnp.swapaxes(ws_m, -1, -2))
    bs_full = jnp.repeat(jnp.swapaxes(b_s, -1, -2), w // n_heads, axis=-1)
    row2 = lambda a, j: a[j][None, :]

    x2 = x.reshape(t_len, d_model)
    p3 = p.reshape(depth, t_len, d_ple)
    lt2 = loss_target.reshape(t_len, d_model)

    xs, saved = [x2], []
    dxn = sse = None
    for i in range(depth):
        j = i // 2
        last = i == depth - 1
        rider, file_weights = carried(("mixer", i))
        if i % 2 == 0:
            outs = _even_fwd(xs[i], layer_w[(i, "in")], row2(b_in_e, j), full_small["conv_a_w"][j], row2(conv_a_b, j),
                             row2(ln_a_g, j), row2(ln_a_b, j), row2(ln_v_g, j), row2(ln_v_b, j), wcat[j], bs_full[j],
                             seq=seq, tm=tm, rider=rider)
            b_out = row2(b_out_e, j)
        else:
            outs = _odd_fwd(xs[i], layer_w[(i, "in")], row2(full_small["b_in_o"], j), w_pool[j].astype(BF16),
                            row2(full_small["pool_scale"], j), full_small["conv_d_w"][j], seq=seq, tm=tm, rider=rider)
            b_out = row2(full_small["b_out_o"], j)
        z, y, s1, s2 = outs[:4]
        file_weights(outs[4:])
        rider, file_weights = carried(("post", i))
        outs = _post_fwd(y, xs[i], p3, i, layer_w[(i, "out")], b_out, row2(ln_g, i), row2(ln_b, i), layer_w[(i, "gate")],
                         row2(b_ple_gate, i), layer_w[(i, "ple")], lt2 if last else None, tm=tm, rider=rider)
        xn, rh, h, rstd, gate, e = outs[:6]
        if last:
            dxn, sse = outs[6:8]
        file_weights(outs[8 if last else 6:])
        xs.append(xn)
        saved.append(dict(z=z, y=y, s1=s1, s2=s2, rh=rh, h=h, rstd=rstd, gate=gate, e=e))


    recv = {"w_in_e": None, "w_in_o": None, "w_out_e": None, "w_out_o": None, "w_ple": None, "w_ple_gate": None}

    def grad_entry(i, kind, g, half=None):
        j, par = i // 2, i % 2
        sfx = "_o" if par else "_e"
        name, src_view, slot = {"in": ("w_in" + sfx, _axis_view(1, in_cols), j), "out": ("w_out" + sfx, _axis_view(0, out_rows), j),
                                "ple": ("w_ple", _axis_view(1, ple_cols), i), "gate": ("w_ple_gate", _axis_view(0, gate_rows), i)}[kind]
        dst = recv[name] if recv[name] is not None else sds((N_DEV,) + weights[name].shape, BF16)
        if half is None:
            dst_view = _slot_view(slot)
        else:
            dst_view = lambda ref, d: ref.at[d, slot, pl.ds(half[0] * half[1], half[1]), :]
        return name, (g, src_view, dst, dst_view)

    def ride(tagged):
        if not tagged:
            return None, lambda landed: None
        return _Rider([e for _, e in tagged]), lambda landed: recv.update(zip([n for n, _ in tagged], landed))

    small = {n: [None] * weights[n].shape[0] for n in REP_NAMES + SH_NAMES}
    grads = {}
    for i in reversed(range(depth)):
        j, par = i // 2, i % 2
        sv = saved[i]
        w_in = layer_w[(i, "in")]
        split = par == 1 and (i + 1, "in") in grads
        rider, file_landed = ride([grad_entry(i + 1, k, grads.pop((i + 1, k))) for k in kinds[1:]] if split else [])
        outs = _post_bwd(dxn, sv["gate"], sv["e"], sv["rh"], sv["rstd"], row2(ln_g, i), layer_w[(i, "gate")],
                         layer_w[(i, "out")], tm=tm, rider=rider)
        de, dgl, dy, dr, acc = outs[:5]
        file_landed(outs[5:])
        small["b_ple_gate"][i], small["ln_g"][i], small["ln_b"][i] = acc[0], acc[1], acc[2]
        small["b_out_o" if par else "b_out_e"][j] = acc[3]
        rider, file_landed = ride([grad_entry(i + 1, k, grads.pop((i + 1, k))) for k in kinds if (i + 1, k) in grads])
        if par == 0:
            outs = _even_bwd(dy, sv["z"], sv["s1"], sv["s2"], dr, w_in, full_small["conv_a_w"][j],
                             row2(ln_a_g, j), row2(ln_a_b, j), row2(ln_v_g, j), row2(ln_v_b, j), wcat_t[j], seq=seq, tm=tm,
                             rider=rider)
            dz, dx, dcw, vec, dbin, dws, dbs = outs[:7]
            file_landed(outs[7:])
            small["conv_a_w"][j], small["conv_a_b"][j] = dcw, vec[0]
            small["ln_a_g"][j], small["ln_a_b"][j], small["ln_v_g"][j], small["ln_v_b"][j] = vec[1], vec[2], vec[3], vec[4]
            small["b_in_e"][j], small["w_s"][j] = dbin[0], dws
            small["b_s"][j] = _head_sums(dbs, n_heads)
        else:
            outs = _odd_bwd(dy, sv["z"], sv["s1"], sv["s2"], dr, w_in, w_pool[j].astype(BF16),
                            row2(full_small["pool_scale"], j), full_small["conv_d_w"][j], seq=seq, tm=tm, rider=rider)
            dz, dx, dwp, vec, dbin = outs[:5]
            file_landed(outs[5:])
            small["w_pool"][j], small["pool_scale"][j], small["conv_d_w"][j] = dwp, vec[0], vec[1:1 + CONV_D_WIDTH]
            small["b_in_o"][j] = dbin[0]
        tag = f"_l{i}"
        grads[(i, "out")], grads[(i, "gate")], grads[(i, "ple")] = _weight_grads_post(
            "dw_post" + tag, sv["y"], dr, sv["h"], dgl, p3, i, de)
        if i > 0:
            grads[(i, "in")] = _weight_grad("dw_in" + tag, xs[i], dz, bn=in_cols * N_DEV // 2)
        dxn = dx
    grad_x = dxn.reshape(batch, seq, d_model)

    half_rows = d_model // 2
    rider, file_landed = ride([grad_entry(0, k, grads.pop((0, k))) for k in kinds[1:]])
    outs = _weight_grad("dw_in_l0_top", xs[0], dz, a_cols=(0, half_rows), rider=rider)
    file_landed(outs[1:])
    rider, file_landed = ride([grad_entry(0, "in", outs[0], half=(0, half_rows))])
    outs = _weight_grad("dw_in_l0_bottom", xs[0], dz, a_cols=(1, half_rows), rider=rider)
    file_landed(outs[1:])

    small_full = {n: jnp.stack(small[n]) for n in small}
    sh_part = jnp.concatenate([_to_dest_major(small_full[n]) for n in SH_NAMES], axis=1)
    sh_part = jnp.concatenate([sh_part, jnp.zeros((N_DEV, sh_pad - sh_len), F32)], axis=1).reshape(N_DEV, sh_rows, LANE)
    name, entry = grad_entry(0, "in", outs[0], half=(1, half_rows))
    landed = _exchange("exchange_last", [entry, (sh_part, _slot_view(), sds((N_DEV, sh_rows, LANE), F32), _slot_view())])
    recv[name] = landed[0]

    rep_width = 4 * LANE
    rep_len = sum(math.prod(weights[n].shape) for n in REP_NAMES)
    rep_block = N_DEV * SUBLANE
    rep_pad = -(-(rep_len + 1) // (rep_block * rep_width)) * (rep_block * rep_width)
    rep_sum = _allreduce_rows("allreduce_replicated",
                              _pack_rows([small_full[n] for n in REP_NAMES] + [sse[0:1, 0]], rep_pad, rep_width))
    loss = (0.5 / d_model) * rep_sum.reshape(-1)[rep_len]

    results = {}
    for n, parts in recv.items():
        shp = weights[n].shape
        rows, cols = math.prod(shp[:-1]), shp[-1]
        two = lambda a: a.reshape(rows, cols)
        outs = _adamw_reduce("adamw_" + n, parts.reshape(N_DEV, rows, cols), two(weights[n]), two(mom_m[n]), two(mom_v[n]),
                             _block_rows(rows, cols))
        results[n] = [o.reshape(shp) for o in outs]
    pack_sh = lambda d: _pack_rows([d[n] for n in SH_NAMES], sh_pad)
    outs = _adamw_reduce("adamw_small_sharded", landed[1], pack_sh(weights), pack_sh(mom_m), pack_sh(mom_v), sh_rows)
    for n, *vals in zip(SH_NAMES, *[_unpack_rows(o, sh_shapes) for o in outs]):
        results[n] = vals
    pack_rep = lambda d: _pack_rows([d[n] for n in REP_NAMES], rep_pad, rep_width)
    rep_shapes = [weights[n].shape for n in REP_NAMES]
    outs = _adamw_reduce("adamw_replicated", rep_sum[None], pack_rep(weights), pack_rep(mom_m), pack_rep(mom_v), rep_block)
    for n, *vals in zip(REP_NAMES, *[_unpack_rows(o, rep_shapes) for o in outs]):
        results[n] = vals

    return (loss, grad_x, *[results[n][0] for n in names], *[results[n][1] for n in names],
            *[results[n][2] for n in names], *[results[n][3] for n in names])


def _head_sums(dbs, n_heads):
    t, width = dbs.shape
    return jnp.sum(dbs.reshape(t, n_heads, width // n_heads), axis=-1).T
```

```python
import functools
import math

import jax
import jax.numpy as jnp
from jax import lax
from jax.experimental import pallas as pl
from jax.experimental.pallas import tpu as pltpu

F32 = jnp.float32
BF16 = jnp.bfloat16

N_DEV = 8
DEPTH = 4
LN_EPS = 1e-5
DEEPNORM_ALPHA = (2.0 * DEPTH) ** 0.25
POOL_WINDOWS = (2, 4, 8, 16)
CONV_A_WIDTH = 31
CONV_D_WIDTH = 3
GELU_C = math.sqrt(2.0 / math.pi)
GELU_K = 0.044715

ADAM_LR = 0.001
ADAM_B1 = 0.9
ADAM_B2 = 0.999
ADAM_EPS = 1e-08
ADAM_WD = 0.01
ADAM_STEP = 10

LANE = 128
SUBLANE = 8
HALO_A = 32
HALO_C = 16
HALO_D = 8
ROW_CHUNK = 32
VMEM_LIMIT = 56 * 2**20

ANY = pl.BlockSpec(memory_space=pl.ANY)
MESH = pl.DeviceIdType.MESH

REP_NAMES = ("b_in_e", "conv_a_b", "ln_a_g", "ln_a_b", "ln_v_g", "ln_v_b", "w_s", "b_s", "b_out_e", "w_pool", "ln_g",
             "ln_b", "b_ple_gate")
SH_NAMES = ("conv_a_w", "conv_d_w", "pool_scale", "b_in_o", "b_out_o")


def _params(n_grid_axes):
    return pltpu.CompilerParams(dimension_semantics=("arbitrary",) * n_grid_axes, vmem_limit_bytes=VMEM_LIMIT)


def _const(shape):
    nd = len(shape)
    return pl.BlockSpec(shape, lambda *_: (0,) * nd)


def _dot(a, b):
    return jnp.dot(a, b, preferred_element_type=F32)


def _dot_nt(a, b):
    return lax.dot_general(a, b, (((1,), (1,)), ((), ())), preferred_element_type=F32)


def _dot_tn(a, b):
    return lax.dot_general(a, b, (((0,), (0,)), ((), ())), preferred_element_type=F32)


def _sigmoid(x):
    return jax.nn.sigmoid(x)


def _silu(x):
    return x * _sigmoid(x)


def _silu_grad(x):
    s = _sigmoid(x)
    return x * s, s * (1.0 + x * (1.0 - s))


def _gelu(x):
    return 0.5 * x * (1.0 + jnp.tanh(GELU_C * (x + GELU_K * x * x * x)))


def _gelu_grad(x):
    x2 = x * x
    t = jnp.tanh(GELU_C * x * (1.0 + GELU_K * x2))
    return 0.5 * x * (1.0 + t), 0.5 * (1.0 + t) + 0.5 * x * (1.0 - t * t) * GELU_C * (1.0 + 3.0 * GELU_K * x2)


def _norm(v):
    mu = jnp.mean(v, axis=-1, keepdims=True)
    d = v - mu
    var = jnp.mean(d * d, axis=-1, keepdims=True)
    rstd = lax.rsqrt(var + LN_EPS)
    return d * rstd, rstd


def _norm_bwd(dxh, xh, rstd):
    return rstd * (dxh - jnp.mean(dxh, axis=-1, keepdims=True) - xh * jnp.mean(dxh * xh, axis=-1, keepdims=True))


def _sum0(v):
    return jnp.sum(v, axis=0, keepdims=True)


def _head_low_mask(n_cols):
    lane = lax.broadcasted_iota(jnp.int32, (LANE, n_cols), 1)
    return (lane & (LANE - 1)) < (LANE // 2)


def _blocks_to_lanes(v, j, nb):
    return jnp.concatenate([v[n * LANE:(n + 1) * LANE, j * LANE:(j + 1) * LANE] for n in range(nb)], axis=1)


def _axis_view(axis, size):
    def view(ref, d):
        idx = [slice(None)] * len(ref.shape)
        idx[axis] = pl.ds(pl.multiple_of(d * size, size), size)
        return ref.at[tuple(idx)]
    return view


def _slot_view(*slot):
    return lambda ref, d: ref.at[(d,) + slot]


def _whole_view(ref, d):
    return ref


PEER_BITS = (1, 2, 4, 6, 3, 5, 7)


class _Rider:
    def __init__(self, entries):
        self.n = len(entries)
        self.srcs = [e[0] for e in entries]
        self.src_views = [e[1] for e in entries]
        self.dsts = [e[2] for e in entries]
        self.dst_views = [e[3] for e in entries]
        self.passed = [a for a, d in enumerate(self.dsts) if not isinstance(d, jax.ShapeDtypeStruct)]

    def operands(self):
        return self.srcs + [self.dsts[a] for a in self.passed]

    def out_shape(self):
        return [jax.ShapeDtypeStruct(d.shape, d.dtype) for d in self.dsts]

    def scratch(self):
        return [pltpu.SemaphoreType.DMA((7 * self.n,)), pltpu.SemaphoreType.DMA((7 * self.n,)), pltpu.SemaphoreType.DMA((self.n,))]

    def aliases(self, n_in_before, n_out_before):
        return {n_in_before + self.n + q: n_out_before + a for q, a in enumerate(self.passed)}

    def _copies(self, src, dst, sems):
        send_sems, recv_sems, local_sems = sems
        x, y, c = lax.axis_index("x"), lax.axis_index("y"), lax.axis_index("c")
        me = 4 * x + 2 * y + c
        local = [pltpu.make_async_copy(self.src_views[a](src[a], me), self.dst_views[a](dst[a], me), local_sems.at[a])
                 for a in range(self.n)]
        sends, recvs = [], []
        for ki, k in enumerate(PEER_BITS):
            px, py, pc = x ^ (k >> 2), y ^ ((k >> 1) & 1), c ^ (k & 1)
            peer = 4 * px + 2 * py + pc
            for a in range(self.n):
                s = a * 7 + ki
                mk = lambda landing: pltpu.make_async_remote_copy(
                    src_ref=self.src_views[a](src[a], peer), dst_ref=self.dst_views[a](dst[a], landing),
                    send_sem=send_sems.at[s], recv_sem=recv_sems.at[s], device_id=(px, py, pc), device_id_type=MESH)
                sends.append(mk(me))
                recvs.append(mk(peer))
        return local, sends, recvs

    def start(self, src, dst, sems):
        local, sends, _ = self._copies(src, dst, sems)
        for cp in local + sends:
            cp.start()

    def wait(self, src, dst, sems):
        local, sends, recvs = self._copies(src, dst, sems)
        for cp in recvs:
            cp.wait_recv()
        for cp in sends:
            cp.wait_send()
        for cp in local:
            cp.wait()


def _call(body, *, name, grid, in_specs, args, out_shape, out_specs, scratch=(), rider=None, aliases=None):
    n_in, n_out, n_scr = len(args), len(out_shape), len(scratch)
    aliases = dict(aliases or {})
    if rider is None:
        return pl.pallas_call(body, name=name, grid=grid, out_shape=tuple(out_shape), in_specs=list(in_specs),
                              out_specs=tuple(out_specs), scratch_shapes=list(scratch), input_output_aliases=aliases,
                              compiler_params=_params(len(grid)))(*args)
    r_ops = rider.operands()

    def full_body(*refs):
        ins, refs = refs[:n_in], refs[n_in:]
        r_src, refs = refs[:rider.n], refs[len(r_ops):]
        outs, refs = refs[:n_out], refs[n_out:]
        r_dst, refs = refs[:rider.n], refs[rider.n:]
        scr, sems = refs[:n_scr], refs[n_scr:]
        if grid:
            first = last = None
            for axis, size in enumerate(grid):
                at_start, at_end = pl.program_id(axis) == 0, pl.program_id(axis) == size - 1
                first = at_start if first is None else jnp.logical_and(first, at_start)
                last = at_end if last is None else jnp.logical_and(last, at_end)

            @pl.when(first)
            def _():
                rider.start(r_src, r_dst, sems)
            body(*ins, *outs, *scr)

            @pl.when(last)
            def _():
                rider.wait(r_src, r_dst, sems)
        else:
            rider.start(r_src, r_dst, sems)
            rider.wait(r_src, r_dst, sems)

    aliases.update(rider.aliases(n_in, n_out))
    kw = dict(compiler_params=_params(len(grid))) if grid else {}
    if grid:
        kw["grid"] = grid
    return pl.pallas_call(
        full_body, name=name, out_shape=tuple(out_shape) + tuple(rider.out_shape()),
        in_specs=list(in_specs) + [ANY] * len(r_ops), out_specs=tuple(out_specs) + tuple([ANY] * rider.n),
        scratch_shapes=list(scratch) + rider.scratch(), input_output_aliases=aliases, **kw)(*args, *r_ops)


def _exchange(name, entries):
    return _call(None, name=name, grid=(), in_specs=[], args=[], out_shape=[], out_specs=[], rider=_Rider(entries))


def _gather_two_level(name, shards, dst_shapes, dst_views):
    n = len(shards)

    def body(*refs):
        src, dst = refs[:n], refs[n:2 * n]
        send_sems, recv_sems, local_sems = refs[2 * n:]
        x, y, c = lax.axis_index("x"), lax.axis_index("y"), lax.axis_index("c")
        flat = lambda dev: 4 * dev[0] + 2 * dev[1] + dev[2]
        me, sibling = (x, y, c), (x, y, 1 - c)
        chips = [(1 - x, y), (x, 1 - y), (1 - x, 1 - y)]

        def copy(a, k, block, to, from_shard=False):
            place = dst_views[a](dst[a], flat(block))
            return pltpu.make_async_remote_copy(src_ref=src[a] if from_shard else place, dst_ref=place,
                                                send_sem=send_sems.at[7 * a + k], recv_sem=recv_sems.at[7 * a + k],
                                                device_id=to, device_id_type=MESH)

        mine = [pltpu.make_async_copy(src[a], dst_views[a](dst[a], flat(me)), local_sems.at[a]) for a in range(n)]
        first = [copy(a, 0, me, sibling, True) for a in range(n)]
        first += [copy(a, 1 + j, me, (*chip, c), True) for j, chip in enumerate(chips) for a in range(n)]
        for cp in mine + first:
            cp.start()
        passed = []
        for j, chip in enumerate(chips):
            for a in range(n):
                copy(a, 1 + j, (*chip, c), me).wait_recv()
                passed.append(copy(a, 4 + j, (*chip, c), sibling))
                passed[-1].start()
        for a in range(n):
            copy(a, 0, sibling, me).wait_recv()
            for j, chip in enumerate(chips):
                copy(a, 4 + j, (*chip, 1 - c), me).wait_recv()
        for cp in first + passed:
            cp.wait_send()
        for cp in mine:
            cp.wait()

    return pl.pallas_call(
        body, name=name, out_shape=tuple(dst_shapes), in_specs=[ANY] * n, out_specs=tuple([ANY] * n),
        scratch_shapes=[pltpu.SemaphoreType.DMA((7 * n,)), pltpu.SemaphoreType.DMA((7 * n,)), pltpu.SemaphoreType.DMA((n,))],
    )(*shards)


def _allreduce_rows(name, part):
    n_rows, n_cols = part.shape
    sl = n_rows // N_DEV

    def body(p_ref, o_ref, recv_v, sum_v, send1, recv1, send2, recv2, local_sems):
        x, y, c = lax.axis_index("x"), lax.axis_index("y"), lax.axis_index("c")
        me = 4 * x + 2 * y + c
        rows_of = lambda d: pl.ds(pl.multiple_of(d * sl, SUBLANE), sl)
        peers = []
        for ki, k in enumerate(PEER_BITS):
            px, py, pc = x ^ (k >> 2), y ^ ((k >> 1) & 1), c ^ (k & 1)
            peers.append((ki, (px, py, pc), 4 * px + 2 * py + pc))

        def scatter(ki, dev, peer, landing):
            return pltpu.make_async_remote_copy(src_ref=p_ref.at[rows_of(peer)], dst_ref=recv_v.at[landing],
                                                send_sem=send1.at[ki], recv_sem=recv1.at[ki], device_id=dev, device_id_type=MESH)

        def gather(ki, dev, landing):
            return pltpu.make_async_remote_copy(src_ref=sum_v, dst_ref=o_ref.at[rows_of(landing)],
                                                send_sem=send2.at[ki], recv_sem=recv2.at[ki], device_id=dev, device_id_type=MESH)

        own = pltpu.make_async_copy(p_ref.at[rows_of(me)], recv_v.at[me], local_sems.at[0])
        own.start()
        for ki, dev, peer in peers:
            scatter(ki, dev, peer, me).start()
        for ki, dev, peer in peers:
            scatter(ki, dev, peer, peer).wait_recv()
        own.wait()
        total = recv_v[0]
        for d in range(1, N_DEV):
            total = total + recv_v[d]
        sum_v[...] = total
        own = pltpu.make_async_copy(sum_v, o_ref.at[rows_of(me)], local_sems.at[1])
        own.start()
        for ki, dev, peer in peers:
            gather(ki, dev, me).start()
        for ki, dev, peer in peers:
            gather(ki, dev, peer).wait_recv()
        for ki, dev, peer in peers:
            scatter(ki, dev, peer, me).wait_send()
            gather(ki, dev, me).wait_send()
        own.wait()

    return pl.pallas_call(
        body, name=name, out_shape=jax.ShapeDtypeStruct((n_rows, n_cols), F32), in_specs=[ANY], out_specs=ANY,
        scratch_shapes=[pltpu.VMEM((N_DEV, sl, n_cols), F32), pltpu.VMEM((sl, n_cols), F32),
                        pltpu.SemaphoreType.DMA((7,)), pltpu.SemaphoreType.DMA((7,)), pltpu.SemaphoreType.DMA((7,)),
                        pltpu.SemaphoreType.DMA((7,)), pltpu.SemaphoreType.DMA((2,))],
    )(part)


def _shifted_windows(buf, shift_s, n_rows):
    for o in range(1, SUBLANE):
        shift_s[o - 1, 0:n_rows - SUBLANE, :] = buf[pl.ds(o, n_rows - SUBLANE), :]

    def window(s):
        q, o = divmod(s, SUBLANE)
        src = buf if o == 0 else shift_s.at[o - 1]
        return src[pl.ds(q * SUBLANE, ROW_CHUNK), :]
    return window


def _z_parts(x_ref, w_ref, b_ref, z_ref, width):
    xb = x_ref[...].astype(BF16)

    def part(k, keep=True, half=None):
        cols = slice(k * width, (k + 1) * width)
        if half is not None:
            cols = slice(k * width + half * (width // 2), k * width + (half + 1) * (width // 2))
        zk = (_dot(xb, w_ref[:, cols]) + b_ref[:, cols]).astype(BF16)
        z_ref[:, cols] = zk
        return zk.astype(F32) if keep else None
    return part


def _even_fwd(x, w_in, b_in, conv_w, conv_b, ln_a_g, ln_a_b, ln_v_g, ln_v_b, wcat, bs_full, *, seq, tm, rider=None):
    t_len, d_model = x.shape
    w = d_model // 2
    nt, tps, nb = t_len // tm, seq // tm, tm // LANE

    def body(x_ref, w_ref, b_ref, cw_ref, cb_ref, lag_ref, lab_ref, lvg_ref, lvb_ref, wcat_ref, bs_ref,
             z_ref, y_ref, a1_ref, sg_ref, a0_s, shift_s):
        i = pl.program_id(0)

        @pl.when(i % tps == 0)
        def _():
            a0_s[0:HALO_A, :] = jnp.zeros((HALO_A, w), F32)

        part = _z_parts(x_ref, w_ref, b_ref, z_ref, w)
        a0_s[HALO_A:HALO_A + tm, :] = part(0) * _sigmoid(part(1))
        window = _shifted_windows(a0_s, shift_s, tm + HALO_A)
        n_chunks = tm // ROW_CHUNK
        pieces = [(k, h) for k in range(2, 6) for h in range(2)]
        later = dict(zip(range(n_chunks - 1, -1, -max(1, n_chunks // len(pieces))), reversed(pieces)))
        for c, r in enumerate(range(0, tm, ROW_CHUNK)):
            acc = jnp.zeros((ROW_CHUNK, w), F32) + cb_ref[...]
            for k in range(CONV_A_WIDTH):
                acc = acc + cw_ref[k:k + 1, :] * window(HALO_A - (CONV_A_WIDTH - 1) + k + r)
            a1_ref[r:r + ROW_CHUNK, :] = acc
            if c in later:
                part(later[c][0], keep=False, half=later[c][1])
        for piece in pieces:
            if piece not in later.values():
                part(piece[0], keep=False, half=piece[1])
        zp = lambda k: z_ref[:, k * w:(k + 1) * w].astype(F32)
        a0_s[0:HALO_A, :] = a0_s[tm:tm + HALO_A, :]
        ah, _ = _norm(a1_ref[...])
        a = _silu(ah * lag_ref[...] + lab_ref[...]) * _silu(zp(2))
        y_ref[:, 0:w] = a.astype(BF16)

        u = zp(3)
        vh, _ = _norm(_gelu(zp(4)))
        v2 = vh * lvg_ref[...] + lvb_ref[...]
        low = _head_low_mask(nb * LANE)
        for j in range(w // LANE):
            vt = _blocks_to_lanes(v2, j, nb)
            rhs = jnp.concatenate([jnp.where(low, vt, 0.0), jnp.where(low, 0.0, vt)], axis=0).astype(BF16)
            out = _dot(wcat_ref[j], rhs)
            for n in range(nb):
                sg_ref[n * LANE:(n + 1) * LANE, j * LANE:(j + 1) * LANE] = (
                    out[:, n * LANE:(n + 1) * LANE] + bs_ref[:, j * LANE:(j + 1) * LANE]).astype(BF16)
        g = _gelu(u) * sg_ref[...].astype(F32) * _silu(zp(5))
        y_ref[:, w:2 * w] = g.astype(BF16)

    row = lambda cols: pl.BlockSpec((tm, cols), lambda i: (i, 0))
    return _call(
        body, name="even_fwd", grid=(nt,), rider=rider,
        out_shape=(jax.ShapeDtypeStruct((t_len, 6 * w), BF16), jax.ShapeDtypeStruct((t_len, d_model), BF16),
                   jax.ShapeDtypeStruct((t_len, w), F32), jax.ShapeDtypeStruct((t_len, w), BF16)),
        in_specs=[row(d_model), _const(w_in.shape), _const(b_in.shape), _const(conv_w.shape), _const(conv_b.shape),
                  _const(ln_a_g.shape), _const(ln_a_b.shape), _const(ln_v_g.shape), _const(ln_v_b.shape),
                  _const(wcat.shape), _const(bs_full.shape)],
        out_specs=(row(6 * w), row(d_model), row(w), row(w)),
        scratch=[pltpu.VMEM((tm + HALO_A, w), F32), pltpu.VMEM((SUBLANE - 1, tm + HALO_A, w), F32)],
        args=[x, w_in, b_in, conv_w, conv_b, ln_a_g, ln_a_b, ln_v_g, ln_v_b, wcat, bs_full])


def _doubling_sums(src, cols, lv_s, win, lo, n, step):
    read = lambda off: src[pl.ds(lo + off, n), cols]
    shift, level = 1, 0
    while shift < win:
        dst = lv_s.at[level % 2]
        dst[pl.ds(lo, n), :] = read(0) + read(step * shift)
        read = lambda off, d=dst: d[pl.ds(lo + off, n), :]
        shift, level = 2 * shift, level + 1
    final = lv_s.at[(level - 1) % 2]
    return lambda start, rows: final[pl.ds(start, rows), :]


def _pool_inverse(tile_index, tm, seq, window):
    row = tile_index * tm + lax.broadcasted_iota(jnp.int32, (tm, 1), 0)
    pos = (row % seq + 1).astype(F32)
    return 1.0 / jnp.minimum(pos, float(window))


def _odd_fwd(x, w_in, b_in, w_pool, pool_scale, conv_w, *, seq, tm, rider=None):
    t_len, d_model = x.shape
    w = d_model // 2
    nt, tps = t_len // tm, seq // tm

    def body(x_ref, w_ref, b_ref, wp_ref, ps_ref, cw_ref, z_ref, y_ref, pooled_ref, q_ref, cv_s, hc_s, lv_s):
        i = pl.program_id(0)
        first = SUBLANE + HALO_C

        @pl.when(i == 0)
        def _():
            cv_s[0:SUBLANE, :] = jnp.zeros((SUBLANE, w), F32)
            lv_s[:, 0:SUBLANE, :] = jnp.zeros((2, SUBLANE, LANE), F32)

        @pl.when(i % tps == 0)
        def _():
            cv_s[SUBLANE:first, :] = jnp.zeros((HALO_C, w), F32)
            hc_s[0:HALO_D, :] = jnp.zeros((HALO_D, w), F32)

        part = _z_parts(x_ref, w_ref, b_ref, z_ref, w)
        c_val = part(0)
        c_gate = part(1)
        cv_s[first:first + tm, :] = c_val
        for gi, win in enumerate(POOL_WINDOWS):
            cols = slice(gi * LANE, (gi + 1) * LANE)
            s = _doubling_sums(cv_s, cols, lv_s, win, SUBLANE, HALO_C + tm, -1)(first, tm)
            pooled = (s * _pool_inverse(i, tm, seq, win) - c_val[:, cols]).astype(BF16)
            pooled_ref[:, cols] = pooled
            c = _dot(pooled, wp_ref[gi]) * ps_ref[:, cols] * _silu(c_gate[:, cols])
            y_ref[:, cols] = c.astype(BF16)
        cv_s[SUBLANE:first, :] = cv_s[tm + SUBLANE:tm + first, :]

        d_h = part(2)
        d_b = part(3)
        hc_s[HALO_D:HALO_D + tm, :] = part(4) * d_h
        q = jnp.zeros((tm, w), F32)
        for k in range(CONV_D_WIDTH):
            q = q + cw_ref[k:k + 1, :] * hc_s[pl.ds(HALO_D - (CONV_D_WIDTH - 1) + k, tm), :]
        hc_s[0:HALO_D, :] = hc_s[tm:tm + HALO_D, :]
        qb = q.astype(BF16)
        q_ref[...] = qb
        y_ref[:, w:2 * w] = (d_b * qb.astype(F32) * _silu(part(5))).astype(BF16)

    row = lambda cols: pl.BlockSpec((tm, cols), lambda i: (i, 0))
    return _call(
        body, name="odd_fwd", grid=(nt,), rider=rider,
        out_shape=(jax.ShapeDtypeStruct((t_len, 6 * w), BF16), jax.ShapeDtypeStruct((t_len, d_model), BF16),
                   jax.ShapeDtypeStruct((t_len, w), BF16), jax.ShapeDtypeStruct((t_len, w), BF16)),
        in_specs=[row(d_model), _const(w_in.shape), _const(b_in.shape), _const(w_pool.shape), _const(pool_scale.shape),
                  _const(conv_w.shape)],
        out_specs=(row(6 * w), row(d_model), row(w), row(w)),
        scratch=[pltpu.VMEM((tm + HALO_C + SUBLANE, w), F32), pltpu.VMEM((tm + HALO_D, w), F32),
                 pltpu.VMEM((2, tm + HALO_C + SUBLANE, LANE), F32)],
        args=[x, w_in, b_in, w_pool, pool_scale, conv_w])


def _post_fwd(y, x, p_all, layer, w_out, b_out, ln_g, ln_b, wg, bg, wp, loss_target, *, tm, rider=None):
    t_len, d_model = x.shape
    d_ple = p_all.shape[-1]
    nt = t_len // tm
    last = loss_target is not None

    def body(*refs):
        y_ref, x_ref, p_ref, wo_ref, bo_ref, g_ref, b_ref, wg_ref, bg_ref, wp_ref = refs[:10]
        rest = refs[10:]
        if last:
            lt_ref, rest = rest[0], rest[1:]
        rh_ref, rstd_ref, gate_ref, e_ref = rest[:4]
        r = DEEPNORM_ALPHA * x_ref[...] + _dot(y_ref[...], wo_ref[...]) + bo_ref[...]
        rh, rstd = _norm(r)
        h = rh * g_ref[...] + b_ref[...]
        gate = _sigmoid(_dot(h.astype(BF16), wg_ref[...]) + bg_ref[...])
        e = _dot(p_ref[...].astype(BF16), wp_ref[...])
        xn = h + gate * e
        rh_ref[...] = rh.astype(BF16)
        rstd_ref[...] = jnp.broadcast_to(rstd, (tm, LANE))
        gate_ref[...] = gate.astype(BF16)
        e_ref[...] = e.astype(BF16)
        if not last:
            rest[4][...] = xn
        else:
            dxn_ref, sse_ref = rest[4:]
            diff = xn - lt_ref[...]
            dxn_ref[...] = diff * (1.0 / d_model)

            @pl.when(pl.program_id(0) == 0)
            def _():
                sse_ref[...] = jnp.zeros_like(sse_ref)
            sse_ref[...] += jnp.sum(_sum0(diff * diff), axis=1, keepdims=True)

    row = lambda cols: pl.BlockSpec((tm, cols), lambda i: (i, 0))
    in_specs = [row(d_model), row(d_model), pl.BlockSpec((None, tm, d_ple), lambda i: (layer, i, 0)),
                _const(w_out.shape), _const(b_out.shape), _const(ln_g.shape), _const(ln_b.shape), _const(wg.shape),
                _const(bg.shape), _const(wp.shape)]
    args = [y, x, p_all, w_out, b_out, ln_g, ln_b, wg, bg, wp]
    out_shape = [jax.ShapeDtypeStruct((t_len, d_model), BF16), jax.ShapeDtypeStruct((t_len, LANE), F32),
                 jax.ShapeDtypeStruct((t_len, d_model), BF16), jax.ShapeDtypeStruct((t_len, d_model), BF16),
                 jax.ShapeDtypeStruct((t_len, d_model), F32)]
    out_specs = [row(d_model), row(LANE), row(d_model), row(d_model), row(d_model)]
    if last:
        in_specs.append(row(d_model))
        args.append(loss_target)
        out_shape.append(jax.ShapeDtypeStruct((SUBLANE, LANE), F32))
        out_specs.append(_const((SUBLANE, LANE)))
    return _call(body, name="post_fwd_last" if last else "post_fwd", grid=(nt,), out_shape=out_shape, in_specs=in_specs,
                 out_specs=out_specs, args=args, rider=rider)


def _post_bwd(dxn, gate, e, rh, rstd, ln_g, wg, w_out, *, tm, rider=None):
    t_len, d_model = dxn.shape
    nt = t_len // tm

    def body(dxn_ref, gate_ref, e_ref, rh_ref, rstd_ref, g_ref, wg_ref, wo_ref, de_ref, dgl_ref, dy_ref, dr_ref, acc_ref):
        @pl.when(pl.program_id(0) == 0)
        def _():
            acc_ref[...] = jnp.zeros_like(acc_ref)

        d = dxn_ref[...]
        gt = gate_ref[...].astype(F32)
        rhat = rh_ref[...].astype(F32)
        de_ref[...] = (d * gt).astype(BF16)
        dgl = d * e_ref[...].astype(F32) * gt * (1.0 - gt)
        dglb = dgl.astype(BF16)
        dgl_ref[...] = dglb
        dh = d + _dot_nt(dglb, wg_ref[...])
        dr = _norm_bwd(dh * g_ref[...], rhat, rstd_ref[:, 0:1])
        drb = dr.astype(BF16)
        dr_ref[...] = drb
        dy_ref[...] = _dot_nt(drb, wo_ref[...]).astype(BF16)
        acc_ref[0:1, :] += _sum0(dgl)
        acc_ref[1:2, :] += _sum0(dh * rhat)
        acc_ref[2:3, :] += _sum0(dh)
        acc_ref[3:4, :] += _sum0(dr)

    row = lambda cols: pl.BlockSpec((tm, cols), lambda i: (i, 0))
    return _call(
        body, name="post_bwd", grid=(nt,), rider=rider,
        out_shape=(jax.ShapeDtypeStruct((t_len, d_model), BF16), jax.ShapeDtypeStruct((t_len, d_model), BF16),
                   jax.ShapeDtypeStruct((t_len, d_model), BF16), jax.ShapeDtypeStruct((t_len, d_model), BF16),
                   jax.ShapeDtypeStruct((SUBLANE, d_model), F32)),
        in_specs=[row(d_model), row(d_model), row(d_model), row(d_model), row(LANE), _const(ln_g.shape), _const(wg.shape),
                  _const(w_out.shape)],
        out_specs=(row(d_model), row(d_model), row(d_model), row(d_model), _const((SUBLANE, d_model))),
        args=[dxn, gate, e, rh, rstd, ln_g, wg, w_out])


def _dz_store(dz_ref, dbin_ref, width):
    def store(k, v):
        cols = slice(k * width, (k + 1) * width)
        vb = v.astype(BF16)
        dz_ref[:, cols] = vb
        dbin_ref[0:1, cols] += _sum0(v)
        return vb
    return store


def _even_bwd(dy, z, a1, sg, dr, w_in, conv_w, ln_a_g, ln_a_b, ln_v_g, ln_v_b, wcat_t, *, seq, tm, rider=None):
    t_len, d_model = dr.shape
    w = d_model // 2
    nt, tps, nb = t_len // tm, seq // tm, tm // LANE
    n_heads = 2 * (w // LANE)

    def body(dy_ref, z_ref, a1_ref, sg_ref, dr_ref, w_ref, cw_ref, lag_ref, lab_ref, lvg_ref, lvb_ref, wct_ref,
             dz_ref, dx_ref, dcw_ref, vec_ref, dbin_ref, dws_ref, dbs_ref, da1_s, a0_s, da0_s, cw_acc, shift_s):
        i = pl.program_id(0)
        tile = nt - 1 - i

        @pl.when(i == 0)
        def _():
            vec_ref[...] = jnp.zeros_like(vec_ref)
            dbin_ref[...] = jnp.zeros_like(dbin_ref)
            dws_ref[...] = jnp.zeros_like(dws_ref)
            dbs_ref[...] = jnp.zeros_like(dbs_ref)
            cw_acc[...] = jnp.zeros_like(cw_acc)

        @pl.when((tile + 1) % tps == 0)
        def _():
            da1_s[tm:tm + HALO_A, :] = jnp.zeros((HALO_A, w), F32)

        zp = lambda k: z_ref[:, k * w:(k + 1) * w].astype(F32)
        store = _dz_store(dz_ref, dbin_ref, w)

        dg = dy_ref[:, w:2 * w].astype(F32)
        u, v, gg = zp(3), zp(4), zp(5)
        sgv = sg_ref[...].astype(F32)
        gelu_u, dgelu_u = _gelu_grad(u)
        silu_gg, dsilu_gg = _silu_grad(gg)
        dzb5 = store(5, dg * gelu_u * sgv * dsilu_gg)
        t1 = dg * silu_gg
        dzb3 = store(3, t1 * sgv * dgelu_u)
        dsg = t1 * gelu_u
        gelu_v, dgelu_v = _gelu_grad(v)
        vh, rstd_v = _norm(gelu_v)
        v2 = vh * lvg_ref[...] + lvb_ref[...]
        low = _head_low_mask(nb * LANE)
        for j in range(w // LANE):
            dt = _blocks_to_lanes(dsg, j, nb)
            d_lo = jnp.where(low, dt, 0.0).astype(BF16)
            d_hi = jnp.where(low, 0.0, dt).astype(BF16)
            v2t = _blocks_to_lanes(v2, j, nb).astype(BF16)
            dv2t = _dot(wct_ref[j], jnp.concatenate([d_lo, d_hi], axis=0))
            for n in range(nb):
                da0_s[n * LANE:(n + 1) * LANE, j * LANE:(j + 1) * LANE] = dv2t[:, n * LANE:(n + 1) * LANE]
            dws_ref[2 * j] += _dot_nt(d_lo, v2t)
            dws_ref[2 * j + 1] += _dot_nt(d_hi, v2t)
            bsum = dt[:, 0:LANE]
            for n in range(1, nb):
                bsum = bsum + dt[:, n * LANE:(n + 1) * LANE]
            dbs_ref[:, j * LANE:(j + 1) * LANE] += bsum
        dv2 = da0_s[...]
        vec_ref[3:4, :] += _sum0(dv2 * vh)
        vec_ref[4:5, :] += _sum0(dv2)
        dzb4 = store(4, _norm_bwd(dv2 * lvg_ref[...], vh, rstd_v) * dgelu_v)
        dx_b = DEEPNORM_ALPHA * dr_ref[...].astype(F32)
        for k, dzb in ((3, dzb3), (4, dzb4), (5, dzb5)):
            dx_b = dx_b + _dot_nt(dzb, w_ref[:, k * w:(k + 1) * w])
        dx_ref[...] = dx_b

        da = dy_ref[:, 0:w].astype(F32)
        a_val, a_glu, a_gate = zp(0), zp(1), zp(2)
        s_glu = _sigmoid(a_glu)
        a0_s[...] = a_val * s_glu
        ah, rstd_a = _norm(a1_ref[...])
        silu_a2, dsilu_a2 = _silu_grad(ah * lag_ref[...] + lab_ref[...])
        silu_ag, dsilu_ag = _silu_grad(a_gate)
        dzb2 = store(2, da * silu_a2 * dsilu_ag)
        da2 = da * silu_ag * dsilu_a2
        vec_ref[1:2, :] += _sum0(da2 * ah)
        vec_ref[2:3, :] += _sum0(da2)
        da1 = _norm_bwd(da2 * lag_ref[...], ah, rstd_a)
        vec_ref[0:1, :] += _sum0(da1)
        da1_s[0:tm, :] = da1
        window = _shifted_windows(da1_s, shift_s, tm + HALO_A)
        for r in range(0, tm, ROW_CHUNK):
            a0c = a0_s[r:r + ROW_CHUNK, :]
            acc = jnp.zeros((ROW_CHUNK, w), F32)
            for k in range(CONV_A_WIDTH):
                d = window(r + (CONV_A_WIDTH - 1) - k)
                acc = acc + cw_ref[k:k + 1, :] * d
                pw = a0c * d
                p8 = pw[0:SUBLANE]
                for q in range(1, ROW_CHUNK // SUBLANE):
                    p8 = p8 + pw[q * SUBLANE:(q + 1) * SUBLANE]
                cw_acc[k * SUBLANE:(k + 1) * SUBLANE, :] += p8
            da0_s[r:r + ROW_CHUNK, :] = acc
        da1_s[tm:tm + HALO_A, :] = da1_s[0:HALO_A, :]
        da0 = da0_s[...]
        dzb0 = store(0, da0 * s_glu)
        dzb1 = store(1, da0 * a_val * s_glu * (1.0 - s_glu))

        dx = dx_ref[...]
        for k, dzb in ((0, dzb0), (1, dzb1), (2, dzb2)):
            dx = dx + _dot_nt(dzb, w_ref[:, k * w:(k + 1) * w])
        dx_ref[...] = dx

        @pl.when(i == nt - 1)
        def _():
            for k in range(CONV_A_WIDTH):
                dcw_ref[k:k + 1, :] = _sum0(cw_acc[k * SUBLANE:(k + 1) * SUBLANE, :])
            keep = (lax.broadcasted_iota(jnp.int32, (LANE, LANE), 0) >= lax.broadcasted_iota(jnp.int32, (LANE, LANE), 1))
            for hd in range(n_heads):
                dws_ref[hd] = jnp.where(keep, dws_ref[hd], 0.0)

    rev = lambda cols: pl.BlockSpec((tm, cols), lambda i: (nt - 1 - i, 0))
    return _call(
        body, name="even_bwd", grid=(nt,), rider=rider,
        out_shape=(jax.ShapeDtypeStruct((t_len, 6 * w), BF16), jax.ShapeDtypeStruct((t_len, d_model), F32),
                   jax.ShapeDtypeStruct((CONV_A_WIDTH, w), F32), jax.ShapeDtypeStruct((SUBLANE, w), F32),
                   jax.ShapeDtypeStruct((SUBLANE, 6 * w), F32), jax.ShapeDtypeStruct((n_heads, LANE, LANE), F32),
                   jax.ShapeDtypeStruct((LANE, w), F32)),
        in_specs=[rev(d_model), rev(6 * w), rev(w), rev(w), rev(d_model), _const(w_in.shape), _const(conv_w.shape),
                  _const(ln_a_g.shape), _const(ln_a_b.shape), _const(ln_v_g.shape), _const(ln_v_b.shape), _const(wcat_t.shape)],
        out_specs=(rev(6 * w), rev(d_model), _const((CONV_A_WIDTH, w)), _const((SUBLANE, w)), _const((SUBLANE, 6 * w)),
                   _const((n_heads, LANE, LANE)), _const((LANE, w))),
        scratch=[pltpu.VMEM((tm + HALO_A, w), F32), pltpu.VMEM((tm, w), F32), pltpu.VMEM((tm, w), F32),
                 pltpu.VMEM((CONV_A_WIDTH * SUBLANE, w), F32), pltpu.VMEM((SUBLANE - 1, tm + HALO_A, w), F32)],
        args=[dy, z, a1, sg, dr, w_in, conv_w, ln_a_g, ln_a_b, ln_v_g, ln_v_b, wcat_t])


def _odd_bwd(dy, z, pooled, q, dr, w_in, w_pool, pool_scale, conv_w, *, seq, tm, rider=None):
    t_len, d_model = dr.shape
    w = d_model // 2
    nt, tps = t_len // tm, seq // tm
    n_groups = len(POOL_WINDOWS)

    def body(dy_ref, z_ref, pooled_ref, q_ref, dr_ref, w_ref, wp_ref, ps_ref, cw_ref,
             dz_ref, dx_ref, dwp_ref, vec_ref, dbin_ref, dm_s, dq_s, lv_s):
        i = pl.program_id(0)
        tile = nt - 1 - i

        @pl.when(i == 0)
        def _():
            dwp_ref[...] = jnp.zeros_like(dwp_ref)
            vec_ref[...] = jnp.zeros_like(vec_ref)
            dbin_ref[...] = jnp.zeros_like(dbin_ref)
            dm_s[tm + HALO_C:tm + HALO_C + SUBLANE, :] = jnp.zeros((SUBLANE, w), F32)
            lv_s[:, tm + HALO_C:tm + HALO_C + SUBLANE, :] = jnp.zeros((2, SUBLANE, LANE), F32)

        @pl.when((tile + 1) % tps == 0)
        def _():
            dm_s[tm:tm + HALO_C, :] = jnp.zeros((HALO_C, w), F32)
            dq_s[tm:tm + HALO_D, :] = jnp.zeros((HALO_D, w), F32)

        zp = lambda k: z_ref[:, k * w:(k + 1) * w].astype(F32)
        store = _dz_store(dz_ref, dbin_ref, w)

        dc = dy_ref[:, 0:w].astype(F32)
        c_gate = zp(1)
        silu_c, dsilu_c = _silu_grad(c_gate)
        dcs = dc * silu_c
        dvg_parts, dcg_parts = [], []
        for gi, win in enumerate(POOL_WINDOWS):
            cols = slice(gi * LANE, (gi + 1) * LANE)
            pooled_g = pooled_ref[:, cols]
            wp = wp_ref[gi]
            cpre = _dot(pooled_g, wp)
            scale = ps_ref[:, cols]
            vec_ref[0:1, cols] += _sum0(dcs[:, cols] * cpre)
            dcg_parts.append(dc[:, cols] * cpre * scale * dsilu_c[:, cols])
            dcp = (dcs[:, cols] * scale).astype(BF16)
            dwp_ref[gi] += _dot_tn(pooled_g, dcp)
            dpooled = _dot_nt(dcp, wp)
            dm_s[0:tm, cols] = dpooled * _pool_inverse(tile, tm, seq, win)
            s = _doubling_sums(dm_s, cols, lv_s, win, 0, tm + HALO_C, 1)(0, tm)
            dvg_parts.append(s - dpooled)
        dm_s[tm:tm + HALO_C, :] = dm_s[0:HALO_C, :]
        dzb0 = store(0, jnp.concatenate(dvg_parts, axis=1))
        dzb1 = store(1, jnp.concatenate(dcg_parts, axis=1))

        dd = dy_ref[:, w:2 * w].astype(F32)
        d_h, d_b, d_c, d_gate = zp(2), zp(3), zp(4), zp(5)
        qv = q_ref[...].astype(F32)
        silu_d, dsilu_d = _silu_grad(d_gate)
        dzb5 = store(5, dd * d_b * qv * dsilu_d)
        dzb3 = store(3, dd * qv * silu_d)
        dq_s[0:tm, :] = dd * d_b * silu_d
        hc = d_c * d_h
        dhc = jnp.zeros((tm, w), F32)
        for k in range(CONV_D_WIDTH):
            d = dq_s[pl.ds((CONV_D_WIDTH - 1) - k, tm), :]
            dhc = dhc + cw_ref[k:k + 1, :] * d
            vec_ref[1 + k:2 + k, :] += _sum0(hc * d)
        dq_s[tm:tm + HALO_D, :] = dq_s[0:HALO_D, :]
        dzb2 = store(2, dhc * d_c)
        dzb4 = store(4, dhc * d_h)

        dx = DEEPNORM_ALPHA * dr_ref[...].astype(F32)
        for k, dzb in enumerate((dzb0, dzb1, dzb2, dzb3, dzb4, dzb5)):
            dx = dx + _dot_nt(dzb, w_ref[:, k * w:(k + 1) * w])
        dx_ref[...] = dx

    rev = lambda cols: pl.BlockSpec((tm, cols), lambda i: (nt - 1 - i, 0))
    return _call(
        body, name="odd_bwd", grid=(nt,), rider=rider,
        out_shape=(jax.ShapeDtypeStruct((t_len, 6 * w), BF16), jax.ShapeDtypeStruct((t_len, d_model), F32),
                   jax.ShapeDtypeStruct((n_groups, LANE, LANE), F32), jax.ShapeDtypeStruct((SUBLANE, w), F32),
                   jax.ShapeDtypeStruct((SUBLANE, 6 * w), F32)),
        in_specs=[rev(d_model), rev(6 * w), rev(w), rev(w), rev(d_model), _const(w_in.shape), _const(w_pool.shape),
                  _const(pool_scale.shape), _const(conv_w.shape)],
        out_specs=(rev(6 * w), rev(d_model), _const((n_groups, LANE, LANE)), _const((SUBLANE, w)), _const((SUBLANE, 6 * w))),
        scratch=[pltpu.VMEM((tm + HALO_C + SUBLANE, w), F32), pltpu.VMEM((tm + HALO_D, w), F32),
                 pltpu.VMEM((2, tm + HALO_C + SUBLANE, LANE), F32)],
        args=[dy, z, pooled, q, dr, w_in, w_pool, pool_scale, conv_w])


def _weight_grad(name, a, b, a_layer=None, bn=None, a_cols=None, rider=None):
    if a_layer is None:
        t_len, m = a.shape
        col = 0
        if a_cols is not None:
            col, m = a_cols
        a_spec = lambda tk: pl.BlockSpec((tk, m), lambda n, k: (k, col))
    else:
        _, t_len, m = a.shape
        a_spec = lambda tk: pl.BlockSpec((None, tk, m), lambda n, k: (a_layer, k, 0))
    n_cols = b.shape[1]
    bn = n_cols if bn is None else bn
    tk = min(t_len, 1024)
    n_k = t_len // tk

    def body(a_ref, b_ref, o_ref, acc):
        k = pl.program_id(1)

        @pl.when(k == 0)
        def _():
            acc[...] = jnp.zeros_like(acc)
        acc[...] += _dot_tn(a_ref[...].astype(BF16), b_ref[...].astype(BF16))

        @pl.when(k == n_k - 1)
        def _():
            o_ref[...] = acc[...].astype(BF16)

    outs = _call(
        body, name=name, grid=(n_cols // bn, n_k), out_shape=[jax.ShapeDtypeStruct((m, n_cols), BF16)],
        in_specs=[a_spec(tk), pl.BlockSpec((tk, bn), lambda n, k: (k, n))],
        out_specs=[pl.BlockSpec((m, bn), lambda n, k: (0, n))], scratch=[pltpu.VMEM((m, bn), F32)],
        args=[a, b], rider=rider)
    return outs[0] if rider is None else outs


def _weight_grads_post(name, y, dr, rh, ln_g, ln_b, dgl, p_all, layer, de):
    t_len, d_model = y.shape
    d_ple = p_all.shape[-1]
    tk = min(t_len, 1024)
    n_k = t_len // tk

    def body(y_ref, dr_ref, rh_ref, g_ref, b_ref, dgl_ref, p_ref, de_ref, o_out, o_gate, o_ple, acc_out, acc_gate, acc_ple):
        k = pl.program_id(0)

        @pl.when(k == 0)
        def _():
            acc_out[...] = jnp.zeros_like(acc_out)
            acc_gate[...] = jnp.zeros_like(acc_gate)
            acc_ple[...] = jnp.zeros_like(acc_ple)
        acc_out[...] += _dot_tn(y_ref[...], dr_ref[...])
        h = (rh_ref[...].astype(F32) * g_ref[...] + b_ref[...]).astype(BF16)
        acc_gate[...] += _dot_tn(h, dgl_ref[...])
        acc_ple[...] += _dot_tn(p_ref[...].astype(BF16), de_ref[...])

        @pl.when(k == n_k - 1)
        def _():
            o_out[...] = acc_out[...].astype(BF16)
            o_gate[...] = acc_gate[...].astype(BF16)
            o_ple[...] = acc_ple[...].astype(BF16)

    row = lambda cols: pl.BlockSpec((tk, cols), lambda k: (k, 0))
    return pl.pallas_call(
        body, name=name, grid=(n_k,),
        out_shape=(jax.ShapeDtypeStruct((d_model, d_model), BF16), jax.ShapeDtypeStruct((d_model, d_model), BF16),
                   jax.ShapeDtypeStruct((d_ple, d_model), BF16)),
        in_specs=[row(d_model), row(d_model), row(d_model), _const(ln_g.shape), _const(ln_b.shape), row(d_model),
                  pl.BlockSpec((None, tk, d_ple), lambda k: (layer, k, 0)), row(d_model)],
        out_specs=(_const((d_model, d_model)), _const((d_model, d_model)), _const((d_ple, d_model))),
        scratch_shapes=[pltpu.VMEM((d_model, d_model), F32), pltpu.VMEM((d_model, d_model), F32),
                        pltpu.VMEM((d_ple, d_model), F32)],
        compiler_params=_params(1),
    )(y, dr, rh, ln_g, ln_b, dgl, p_all, de)


def _adamw_reduce(name, parts, w, m, v, rows_per_block):
    n_rows, n_cols = w.shape
    br = rows_per_block
    n_parts = parts.shape[0]

    def body(p_ref, w_ref, m_ref, v_ref, g_ref, d_ref, nm_ref, nv_ref):
        g = p_ref[0].astype(F32)
        for k in range(1, n_parts):
            g = g + p_ref[k].astype(F32)
        nm = ADAM_B1 * m_ref[...] + (1.0 - ADAM_B1) * g
        nv = ADAM_B2 * v_ref[...] + (1.0 - ADAM_B2) * (g * g)
        m_hat = nm / (1.0 - ADAM_B1 ** ADAM_STEP)
        v_hat = nv / (1.0 - ADAM_B2 ** ADAM_STEP)
        g_ref[...] = g
        d_ref[...] = -ADAM_LR * (m_hat / (jnp.sqrt(v_hat) + ADAM_EPS) + ADAM_WD * w_ref[...])
        nm_ref[...] = nm
        nv_ref[...] = nv

    blk = pl.BlockSpec((br, n_cols), lambda i: (i, 0))
    shp = jax.ShapeDtypeStruct((n_rows, n_cols), F32)
    return pl.pallas_call(
        body, name=name, grid=(n_rows // br,), out_shape=(shp, shp, shp, shp),
        in_specs=[pl.BlockSpec((n_parts, br, n_cols), lambda i: (0, i, 0)), blk, blk, blk], out_specs=(blk, blk, blk, blk),
        compiler_params=_params(1),
    )(parts, w, m, v)


def _pack_rows(flat_parts, pad_to=None, width=LANE):
    flat = jnp.concatenate([a.reshape(-1) for a in flat_parts])
    if pad_to is not None and pad_to > flat.shape[0]:
        flat = jnp.concatenate([flat, jnp.zeros((pad_to - flat.shape[0],), flat.dtype)])
    return flat.reshape(-1, width)


def _unpack_rows(packed, shapes):
    flat = packed.reshape(-1)
    out, off = [], 0
    for s in shapes:
        n = math.prod(s)
        out.append(flat[off:off + n].reshape(s))
        off += n
    return out


def _to_dest_major(full):
    lead, last = full.shape[:-1], full.shape[-1]
    t = full.reshape(lead + (N_DEV, last // N_DEV))
    return jnp.moveaxis(t, -2, 0).reshape(N_DEV, -1)


def _from_source_major(blocks, shard_shape):
    t = blocks.reshape((N_DEV,) + tuple(shard_shape))
    t = jnp.moveaxis(t, 0, -2)
    return t.reshape(tuple(shard_shape[:-1]) + (N_DEV * shard_shape[-1],))


def _block_rows(n_rows, n_cols, target_elems=96 * 1024):
    best = None
    for br in range(SUBLANE, n_rows + 1, SUBLANE):
        if n_rows % br == 0 and br * n_cols <= target_elems:
            best = br
    return n_rows if best is None else best


def kernel(x, p, w_in_e, b_in_e, conv_a_w, conv_a_b, ln_a_g, ln_a_b, ln_v_g, ln_v_b, w_s, b_s, w_out_e, b_out_e, w_in_o, b_in_o, w_pool, pool_scale, conv_d_w, w_out_o, b_out_o, ln_g, ln_b, w_ple, w_ple_gate, b_ple_gate, loss_target, m_w_in_e, m_b_in_e, m_conv_a_w, m_conv_a_b, m_ln_a_g, m_ln_a_b, m_ln_v_g, m_ln_v_b, m_w_s, m_b_s, m_w_out_e, m_b_out_e, m_w_in_o, m_b_in_o, m_w_pool, m_pool_scale, m_conv_d_w, m_w_out_o, m_b_out_o, m_ln_g, m_ln_b, m_w_ple, m_w_ple_gate, m_b_ple_gate, v_w_in_e, v_b_in_e, v_conv_a_w, v_conv_a_b, v_ln_a_g, v_ln_a_b, v_ln_v_g, v_ln_v_b, v_w_s, v_b_s, v_w_out_e, v_b_out_e, v_w_in_o, v_b_in_o, v_w_pool, v_pool_scale, v_conv_d_w, v_w_out_o, v_b_out_o, v_ln_g, v_ln_b, v_w_ple, v_w_ple_gate, v_b_ple_gate):
    weights = dict(w_in_e=w_in_e, b_in_e=b_in_e, conv_a_w=conv_a_w, conv_a_b=conv_a_b, ln_a_g=ln_a_g, ln_a_b=ln_a_b,
                   ln_v_g=ln_v_g, ln_v_b=ln_v_b, w_s=w_s, b_s=b_s, w_out_e=w_out_e, b_out_e=b_out_e, w_in_o=w_in_o,
                   b_in_o=b_in_o, w_pool=w_pool, pool_scale=pool_scale, conv_d_w=conv_d_w, w_out_o=w_out_o,
                   b_out_o=b_out_o, ln_g=ln_g, ln_b=ln_b, w_ple=w_ple, w_ple_gate=w_ple_gate, b_ple_gate=b_ple_gate)
    mom_m = dict(w_in_e=m_w_in_e, b_in_e=m_b_in_e, conv_a_w=m_conv_a_w, conv_a_b=m_conv_a_b, ln_a_g=m_ln_a_g,
                 ln_a_b=m_ln_a_b, ln_v_g=m_ln_v_g, ln_v_b=m_ln_v_b, w_s=m_w_s, b_s=m_b_s, w_out_e=m_w_out_e,
                 b_out_e=m_b_out_e, w_in_o=m_w_in_o, b_in_o=m_b_in_o, w_pool=m_w_pool, pool_scale=m_pool_scale,
                 conv_d_w=m_conv_d_w, w_out_o=m_w_out_o, b_out_o=m_b_out_o, ln_g=m_ln_g, ln_b=m_ln_b, w_ple=m_w_ple,
                 w_ple_gate=m_w_ple_gate, b_ple_gate=m_b_ple_gate)
    mom_v = dict(w_in_e=v_w_in_e, b_in_e=v_b_in_e, conv_a_w=v_conv_a_w, conv_a_b=v_conv_a_b, ln_a_g=v_ln_a_g,
                 ln_a_b=v_ln_a_b, ln_v_g=v_ln_v_g, ln_v_b=v_ln_v_b, w_s=v_w_s, b_s=v_b_s, w_out_e=v_w_out_e,
                 b_out_e=v_b_out_e, w_in_o=v_w_in_o, b_in_o=v_b_in_o, w_pool=v_w_pool, pool_scale=v_pool_scale,
                 conv_d_w=v_conv_d_w, w_out_o=v_w_out_o, b_out_o=v_b_out_o, ln_g=v_ln_g, ln_b=v_ln_b, w_ple=v_w_ple,
                 w_ple_gate=v_w_ple_gate, b_ple_gate=v_b_ple_gate)
    names = tuple(weights)

    batch, seq, d_model = x.shape
    t_len = batch * seq
    w = d_model // 2
    n_even = w_in_e.shape[0]
    n_odd = w_in_o.shape[0]
    depth = ln_g.shape[0]
    d_ple = p.shape[-1]
    n_heads = w_s.shape[1]
    tm = 512 if seq % 512 == 0 and seq >= 1024 else seq // 2
    in_cols = w_in_e.shape[-1]
    out_rows = w_out_e.shape[1]
    ple_cols = w_ple.shape[-1]
    gate_rows = w_ple_gate.shape[1]

    sh_shapes = [weights[n].shape for n in SH_NAMES]
    sh_len = sum(math.prod(s) for s in sh_shapes)
    sh_pad = -(-sh_len // (SUBLANE * LANE)) * (SUBLANE * LANE)
    sh_rows = sh_pad // LANE
    sds = jax.ShapeDtypeStruct
    w_in16 = (w_in_e.astype(BF16), w_in_o.astype(BF16))
    w_out16 = (w_out_e.astype(BF16), w_out_o.astype(BF16))
    w_ple16, w_gate16 = w_ple.astype(BF16), w_ple_gate.astype(BF16)

    kinds = ("in", "out", "ple", "gate")

    def weight_entries(i, which=kinds):
        j, par = i // 2, i % 2
        all_four = {"in": (w_in16[par][j], _whole_view, sds((d_model, N_DEV * in_cols), BF16), _axis_view(1, in_cols)),
                    "out": (w_out16[par][j], _whole_view, sds((N_DEV * out_rows, d_model), BF16), _axis_view(0, out_rows)),
                    "ple": (w_ple16[i], _whole_view, sds((d_ple, N_DEV * ple_cols), BF16), _axis_view(1, ple_cols)),
                    "gate": (w_gate16[i], _whole_view, sds((N_DEV * gate_rows, d_model), BF16), _axis_view(0, gate_rows))}
        return [((i, k), all_four[k]) for k in which]

    fwd_riders = {("mixer", 0): weight_entries(0, kinds[1:]) + weight_entries(1, kinds[:1]),
                  ("post", 0): weight_entries(1, kinds[1:])}
    for i in range(1, depth - 1):
        if i % 2:
            fwd_riders[("mixer", i)] = weight_entries(i + 1, kinds[:1])
            fwd_riders[("post", i)] = weight_entries(i + 1, kinds[1:])
        else:
            fwd_riders[("mixer", i)] = weight_entries(i + 1)
    layer_w = {}

    def carried(where):
        tagged = fwd_riders.get(where)
        if tagged is None:
            return None, lambda landed: None
        return _Rider([e for _, e in tagged]), lambda landed: layer_w.update(zip([t for t, _ in tagged], landed))

    w_in_first = weight_entries(0, kinds[:1])[0][1]
    first = _gather_two_level("gather_first", [w_in_first[0], _pack_rows([weights[n] for n in SH_NAMES], sh_pad)],
                              [w_in_first[2], sds((N_DEV, sh_rows, LANE), F32)], [w_in_first[3], _slot_view()])
    layer_w[(0, "in")] = first[0]
    sh_flat = first[1].reshape(N_DEV, sh_pad)
    full_small, off = {}, 0
    for n, s in zip(SH_NAMES, sh_shapes):
        size = math.prod(s)
        full_small[n] = _from_source_major(sh_flat[:, off:off + size], s)
        off += size

    tril = jnp.tril(jnp.ones((LANE, LANE), dtype=bool))
    ws_m = jnp.where(tril[None, None], w_s, 0.0)
    pair = lambda t: jnp.concatenate([t[:, 0::2], t[:, 1::2]], axis=-1).astype(BF16)
    wcat = pair(ws_m)
    wcat_t = pair(jnp.swapaxes(ws_m, -1, -2))
    bs_full = jnp.repeat(jnp.swapaxes(b_s, -1, -2), w // n_heads, axis=-1)
    row2 = lambda a, j: a[j][None, :]

    x2 = x.reshape(t_len, d_model)
    p3 = p.reshape(depth, t_len, d_ple)
    lt2 = loss_target.reshape(t_len, d_model)

    xs, saved = [x2], []
    dxn = sse = None
    for i in range(depth):
        j = i // 2
        last = i == depth - 1
        rider, file_weights = carried(("mixer", i))
        if i % 2 == 0:
            outs = _even_fwd(xs[i], layer_w[(i, "in")], row2(b_in_e, j), full_small["conv_a_w"][j], row2(conv_a_b, j),
                             row2(ln_a_g, j), row2(ln_a_b, j), row2(ln_v_g, j), row2(ln_v_b, j), wcat[j], bs_full[j],
                             seq=seq, tm=tm, rider=rider)
            b_out = row2(b_out_e, j)
        else:
            outs = _odd_fwd(xs[i], layer_w[(i, "in")], row2(full_small["b_in_o"], j), w_pool[j].astype(BF16),
                            row2(full_small["pool_scale"], j), full_small["conv_d_w"][j], seq=seq, tm=tm, rider=rider)
            b_out = row2(full_small["b_out_o"], j)
        z, y, s1, s2 = outs[:4]
        file_weights(outs[4:])
        rider, file_weights = carried(("post", i))
        outs = _post_fwd(y, xs[i], p3, i, layer_w[(i, "out")], b_out, row2(ln_g, i), row2(ln_b, i), layer_w[(i, "gate")],
                         row2(b_ple_gate, i), layer_w[(i, "ple")], lt2 if last else None, tm=tm, rider=rider)
        rh, rstd, gate, e = outs[:4]
        if last:
            dxn, sse = outs[4:6]
        else:
            xs.append(outs[4])
        file_weights(outs[6 if last else 5:])
        saved.append(dict(z=z, y=y, s1=s1, s2=s2, rh=rh, rstd=rstd, gate=gate, e=e))


    recv = {"w_in_e": None, "w_in_o": None, "w_out_e": None, "w_out_o": None, "w_ple": None, "w_ple_gate": None}

    def grad_entry(i, kind, g, half=None):
        j, par = i // 2, i % 2
        sfx = "_o" if par else "_e"
        name, src_view, slot = {"in": ("w_in" + sfx, _axis_view(1, in_cols), j), "out": ("w_out" + sfx, _axis_view(0, out_rows), j),
                                "ple": ("w_ple", _axis_view(1, ple_cols), i), "gate": ("w_ple_gate", _axis_view(0, gate_rows), i)}[kind]
        dst = recv[name] if recv[name] is not None else sds((N_DEV,) + weights[name].shape, BF16)
        if half is None:
            dst_view = _slot_view(slot)
        else:
            dst_view = lambda ref, d: ref.at[d, slot, pl.ds(half[0] * half[1], half[1]), :]
        return name, (g, src_view, dst, dst_view)

    def ride(tagged):
        if not tagged:
            return None, lambda landed: None
        return _Rider([e for _, e in tagged]), lambda landed: recv.update(zip([n for n, _ in tagged], landed))

    small = {n: [None] * weights[n].shape[0] for n in REP_NAMES + SH_NAMES}
    grads = {}
    for i in reversed(range(depth)):
        j, par = i // 2, i % 2
        sv = saved[i]
        w_in = layer_w[(i, "in")]
        early = [(i + 1, "in")] if par == 1 and (i + 1, "in") in grads else []
        rider, file_landed = ride([grad_entry(l, k, grads.pop((l, k))) for l, k in early])
        outs = _post_bwd(dxn, sv["gate"], sv["e"], sv["rh"], sv["rstd"], row2(ln_g, i), layer_w[(i, "gate")],
                         layer_w[(i, "out")], tm=tm, rider=rider)
        de, dgl, dy, dr, acc = outs[:5]
        file_landed(outs[5:])
        small["b_ple_gate"][i], small["ln_g"][i], small["ln_b"][i] = acc[0], acc[1], acc[2]
        small["b_out_o" if par else "b_out_e"][j] = acc[3]
        grads[(i, "out")], grads[(i, "gate")], grads[(i, "ple")] = _weight_grads_post(
            f"dw_post_l{i}", sv["y"], dr, sv["rh"], row2(ln_g, i), row2(ln_b, i), dgl, p3, i, de)
        rider, file_landed = ride([grad_entry(l, k, grads.pop((l, k))) for l, k in list(grads)])
        if par == 0:
            outs = _even_bwd(dy, sv["z"], sv["s1"], sv["s2"], dr, w_in, full_small["conv_a_w"][j],
                             row2(ln_a_g, j), row2(ln_a_b, j), row2(ln_v_g, j), row2(ln_v_b, j), wcat_t[j], seq=seq, tm=tm,
                             rider=rider)
            dz, dx, dcw, vec, dbin, dws, dbs = outs[:7]
            file_landed(outs[7:])
            small["conv_a_w"][j], small["conv_a_b"][j] = dcw, vec[0]
            small["ln_a_g"][j], small["ln_a_b"][j], small["ln_v_g"][j], small["ln_v_b"][j] = vec[1], vec[2], vec[3], vec[4]
            small["b_in_e"][j], small["w_s"][j] = dbin[0], dws
            small["b_s"][j] = _head_sums(dbs, n_heads)
        else:
            outs = _odd_bwd(dy, sv["z"], sv["s1"], sv["s2"], dr, w_in, w_pool[j].astype(BF16),
                            row2(full_small["pool_scale"], j), full_small["conv_d_w"][j], seq=seq, tm=tm, rider=rider)
            dz, dx, dwp, vec, dbin = outs[:5]
            file_landed(outs[5:])
            small["w_pool"][j], small["pool_scale"][j], small["conv_d_w"][j] = dwp, vec[0], vec[1:1 + CONV_D_WIDTH]
            small["b_in_o"][j] = dbin[0]
        if i > 0:
            grads[(i, "in")] = _weight_grad(f"dw_in_l{i}", xs[i], dz, bn=in_cols * N_DEV // 2)
        dxn = dx
    grad_x = dxn.reshape(batch, seq, d_model)

    half_rows = d_model // 2
    top = _weight_grad("dw_in_l0_top", xs[0], dz, a_cols=(0, half_rows))
    rider, file_landed = ride([grad_entry(0, "in", top, half=(0, half_rows))])
    outs = _weight_grad("dw_in_l0_bottom", xs[0], dz, a_cols=(1, half_rows), rider=rider)
    file_landed(outs[1:])

    small_full = {n: jnp.stack(small[n]) for n in small}
    sh_part = jnp.concatenate([_to_dest_major(small_full[n]) for n in SH_NAMES], axis=1)
    sh_part = jnp.concatenate([sh_part, jnp.zeros((N_DEV, sh_pad - sh_len), F32)], axis=1).reshape(N_DEV, sh_rows, LANE)
    name, entry = grad_entry(0, "in", outs[0], half=(1, half_rows))
    landed = _exchange("exchange_last", [entry, (sh_part, _slot_view(), sds((N_DEV, sh_rows, LANE), F32), _slot_view())])
    recv[name] = landed[0]

    rep_width = 4 * LANE
    rep_len = sum(math.prod(weights[n].shape) for n in REP_NAMES)
    rep_block = N_DEV * SUBLANE
    rep_pad = -(-(rep_len + 1) // (rep_block * rep_width)) * (rep_block * rep_width)
    rep_sum = _allreduce_rows("allreduce_replicated",
                              _pack_rows([small_full[n] for n in REP_NAMES] + [sse[0:1, 0]], rep_pad, rep_width))
    loss = (0.5 / d_model) * rep_sum.reshape(-1)[rep_len]

    results = {}
    for n, parts in recv.items():
        shp = weights[n].shape
        rows, cols = math.prod(shp[:-1]), shp[-1]
        two = lambda a: a.reshape(rows, cols)
        outs = _adamw_reduce("adamw_" + n, parts.reshape(N_DEV, rows, cols), two(weights[n]), two(mom_m[n]), two(mom_v[n]),
                             _block_rows(rows, cols))
        results[n] = [o.reshape(shp) for o in outs]
    pack_sh = lambda d: _pack_rows([d[n] for n in SH_NAMES], sh_pad)
    outs = _adamw_reduce("adamw_small_sharded", landed[1], pack_sh(weights), pack_sh(mom_m), pack_sh(mom_v), sh_rows)
    for n, *vals in zip(SH_NAMES, *[_unpack_rows(o, sh_shapes) for o in outs]):
        results[n] = vals
    pack_rep = lambda d: _pack_rows([d[n] for n in REP_NAMES], rep_pad, rep_width)
    rep_shapes = [weights[n].shape for n in REP_NAMES]
    outs = _adamw_reduce("adamw_replicated", rep_sum[None], pack_rep(weights), pack_rep(mom_m), pack_rep(mom_v), rep_block)
    for n, *vals in zip(REP_NAMES, *[_unpack_rows(o, rep_shapes) for o in outs]):
        results[n] = vals

    return (loss, grad_x, *[results[n][0] for n in names], *[results[n][1] for n in names],
            *[results[n][2] for n in names], *[results[n][3] for n in names])


def _head_sums(dbs, n_heads):
    t, width = dbs.shape
    return jnp.sum(dbs.reshape(t, n_heads, width // n_heads), axis=-1).T
```

```python
import functools
import math

import jax
import jax.numpy as jnp
from jax import lax
from jax.experimental import pallas as pl
from jax.experimental.pallas import tpu as pltpu

F32 = jnp.float32
BF16 = jnp.bfloat16

N_DEV = 8
DEPTH = 4
LN_EPS = 1e-5
DEEPNORM_ALPHA = (2.0 * DEPTH) ** 0.25
POOL_WINDOWS = (2, 4, 8, 16)
CONV_A_WIDTH = 31
CONV_D_WIDTH = 3
GELU_C = math.sqrt(2.0 / math.pi)
GELU_K = 0.044715

ADAM_LR = 0.001
ADAM_B1 = 0.9
ADAM_B2 = 0.999
ADAM_EPS = 1e-08
ADAM_WD = 0.01
ADAM_STEP = 10

LANE = 128
SUBLANE = 8
HALO_A = 32
HALO_C = 16
HALO_D = 8
ROW_CHUNK = 32
VMEM_LIMIT = 56 * 2**20

ANY = pl.BlockSpec(memory_space=pl.ANY)
MESH = pl.DeviceIdType.MESH

REP_NAMES = ("b_in_e", "conv_a_b", "ln_a_g", "ln_a_b", "ln_v_g", "ln_v_b", "w_s", "b_s", "b_out_e", "w_pool", "ln_g",
             "ln_b", "b_ple_gate")
SH_NAMES = ("conv_a_w", "conv_d_w", "pool_scale", "b_in_o", "b_out_o")


def _params(n_grid_axes):
    return pltpu.CompilerParams(dimension_semantics=("arbitrary",) * n_grid_axes, vmem_limit_bytes=VMEM_LIMIT)


def _const(shape):
    nd = len(shape)
    return pl.BlockSpec(shape, lambda *_: (0,) * nd)


def _dot(a, b):
    return jnp.dot(a, b, preferred_element_type=F32)


def _dot_nt(a, b):
    return lax.dot_general(a, b, (((1,), (1,)), ((), ())), preferred_element_type=F32)


def _dot_tn(a, b):
    return lax.dot_general(a, b, (((0,), (0,)), ((), ())), preferred_element_type=F32)


def _sigmoid(x):
    return jax.nn.sigmoid(x)


def _silu(x):
    return x * _sigmoid(x)


def _silu_grad(x):
    s = _sigmoid(x)
    return x * s, s * (1.0 + x * (1.0 - s))


def _gelu(x):
    return 0.5 * x * (1.0 + jnp.tanh(GELU_C * (x + GELU_K * x * x * x)))


def _gelu_grad(x):
    x2 = x * x
    t = jnp.tanh(GELU_C * x * (1.0 + GELU_K * x2))
    return 0.5 * x * (1.0 + t), 0.5 * (1.0 + t) + 0.5 * x * (1.0 - t * t) * GELU_C * (1.0 + 3.0 * GELU_K * x2)


def _norm(v):
    mu = jnp.mean(v, axis=-1, keepdims=True)
    d = v - mu
    var = jnp.mean(d * d, axis=-1, keepdims=True)
    rstd = lax.rsqrt(var + LN_EPS)
    return d * rstd, rstd


def _norm_bwd(dxh, xh, rstd):
    return rstd * (dxh - jnp.mean(dxh, axis=-1, keepdims=True) - xh * jnp.mean(dxh * xh, axis=-1, keepdims=True))


def _sum0(v):
    return jnp.sum(v, axis=0, keepdims=True)


def _head_low_mask(n_cols):
    lane = lax.broadcasted_iota(jnp.int32, (LANE, n_cols), 1)
    return (lane & (LANE - 1)) < (LANE // 2)


def _blocks_to_lanes(v, j, nb):
    return jnp.concatenate([v[n * LANE:(n + 1) * LANE, j * LANE:(j + 1) * LANE] for n in range(nb)], axis=1)


def _axis_view(axis, size):
    def view(ref, d):
        idx = [slice(None)] * len(ref.shape)
        idx[axis] = pl.ds(pl.multiple_of(d * size, size), size)
        return ref.at[tuple(idx)]
    return view


def _slot_view(*slot):
    return lambda ref, d: ref.at[(d,) + slot]


def _whole_view(ref, d):
    return ref


PEER_BITS = (1, 2, 4, 6, 3, 5, 7)


class _Rider:
    def __init__(self, entries):
        self.n = len(entries)
        self.srcs = [e[0] for e in entries]
        self.src_views = [e[1] for e in entries]
        self.dsts = [e[2] for e in entries]
        self.dst_views = [e[3] for e in entries]
        self.passed = [a for a, d in enumerate(self.dsts) if not isinstance(d, jax.ShapeDtypeStruct)]

    def operands(self):
        return self.srcs + [self.dsts[a] for a in self.passed]

    def out_shape(self):
        return [jax.ShapeDtypeStruct(d.shape, d.dtype) for d in self.dsts]

    def scratch(self):
        return [pltpu.SemaphoreType.DMA((7 * self.n,)), pltpu.SemaphoreType.DMA((7 * self.n,)), pltpu.SemaphoreType.DMA((self.n,))]

    def aliases(self, n_in_before, n_out_before):
        return {n_in_before + self.n + q: n_out_before + a for q, a in enumerate(self.passed)}

    def _copies(self, src, dst, sems):
        send_sems, recv_sems, local_sems = sems
        x, y, c = lax.axis_index("x"), lax.axis_index("y"), lax.axis_index("c")
        me = 4 * x + 2 * y + c
        local = [pltpu.make_async_copy(self.src_views[a](src[a], me), self.dst_views[a](dst[a], me), local_sems.at[a])
                 for a in range(self.n)]
        sends, recvs = [], []
        for ki, k in enumerate(PEER_BITS):
            px, py, pc = x ^ (k >> 2), y ^ ((k >> 1) & 1), c ^ (k & 1)
            peer = 4 * px + 2 * py + pc
            for a in range(self.n):
                s = a * 7 + ki
                mk = lambda landing: pltpu.make_async_remote_copy(
                    src_ref=self.src_views[a](src[a], peer), dst_ref=self.dst_views[a](dst[a], landing),
                    send_sem=send_sems.at[s], recv_sem=recv_sems.at[s], device_id=(px, py, pc), device_id_type=MESH)
                sends.append(mk(me))
                recvs.append(mk(peer))
        return local, sends, recvs

    def start(self, src, dst, sems):
        local, sends, _ = self._copies(src, dst, sems)
        for cp in local + sends:
            cp.start()

    def wait(self, src, dst, sems):
        local, sends, recvs = self._copies(src, dst, sems)
        for cp in recvs:
            cp.wait_recv()
        for cp in sends:
            cp.wait_send()
        for cp in local:
            cp.wait()


def _call(body, *, name, grid, in_specs, args, out_shape, out_specs, scratch=(), rider=None, aliases=None):
    n_in, n_out, n_scr = len(args), len(out_shape), len(scratch)
    aliases = dict(aliases or {})
    if rider is None:
        return pl.pallas_call(body, name=name, grid=grid, out_shape=tuple(out_shape), in_specs=list(in_specs),
                              out_specs=tuple(out_specs), scratch_shapes=list(scratch), input_output_aliases=aliases,
                              compiler_params=_params(len(grid)))(*args)
    r_ops = rider.operands()

    def full_body(*refs):
        ins, refs = refs[:n_in], refs[n_in:]
        r_src, refs = refs[:rider.n], refs[len(r_ops):]
        outs, refs = refs[:n_out], refs[n_out:]
        r_dst, refs = refs[:rider.n], refs[rider.n:]
        scr, sems = refs[:n_scr], refs[n_scr:]
        if grid:
            first = last = None
            for axis, size in enumerate(grid):
                at_start, at_end = pl.program_id(axis) == 0, pl.program_id(axis) == size - 1
                first = at_start if first is None else jnp.logical_and(first, at_start)
                last = at_end if last is None else jnp.logical_and(last, at_end)

            @pl.when(first)
            def _():
                rider.start(r_src, r_dst, sems)
            body(*ins, *outs, *scr)

            @pl.when(last)
            def _():
                rider.wait(r_src, r_dst, sems)
        else:
            rider.start(r_src, r_dst, sems)
            if body is not None:
                body(*ins, *outs, *scr)
            rider.wait(r_src, r_dst, sems)

    aliases.update(rider.aliases(n_in, n_out))
    kw = dict(compiler_params=_params(len(grid))) if grid else {}
    if grid:
        kw["grid"] = grid
    return pl.pallas_call(
        full_body, name=name, out_shape=tuple(out_shape) + tuple(rider.out_shape()),
        in_specs=list(in_specs) + [ANY] * len(r_ops), out_specs=tuple(out_specs) + tuple([ANY] * rider.n),
        scratch_shapes=list(scratch) + rider.scratch(), input_output_aliases=aliases, **kw)(*args, *r_ops)


def _gather_two_level(name, shards, dst_shapes, dst_views):
    n = len(shards)

    def body(*refs):
        src, dst = refs[:n], refs[n:2 * n]
        send_sems, recv_sems, local_sems = refs[2 * n:]
        x, y, c = lax.axis_index("x"), lax.axis_index("y"), lax.axis_index("c")
        flat = lambda dev: 4 * dev[0] + 2 * dev[1] + dev[2]
        me, sibling = (x, y, c), (x, y, 1 - c)
        chips = [(1 - x, y), (x, 1 - y), (1 - x, 1 - y)]

        def copy(a, k, block, to, from_shard=False):
            place = dst_views[a](dst[a], flat(block))
            return pltpu.make_async_remote_copy(src_ref=src[a] if from_shard else place, dst_ref=place,
                                                send_sem=send_sems.at[7 * a + k], recv_sem=recv_sems.at[7 * a + k],
                                                device_id=to, device_id_type=MESH)

        mine = [pltpu.make_async_copy(src[a], dst_views[a](dst[a], flat(me)), local_sems.at[a]) for a in range(n)]
        first = [copy(a, 0, me, sibling, True) for a in range(n)]
        first += [copy(a, 1 + j, me, (*chip, c), True) for j, chip in enumerate(chips) for a in range(n)]
        for cp in mine + first:
            cp.start()
        passed = []
        for j, chip in enumerate(chips):
            for a in range(n):
                copy(a, 1 + j, (*chip, c), me).wait_recv()
                passed.append(copy(a, 4 + j, (*chip, c), sibling))
                passed[-1].start()
        for a in range(n):
            copy(a, 0, sibling, me).wait_recv()
            for j, chip in enumerate(chips):
                copy(a, 4 + j, (*chip, 1 - c), me).wait_recv()
        for cp in first + passed:
            cp.wait_send()
        for cp in mine:
            cp.wait()

    return pl.pallas_call(
        body, name=name, out_shape=tuple(dst_shapes), in_specs=[ANY] * n, out_specs=tuple([ANY] * n),
        scratch_shapes=[pltpu.SemaphoreType.DMA((7 * n,)), pltpu.SemaphoreType.DMA((7 * n,)), pltpu.SemaphoreType.DMA((n,))],
    )(*shards)


def _allreduce_rows(name, part, rider):
    n_rows, n_cols = part.shape
    sl = n_rows // N_DEV

    def body(p_ref, o_ref, recv_v, sum_v, send1, recv1, send2, recv2, local_sems):
        x, y, c = lax.axis_index("x"), lax.axis_index("y"), lax.axis_index("c")
        me = 4 * x + 2 * y + c
        rows_of = lambda d: pl.ds(pl.multiple_of(d * sl, SUBLANE), sl)
        peers = []
        for ki, k in enumerate(PEER_BITS):
            px, py, pc = x ^ (k >> 2), y ^ ((k >> 1) & 1), c ^ (k & 1)
            peers.append((ki, (px, py, pc), 4 * px + 2 * py + pc))

        def scatter(ki, dev, peer, landing):
            return pltpu.make_async_remote_copy(src_ref=p_ref.at[rows_of(peer)], dst_ref=recv_v.at[landing],
                                                send_sem=send1.at[ki], recv_sem=recv1.at[ki], device_id=dev, device_id_type=MESH)

        def gather(ki, dev, landing):
            return pltpu.make_async_remote_copy(src_ref=sum_v, dst_ref=o_ref.at[rows_of(landing)],
                                                send_sem=send2.at[ki], recv_sem=recv2.at[ki], device_id=dev, device_id_type=MESH)

        own = pltpu.make_async_copy(p_ref.at[rows_of(me)], recv_v.at[me], local_sems.at[0])
        own.start()
        for ki, dev, peer in peers:
            scatter(ki, dev, peer, me).start()
        for ki, dev, peer in peers:
            scatter(ki, dev, peer, peer).wait_recv()
        own.wait()
        total = recv_v[0]
        for d in range(1, N_DEV):
            total = total + recv_v[d]
        sum_v[...] = total
        own = pltpu.make_async_copy(sum_v, o_ref.at[rows_of(me)], local_sems.at[1])
        own.start()
        for ki, dev, peer in peers:
            gather(ki, dev, me).start()
        for ki, dev, peer in peers:
            gather(ki, dev, peer).wait_recv()
        for ki, dev, peer in peers:
            scatter(ki, dev, peer, me).wait_send()
            gather(ki, dev, me).wait_send()
        own.wait()

    return _call(
        body, name=name, grid=(), out_shape=[jax.ShapeDtypeStruct((n_rows, n_cols), F32)], in_specs=[ANY], out_specs=[ANY],
        scratch=[pltpu.VMEM((N_DEV, sl, n_cols), F32), pltpu.VMEM((sl, n_cols), F32),
                 pltpu.SemaphoreType.DMA((7,)), pltpu.SemaphoreType.DMA((7,)), pltpu.SemaphoreType.DMA((7,)),
                 pltpu.SemaphoreType.DMA((7,)), pltpu.SemaphoreType.DMA((2,))],
        args=[part], rider=rider)


def _shifted_windows(buf, shift_s, n_rows):
    for o in range(1, SUBLANE):
        shift_s[o - 1, 0:n_rows - SUBLANE, :] = buf[pl.ds(o, n_rows - SUBLANE), :]

    def window(s):
        q, o = divmod(s, SUBLANE)
        src = buf if o == 0 else shift_s.at[o - 1]
        return src[pl.ds(q * SUBLANE, ROW_CHUNK), :]
    return window


def _z_parts(x_ref, w_ref, b_ref, z_ref, width):
    xb = x_ref[...].astype(BF16)

    def part(k, keep=True, half=None):
        cols = slice(k * width, (k + 1) * width)
        if half is not None:
            cols = slice(k * width + half * (width // 2), k * width + (half + 1) * (width // 2))
        zk = (_dot(xb, w_ref[:, cols]) + b_ref[:, cols]).astype(BF16)
        z_ref[:, cols] = zk
        return zk.astype(F32) if keep else None
    return part


def _even_fwd(x, w_in, b_in, conv_w, conv_b, ln_a_g, ln_a_b, ln_v_g, ln_v_b, wcat, bs_full, *, seq, tm, rider=None):
    t_len, d_model = x.shape
    w = d_model // 2
    nt, tps, nb = t_len // tm, seq // tm, tm // LANE

    def body(x_ref, w_ref, b_ref, cw_ref, cb_ref, lag_ref, lab_ref, lvg_ref, lvb_ref, wcat_ref, bs_ref,
             z_ref, y_ref, a1_ref, sg_ref, a0_s, shift_s):
        i = pl.program_id(0)

        @pl.when(i % tps == 0)
        def _():
            a0_s[0:HALO_A, :] = jnp.zeros((HALO_A, w), F32)

        part = _z_parts(x_ref, w_ref, b_ref, z_ref, w)
        a0_s[HALO_A:HALO_A + tm, :] = part(0) * _sigmoid(part(1))
        window = _shifted_windows(a0_s, shift_s, tm + HALO_A)
        n_chunks = tm // ROW_CHUNK
        pieces = [(k, h) for k in range(2, 6) for h in range(2)]
        later = dict(zip(range(n_chunks - 1, -1, -max(1, n_chunks // len(pieces))), reversed(pieces)))
        for c, r in enumerate(range(0, tm, ROW_CHUNK)):
            acc = jnp.zeros((ROW_CHUNK, w), F32) + cb_ref[...]
            for k in range(CONV_A_WIDTH):
                acc = acc + cw_ref[k:k + 1, :] * window(HALO_A - (CONV_A_WIDTH - 1) + k + r)
            a1_ref[r:r + ROW_CHUNK, :] = acc
            if c in later:
                part(later[c][0], keep=False, half=later[c][1])
        for piece in pieces:
            if piece not in later.values():
                part(piece[0], keep=False, half=piece[1])
        zp = lambda k: z_ref[:, k * w:(k + 1) * w].astype(F32)
        a0_s[0:HALO_A, :] = a0_s[tm:tm + HALO_A, :]
        ah, _ = _norm(a1_ref[...])
        a = _silu(ah * lag_ref[...] + lab_ref[...]) * _silu(zp(2))
        y_ref[:, 0:w] = a.astype(BF16)

        u = zp(3)
        vh, _ = _norm(_gelu(zp(4)))
        v2 = vh * lvg_ref[...] + lvb_ref[...]
        low = _head_low_mask(nb * LANE)
        for j in range(w // LANE):
            vt = _blocks_to_lanes(v2, j, nb)
            rhs = jnp.concatenate([jnp.where(low, vt, 0.0), jnp.where(low, 0.0, vt)], axis=0).astype(BF16)
            out = _dot(wcat_ref[j], rhs)
            for n in range(nb):
                sg_ref[n * LANE:(n + 1) * LANE, j * LANE:(j + 1) * LANE] = (
                    out[:, n * LANE:(n + 1) * LANE] + bs_ref[:, j * LANE:(j + 1) * LANE]).astype(BF16)
        g = _gelu(u) * sg_ref[...].astype(F32) * _silu(zp(5))
        y_ref[:, w:2 * w] = g.astype(BF16)

    row = lambda cols: pl.BlockSpec((tm, cols), lambda i: (i, 0))
    return _call(
        body, name="even_fwd", grid=(nt,), rider=rider,
        out_shape=(jax.ShapeDtypeStruct((t_len, 6 * w), BF16), jax.ShapeDtypeStruct((t_len, d_model), BF16),
                   jax.ShapeDtypeStruct((t_len, w), F32), jax.ShapeDtypeStruct((t_len, w), BF16)),
        in_specs=[row(d_model), _const(w_in.shape), _const(b_in.shape), _const(conv_w.shape), _const(conv_b.shape),
                  _const(ln_a_g.shape), _const(ln_a_b.shape), _const(ln_v_g.shape), _const(ln_v_b.shape),
                  _const(wcat.shape), _const(bs_full.shape)],
        out_specs=(row(6 * w), row(d_model), row(w), row(w)),
        scratch=[pltpu.VMEM((tm + HALO_A, w), F32), pltpu.VMEM((SUBLANE - 1, tm + HALO_A, w), F32)],
        args=[x, w_in, b_in, conv_w, conv_b, ln_a_g, ln_a_b, ln_v_g, ln_v_b, wcat, bs_full])


def _doubling_sums(src, cols, lv_s, win, lo, n, step):
    read = lambda off: src[pl.ds(lo + off, n), cols]
    shift, level = 1, 0
    while shift < win:
        dst = lv_s.at[level % 2]
        dst[pl.ds(lo, n), :] = read(0) + read(step * shift)
        read = lambda off, d=dst: d[pl.ds(lo + off, n), :]
        shift, level = 2 * shift, level + 1
    final = lv_s.at[(level - 1) % 2]
    return lambda start, rows: final[pl.ds(start, rows), :]


def _pool_inverse(tile_index, tm, seq, window):
    row = tile_index * tm + lax.broadcasted_iota(jnp.int32, (tm, 1), 0)
    pos = (row % seq + 1).astype(F32)
    return 1.0 / jnp.minimum(pos, float(window))


def _odd_fwd(x, w_in, b_in, w_pool, pool_scale, conv_w, *, seq, tm, rider=None):
    t_len, d_model = x.shape
    w = d_model // 2
    nt, tps = t_len // tm, seq // tm

    def body(x_ref, w_ref, b_ref, wp_ref, ps_ref, cw_ref, z_ref, y_ref, pooled_ref, q_ref, cv_s, hc_s, lv_s):
        i = pl.program_id(0)
        first = SUBLANE + HALO_C

        @pl.when(i == 0)
        def _():
            cv_s[0:SUBLANE, :] = jnp.zeros((SUBLANE, w), F32)
            lv_s[:, 0:SUBLANE, :] = jnp.zeros((2, SUBLANE, LANE), F32)

        @pl.when(i % tps == 0)
        def _():
            cv_s[SUBLANE:first, :] = jnp.zeros((HALO_C, w), F32)
            hc_s[0:HALO_D, :] = jnp.zeros((HALO_D, w), F32)

        part = _z_parts(x_ref, w_ref, b_ref, z_ref, w)
        c_val = part(0)
        c_gate = part(1)
        cv_s[first:first + tm, :] = c_val
        for gi, win in enumerate(POOL_WINDOWS):
            cols = slice(gi * LANE, (gi + 1) * LANE)
            s = _doubling_sums(cv_s, cols, lv_s, win, SUBLANE, HALO_C + tm, -1)(first, tm)
            pooled = (s * _pool_inverse(i, tm, seq, win) - c_val[:, cols]).astype(BF16)
            pooled_ref[:, cols] = pooled
            c = _dot(pooled, wp_ref[gi]) * ps_ref[:, cols] * _silu(c_gate[:, cols])
            y_ref[:, cols] = c.astype(BF16)
        cv_s[SUBLANE:first, :] = cv_s[tm + SUBLANE:tm + first, :]

        d_h = part(2)
        d_b = part(3)
        hc_s[HALO_D:HALO_D + tm, :] = part(4) * d_h
        q = jnp.zeros((tm, w), F32)
        for k in range(CONV_D_WIDTH):
            q = q + cw_ref[k:k + 1, :] * hc_s[pl.ds(HALO_D - (CONV_D_WIDTH - 1) + k, tm), :]
        hc_s[0:HALO_D, :] = hc_s[tm:tm + HALO_D, :]
        qb = q.astype(BF16)
        q_ref[...] = qb
        y_ref[:, w:2 * w] = (d_b * qb.astype(F32) * _silu(part(5))).astype(BF16)

    row = lambda cols: pl.BlockSpec((tm, cols), lambda i: (i, 0))
    return _call(
        body, name="odd_fwd", grid=(nt,), rider=rider,
        out_shape=(jax.ShapeDtypeStruct((t_len, 6 * w), BF16), jax.ShapeDtypeStruct((t_len, d_model), BF16),
                   jax.ShapeDtypeStruct((t_len, w), BF16), jax.ShapeDtypeStruct((t_len, w), BF16)),
        in_specs=[row(d_model), _const(w_in.shape), _const(b_in.shape), _const(w_pool.shape), _const(pool_scale.shape),
                  _const(conv_w.shape)],
        out_specs=(row(6 * w), row(d_model), row(w), row(w)),
        scratch=[pltpu.VMEM((tm + HALO_C + SUBLANE, w), F32), pltpu.VMEM((tm + HALO_D, w), F32),
                 pltpu.VMEM((2, tm + HALO_C + SUBLANE, LANE), F32)],
        args=[x, w_in, b_in, w_pool, pool_scale, conv_w])


def _post_fwd(y, x, p_all, layer, w_out, b_out, ln_g, ln_b, wg, bg, wp, loss_target, *, tm, rider=None):
    t_len, d_model = x.shape
    d_ple = p_all.shape[-1]
    nt = t_len // tm
    last = loss_target is not None

    def body(*refs):
        y_ref, x_ref, p_ref, wo_ref, bo_ref, g_ref, b_ref, wg_ref, bg_ref, wp_ref = refs[:10]
        rest = refs[10:]
        if last:
            lt_ref, rest = rest[0], rest[1:]
        rh_ref, rstd_ref, gate_ref, e_ref = rest[:4]
        r = DEEPNORM_ALPHA * x_ref[...] + _dot(y_ref[...], wo_ref[...]) + bo_ref[...]
        rh, rstd = _norm(r)
        h = rh * g_ref[...] + b_ref[...]
        gate = _sigmoid(_dot(h.astype(BF16), wg_ref[...]) + bg_ref[...])
        e = _dot(p_ref[...].astype(BF16), wp_ref[...])
        xn = h + gate * e
        rh_ref[...] = rh.astype(BF16)
        rstd_ref[...] = jnp.broadcast_to(rstd, (tm, LANE))
        gate_ref[...] = gate.astype(BF16)
        e_ref[...] = e.astype(BF16)
        if not last:
            rest[4][...] = xn
        else:
            dxn_ref, sse_ref = rest[4:]
            diff = xn - lt_ref[...]
            dxn_ref[...] = diff * (1.0 / d_model)

            @pl.when(pl.program_id(0) == 0)
            def _():
                sse_ref[...] = jnp.zeros_like(sse_ref)
            sse_ref[...] += jnp.sum(_sum0(diff * diff), axis=1, keepdims=True)

    row = lambda cols: pl.BlockSpec((tm, cols), lambda i: (i, 0))
    in_specs = [row(d_model), row(d_model), pl.BlockSpec((None, tm, d_ple), lambda i: (layer, i, 0)),
                _const(w_out.shape), _const(b_out.shape), _const(ln_g.shape), _const(ln_b.shape), _const(wg.shape),
                _const(bg.shape), _const(wp.shape)]
    args = [y, x, p_all, w_out, b_out, ln_g, ln_b, wg, bg, wp]
    out_shape = [jax.ShapeDtypeStruct((t_len, d_model), BF16), jax.ShapeDtypeStruct((t_len, LANE), F32),
                 jax.ShapeDtypeStruct((t_len, d_model), BF16), jax.ShapeDtypeStruct((t_len, d_model), BF16),
                 jax.ShapeDtypeStruct((t_len, d_model), F32)]
    out_specs = [row(d_model), row(LANE), row(d_model), row(d_model), row(d_model)]
    if last:
        in_specs.append(row(d_model))
        args.append(loss_target)
        out_shape.append(jax.ShapeDtypeStruct((SUBLANE, LANE), F32))
        out_specs.append(_const((SUBLANE, LANE)))
    return _call(body, name="post_fwd_last" if last else "post_fwd", grid=(nt,), out_shape=out_shape, in_specs=in_specs,
                 out_specs=out_specs, args=args, rider=rider)


def _post_bwd(dxn, gate, e, rh, rstd, ln_g, wg, w_out, *, tm, rider=None):
    t_len, d_model = dxn.shape
    nt = t_len // tm

    def body(dxn_ref, gate_ref, e_ref, rh_ref, rstd_ref, g_ref, wg_ref, wo_ref, de_ref, dgl_ref, dy_ref, dr_ref, acc_ref):
        @pl.when(pl.program_id(0) == 0)
        def _():
            acc_ref[...] = jnp.zeros_like(acc_ref)

        d = dxn_ref[...]
        gt = gate_ref[...].astype(F32)
        rhat = rh_ref[...].astype(F32)
        de_ref[...] = (d * gt).astype(BF16)
        dgl = d * e_ref[...].astype(F32) * gt * (1.0 - gt)
        dglb = dgl.astype(BF16)
        dgl_ref[...] = dglb
        dh = d + _dot_nt(dglb, wg_ref[...])
        dr = _norm_bwd(dh * g_ref[...], rhat, rstd_ref[:, 0:1])
        drb = dr.astype(BF16)
        dr_ref[...] = drb
        dy_ref[...] = _dot_nt(drb, wo_ref[...]).astype(BF16)
        acc_ref[0:1, :] += _sum0(dgl)
        acc_ref[1:2, :] += _sum0(dh * rhat)
        acc_ref[2:3, :] += _sum0(dh)
        acc_ref[3:4, :] += _sum0(dr)

    row = lambda cols: pl.BlockSpec((tm, cols), lambda i: (i, 0))
    return _call(
        body, name="post_bwd", grid=(nt,), rider=rider,
        out_shape=(jax.ShapeDtypeStruct((t_len, d_model), BF16), jax.ShapeDtypeStruct((t_len, d_model), BF16),
                   jax.ShapeDtypeStruct((t_len, d_model), BF16), jax.ShapeDtypeStruct((t_len, d_model), BF16),
                   jax.ShapeDtypeStruct((SUBLANE, d_model), F32)),
        in_specs=[row(d_model), row(d_model), row(d_model), row(d_model), row(LANE), _const(ln_g.shape), _const(wg.shape),
                  _const(w_out.shape)],
        out_specs=(row(d_model), row(d_model), row(d_model), row(d_model), _const((SUBLANE, d_model))),
        args=[dxn, gate, e, rh, rstd, ln_g, wg, w_out])


def _dz_store(dz_ref, dbin_ref, width):
    def store(k, v):
        cols = slice(k * width, (k + 1) * width)
        vb = v.astype(BF16)
        dz_ref[:, cols] = vb
        dbin_ref[0:1, cols] += _sum0(v)
        return vb
    return store


def _even_bwd(dy, z, a1, sg, dr, w_in, conv_w, ln_a_g, ln_a_b, ln_v_g, ln_v_b, wcat_t, *, seq, tm, rider=None):
    t_len, d_model = dr.shape
    w = d_model // 2
    nt, tps, nb = t_len // tm, seq // tm, tm // LANE
    n_heads = 2 * (w // LANE)

    def body(dy_ref, z_ref, a1_ref, sg_ref, dr_ref, w_ref, cw_ref, lag_ref, lab_ref, lvg_ref, lvb_ref, wct_ref,
             dz_ref, dx_ref, dcw_ref, vec_ref, dbin_ref, dws_ref, dbs_ref, da1_s, a0_s, da0_s, cw_acc, shift_s):
        i = pl.program_id(0)
        tile = nt - 1 - i

        @pl.when(i == 0)
        def _():
            vec_ref[...] = jnp.zeros_like(vec_ref)
            dbin_ref[...] = jnp.zeros_like(dbin_ref)
            dws_ref[...] = jnp.zeros_like(dws_ref)
            dbs_ref[...] = jnp.zeros_like(dbs_ref)
            cw_acc[...] = jnp.zeros_like(cw_acc)

        @pl.when((tile + 1) % tps == 0)
        def _():
            da1_s[tm:tm + HALO_A, :] = jnp.zeros((HALO_A, w), F32)

        zp = lambda k: z_ref[:, k * w:(k + 1) * w].astype(F32)
        store = _dz_store(dz_ref, dbin_ref, w)

        dg = dy_ref[:, w:2 * w].astype(F32)
        u, v, gg = zp(3), zp(4), zp(5)
        sgv = sg_ref[...].astype(F32)
        gelu_u, dgelu_u = _gelu_grad(u)
        silu_gg, dsilu_gg = _silu_grad(gg)
        dzb5 = store(5, dg * gelu_u * sgv * dsilu_gg)
        t1 = dg * silu_gg
        dzb3 = store(3, t1 * sgv * dgelu_u)
        dsg = t1 * gelu_u
        gelu_v, dgelu_v = _gelu_grad(v)
        vh, rstd_v = _norm(gelu_v)
        v2 = vh * lvg_ref[...] + lvb_ref[...]
        low = _head_low_mask(nb * LANE)
        for j in range(w // LANE):
            dt = _blocks_to_lanes(dsg, j, nb)
            d_lo = jnp.where(low, dt, 0.0).astype(BF16)
            d_hi = jnp.where(low, 0.0, dt).astype(BF16)
            v2t = _blocks_to_lanes(v2, j, nb).astype(BF16)
            dv2t = _dot(wct_ref[j], jnp.concatenate([d_lo, d_hi], axis=0))
            for n in range(nb):
                da0_s[n * LANE:(n + 1) * LANE, j * LANE:(j + 1) * LANE] = dv2t[:, n * LANE:(n + 1) * LANE]
            dws_ref[2 * j] += _dot_nt(d_lo, v2t)
            dws_ref[2 * j + 1] += _dot_nt(d_hi, v2t)
            bsum = dt[:, 0:LANE]
            for n in range(1, nb):
                bsum = bsum + dt[:, n * LANE:(n + 1) * LANE]
            dbs_ref[:, j * LANE:(j + 1) * LANE] += bsum
        dv2 = da0_s[...]
        vec_ref[3:4, :] += _sum0(dv2 * vh)
        vec_ref[4:5, :] += _sum0(dv2)
        dzb4 = store(4, _norm_bwd(dv2 * lvg_ref[...], vh, rstd_v) * dgelu_v)
        dx_b = DEEPNORM_ALPHA * dr_ref[...].astype(F32)
        for k, dzb in ((3, dzb3), (4, dzb4), (5, dzb5)):
            dx_b = dx_b + _dot_nt(dzb, w_ref[:, k * w:(k + 1) * w])
        dx_ref[...] = dx_b

        da = dy_ref[:, 0:w].astype(F32)
        a_val, a_glu, a_gate = zp(0), zp(1), zp(2)
        s_glu = _sigmoid(a_glu)
        a0_s[...] = a_val * s_glu
        ah, rstd_a = _norm(a1_ref[...])
        silu_a2, dsilu_a2 = _silu_grad(ah * lag_ref[...] + lab_ref[...])
        silu_ag, dsilu_ag = _silu_grad(a_gate)
        dzb2 = store(2, da * silu_a2 * dsilu_ag)
        da2 = da * silu_ag * dsilu_a2
        vec_ref[1:2, :] += _sum0(da2 * ah)
        vec_ref[2:3, :] += _sum0(da2)
        da1 = _norm_bwd(da2 * lag_ref[...], ah, rstd_a)
        vec_ref[0:1, :] += _sum0(da1)
        da1_s[0:tm, :] = da1
        window = _shifted_windows(da1_s, shift_s, tm + HALO_A)
        for r in range(0, tm, ROW_CHUNK):
            a0c = a0_s[r:r + ROW_CHUNK, :]
            acc = jnp.zeros((ROW_CHUNK, w), F32)
            for k in range(CONV_A_WIDTH):
                d = window(r + (CONV_A_WIDTH - 1) - k)
                acc = acc + cw_ref[k:k + 1, :] * d
                pw = a0c * d
                p8 = pw[0:SUBLANE]
                for q in range(1, ROW_CHUNK // SUBLANE):
                    p8 = p8 + pw[q * SUBLANE:(q + 1) * SUBLANE]
                cw_acc[k * SUBLANE:(k + 1) * SUBLANE, :] += p8
            da0_s[r:r + ROW_CHUNK, :] = acc
        da1_s[tm:tm + HALO_A, :] = da1_s[0:HALO_A, :]
        da0 = da0_s[...]
        dzb0 = store(0, da0 * s_glu)
        dzb1 = store(1, da0 * a_val * s_glu * (1.0 - s_glu))

        dx = dx_ref[...]
        for k, dzb in ((0, dzb0), (1, dzb1), (2, dzb2)):
            dx = dx + _dot_nt(dzb, w_ref[:, k * w:(k + 1) * w])
        dx_ref[...] = dx

        @pl.when(i == nt - 1)
        def _():
            for k in range(CONV_A_WIDTH):
                dcw_ref[k:k + 1, :] = _sum0(cw_acc[k * SUBLANE:(k + 1) * SUBLANE, :])
            keep = (lax.broadcasted_iota(jnp.int32, (LANE, LANE), 0) >= lax.broadcasted_iota(jnp.int32, (LANE, LANE), 1))
            for hd in range(n_heads):
                dws_ref[hd] = jnp.where(keep, dws_ref[hd], 0.0)

    rev = lambda cols: pl.BlockSpec((tm, cols), lambda i: (nt - 1 - i, 0))
    return _call(
        body, name="even_bwd", grid=(nt,), rider=rider,
        out_shape=(jax.ShapeDtypeStruct((t_len, 6 * w), BF16), jax.ShapeDtypeStruct((t_len, d_model), F32),
                   jax.ShapeDtypeStruct((CONV_A_WIDTH, w), F32), jax.ShapeDtypeStruct((SUBLANE, w), F32),
                   jax.ShapeDtypeStruct((SUBLANE, 6 * w), F32), jax.ShapeDtypeStruct((n_heads, LANE, LANE), F32),
                   jax.ShapeDtypeStruct((LANE, w), F32)),
        in_specs=[rev(d_model), rev(6 * w), rev(w), rev(w), rev(d_model), _const(w_in.shape), _const(conv_w.shape),
                  _const(ln_a_g.shape), _const(ln_a_b.shape), _const(ln_v_g.shape), _const(ln_v_b.shape), _const(wcat_t.shape)],
        out_specs=(rev(6 * w), rev(d_model), _const((CONV_A_WIDTH, w)), _const((SUBLANE, w)), _const((SUBLANE, 6 * w)),
                   _const((n_heads, LANE, LANE)), _const((LANE, w))),
        scratch=[pltpu.VMEM((tm + HALO_A, w), F32), pltpu.VMEM((tm, w), F32), pltpu.VMEM((tm, w), F32),
                 pltpu.VMEM((CONV_A_WIDTH * SUBLANE, w), F32), pltpu.VMEM((SUBLANE - 1, tm + HALO_A, w), F32)],
        args=[dy, z, a1, sg, dr, w_in, conv_w, ln_a_g, ln_a_b, ln_v_g, ln_v_b, wcat_t])


def _odd_bwd(dy, z, pooled, q, dr, w_in, w_pool, pool_scale, conv_w, *, seq, tm, rider=None):
    t_len, d_model = dr.shape
    w = d_model // 2
    nt, tps = t_len // tm, seq // tm
    n_groups = len(POOL_WINDOWS)

    def body(dy_ref, z_ref, pooled_ref, q_ref, dr_ref, w_ref, wp_ref, ps_ref, cw_ref,
             dz_ref, dx_ref, dwp_ref, vec_ref, dbin_ref, dm_s, dq_s, lv_s):
        i = pl.program_id(0)
        tile = nt - 1 - i

        @pl.when(i == 0)
        def _():
            dwp_ref[...] = jnp.zeros_like(dwp_ref)
            vec_ref[...] = jnp.zeros_like(vec_ref)
            dbin_ref[...] = jnp.zeros_like(dbin_ref)
            dm_s[tm + HALO_C:tm + HALO_C + SUBLANE, :] = jnp.zeros((SUBLANE, w), F32)
            lv_s[:, tm + HALO_C:tm + HALO_C + SUBLANE, :] = jnp.zeros((2, SUBLANE, LANE), F32)

        @pl.when((tile + 1) % tps == 0)
        def _():
            dm_s[tm:tm + HALO_C, :] = jnp.zeros((HALO_C, w), F32)
            dq_s[tm:tm + HALO_D, :] = jnp.zeros((HALO_D, w), F32)

        zp = lambda k: z_ref[:, k * w:(k + 1) * w].astype(F32)
        store = _dz_store(dz_ref, dbin_ref, w)

        dc = dy_ref[:, 0:w].astype(F32)
        c_gate = zp(1)
        silu_c, dsilu_c = _silu_grad(c_gate)
        dcs = dc * silu_c
        dvg_parts, dcg_parts = [], []
        for gi, win in enumerate(POOL_WINDOWS):
            cols = slice(gi * LANE, (gi + 1) * LANE)
            pooled_g = pooled_ref[:, cols]
            wp = wp_ref[gi]
            cpre = _dot(pooled_g, wp)
            scale = ps_ref[:, cols]
            vec_ref[0:1, cols] += _sum0(dcs[:, cols] * cpre)
            dcg_parts.append(dc[:, cols] * cpre * scale * dsilu_c[:, cols])
            dcp = (dcs[:, cols] * scale).astype(BF16)
            dwp_ref[gi] += _dot_tn(pooled_g, dcp)
            dpooled = _dot_nt(dcp, wp)
            dm_s[0:tm, cols] = dpooled * _pool_inverse(tile, tm, seq, win)
            s = _doubling_sums(dm_s, cols, lv_s, win, 0, tm + HALO_C, 1)(0, tm)
            dvg_parts.append(s - dpooled)
        dm_s[tm:tm + HALO_C, :] = dm_s[0:HALO_C, :]
        dzb0 = store(0, jnp.concatenate(dvg_parts, axis=1))
        dzb1 = store(1, jnp.concatenate(dcg_parts, axis=1))

        dd = dy_ref[:, w:2 * w].astype(F32)
        d_h, d_b, d_c, d_gate = zp(2), zp(3), zp(4), zp(5)
        qv = q_ref[...].astype(F32)
        silu_d, dsilu_d = _silu_grad(d_gate)
        dzb5 = store(5, dd * d_b * qv * dsilu_d)
        dzb3 = store(3, dd * qv * silu_d)
        dq_s[0:tm, :] = dd * d_b * silu_d
        hc = d_c * d_h
        dhc = jnp.zeros((tm, w), F32)
        for k in range(CONV_D_WIDTH):
            d = dq_s[pl.ds((CONV_D_WIDTH - 1) - k, tm), :]
            dhc = dhc + cw_ref[k:k + 1, :] * d
            vec_ref[1 + k:2 + k, :] += _sum0(hc * d)
        dq_s[tm:tm + HALO_D, :] = dq_s[0:HALO_D, :]
        dzb2 = store(2, dhc * d_c)
        dzb4 = store(4, dhc * d_h)

        dx = DEEPNORM_ALPHA * dr_ref[...].astype(F32)
        for k, dzb in enumerate((dzb0, dzb1, dzb2, dzb3, dzb4, dzb5)):
            dx = dx + _dot_nt(dzb, w_ref[:, k * w:(k + 1) * w])
        dx_ref[...] = dx

    rev = lambda cols: pl.BlockSpec((tm, cols), lambda i: (nt - 1 - i, 0))
    return _call(
        body, name="odd_bwd", grid=(nt,), rider=rider,
        out_shape=(jax.ShapeDtypeStruct((t_len, 6 * w), BF16), jax.ShapeDtypeStruct((t_len, d_model), F32),
                   jax.ShapeDtypeStruct((n_groups, LANE, LANE), F32), jax.ShapeDtypeStruct((SUBLANE, w), F32),
                   jax.ShapeDtypeStruct((SUBLANE, 6 * w), F32)),
        in_specs=[rev(d_model), rev(6 * w), rev(w), rev(w), rev(d_model), _const(w_in.shape), _const(w_pool.shape),
                  _const(pool_scale.shape), _const(conv_w.shape)],
        out_specs=(rev(6 * w), rev(d_model), _const((n_groups, LANE, LANE)), _const((SUBLANE, w)), _const((SUBLANE, 6 * w))),
        scratch=[pltpu.VMEM((tm + HALO_C + SUBLANE, w), F32), pltpu.VMEM((tm + HALO_D, w), F32),
                 pltpu.VMEM((2, tm + HALO_C + SUBLANE, LANE), F32)],
        args=[dy, z, pooled, q, dr, w_in, w_pool, pool_scale, conv_w])


def _weight_grad(name, a, b, a_layer=None, bn=None, a_cols=None, rider=None):
    if a_layer is None:
        t_len, m = a.shape
        col = 0
        if a_cols is not None:
            col, m = a_cols
        a_spec = lambda tk: pl.BlockSpec((tk, m), lambda n, k: (k, col))
    else:
        _, t_len, m = a.shape
        a_spec = lambda tk: pl.BlockSpec((None, tk, m), lambda n, k: (a_layer, k, 0))
    n_cols = b.shape[1]
    bn = n_cols if bn is None else bn
    tk = min(t_len, 1024)
    n_k = t_len // tk

    def body(a_ref, b_ref, o_ref, acc):
        k = pl.program_id(1)

        @pl.when(k == 0)
        def _():
            acc[...] = jnp.zeros_like(acc)
        acc[...] += _dot_tn(a_ref[...].astype(BF16), b_ref[...].astype(BF16))

        @pl.when(k == n_k - 1)
        def _():
            o_ref[...] = acc[...].astype(BF16)

    outs = _call(
        body, name=name, grid=(n_cols // bn, n_k), out_shape=[jax.ShapeDtypeStruct((m, n_cols), BF16)],
        in_specs=[a_spec(tk), pl.BlockSpec((tk, bn), lambda n, k: (k, n))],
        out_specs=[pl.BlockSpec((m, bn), lambda n, k: (0, n))], scratch=[pltpu.VMEM((m, bn), F32)],
        args=[a, b], rider=rider)
    return outs[0] if rider is None else outs


def _weight_grads_post(name, y, dr, rh, ln_g, ln_b, dgl, p_all, layer, de):
    t_len, d_model = y.shape
    d_ple = p_all.shape[-1]
    tk = min(t_len, 1024)
    n_k = t_len // tk

    def body(y_ref, dr_ref, rh_ref, g_ref, b_ref, dgl_ref, p_ref, de_ref, o_out, o_gate, o_ple, acc_out, acc_gate, acc_ple):
        k = pl.program_id(0)

        @pl.when(k == 0)
        def _():
            acc_out[...] = jnp.zeros_like(acc_out)
            acc_gate[...] = jnp.zeros_like(acc_gate)
            acc_ple[...] = jnp.zeros_like(acc_ple)
        acc_out[...] += _dot_tn(y_ref[...], dr_ref[...])
        h = (rh_ref[...].astype(F32) * g_ref[...] + b_ref[...]).astype(BF16)
        acc_gate[...] += _dot_tn(h, dgl_ref[...])
        acc_ple[...] += _dot_tn(p_ref[...].astype(BF16), de_ref[...])

        @pl.when(k == n_k - 1)
        def _():
            o_out[...] = acc_out[...].astype(BF16)
            o_gate[...] = acc_gate[...].astype(BF16)
            o_ple[...] = acc_ple[...].astype(BF16)

    row = lambda cols: pl.BlockSpec((tk, cols), lambda k: (k, 0))
    return pl.pallas_call(
        body, name=name, grid=(n_k,),
        out_shape=(jax.ShapeDtypeStruct((d_model, d_model), BF16), jax.ShapeDtypeStruct((d_model, d_model), BF16),
                   jax.ShapeDtypeStruct((d_ple, d_model), BF16)),
        in_specs=[row(d_model), row(d_model), row(d_model), _const(ln_g.shape), _const(ln_b.shape), row(d_model),
                  pl.BlockSpec((None, tk, d_ple), lambda k: (layer, k, 0)), row(d_model)],
        out_specs=(_const((d_model, d_model)), _const((d_model, d_model)), _const((d_ple, d_model))),
        scratch_shapes=[pltpu.VMEM((d_model, d_model), F32), pltpu.VMEM((d_model, d_model), F32),
                        pltpu.VMEM((d_ple, d_model), F32)],
        compiler_params=_params(1),
    )(y, dr, rh, ln_g, ln_b, dgl, p_all, de)


def _adamw_reduce(name, parts, w, m, v, rows_per_block):
    n_rows, n_cols = w.shape
    br = rows_per_block
    n_parts = parts.shape[0]

    def body(p_ref, w_ref, m_ref, v_ref, g_ref, d_ref, nm_ref, nv_ref):
        g = p_ref[0].astype(F32)
        for k in range(1, n_parts):
            g = g + p_ref[k].astype(F32)
        nm = ADAM_B1 * m_ref[...] + (1.0 - ADAM_B1) * g
        nv = ADAM_B2 * v_ref[...] + (1.0 - ADAM_B2) * (g * g)
        m_hat = nm / (1.0 - ADAM_B1 ** ADAM_STEP)
        v_hat = nv / (1.0 - ADAM_B2 ** ADAM_STEP)
        g_ref[...] = g
        d_ref[...] = -ADAM_LR * (m_hat / (jnp.sqrt(v_hat) + ADAM_EPS) + ADAM_WD * w_ref[...])
        nm_ref[...] = nm
        nv_ref[...] = nv

    blk = pl.BlockSpec((br, n_cols), lambda i: (i, 0))
    shp = jax.ShapeDtypeStruct((n_rows, n_cols), F32)
    return pl.pallas_call(
        body, name=name, grid=(n_rows // br,), out_shape=(shp, shp, shp, shp),
        in_specs=[pl.BlockSpec((n_parts, br, n_cols), lambda i: (0, i, 0)), blk, blk, blk], out_specs=(blk, blk, blk, blk),
        compiler_params=_params(1),
    )(parts, w, m, v)


def _pack_rows(flat_parts, pad_to=None, width=LANE):
    flat = jnp.concatenate([a.reshape(-1) for a in flat_parts])
    if pad_to is not None and pad_to > flat.shape[0]:
        flat = jnp.concatenate([flat, jnp.zeros((pad_to - flat.shape[0],), flat.dtype)])
    return flat.reshape(-1, width)


def _unpack_rows(packed, shapes):
    flat = packed.reshape(-1)
    out, off = [], 0
    for s in shapes:
        n = math.prod(s)
        out.append(flat[off:off + n].reshape(s))
        off += n
    return out


def _to_dest_major(full):
    lead, last = full.shape[:-1], full.shape[-1]
    t = full.reshape(lead + (N_DEV, last // N_DEV))
    return jnp.moveaxis(t, -2, 0).reshape(N_DEV, -1)


def _from_source_major(blocks, shard_shape):
    t = blocks.reshape((N_DEV,) + tuple(shard_shape))
    t = jnp.moveaxis(t, 0, -2)
    return t.reshape(tuple(shard_shape[:-1]) + (N_DEV * shard_shape[-1],))


def _block_rows(n_rows, n_cols, target_elems=96 * 1024):
    best = None
    for br in range(SUBLANE, n_rows + 1, SUBLANE):
        if n_rows % br == 0 and br * n_cols <= target_elems:
            best = br
    return n_rows if best is None else best


def kernel(x, p, w_in_e, b_in_e, conv_a_w, conv_a_b, ln_a_g, ln_a_b, ln_v_g, ln_v_b, w_s, b_s, w_out_e, b_out_e, w_in_o, b_in_o, w_pool, pool_scale, conv_d_w, w_out_o, b_out_o, ln_g, ln_b, w_ple, w_ple_gate, b_ple_gate, loss_target, m_w_in_e, m_b_in_e, m_conv_a_w, m_conv_a_b, m_ln_a_g, m_ln_a_b, m_ln_v_g, m_ln_v_b, m_w_s, m_b_s, m_w_out_e, m_b_out_e, m_w_in_o, m_b_in_o, m_w_pool, m_pool_scale, m_conv_d_w, m_w_out_o, m_b_out_o, m_ln_g, m_ln_b, m_w_ple, m_w_ple_gate, m_b_ple_gate, v_w_in_e, v_b_in_e, v_conv_a_w, v_conv_a_b, v_ln_a_g, v_ln_a_b, v_ln_v_g, v_ln_v_b, v_w_s, v_b_s, v_w_out_e, v_b_out_e, v_w_in_o, v_b_in_o, v_w_pool, v_pool_scale, v_conv_d_w, v_w_out_o, v_b_out_o, v_ln_g, v_ln_b, v_w_ple, v_w_ple_gate, v_b_ple_gate):
    weights = dict(w_in_e=w_in_e, b_in_e=b_in_e, conv_a_w=conv_a_w, conv_a_b=conv_a_b, ln_a_g=ln_a_g, ln_a_b=ln_a_b,
                   ln_v_g=ln_v_g, ln_v_b=ln_v_b, w_s=w_s, b_s=b_s, w_out_e=w_out_e, b_out_e=b_out_e, w_in_o=w_in_o,
                   b_in_o=b_in_o, w_pool=w_pool, pool_scale=pool_scale, conv_d_w=conv_d_w, w_out_o=w_out_o,
                   b_out_o=b_out_o, ln_g=ln_g, ln_b=ln_b, w_ple=w_ple, w_ple_gate=w_ple_gate, b_ple_gate=b_ple_gate)
    mom_m = dict(w_in_e=m_w_in_e, b_in_e=m_b_in_e, conv_a_w=m_conv_a_w, conv_a_b=m_conv_a_b, ln_a_g=m_ln_a_g,
                 ln_a_b=m_ln_a_b, ln_v_g=m_ln_v_g, ln_v_b=m_ln_v_b, w_s=m_w_s, b_s=m_b_s, w_out_e=m_w_out_e,
                 b_out_e=m_b_out_e, w_in_o=m_w_in_o, b_in_o=m_b_in_o, w_pool=m_w_pool, pool_scale=m_pool_scale,
                 conv_d_w=m_conv_d_w, w_out_o=m_w_out_o, b_out_o=m_b_out_o, ln_g=m_ln_g, ln_b=m_ln_b, w_ple=m_w_ple,
                 w_ple_gate=m_w_ple_gate, b_ple_gate=m_b_ple_gate)
    mom_v = dict(w_in_e=v_w_in_e, b_in_e=v_b_in_e, conv_a_w=v_conv_a_w, conv_a_b=v_conv_a_b, ln_a_g=v_ln_a_g,
                 ln_a_b=v_ln_a_b, ln_v_g=v_ln_v_g, ln_v_b=v_ln_v_b, w_s=v_w_s, b_s=v_b_s, w_out_e=v_w_out_e,
                 b_out_e=v_b_out_e, w_in_o=v_w_in_o, b_in_o=v_b_in_o, w_pool=v_w_pool, pool_scale=v_pool_scale,
                 conv_d_w=v_conv_d_w, w_out_o=v_w_out_o, b_out_o=v_b_out_o, ln_g=v_ln_g, ln_b=v_ln_b, w_ple=v_w_ple,
                 w_ple_gate=v_w_ple_gate, b_ple_gate=v_b_ple_gate)
    names = tuple(weights)

    batch, seq, d_model = x.shape
    t_len = batch * seq
    w = d_model // 2
    n_even = w_in_e.shape[0]
    n_odd = w_in_o.shape[0]
    depth = ln_g.shape[0]
    d_ple = p.shape[-1]
    n_heads = w_s.shape[1]
    tm = 512 if seq % 512 == 0 and seq >= 1024 else seq // 2
    in_cols = w_in_e.shape[-1]
    out_rows = w_out_e.shape[1]
    ple_cols = w_ple.shape[-1]
    gate_rows = w_ple_gate.shape[1]

    sh_shapes = [weights[n].shape for n in SH_NAMES]
    sh_len = sum(math.prod(s) for s in sh_shapes)
    sh_pad = -(-sh_len // (SUBLANE * LANE)) * (SUBLANE * LANE)
    sh_rows = sh_pad // LANE
    sds = jax.ShapeDtypeStruct
    w_in16 = (w_in_e.astype(BF16), w_in_o.astype(BF16))
    w_out16 = (w_out_e.astype(BF16), w_out_o.astype(BF16))
    w_ple16, w_gate16 = w_ple.astype(BF16), w_ple_gate.astype(BF16)

    kinds = ("in", "out", "ple", "gate")

    def weight_entries(i, which=kinds):
        j, par = i // 2, i % 2
        all_four = {"in": (w_in16[par][j], _whole_view, sds((d_model, N_DEV * in_cols), BF16), _axis_view(1, in_cols)),
                    "out": (w_out16[par][j], _whole_view, sds((N_DEV * out_rows, d_model), BF16), _axis_view(0, out_rows)),
                    "ple": (w_ple16[i], _whole_view, sds((d_ple, N_DEV * ple_cols), BF16), _axis_view(1, ple_cols)),
                    "gate": (w_gate16[i], _whole_view, sds((N_DEV * gate_rows, d_model), BF16), _axis_view(0, gate_rows))}
        return [((i, k), all_four[k]) for k in which]

    fwd_riders = {("mixer", 0): weight_entries(0, kinds[1:]) + weight_entries(1, kinds[:1]),
                  ("post", 0): weight_entries(1, kinds[1:])}
    for i in range(1, depth - 1):
        if i % 2:
            fwd_riders[("mixer", i)] = weight_entries(i + 1, kinds[:1])
            fwd_riders[("post", i)] = weight_entries(i + 1, kinds[1:])
        else:
            fwd_riders[("mixer", i)] = weight_entries(i + 1)
    layer_w = {}

    def carried(where):
        tagged = fwd_riders.get(where)
        if tagged is None:
            return None, lambda landed: None
        return _Rider([e for _, e in tagged]), lambda landed: layer_w.update(zip([t for t, _ in tagged], landed))

    w_in_first = weight_entries(0, kinds[:1])[0][1]
    first = _gather_two_level("gather_first", [w_in_first[0], _pack_rows([weights[n] for n in SH_NAMES], sh_pad)],
                              [w_in_first[2], sds((N_DEV, sh_rows, LANE), F32)], [w_in_first[3], _slot_view()])
    layer_w[(0, "in")] = first[0]
    sh_flat = first[1].reshape(N_DEV, sh_pad)
    full_small, off = {}, 0
    for n, s in zip(SH_NAMES, sh_shapes):
        size = math.prod(s)
        full_small[n] = _from_source_major(sh_flat[:, off:off + size], s)
        off += size

    tril = jnp.tril(jnp.ones((LANE, LANE), dtype=bool))
    ws_m = jnp.where(tril[None, None], w_s, 0.0)
    pair = lambda t: jnp.concatenate([t[:, 0::2], t[:, 1::2]], axis=-1).astype(BF16)
    wcat = pair(ws_m)
    wcat_t = pair(jnp.swapaxes(ws_m, -1, -2))
    bs_full = jnp.repeat(jnp.swapaxes(b_s, -1, -2), w // n_heads, axis=-1)
    row2 = lambda a, j: a[j][None, :]

    x2 = x.reshape(t_len, d_model)
    p3 = p.reshape(depth, t_len, d_ple)
    lt2 = loss_target.reshape(t_len, d_model)

    xs, saved = [x2], []
    dxn = sse = None
    for i in range(depth):
        j = i // 2
        last = i == depth - 1
        rider, file_weights = carried(("mixer", i))
        if i % 2 == 0:
            outs = _even_fwd(xs[i], layer_w[(i, "in")], row2(b_in_e, j), full_small["conv_a_w"][j], row2(conv_a_b, j),
                             row2(ln_a_g, j), row2(ln_a_b, j), row2(ln_v_g, j), row2(ln_v_b, j), wcat[j], bs_full[j],
                             seq=seq, tm=tm, rider=rider)
            b_out = row2(b_out_e, j)
        else:
            outs = _odd_fwd(xs[i], layer_w[(i, "in")], row2(full_small["b_in_o"], j), w_pool[j].astype(BF16),
                            row2(full_small["pool_scale"], j), full_small["conv_d_w"][j], seq=seq, tm=tm, rider=rider)
            b_out = row2(full_small["b_out_o"], j)
        z, y, s1, s2 = outs[:4]
        file_weights(outs[4:])
        rider, file_weights = carried(("post", i))
        outs = _post_fwd(y, xs[i], p3, i, layer_w[(i, "out")], b_out, row2(ln_g, i), row2(ln_b, i), layer_w[(i, "gate")],
                         row2(b_ple_gate, i), layer_w[(i, "ple")], lt2 if last else None, tm=tm, rider=rider)
        rh, rstd, gate, e = outs[:4]
        if last:
            dxn, sse = outs[4:6]
        else:
            xs.append(outs[4])
        file_weights(outs[6 if last else 5:])
        saved.append(dict(z=z, y=y, s1=s1, s2=s2, rh=rh, rstd=rstd, gate=gate, e=e))


    recv = {"w_in_e": None, "w_in_o": None, "w_out_e": None, "w_out_o": None, "w_ple": None, "w_ple_gate": None}

    def grad_entry(i, kind, g, half=None):
        j, par = i // 2, i % 2
        sfx = "_o" if par else "_e"
        name, src_view, slot = {"in": ("w_in" + sfx, _axis_view(1, in_cols), j), "out": ("w_out" + sfx, _axis_view(0, out_rows), j),
                                "ple": ("w_ple", _axis_view(1, ple_cols), i), "gate": ("w_ple_gate", _axis_view(0, gate_rows), i)}[kind]
        dst = recv[name] if recv[name] is not None else sds((N_DEV,) + weights[name].shape, BF16)
        if half is None:
            dst_view = _slot_view(slot)
        else:
            rows = pl.ds(half[0] * half[1], half[1])
            dst_view = lambda ref, d: ref.at[d, slot, rows, :]
            if g.shape[0] != half[1]:
                src_view = lambda ref, d: ref.at[rows, pl.ds(pl.multiple_of(d * in_cols, in_cols), in_cols)]
        return name, (g, src_view, dst, dst_view)

    def ride(tagged):
        if not tagged:
            return None, lambda landed: None
        return _Rider([e for _, e in tagged]), lambda landed: recv.update(zip([n for n, _ in tagged], landed))

    small = {n: [None] * weights[n].shape[0] for n in REP_NAMES + SH_NAMES}
    grads = {}
    half_rows = d_model // 2
    for i in reversed(range(depth)):
        j, par = i // 2, i % 2
        sv = saved[i]
        w_in = layer_w[(i, "in")]
        split = (i + 1, "in") if par == 1 and (i + 1, "in") in grads else None
        rider, file_landed = ride([grad_entry(*split, grads[split], half=(0, half_rows))] if split else [])
        outs = _post_bwd(dxn, sv["gate"], sv["e"], sv["rh"], sv["rstd"], row2(ln_g, i), layer_w[(i, "gate")],
                         layer_w[(i, "out")], tm=tm, rider=rider)
        de, dgl, dy, dr, acc = outs[:5]
        file_landed(outs[5:])
        small["b_ple_gate"][i], small["ln_g"][i], small["ln_b"][i] = acc[0], acc[1], acc[2]
        small["b_out_o" if par else "b_out_e"][j] = acc[3]
        grads[(i, "out")], grads[(i, "gate")], grads[(i, "ple")] = _weight_grads_post(
            f"dw_post_l{i}", sv["y"], dr, sv["rh"], row2(ln_g, i), row2(ln_b, i), dgl, p3, i, de)
        rider, file_landed = ride([grad_entry(l, k, grads.pop((l, k)), half=(1, half_rows) if (l, k) == split else None)
                                   for l, k in list(grads)])
        if par == 0:
            outs = _even_bwd(dy, sv["z"], sv["s1"], sv["s2"], dr, w_in, full_small["conv_a_w"][j],
                             row2(ln_a_g, j), row2(ln_a_b, j), row2(ln_v_g, j), row2(ln_v_b, j), wcat_t[j], seq=seq, tm=tm,
                             rider=rider)
            dz, dx, dcw, vec, dbin, dws, dbs = outs[:7]
            file_landed(outs[7:])
            small["conv_a_w"][j], small["conv_a_b"][j] = dcw, vec[0]
            small["ln_a_g"][j], small["ln_a_b"][j], small["ln_v_g"][j], small["ln_v_b"][j] = vec[1], vec[2], vec[3], vec[4]
            small["b_in_e"][j], small["w_s"][j] = dbin[0], dws
            small["b_s"][j] = _head_sums(dbs, n_heads)
        else:
            outs = _odd_bwd(dy, sv["z"], sv["s1"], sv["s2"], dr, w_in, w_pool[j].astype(BF16),
                            row2(full_small["pool_scale"], j), full_small["conv_d_w"][j], seq=seq, tm=tm, rider=rider)
            dz, dx, dwp, vec, dbin = outs[:5]
            file_landed(outs[5:])
            small["w_pool"][j], small["pool_scale"][j], small["conv_d_w"][j] = dwp, vec[0], vec[1:1 + CONV_D_WIDTH]
            small["b_in_o"][j] = dbin[0]
        if i > 0:
            grads[(i, "in")] = _weight_grad(f"dw_in_l{i}", xs[i], dz, bn=in_cols * N_DEV // 2)
        dxn = dx
    grad_x = dxn.reshape(batch, seq, d_model)

    top = _weight_grad("dw_in_l0_top", xs[0], dz, a_cols=(0, half_rows))
    rider, file_landed = ride([grad_entry(0, "in", top, half=(0, half_rows))])
    outs = _weight_grad("dw_in_l0_bottom", xs[0], dz, a_cols=(1, half_rows), rider=rider)
    file_landed(outs[1:])

    small_full = {n: jnp.stack(small[n]) for n in small}
    sh_part = jnp.concatenate([_to_dest_major(small_full[n]) for n in SH_NAMES], axis=1)
    sh_part = jnp.concatenate([sh_part, jnp.zeros((N_DEV, sh_pad - sh_len), F32)], axis=1).reshape(N_DEV, sh_rows, LANE)
    name, entry = grad_entry(0, "in", outs[0], half=(1, half_rows))
    rep_width = 4 * LANE
    rep_len = sum(math.prod(weights[n].shape) for n in REP_NAMES)
    rep_block = N_DEV * SUBLANE
    rep_pad = -(-(rep_len + 1) // (rep_block * rep_width)) * (rep_block * rep_width)
    rep_sum, recv[name], sh_landed = _allreduce_rows(
        "exchange_last", _pack_rows([small_full[n] for n in REP_NAMES] + [sse[0:1, 0]], rep_pad, rep_width),
        _Rider([entry, (sh_part, _slot_view(), sds((N_DEV, sh_rows, LANE), F32), _slot_view())]))
    loss = (0.5 / d_model) * rep_sum.reshape(-1)[rep_len]

    results = {}
    for n, parts in recv.items():
        shp = weights[n].shape
        rows, cols = math.prod(shp[:-1]), shp[-1]
        two = lambda a: a.reshape(rows, cols)
        outs = _adamw_reduce("adamw_" + n, parts.reshape(N_DEV, rows, cols), two(weights[n]), two(mom_m[n]), two(mom_v[n]),
                             _block_rows(rows, cols))
        results[n] = [o.reshape(shp) for o in outs]
    pack_sh = lambda d: _pack_rows([d[n] for n in SH_NAMES], sh_pad)
    outs = _adamw_reduce("adamw_small_sharded", sh_landed, pack_sh(weights), pack_sh(mom_m), pack_sh(mom_v), sh_rows)
    for n, *vals in zip(SH_NAMES, *[_unpack_rows(o, sh_shapes) for o in outs]):
        results[n] = vals
    pack_rep = lambda d: _pack_rows([d[n] for n in REP_NAMES], rep_pad, rep_width)
    rep_shapes = [weights[n].shape for n in REP_NAMES]
    outs = _adamw_reduce("adamw_replicated", rep_sum[None], pack_rep(weights), pack_rep(mom_m), pack_rep(mom_v), rep_block)
    for n, *vals in zip(REP_NAMES, *[_unpack_rows(o, rep_shapes) for o in outs]):
        results[n] = vals

    return (loss, grad_x, *[results[n][0] for n in names], *[results[n][1] for n in names],
            *[results[n][2] for n in names], *[results[n][3] for n in names])


def _head_sums(dbs, n_heads):
    t, width = dbs.shape
    return jnp.sum(dbs.reshape(t, n_heads, width // n_heads), axis=-1).T
```

```python
import functools
import math

import jax
import jax.numpy as jnp
from jax import lax
from jax.experimental import pallas as pl
from jax.experimental.pallas import tpu as pltpu

F32 = jnp.float32
BF16 = jnp.bfloat16

N_DEV = 8
DEPTH = 4
LN_EPS = 1e-5
DEEPNORM_ALPHA = (2.0 * DEPTH) ** 0.25
POOL_WINDOWS = (2, 4, 8, 16)
CONV_A_WIDTH = 31
CONV_D_WIDTH = 3
GELU_C = math.sqrt(2.0 / math.pi)
GELU_K = 0.044715

ADAM_LR = 0.001
ADAM_B1 = 0.9
ADAM_B2 = 0.999
ADAM_EPS = 1e-08
ADAM_WD = 0.01
ADAM_STEP = 10

LANE = 128
SUBLANE = 8
HALO_A = 32
HALO_C = 16
HALO_D = 8
ROW_CHUNK = 32
VMEM_LIMIT = 56 * 2**20

ANY = pl.BlockSpec(memory_space=pl.ANY)
MESH = pl.DeviceIdType.MESH

REP_NAMES = ("b_in_e", "conv_a_b", "ln_a_g", "ln_a_b", "ln_v_g", "ln_v_b", "w_s", "b_s", "b_out_e", "w_pool", "ln_g",
             "ln_b", "b_ple_gate")
SH_NAMES = ("conv_a_w", "conv_d_w", "pool_scale", "b_in_o", "b_out_o")


def _params(n_grid_axes):
    return pltpu.CompilerParams(dimension_semantics=("arbitrary",) * n_grid_axes, vmem_limit_bytes=VMEM_LIMIT)


def _const(shape):
    nd = len(shape)
    return pl.BlockSpec(shape, lambda *_: (0,) * nd)


def _dot(a, b):
    return jnp.dot(a, b, preferred_element_type=F32)


def _dot_nt(a, b):
    return lax.dot_general(a, b, (((1,), (1,)), ((), ())), preferred_element_type=F32)


def _dot_tn(a, b):
    return lax.dot_general(a, b, (((0,), (0,)), ((), ())), preferred_element_type=F32)


def _sigmoid(x):
    return jax.nn.sigmoid(x)


def _silu(x):
    return x * _sigmoid(x)


def _silu_grad(x):
    s = _sigmoid(x)
    return x * s, s * (1.0 + x * (1.0 - s))


def _gelu(x):
    return 0.5 * x * (1.0 + jnp.tanh(GELU_C * (x + GELU_K * x * x * x)))


def _gelu_grad(x):
    x2 = x * x
    t = jnp.tanh(GELU_C * x * (1.0 + GELU_K * x2))
    return 0.5 * x * (1.0 + t), 0.5 * (1.0 + t) + 0.5 * x * (1.0 - t * t) * GELU_C * (1.0 + 3.0 * GELU_K * x2)


def _norm(v):
    mu = jnp.mean(v, axis=-1, keepdims=True)
    d = v - mu
    var = jnp.mean(d * d, axis=-1, keepdims=True)
    rstd = lax.rsqrt(var + LN_EPS)
    return d * rstd, rstd


def _norm_bwd(dxh, xh, rstd):
    return rstd * (dxh - jnp.mean(dxh, axis=-1, keepdims=True) - xh * jnp.mean(dxh * xh, axis=-1, keepdims=True))


def _sum0(v):
    return jnp.sum(v, axis=0, keepdims=True)


def _head_low_mask(n_cols):
    lane = lax.broadcasted_iota(jnp.int32, (LANE, n_cols), 1)
    return (lane & (LANE - 1)) < (LANE // 2)


def _blocks_to_lanes(v, j, nb):
    return jnp.concatenate([v[n * LANE:(n + 1) * LANE, j * LANE:(j + 1) * LANE] for n in range(nb)], axis=1)


def _axis_view(axis, size):
    def view(ref, d):
        idx = [slice(None)] * len(ref.shape)
        idx[axis] = pl.ds(pl.multiple_of(d * size, size), size)
        return ref.at[tuple(idx)]
    return view


def _slot_view(*slot):
    return lambda ref, d: ref.at[(d,) + slot]


def _whole_view(ref, d):
    return ref


PEER_BITS = (1, 2, 4, 6, 3, 5, 7)


class _Rider:
    def __init__(self, entries):
        self.n = len(entries)
        self.srcs = [e[0] for e in entries]
        self.src_views = [e[1] for e in entries]
        self.dsts = [e[2] for e in entries]
        self.dst_views = [e[3] for e in entries]
        self.passed = [a for a, d in enumerate(self.dsts) if not isinstance(d, jax.ShapeDtypeStruct)]

    def operands(self):
        return self.srcs + [self.dsts[a] for a in self.passed]

    def out_shape(self):
        return [jax.ShapeDtypeStruct(d.shape, d.dtype) for d in self.dsts]

    def scratch(self):
        return [pltpu.SemaphoreType.DMA((7 * self.n,)), pltpu.SemaphoreType.DMA((7 * self.n,)), pltpu.SemaphoreType.DMA((self.n,))]

    def aliases(self, n_in_before, n_out_before):
        return {n_in_before + self.n + q: n_out_before + a for q, a in enumerate(self.passed)}

    def _copies(self, src, dst, sems):
        send_sems, recv_sems, local_sems = sems
        x, y, c = lax.axis_index("x"), lax.axis_index("y"), lax.axis_index("c")
        me = 4 * x + 2 * y + c
        local = [pltpu.make_async_copy(self.src_views[a](src[a], me), self.dst_views[a](dst[a], me), local_sems.at[a])
                 for a in range(self.n)]
        sends, recvs = [], []
        for ki, k in enumerate(PEER_BITS):
            px, py, pc = x ^ (k >> 2), y ^ ((k >> 1) & 1), c ^ (k & 1)
            peer = 4 * px + 2 * py + pc
            for a in range(self.n):
                s = a * 7 + ki
                mk = lambda landing: pltpu.make_async_remote_copy(
                    src_ref=self.src_views[a](src[a], peer), dst_ref=self.dst_views[a](dst[a], landing),
                    send_sem=send_sems.at[s], recv_sem=recv_sems.at[s], device_id=(px, py, pc), device_id_type=MESH)
                sends.append(mk(me))
                recvs.append(mk(peer))
        return local, sends, recvs

    def start(self, src, dst, sems):
        local, sends, _ = self._copies(src, dst, sems)
        for cp in local + sends:
            cp.start()

    def wait(self, src, dst, sems):
        local, sends, recvs = self._copies(src, dst, sems)
        for cp in recvs:
            cp.wait_recv()
        for cp in sends:
            cp.wait_send()
        for cp in local:
            cp.wait()


def _call(body, *, name, grid, in_specs, args, out_shape, out_specs, scratch=(), rider=None, aliases=None):
    n_in, n_out, n_scr = len(args), len(out_shape), len(scratch)
    aliases = dict(aliases or {})
    if rider is None:
        return pl.pallas_call(body, name=name, grid=grid, out_shape=tuple(out_shape), in_specs=list(in_specs),
                              out_specs=tuple(out_specs), scratch_shapes=list(scratch), input_output_aliases=aliases,
                              compiler_params=_params(len(grid)))(*args)
    r_ops = rider.operands()

    def full_body(*refs):
        ins, refs = refs[:n_in], refs[n_in:]
        r_src, refs = refs[:rider.n], refs[len(r_ops):]
        outs, refs = refs[:n_out], refs[n_out:]
        r_dst, refs = refs[:rider.n], refs[rider.n:]
        scr, sems = refs[:n_scr], refs[n_scr:]
        if grid:
            first = last = None
            for axis, size in enumerate(grid):
                at_start, at_end = pl.program_id(axis) == 0, pl.program_id(axis) == size - 1
                first = at_start if first is None else jnp.logical_and(first, at_start)
                last = at_end if last is None else jnp.logical_and(last, at_end)

            @pl.when(first)
            def _():
                rider.start(r_src, r_dst, sems)
            body(*ins, *outs, *scr)

            @pl.when(last)
            def _():
                rider.wait(r_src, r_dst, sems)
        else:
            rider.start(r_src, r_dst, sems)
            if body is not None:
                body(*ins, *outs, *scr)
            rider.wait(r_src, r_dst, sems)

    aliases.update(rider.aliases(n_in, n_out))
    kw = dict(compiler_params=_params(len(grid))) if grid else {}
    if grid:
        kw["grid"] = grid
    return pl.pallas_call(
        full_body, name=name, out_shape=tuple(out_shape) + tuple(rider.out_shape()),
        in_specs=list(in_specs) + [ANY] * len(r_ops), out_specs=tuple(out_specs) + tuple([ANY] * rider.n),
        scratch_shapes=list(scratch) + rider.scratch(), input_output_aliases=aliases, **kw)(*args, *r_ops)


def _gather_two_level(name, shards, dst_shapes, dst_views):
    n = len(shards)

    def body(*refs):
        src, dst = refs[:n], refs[n:2 * n]
        send_sems, recv_sems, local_sems = refs[2 * n:]
        x, y, c = lax.axis_index("x"), lax.axis_index("y"), lax.axis_index("c")
        flat = lambda dev: 4 * dev[0] + 2 * dev[1] + dev[2]
        me, sibling = (x, y, c), (x, y, 1 - c)
        chips = [(1 - x, y), (x, 1 - y), (1 - x, 1 - y)]

        def copy(a, k, block, to, from_shard=False):
            place = dst_views[a](dst[a], flat(block))
            return pltpu.make_async_remote_copy(src_ref=src[a] if from_shard else place, dst_ref=place,
                                                send_sem=send_sems.at[7 * a + k], recv_sem=recv_sems.at[7 * a + k],
                                                device_id=to, device_id_type=MESH)

        mine = [pltpu.make_async_copy(src[a], dst_views[a](dst[a], flat(me)), local_sems.at[a]) for a in range(n)]
        first = [copy(a, 0, me, sibling, True) for a in range(n)]
        first += [copy(a, 1 + j, me, (*chip, c), True) for j, chip in enumerate(chips) for a in range(n)]
        for cp in mine + first:
            cp.start()
        passed = []
        for j, chip in enumerate(chips):
            for a in range(n):
                copy(a, 1 + j, (*chip, c), me).wait_recv()
                passed.append(copy(a, 4 + j, (*chip, c), sibling))
                passed[-1].start()
        for a in range(n):
            copy(a, 0, sibling, me).wait_recv()
            for j, chip in enumerate(chips):
                copy(a, 4 + j, (*chip, 1 - c), me).wait_recv()
        for cp in first + passed:
            cp.wait_send()
        for cp in mine:
            cp.wait()

    return pl.pallas_call(
        body, name=name, out_shape=tuple(dst_shapes), in_specs=[ANY] * n, out_specs=tuple([ANY] * n),
        scratch_shapes=[pltpu.SemaphoreType.DMA((7 * n,)), pltpu.SemaphoreType.DMA((7 * n,)), pltpu.SemaphoreType.DMA((n,))],
    )(*shards)


def _allreduce_rows(name, part, rider):
    n_rows, n_cols = part.shape
    sl = n_rows // N_DEV

    def body(p_ref, o_ref, recv_v, sum_v, send1, recv1, send2, recv2, local_sems):
        x, y, c = lax.axis_index("x"), lax.axis_index("y"), lax.axis_index("c")
        me = 4 * x + 2 * y + c
        rows_of = lambda d: pl.ds(pl.multiple_of(d * sl, SUBLANE), sl)
        peers = []
        for ki, k in enumerate(PEER_BITS):
            px, py, pc = x ^ (k >> 2), y ^ ((k >> 1) & 1), c ^ (k & 1)
            peers.append((ki, (px, py, pc), 4 * px + 2 * py + pc))

        def scatter(ki, dev, peer, landing):
            return pltpu.make_async_remote_copy(src_ref=p_ref.at[rows_of(peer)], dst_ref=recv_v.at[landing],
                                                send_sem=send1.at[ki], recv_sem=recv1.at[ki], device_id=dev, device_id_type=MESH)

        def gather(ki, dev, landing):
            return pltpu.make_async_remote_copy(src_ref=sum_v, dst_ref=o_ref.at[rows_of(landing)],
                                                send_sem=send2.at[ki], recv_sem=recv2.at[ki], device_id=dev, device_id_type=MESH)

        own = pltpu.make_async_copy(p_ref.at[rows_of(me)], recv_v.at[me], local_sems.at[0])
        own.start()
        for ki, dev, peer in peers:
            scatter(ki, dev, peer, me).start()
        for ki, dev, peer in peers:
            scatter(ki, dev, peer, peer).wait_recv()
        own.wait()
        total = recv_v[0]
        for d in range(1, N_DEV):
            total = total + recv_v[d]
        sum_v[...] = total
        own = pltpu.make_async_copy(sum_v, o_ref.at[rows_of(me)], local_sems.at[1])
        own.start()
        for ki, dev, peer in peers:
            gather(ki, dev, me).start()
        for ki, dev, peer in peers:
            gather(ki, dev, peer).wait_recv()
        for ki, dev, peer in peers:
            scatter(ki, dev, peer, me).wait_send()
            gather(ki, dev, me).wait_send()
        own.wait()

    return _call(
        body, name=name, grid=(), out_shape=[jax.ShapeDtypeStruct((n_rows, n_cols), F32)], in_specs=[ANY], out_specs=[ANY],
        scratch=[pltpu.VMEM((N_DEV, sl, n_cols), F32), pltpu.VMEM((sl, n_cols), F32),
                 pltpu.SemaphoreType.DMA((7,)), pltpu.SemaphoreType.DMA((7,)), pltpu.SemaphoreType.DMA((7,)),
                 pltpu.SemaphoreType.DMA((7,)), pltpu.SemaphoreType.DMA((2,))],
        args=[part], rider=rider)


def _shifted_windows(buf, shift_s, n_rows):
    for o in range(1, SUBLANE):
        shift_s[o - 1, 0:n_rows - SUBLANE, :] = buf[pl.ds(o, n_rows - SUBLANE), :]

    def window(s):
        q, o = divmod(s, SUBLANE)
        src = buf if o == 0 else shift_s.at[o - 1]
        return src[pl.ds(q * SUBLANE, ROW_CHUNK), :]
    return window


def _z_parts(x_ref, w_ref, b_ref, z_ref, width):
    xb = x_ref[...].astype(BF16)

    def part(k, keep=True, half=None):
        cols = slice(k * width, (k + 1) * width)
        if half is not None:
            cols = slice(k * width + half * (width // 2), k * width + (half + 1) * (width // 2))
        zk = (_dot(xb, w_ref[:, cols]) + b_ref[:, cols]).astype(BF16)
        z_ref[:, cols] = zk
        return zk.astype(F32) if keep else None
    return part


def _even_fwd(x, w_in, b_in, conv_w, conv_b, ln_a_g, ln_a_b, ln_v_g, ln_v_b, wcat, bs_full, *, seq, tm, rider=None):
    t_len, d_model = x.shape
    w = d_model // 2
    nt, tps, nb = t_len // tm, seq // tm, tm // LANE

    def body(x_ref, w_ref, b_ref, cw_ref, cb_ref, lag_ref, lab_ref, lvg_ref, lvb_ref, wcat_ref, bs_ref,
             z_ref, y_ref, a1_ref, sg_ref, a0_s, shift_s):
        i = pl.program_id(0)

        @pl.when(i % tps == 0)
        def _():
            a0_s[0:HALO_A, :] = jnp.zeros((HALO_A, w), F32)

        part = _z_parts(x_ref, w_ref, b_ref, z_ref, w)
        a0_s[HALO_A:HALO_A + tm, :] = part(0) * _sigmoid(part(1))
        window = _shifted_windows(a0_s, shift_s, tm + HALO_A)
        n_chunks = tm // ROW_CHUNK
        pieces = [(k, h) for k in range(2, 6) for h in range(2)]
        later = dict(zip(range(n_chunks - 1, -1, -max(1, n_chunks // len(pieces))), reversed(pieces)))
        for c, r in enumerate(range(0, tm, ROW_CHUNK)):
            acc = jnp.zeros((ROW_CHUNK, w), F32) + cb_ref[...]
            for k in range(CONV_A_WIDTH):
                acc = acc + cw_ref[k:k + 1, :] * window(HALO_A - (CONV_A_WIDTH - 1) + k + r)
            a1_ref[r:r + ROW_CHUNK, :] = acc
            if c in later:
                part(later[c][0], keep=False, half=later[c][1])
        for piece in pieces:
            if piece not in later.values():
                part(piece[0], keep=False, half=piece[1])
        zp = lambda k: z_ref[:, k * w:(k + 1) * w].astype(F32)
        a0_s[0:HALO_A, :] = a0_s[tm:tm + HALO_A, :]
        ah, _ = _norm(a1_ref[...])
        a = _silu(ah * lag_ref[...] + lab_ref[...]) * _silu(zp(2))
        y_ref[:, 0:w] = a.astype(BF16)

        u = zp(3)
        vh, _ = _norm(_gelu(zp(4)))
        v2 = vh * lvg_ref[...] + lvb_ref[...]
        low = _head_low_mask(nb * LANE)
        for j in range(w // LANE):
            vt = _blocks_to_lanes(v2, j, nb)
            rhs = jnp.concatenate([jnp.where(low, vt, 0.0), jnp.where(low, 0.0, vt)], axis=0).astype(BF16)
            out = _dot(wcat_ref[j], rhs)
            for n in range(nb):
                sg_ref[n * LANE:(n + 1) * LANE, j * LANE:(j + 1) * LANE] = (
                    out[:, n * LANE:(n + 1) * LANE] + bs_ref[:, j * LANE:(j + 1) * LANE]).astype(BF16)
        g = _gelu(u) * sg_ref[...].astype(F32) * _silu(zp(5))
        y_ref[:, w:2 * w] = g.astype(BF16)

    row = lambda cols: pl.BlockSpec((tm, cols), lambda i: (i, 0))
    return _call(
        body, name="even_fwd", grid=(nt,), rider=rider,
        out_shape=(jax.ShapeDtypeStruct((t_len, 6 * w), BF16), jax.ShapeDtypeStruct((t_len, d_model), BF16),
                   jax.ShapeDtypeStruct((t_len, w), F32), jax.ShapeDtypeStruct((t_len, w), BF16)),
        in_specs=[row(d_model), _const(w_in.shape), _const(b_in.shape), _const(conv_w.shape), _const(conv_b.shape),
                  _const(ln_a_g.shape), _const(ln_a_b.shape), _const(ln_v_g.shape), _const(ln_v_b.shape),
                  _const(wcat.shape), _const(bs_full.shape)],
        out_specs=(row(6 * w), row(d_model), row(w), row(w)),
        scratch=[pltpu.VMEM((tm + HALO_A, w), F32), pltpu.VMEM((SUBLANE - 1, tm + HALO_A, w), F32)],
        args=[x, w_in, b_in, conv_w, conv_b, ln_a_g, ln_a_b, ln_v_g, ln_v_b, wcat, bs_full])


def _doubling_sums(src, cols, lv_s, win, lo, n, step):
    read = lambda off: src[pl.ds(lo + off, n), cols]
    shift, level = 1, 0
    while shift < win:
        dst = lv_s.at[level % 2]
        dst[pl.ds(lo, n), :] = read(0) + read(step * shift)
        read = lambda off, d=dst: d[pl.ds(lo + off, n), :]
        shift, level = 2 * shift, level + 1
    final = lv_s.at[(level - 1) % 2]
    return lambda start, rows: final[pl.ds(start, rows), :]


def _pool_inverse(tile_index, tm, seq, window):
    row = tile_index * tm + lax.broadcasted_iota(jnp.int32, (tm, 1), 0)
    pos = (row % seq + 1).astype(F32)
    return 1.0 / jnp.minimum(pos, float(window))


def _odd_fwd(x, w_in, b_in, w_pool, pool_scale, conv_w, *, seq, tm, rider=None):
    t_len, d_model = x.shape
    w = d_model // 2
    nt, tps = t_len // tm, seq // tm

    def body(x_ref, w_ref, b_ref, wp_ref, ps_ref, cw_ref, z_ref, y_ref, pooled_ref, q_ref, cv_s, hc_s, lv_s):
        i = pl.program_id(0)
        first = SUBLANE + HALO_C

        @pl.when(i == 0)
        def _():
            cv_s[0:SUBLANE, :] = jnp.zeros((SUBLANE, w), F32)
            lv_s[:, 0:SUBLANE, :] = jnp.zeros((2, SUBLANE, LANE), F32)

        @pl.when(i % tps == 0)
        def _():
            cv_s[SUBLANE:first, :] = jnp.zeros((HALO_C, w), F32)
            hc_s[0:HALO_D, :] = jnp.zeros((HALO_D, w), F32)

        part = _z_parts(x_ref, w_ref, b_ref, z_ref, w)
        c_val = part(0)
        c_gate = part(1)
        cv_s[first:first + tm, :] = c_val
        for gi, win in enumerate(POOL_WINDOWS):
            cols = slice(gi * LANE, (gi + 1) * LANE)
            s = _doubling_sums(cv_s, cols, lv_s, win, SUBLANE, HALO_C + tm, -1)(first, tm)
            pooled = (s * _pool_inverse(i, tm, seq, win) - c_val[:, cols]).astype(BF16)
            pooled_ref[:, cols] = pooled
            c = _dot(pooled, wp_ref[gi]) * ps_ref[:, cols] * _silu(c_gate[:, cols])
            y_ref[:, cols] = c.astype(BF16)
        cv_s[SUBLANE:first, :] = cv_s[tm + SUBLANE:tm + first, :]

        d_h = part(2)
        d_b = part(3)
        hc_s[HALO_D:HALO_D + tm, :] = part(4) * d_h
        q = jnp.zeros((tm, w), F32)
        for k in range(CONV_D_WIDTH):
            q = q + cw_ref[k:k + 1, :] * hc_s[pl.ds(HALO_D - (CONV_D_WIDTH - 1) + k, tm), :]
        hc_s[0:HALO_D, :] = hc_s[tm:tm + HALO_D, :]
        qb = q.astype(BF16)
        q_ref[...] = qb
        y_ref[:, w:2 * w] = (d_b * qb.astype(F32) * _silu(part(5))).astype(BF16)

    row = lambda cols: pl.BlockSpec((tm, cols), lambda i: (i, 0))
    return _call(
        body, name="odd_fwd", grid=(nt,), rider=rider,
        out_shape=(jax.ShapeDtypeStruct((t_len, 6 * w), BF16), jax.ShapeDtypeStruct((t_len, d_model), BF16),
                   jax.ShapeDtypeStruct((t_len, w), BF16), jax.ShapeDtypeStruct((t_len, w), BF16)),
        in_specs=[row(d_model), _const(w_in.shape), _const(b_in.shape), _const(w_pool.shape), _const(pool_scale.shape),
                  _const(conv_w.shape)],
        out_specs=(row(6 * w), row(d_model), row(w), row(w)),
        scratch=[pltpu.VMEM((tm + HALO_C + SUBLANE, w), F32), pltpu.VMEM((tm + HALO_D, w), F32),
                 pltpu.VMEM((2, tm + HALO_C + SUBLANE, LANE), F32)],
        args=[x, w_in, b_in, w_pool, pool_scale, conv_w])


def _post_fwd(y, x, p_all, layer, w_out, b_out, ln_g, ln_b, wg, bg, wp, loss_target, *, tm, rider=None):
    t_len, d_model = x.shape
    d_ple = p_all.shape[-1]
    nt = t_len // tm
    last = loss_target is not None

    def body(*refs):
        y_ref, x_ref, p_ref, wo_ref, bo_ref, g_ref, b_ref, wg_ref, bg_ref, wp_ref = refs[:10]
        rest = refs[10:]
        if last:
            lt_ref, rest = rest[0], rest[1:]
        rh_ref, rstd_ref, gate_ref = rest[:3]
        r = DEEPNORM_ALPHA * x_ref[...] + _dot(y_ref[...], wo_ref[...]) + bo_ref[...]
        rh, rstd = _norm(r)
        h = rh * g_ref[...] + b_ref[...]
        gate = _sigmoid(_dot(h.astype(BF16), wg_ref[...]) + bg_ref[...])
        e = _dot(p_ref[...].astype(BF16), wp_ref[...])
        xn = h + gate * e
        rh_ref[...] = rh.astype(BF16)
        rstd_ref[...] = jnp.broadcast_to(rstd, (tm, LANE))
        gate_ref[...] = gate.astype(BF16)
        if not last:
            rest[3][...] = xn
        else:
            dxn_ref, sse_ref = rest[3:]
            diff = xn - lt_ref[...]
            dxn_ref[...] = (diff * (1.0 / d_model)).astype(BF16)

            @pl.when(pl.program_id(0) == 0)
            def _():
                sse_ref[...] = jnp.zeros_like(sse_ref)
            sse_ref[...] += jnp.sum(_sum0(diff * diff), axis=1, keepdims=True)

    row = lambda cols: pl.BlockSpec((tm, cols), lambda i: (i, 0))
    in_specs = [row(d_model), row(d_model), pl.BlockSpec((None, tm, d_ple), lambda i: (layer, i, 0)),
                _const(w_out.shape), _const(b_out.shape), _const(ln_g.shape), _const(ln_b.shape), _const(wg.shape),
                _const(bg.shape), _const(wp.shape)]
    args = [y, x, p_all, w_out, b_out, ln_g, ln_b, wg, bg, wp]
    out_shape = [jax.ShapeDtypeStruct((t_len, d_model), BF16), jax.ShapeDtypeStruct((t_len, LANE), F32),
                 jax.ShapeDtypeStruct((t_len, d_model), BF16), jax.ShapeDtypeStruct((t_len, d_model), BF16 if last else F32)]
    out_specs = [row(d_model), row(LANE), row(d_model), row(d_model)]
    if last:
        in_specs.append(row(d_model))
        args.append(loss_target)
        out_shape.append(jax.ShapeDtypeStruct((SUBLANE, LANE), F32))
        out_specs.append(_const((SUBLANE, LANE)))
    return _call(body, name="post_fwd_last" if last else "post_fwd", grid=(nt,), out_shape=out_shape, in_specs=in_specs,
                 out_specs=out_specs, args=args, rider=rider)


def _post_bwd(dxn, gate, p_all, layer, wp, rh, rstd, ln_g, wg, w_out, *, tm, rider=None):
    t_len, d_model = dxn.shape
    d_ple = p_all.shape[-1]
    nt = t_len // tm

    def body(dxn_ref, gate_ref, p_ref, wp_ref, rh_ref, rstd_ref, g_ref, wg_ref, wo_ref, de_ref, dgl_ref, dy_ref, dr_ref,
             acc_ref):
        @pl.when(pl.program_id(0) == 0)
        def _():
            acc_ref[...] = jnp.zeros_like(acc_ref)

        d = dxn_ref[...].astype(F32)
        gt = gate_ref[...].astype(F32)
        rhat = rh_ref[...].astype(F32)
        de_ref[...] = (d * gt).astype(BF16)
        dgl = d * _dot(p_ref[...].astype(BF16), wp_ref[...]) * gt * (1.0 - gt)
        dglb = dgl.astype(BF16)
        dgl_ref[...] = dglb
        dh = d + _dot_nt(dglb, wg_ref[...])
        dr = _norm_bwd(dh * g_ref[...], rhat, rstd_ref[:, 0:1])
        drb = dr.astype(BF16)
        dr_ref[...] = drb
        dy_ref[...] = _dot_nt(drb, wo_ref[...]).astype(BF16)
        acc_ref[0:1, :] += _sum0(dgl)
        acc_ref[1:2, :] += _sum0(dh * rhat)
        acc_ref[2:3, :] += _sum0(dh)
        acc_ref[3:4, :] += _sum0(dr)

    row = lambda cols: pl.BlockSpec((tm, cols), lambda i: (i, 0))
    return _call(
        body, name="post_bwd", grid=(nt,), rider=rider,
        out_shape=(jax.ShapeDtypeStruct((t_len, d_model), BF16), jax.ShapeDtypeStruct((t_len, d_model), BF16),
                   jax.ShapeDtypeStruct((t_len, d_model), BF16), jax.ShapeDtypeStruct((t_len, d_model), BF16),
                   jax.ShapeDtypeStruct((SUBLANE, d_model), F32)),
        in_specs=[row(d_model), row(d_model), pl.BlockSpec((None, tm, d_ple), lambda i: (layer, i, 0)), _const(wp.shape),
                  row(d_model), row(LANE), _const(ln_g.shape), _const(wg.shape), _const(w_out.shape)],
        out_specs=(row(d_model), row(d_model), row(d_model), row(d_model), _const((SUBLANE, d_model))),
        args=[dxn, gate, p_all, wp, rh, rstd, ln_g, wg, w_out])


def _dz_store(dz_ref, dbin_ref, width):
    def store(k, v):
        cols = slice(k * width, (k + 1) * width)
        vb = v.astype(BF16)
        dz_ref[:, cols] = vb
        dbin_ref[0:1, cols] += _sum0(v)
        return vb
    return store


def _even_bwd(dy, z, a1, sg, dr, w_in, conv_w, ln_a_g, ln_a_b, ln_v_g, ln_v_b, wcat_t, *, seq, tm, dx_dtype, rider=None):
    t_len, d_model = dr.shape
    w = d_model // 2
    nt, tps, nb = t_len // tm, seq // tm, tm // LANE
    n_heads = 2 * (w // LANE)

    def body(dy_ref, z_ref, a1_ref, sg_ref, dr_ref, w_ref, cw_ref, lag_ref, lab_ref, lvg_ref, lvb_ref, wct_ref,
             dz_ref, dx_ref, dcw_ref, vec_ref, dbin_ref, dws_ref, dbs_ref, da1_s, a0_s, da0_s, cw_acc, shift_s, dx_s):
        i = pl.program_id(0)
        tile = nt - 1 - i

        @pl.when(i == 0)
        def _():
            vec_ref[...] = jnp.zeros_like(vec_ref)
            dbin_ref[...] = jnp.zeros_like(dbin_ref)
            dws_ref[...] = jnp.zeros_like(dws_ref)
            dbs_ref[...] = jnp.zeros_like(dbs_ref)
            cw_acc[...] = jnp.zeros_like(cw_acc)

        @pl.when((tile + 1) % tps == 0)
        def _():
            da1_s[tm:tm + HALO_A, :] = jnp.zeros((HALO_A, w), F32)

        zp = lambda k: z_ref[:, k * w:(k + 1) * w].astype(F32)
        store = _dz_store(dz_ref, dbin_ref, w)

        dg = dy_ref[:, w:2 * w].astype(F32)
        u, v, gg = zp(3), zp(4), zp(5)
        sgv = sg_ref[...].astype(F32)
        gelu_u, dgelu_u = _gelu_grad(u)
        silu_gg, dsilu_gg = _silu_grad(gg)
        dzb5 = store(5, dg * gelu_u * sgv * dsilu_gg)
        t1 = dg * silu_gg
        dzb3 = store(3, t1 * sgv * dgelu_u)
        dsg = t1 * gelu_u
        gelu_v, dgelu_v = _gelu_grad(v)
        vh, rstd_v = _norm(gelu_v)
        v2 = vh * lvg_ref[...] + lvb_ref[...]
        low = _head_low_mask(nb * LANE)
        for j in range(w // LANE):
            dt = _blocks_to_lanes(dsg, j, nb)
            d_lo = jnp.where(low, dt, 0.0).astype(BF16)
            d_hi = jnp.where(low, 0.0, dt).astype(BF16)
            v2t = _blocks_to_lanes(v2, j, nb).astype(BF16)
            dv2t = _dot(wct_ref[j], jnp.concatenate([d_lo, d_hi], axis=0))
            for n in range(nb):
                da0_s[n * LANE:(n + 1) * LANE, j * LANE:(j + 1) * LANE] = dv2t[:, n * LANE:(n + 1) * LANE]
            dws_ref[2 * j] += _dot_nt(d_lo, v2t)
            dws_ref[2 * j + 1] += _dot_nt(d_hi, v2t)
            bsum = dt[:, 0:LANE]
            for n in range(1, nb):
                bsum = bsum + dt[:, n * LANE:(n + 1) * LANE]
            dbs_ref[:, j * LANE:(j + 1) * LANE] += bsum
        dv2 = da0_s[...]
        vec_ref[3:4, :] += _sum0(dv2 * vh)
        vec_ref[4:5, :] += _sum0(dv2)
        dzb4 = store(4, _norm_bwd(dv2 * lvg_ref[...], vh, rstd_v) * dgelu_v)
        dx_b = DEEPNORM_ALPHA * dr_ref[...].astype(F32)
        for k, dzb in ((3, dzb3), (4, dzb4), (5, dzb5)):
            dx_b = dx_b + _dot_nt(dzb, w_ref[:, k * w:(k + 1) * w])
        dx_s[...] = dx_b

        da = dy_ref[:, 0:w].astype(F32)
        a_val, a_glu, a_gate = zp(0), zp(1), zp(2)
        s_glu = _sigmoid(a_glu)
        a0_s[...] = a_val * s_glu
        ah, rstd_a = _norm(a1_ref[...])
        silu_a2, dsilu_a2 = _silu_grad(ah * lag_ref[...] + lab_ref[...])
        silu_ag, dsilu_ag = _silu_grad(a_gate)
        dzb2 = store(2, da * silu_a2 * dsilu_ag)
        da2 = da * silu_ag * dsilu_a2
        vec_ref[1:2, :] += _sum0(da2 * ah)
        vec_ref[2:3, :] += _sum0(da2)
        da1 = _norm_bwd(da2 * lag_ref[...], ah, rstd_a)
        vec_ref[0:1, :] += _sum0(da1)
        da1_s[0:tm, :] = da1
        window = _shifted_windows(da1_s, shift_s, tm + HALO_A)
        for r in range(0, tm, ROW_CHUNK):
            a0c = a0_s[r:r + ROW_CHUNK, :]
            acc = jnp.zeros((ROW_CHUNK, w), F32)
            for k in range(CONV_A_WIDTH):
                d = window(r + (CONV_A_WIDTH - 1) - k)
                acc = acc + cw_ref[k:k + 1, :] * d
                pw = a0c * d
                p8 = pw[0:SUBLANE]
                for q in range(1, ROW_CHUNK // SUBLANE):
                    p8 = p8 + pw[q * SUBLANE:(q + 1) * SUBLANE]
                cw_acc[k * SUBLANE:(k + 1) * SUBLANE, :] += p8
            da0_s[r:r + ROW_CHUNK, :] = acc
        da1_s[tm:tm + HALO_A, :] = da1_s[0:HALO_A, :]
        da0 = da0_s[...]
        dzb0 = store(0, da0 * s_glu)
        dzb1 = store(1, da0 * a_val * s_glu * (1.0 - s_glu))

        dx = dx_s[...]
        for k, dzb in ((0, dzb0), (1, dzb1), (2, dzb2)):
            dx = dx + _dot_nt(dzb, w_ref[:, k * w:(k + 1) * w])
        dx_ref[...] = dx.astype(dx_ref.dtype)

        @pl.when(i == nt - 1)
        def _():
            for k in range(CONV_A_WIDTH):
                dcw_ref[k:k + 1, :] = _sum0(cw_acc[k * SUBLANE:(k + 1) * SUBLANE, :])
            keep = (lax.broadcasted_iota(jnp.int32, (LANE, LANE), 0) >= lax.broadcasted_iota(jnp.int32, (LANE, LANE), 1))
            for hd in range(n_heads):
                dws_ref[hd] = jnp.where(keep, dws_ref[hd], 0.0)

    rev = lambda cols: pl.BlockSpec((tm, cols), lambda i: (nt - 1 - i, 0))
    return _call(
        body, name="even_bwd", grid=(nt,), rider=rider,
        out_shape=(jax.ShapeDtypeStruct((t_len, 6 * w), BF16), jax.ShapeDtypeStruct((t_len, d_model), dx_dtype),
                   jax.ShapeDtypeStruct((CONV_A_WIDTH, w), F32), jax.ShapeDtypeStruct((SUBLANE, w), F32),
                   jax.ShapeDtypeStruct((SUBLANE, 6 * w), F32), jax.ShapeDtypeStruct((n_heads, LANE, LANE), F32),
                   jax.ShapeDtypeStruct((LANE, w), F32)),
        in_specs=[rev(d_model), rev(6 * w), rev(w), rev(w), rev(d_model), _const(w_in.shape), _const(conv_w.shape),
                  _const(ln_a_g.shape), _const(ln_a_b.shape), _const(ln_v_g.shape), _const(ln_v_b.shape), _const(wcat_t.shape)],
        out_specs=(rev(6 * w), rev(d_model), _const((CONV_A_WIDTH, w)), _const((SUBLANE, w)), _const((SUBLANE, 6 * w)),
                   _const((n_heads, LANE, LANE)), _const((LANE, w))),
        scratch=[pltpu.VMEM((tm + HALO_A, w), F32), pltpu.VMEM((tm, w), F32), pltpu.VMEM((tm, w), F32),
                 pltpu.VMEM((CONV_A_WIDTH * SUBLANE, w), F32), pltpu.VMEM((SUBLANE - 1, tm + HALO_A, w), F32),
                 pltpu.VMEM((tm, d_model), F32)],
        args=[dy, z, a1, sg, dr, w_in, conv_w, ln_a_g, ln_a_b, ln_v_g, ln_v_b, wcat_t])


def _odd_bwd(dy, z, pooled, q, dr, w_in, w_pool, pool_scale, conv_w, *, seq, tm, dx_dtype, rider=None):
    t_len, d_model = dr.shape
    w = d_model // 2
    nt, tps = t_len // tm, seq // tm
    n_groups = len(POOL_WINDOWS)

    def body(dy_ref, z_ref, pooled_ref, q_ref, dr_ref, w_ref, wp_ref, ps_ref, cw_ref,
             dz_ref, dx_ref, dwp_ref, vec_ref, dbin_ref, dm_s, dq_s, lv_s):
        i = pl.program_id(0)
        tile = nt - 1 - i

        @pl.when(i == 0)
        def _():
            dwp_ref[...] = jnp.zeros_like(dwp_ref)
            vec_ref[...] = jnp.zeros_like(vec_ref)
            dbin_ref[...] = jnp.zeros_like(dbin_ref)
            dm_s[tm + HALO_C:tm + HALO_C + SUBLANE, :] = jnp.zeros((SUBLANE, w), F32)
            lv_s[:, tm + HALO_C:tm + HALO_C + SUBLANE, :] = jnp.zeros((2, SUBLANE, LANE), F32)

        @pl.when((tile + 1) % tps == 0)
        def _():
            dm_s[tm:tm + HALO_C, :] = jnp.zeros((HALO_C, w), F32)
            dq_s[tm:tm + HALO_D, :] = jnp.zeros((HALO_D, w), F32)

        zp = lambda k: z_ref[:, k * w:(k + 1) * w].astype(F32)
        store = _dz_store(dz_ref, dbin_ref, w)

        dc = dy_ref[:, 0:w].astype(F32)
        c_gate = zp(1)
        silu_c, dsilu_c = _silu_grad(c_gate)
        dcs = dc * silu_c
        dvg_parts, dcg_parts = [], []
        for gi, win in enumerate(POOL_WINDOWS):
            cols = slice(gi * LANE, (gi + 1) * LANE)
            pooled_g = pooled_ref[:, cols]
            wp = wp_ref[gi]
            cpre = _dot(pooled_g, wp)
            scale = ps_ref[:, cols]
            vec_ref[0:1, cols] += _sum0(dcs[:, cols] * cpre)
            dcg_parts.append(dc[:, cols] * cpre * scale * dsilu_c[:, cols])
            dcp = (dcs[:, cols] * scale).astype(BF16)
            dwp_ref[gi] += _dot_tn(pooled_g, dcp)
            dpooled = _dot_nt(dcp, wp)
            dm_s[0:tm, cols] = dpooled * _pool_inverse(tile, tm, seq, win)
            s = _doubling_sums(dm_s, cols, lv_s, win, 0, tm + HALO_C, 1)(0, tm)
            dvg_parts.append(s - dpooled)
        dm_s[tm:tm + HALO_C, :] = dm_s[0:HALO_C, :]
        dzb0 = store(0, jnp.concatenate(dvg_parts, axis=1))
        dzb1 = store(1, jnp.concatenate(dcg_parts, axis=1))

        dd = dy_ref[:, w:2 * w].astype(F32)
        d_h, d_b, d_c, d_gate = zp(2), zp(3), zp(4), zp(5)
        qv = q_ref[...].astype(F32)
        silu_d, dsilu_d = _silu_grad(d_gate)
        dzb5 = store(5, dd * d_b * qv * dsilu_d)
        dzb3 = store(3, dd * qv * silu_d)
        dq_s[0:tm, :] = dd * d_b * silu_d
        hc = d_c * d_h
        dhc = jnp.zeros((tm, w), F32)
        for k in range(CONV_D_WIDTH):
            d = dq_s[pl.ds((CONV_D_WIDTH - 1) - k, tm), :]
            dhc = dhc + cw_ref[k:k + 1, :] * d
            vec_ref[1 + k:2 + k, :] += _sum0(hc * d)
        dq_s[tm:tm + HALO_D, :] = dq_s[0:HALO_D, :]
        dzb2 = store(2, dhc * d_c)
        dzb4 = store(4, dhc * d_h)

        dx = DEEPNORM_ALPHA * dr_ref[...].astype(F32)
        for k, dzb in enumerate((dzb0, dzb1, dzb2, dzb3, dzb4, dzb5)):
            dx = dx + _dot_nt(dzb, w_ref[:, k * w:(k + 1) * w])
        dx_ref[...] = dx.astype(dx_ref.dtype)

    rev = lambda cols: pl.BlockSpec((tm, cols), lambda i: (nt - 1 - i, 0))
    return _call(
        body, name="odd_bwd", grid=(nt,), rider=rider,
        out_shape=(jax.ShapeDtypeStruct((t_len, 6 * w), BF16), jax.ShapeDtypeStruct((t_len, d_model), dx_dtype),
                   jax.ShapeDtypeStruct((n_groups, LANE, LANE), F32), jax.ShapeDtypeStruct((SUBLANE, w), F32),
                   jax.ShapeDtypeStruct((SUBLANE, 6 * w), F32)),
        in_specs=[rev(d_model), rev(6 * w), rev(w), rev(w), rev(d_model), _const(w_in.shape), _const(w_pool.shape),
                  _const(pool_scale.shape), _const(conv_w.shape)],
        out_specs=(rev(6 * w), rev(d_model), _const((n_groups, LANE, LANE)), _const((SUBLANE, w)), _const((SUBLANE, 6 * w))),
        scratch=[pltpu.VMEM((tm + HALO_C + SUBLANE, w), F32), pltpu.VMEM((tm + HALO_D, w), F32),
                 pltpu.VMEM((2, tm + HALO_C + SUBLANE, LANE), F32)],
        args=[dy, z, pooled, q, dr, w_in, w_pool, pool_scale, conv_w])


def _weight_grad(name, a, b, a_layer=None, bn=None, a_cols=None, rider=None):
    if a_layer is None:
        t_len, m = a.shape
        col = 0
        if a_cols is not None:
            col, m = a_cols
        a_spec = lambda tk: pl.BlockSpec((tk, m), lambda n, k: (k, col))
    else:
        _, t_len, m = a.shape
        a_spec = lambda tk: pl.BlockSpec((None, tk, m), lambda n, k: (a_layer, k, 0))
    n_cols = b.shape[1]
    bn = n_cols if bn is None else bn
    tk = min(t_len, 1024)
    n_k = t_len // tk

    def body(a_ref, b_ref, o_ref, acc):
        k = pl.program_id(1)

        @pl.when(k == 0)
        def _():
            acc[...] = jnp.zeros_like(acc)
        acc[...] += _dot_tn(a_ref[...].astype(BF16), b_ref[...].astype(BF16))

        @pl.when(k == n_k - 1)
        def _():
            o_ref[...] = acc[...].astype(BF16)

    outs = _call(
        body, name=name, grid=(n_cols // bn, n_k), out_shape=[jax.ShapeDtypeStruct((m, n_cols), BF16)],
        in_specs=[a_spec(tk), pl.BlockSpec((tk, bn), lambda n, k: (k, n))],
        out_specs=[pl.BlockSpec((m, bn), lambda n, k: (0, n))], scratch=[pltpu.VMEM((m, bn), F32)],
        args=[a, b], rider=rider)
    return outs[0] if rider is None else outs


def _weight_grads_post(name, y, dr, rh, ln_g, ln_b, dgl, p_all, layer, de):
    t_len, d_model = y.shape
    d_ple = p_all.shape[-1]
    tk = min(t_len, 1024)
    n_k = t_len // tk

    def body(y_ref, dr_ref, rh_ref, g_ref, b_ref, dgl_ref, p_ref, de_ref, o_out, o_gate, o_ple, acc_out, acc_gate, acc_ple):
        k = pl.program_id(0)

        @pl.when(k == 0)
        def _():
            acc_out[...] = jnp.zeros_like(acc_out)
            acc_gate[...] = jnp.zeros_like(acc_gate)
            acc_ple[...] = jnp.zeros_like(acc_ple)
        acc_out[...] += _dot_tn(y_ref[...], dr_ref[...])
        h = (rh_ref[...].astype(F32) * g_ref[...] + b_ref[...]).astype(BF16)
        acc_gate[...] += _dot_tn(h, dgl_ref[...])
        acc_ple[...] += _dot_tn(p_ref[...].astype(BF16), de_ref[...])

        @pl.when(k == n_k - 1)
        def _():
            o_out[...] = acc_out[...].astype(BF16)
            o_gate[...] = acc_gate[...].astype(BF16)
            o_ple[...] = acc_ple[...].astype(BF16)

    row = lambda cols: pl.BlockSpec((tk, cols), lambda k: (k, 0))
    return pl.pallas_call(
        body, name=name, grid=(n_k,),
        out_shape=(jax.ShapeDtypeStruct((d_model, d_model), BF16), jax.ShapeDtypeStruct((d_model, d_model), BF16),
                   jax.ShapeDtypeStruct((d_ple, d_model), BF16)),
        in_specs=[row(d_model), row(d_model), row(d_model), _const(ln_g.shape), _const(ln_b.shape), row(d_model),
                  pl.BlockSpec((None, tk, d_ple), lambda k: (layer, k, 0)), row(d_model)],
        out_specs=(_const((d_model, d_model)), _const((d_model, d_model)), _const((d_ple, d_model))),
        scratch_shapes=[pltpu.VMEM((d_model, d_model), F32), pltpu.VMEM((d_model, d_model), F32),
                        pltpu.VMEM((d_ple, d_model), F32)],
        compiler_params=_params(1),
    )(y, dr, rh, ln_g, ln_b, dgl, p_all, de)


def _adamw_reduce(name, parts, w, m, v, rows_per_block):
    n_rows, n_cols = w.shape
    br = rows_per_block
    n_parts = parts.shape[0]

    def body(p_ref, w_ref, m_ref, v_ref, g_ref, d_ref, nm_ref, nv_ref):
        g = p_ref[0].astype(F32)
        for k in range(1, n_parts):
            g = g + p_ref[k].astype(F32)
        nm = ADAM_B1 * m_ref[...] + (1.0 - ADAM_B1) * g
        nv = ADAM_B2 * v_ref[...] + (1.0 - ADAM_B2) * (g * g)
        m_hat = nm / (1.0 - ADAM_B1 ** ADAM_STEP)
        v_hat = nv / (1.0 - ADAM_B2 ** ADAM_STEP)
        g_ref[...] = g
        d_ref[...] = -ADAM_LR * (m_hat / (jnp.sqrt(v_hat) + ADAM_EPS) + ADAM_WD * w_ref[...])
        nm_ref[...] = nm
        nv_ref[...] = nv

    blk = pl.BlockSpec((br, n_cols), lambda i: (i, 0))
    shp = jax.ShapeDtypeStruct((n_rows, n_cols), F32)
    return pl.pallas_call(
        body, name=name, grid=(n_rows // br,), out_shape=(shp, shp, shp, shp),
        in_specs=[pl.BlockSpec((n_parts, br, n_cols), lambda i: (0, i, 0)), blk, blk, blk], out_specs=(blk, blk, blk, blk),
        compiler_params=_params(1),
    )(parts, w, m, v)


def _pack_rows(flat_parts, pad_to=None, width=LANE):
    flat = jnp.concatenate([a.reshape(-1) for a in flat_parts])
    if pad_to is not None and pad_to > flat.shape[0]:
        flat = jnp.concatenate([flat, jnp.zeros((pad_to - flat.shape[0],), flat.dtype)])
    return flat.reshape(-1, width)


def _unpack_rows(packed, shapes):
    flat = packed.reshape(-1)
    out, off = [], 0
    for s in shapes:
        n = math.prod(s)
        out.append(flat[off:off + n].reshape(s))
        off += n
    return out


def _to_dest_major(full):
    lead, last = full.shape[:-1], full.shape[-1]
    t = full.reshape(lead + (N_DEV, last // N_DEV))
    return jnp.moveaxis(t, -2, 0).reshape(N_DEV, -1)


def _from_source_major(blocks, shard_shape):
    t = blocks.reshape((N_DEV,) + tuple(shard_shape))
    t = jnp.moveaxis(t, 0, -2)
    return t.reshape(tuple(shard_shape[:-1]) + (N_DEV * shard_shape[-1],))


def _block_rows(n_rows, n_cols, target_elems=96 * 1024):
    best = None
    for br in range(SUBLANE, n_rows + 1, SUBLANE):
        if n_rows % br == 0 and br * n_cols <= target_elems:
            best = br
    return n_rows if best is None else best


def kernel(x, p, w_in_e, b_in_e, conv_a_w, conv_a_b, ln_a_g, ln_a_b, ln_v_g, ln_v_b, w_s, b_s, w_out_e, b_out_e, w_in_o, b_in_o, w_pool, pool_scale, conv_d_w, w_out_o, b_out_o, ln_g, ln_b, w_ple, w_ple_gate, b_ple_gate, loss_target, m_w_in_e, m_b_in_e, m_conv_a_w, m_conv_a_b, m_ln_a_g, m_ln_a_b, m_ln_v_g, m_ln_v_b, m_w_s, m_b_s, m_w_out_e, m_b_out_e, m_w_in_o, m_b_in_o, m_w_pool, m_pool_scale, m_conv_d_w, m_w_out_o, m_b_out_o, m_ln_g, m_ln_b, m_w_ple, m_w_ple_gate, m_b_ple_gate, v_w_in_e, v_b_in_e, v_conv_a_w, v_conv_a_b, v_ln_a_g, v_ln_a_b, v_ln_v_g, v_ln_v_b, v_w_s, v_b_s, v_w_out_e, v_b_out_e, v_w_in_o, v_b_in_o, v_w_pool, v_pool_scale, v_conv_d_w, v_w_out_o, v_b_out_o, v_ln_g, v_ln_b, v_w_ple, v_w_ple_gate, v_b_ple_gate):
    weights = dict(w_in_e=w_in_e, b_in_e=b_in_e, conv_a_w=conv_a_w, conv_a_b=conv_a_b, ln_a_g=ln_a_g, ln_a_b=ln_a_b,
                   ln_v_g=ln_v_g, ln_v_b=ln_v_b, w_s=w_s, b_s=b_s, w_out_e=w_out_e, b_out_e=b_out_e, w_in_o=w_in_o,
                   b_in_o=b_in_o, w_pool=w_pool, pool_scale=pool_scale, conv_d_w=conv_d_w, w_out_o=w_out_o,
                   b_out_o=b_out_o, ln_g=ln_g, ln_b=ln_b, w_ple=w_ple, w_ple_gate=w_ple_gate, b_ple_gate=b_ple_gate)
    mom_m = dict(w_in_e=m_w_in_e, b_in_e=m_b_in_e, conv_a_w=m_conv_a_w, conv_a_b=m_conv_a_b, ln_a_g=m_ln_a_g,
                 ln_a_b=m_ln_a_b, ln_v_g=m_ln_v_g, ln_v_b=m_ln_v_b, w_s=m_w_s, b_s=m_b_s, w_out_e=m_w_out_e,
                 b_out_e=m_b_out_e, w_in_o=m_w_in_o, b_in_o=m_b_in_o, w_pool=m_w_pool, pool_scale=m_pool_scale,
                 conv_d_w=m_conv_d_w, w_out_o=m_w_out_o, b_out_o=m_b_out_o, ln_g=m_ln_g, ln_b=m_ln_b, w_ple=m_w_ple,
                 w_ple_gate=m_w_ple_gate, b_ple_gate=m_b_ple_gate)
    mom_v = dict(w_in_e=v_w_in_e, b_in_e=v_b_in_e, conv_a_w=v_conv_a_w, conv_a_b=v_conv_a_b, ln_a_g=v_ln_a_g,
                 ln_a_b=v_ln_a_b, ln_v_g=v_ln_v_g, ln_v_b=v_ln_v_b, w_s=v_w_s, b_s=v_b_s, w_out_e=v_w_out_e,
                 b_out_e=v_b_out_e, w_in_o=v_w_in_o, b_in_o=v_b_in_o, w_pool=v_w_pool, pool_scale=v_pool_scale,
                 conv_d_w=v_conv_d_w, w_out_o=v_w_out_o, b_out_o=v_b_out_o, ln_g=v_ln_g, ln_b=v_ln_b, w_ple=v_w_ple,
                 w_ple_gate=v_w_ple_gate, b_ple_gate=v_b_ple_gate)
    names = tuple(weights)

    batch, seq, d_model = x.shape
    t_len = batch * seq
    w = d_model // 2
    n_even = w_in_e.shape[0]
    n_odd = w_in_o.shape[0]
    depth = ln_g.shape[0]
    d_ple = p.shape[-1]
    n_heads = w_s.shape[1]
    tm = 512 if seq % 512 == 0 and seq >= 1024 else seq // 2
    in_cols = w_in_e.shape[-1]
    out_rows = w_out_e.shape[1]
    ple_cols = w_ple.shape[-1]
    gate_rows = w_ple_gate.shape[1]

    sh_shapes = [weights[n].shape for n in SH_NAMES]
    sh_len = sum(math.prod(s) for s in sh_shapes)
    sh_pad = -(-sh_len // (SUBLANE * LANE)) * (SUBLANE * LANE)
    sh_rows = sh_pad // LANE
    sds = jax.ShapeDtypeStruct
    w_in16 = (w_in_e.astype(BF16), w_in_o.astype(BF16))
    w_out16 = (w_out_e.astype(BF16), w_out_o.astype(BF16))
    w_ple16, w_gate16 = w_ple.astype(BF16), w_ple_gate.astype(BF16)

    kinds = ("in", "out", "ple", "gate")

    def weight_entries(i, which=kinds):
        j, par = i // 2, i % 2
        all_four = {"in": (w_in16[par][j], _whole_view, sds((d_model, N_DEV * in_cols), BF16), _axis_view(1, in_cols)),
                    "out": (w_out16[par][j], _whole_view, sds((N_DEV * out_rows, d_model), BF16), _axis_view(0, out_rows)),
                    "ple": (w_ple16[i], _whole_view, sds((d_ple, N_DEV * ple_cols), BF16), _axis_view(1, ple_cols)),
                    "gate": (w_gate16[i], _whole_view, sds((N_DEV * gate_rows, d_model), BF16), _axis_view(0, gate_rows))}
        return [((i, k), all_four[k]) for k in which]

    fwd_riders = {("mixer", 0): weight_entries(0, kinds[1:]) + weight_entries(1, kinds[:1]),
                  ("post", 0): weight_entries(1, kinds[1:])}
    for i in range(1, depth - 1):
        if i % 2:
            fwd_riders[("mixer", i)] = weight_entries(i + 1, kinds[:1])
            fwd_riders[("post", i)] = weight_entries(i + 1, kinds[1:])
        else:
            fwd_riders[("mixer", i)] = weight_entries(i + 1)
    layer_w = {}

    def carried(where):
        tagged = fwd_riders.get(where)
        if tagged is None:
            return None, lambda landed: None
        return _Rider([e for _, e in tagged]), lambda landed: layer_w.update(zip([t for t, _ in tagged], landed))

    w_in_first = weight_entries(0, kinds[:1])[0][1]
    first = _gather_two_level("gather_first", [w_in_first[0], _pack_rows([weights[n] for n in SH_NAMES], sh_pad)],
                              [w_in_first[2], sds((N_DEV, sh_rows, LANE), F32)], [w_in_first[3], _slot_view()])
    layer_w[(0, "in")] = first[0]
    sh_flat = first[1].reshape(N_DEV, sh_pad)
    full_small, off = {}, 0
    for n, s in zip(SH_NAMES, sh_shapes):
        size = math.prod(s)
        full_small[n] = _from_source_major(sh_flat[:, off:off + size], s)
        off += size

    tril = jnp.tril(jnp.ones((LANE, LANE), dtype=bool))
    ws_m = jnp.where(tril[None, None], w_s, 0.0)
    pair = lambda t: jnp.concatenate([t[:, 0::2], t[:, 1::2]], axis=-1).astype(BF16)
    wcat = pair(ws_m)
    wcat_t = pair(jnp.swapaxes(ws_m, -1, -2))
    bs_full = jnp.repeat(jnp.swapaxes(b_s, -1, -2), w // n_heads, axis=-1)
    row2 = lambda a, j: a[j][None, :]

    x2 = x.reshape(t_len, d_model)
    p3 = p.reshape(depth, t_len, d_ple)
    lt2 = loss_target.reshape(t_len, d_model)

    xs, saved = [x2], []
    dxn = sse = None
    for i in range(depth):
        j = i // 2
        last = i == depth - 1
        rider, file_weights = carried(("mixer", i))
        if i % 2 == 0:
            outs = _even_fwd(xs[i], layer_w[(i, "in")], row2(b_in_e, j), full_small["conv_a_w"][j], row2(conv_a_b, j),
                             row2(ln_a_g, j), row2(ln_a_b, j), row2(ln_v_g, j), row2(ln_v_b, j), wcat[j], bs_full[j],
                             seq=seq, tm=tm, rider=rider)
            b_out = row2(b_out_e, j)
        else:
            outs = _odd_fwd(xs[i], layer_w[(i, "in")], row2(full_small["b_in_o"], j), w_pool[j].astype(BF16),
                            row2(full_small["pool_scale"], j), full_small["conv_d_w"][j], seq=seq, tm=tm, rider=rider)
            b_out = row2(full_small["b_out_o"], j)
        z, y, s1, s2 = outs[:4]
        file_weights(outs[4:])
        rider, file_weights = carried(("post", i))
        outs = _post_fwd(y, xs[i], p3, i, layer_w[(i, "out")], b_out, row2(ln_g, i), row2(ln_b, i), layer_w[(i, "gate")],
                         row2(b_ple_gate, i), layer_w[(i, "ple")], lt2 if last else None, tm=tm, rider=rider)
        rh, rstd, gate = outs[:3]
        if last:
            dxn, sse = outs[3:5]
        else:
            xs.append(outs[3])
        file_weights(outs[5 if last else 4:])
        saved.append(dict(z=z, y=y, s1=s1, s2=s2, rh=rh, rstd=rstd, gate=gate))


    recv = {"w_in_e": None, "w_in_o": None, "w_out_e": None, "w_out_o": None, "w_ple": None, "w_ple_gate": None}

    def grad_entry(i, kind, g, half=None):
        j, par = i // 2, i % 2
        sfx = "_o" if par else "_e"
        name, src_view, slot = {"in": ("w_in" + sfx, _axis_view(1, in_cols), j), "out": ("w_out" + sfx, _axis_view(0, out_rows), j),
                                "ple": ("w_ple", _axis_view(1, ple_cols), i), "gate": ("w_ple_gate", _axis_view(0, gate_rows), i)}[kind]
        dst = recv[name] if recv[name] is not None else sds((N_DEV,) + weights[name].shape, BF16)
        if half is None:
            dst_view = _slot_view(slot)
        else:
            rows = pl.ds(half[0] * half[1], half[1])
            dst_view = lambda ref, d: ref.at[d, slot, rows, :]
            if g.shape[0] != half[1]:
                src_view = lambda ref, d: ref.at[rows, pl.ds(pl.multiple_of(d * in_cols, in_cols), in_cols)]
        return name, (g, src_view, dst, dst_view)

    def ride(tagged):
        if not tagged:
            return None, lambda landed: None
        return _Rider([e for _, e in tagged]), lambda landed: recv.update(zip([n for n, _ in tagged], landed))

    small = {n: [None] * weights[n].shape[0] for n in REP_NAMES + SH_NAMES}
    grads = {}
    half_rows = d_model // 2
    for i in reversed(range(depth)):
        j, par = i // 2, i % 2
        sv = saved[i]
        w_in = layer_w[(i, "in")]
        dx_dtype = BF16 if i > 0 else F32
        split = (i + 1, "in") if par == 1 and (i + 1, "in") in grads else None
        rider, file_landed = ride([grad_entry(*split, grads[split], half=(0, half_rows))] if split else [])
        outs = _post_bwd(dxn, sv["gate"], p3, i, layer_w[(i, "ple")], sv["rh"], sv["rstd"], row2(ln_g, i),
                         layer_w[(i, "gate")], layer_w[(i, "out")], tm=tm, rider=rider)
        de, dgl, dy, dr, acc = outs[:5]
        file_landed(outs[5:])
        small["b_ple_gate"][i], small["ln_g"][i], small["ln_b"][i] = acc[0], acc[1], acc[2]
        small["b_out_o" if par else "b_out_e"][j] = acc[3]
        grads[(i, "out")], grads[(i, "gate")], grads[(i, "ple")] = _weight_grads_post(
            f"dw_post_l{i}", sv["y"], dr, sv["rh"], row2(ln_g, i), row2(ln_b, i), dgl, p3, i, de)
        rider, file_landed = ride([grad_entry(l, k, grads.pop((l, k)), half=(1, half_rows) if (l, k) == split else None)
                                   for l, k in list(grads)])
        if par == 0:
            outs = _even_bwd(dy, sv["z"], sv["s1"], sv["s2"], dr, w_in, full_small["conv_a_w"][j],
                             row2(ln_a_g, j), row2(ln_a_b, j), row2(ln_v_g, j), row2(ln_v_b, j), wcat_t[j], seq=seq, tm=tm,
                             dx_dtype=dx_dtype, rider=rider)
            dz, dx, dcw, vec, dbin, dws, dbs = outs[:7]
            file_landed(outs[7:])
            small["conv_a_w"][j], small["conv_a_b"][j] = dcw, vec[0]
            small["ln_a_g"][j], small["ln_a_b"][j], small["ln_v_g"][j], small["ln_v_b"][j] = vec[1], vec[2], vec[3], vec[4]
            small["b_in_e"][j], small["w_s"][j] = dbin[0], dws
            small["b_s"][j] = _head_sums(dbs, n_heads)
        else:
            outs = _odd_bwd(dy, sv["z"], sv["s1"], sv["s2"], dr, w_in, w_pool[j].astype(BF16),
                            row2(full_small["pool_scale"], j), full_small["conv_d_w"][j], seq=seq, tm=tm,
                            dx_dtype=dx_dtype, rider=rider)
            dz, dx, dwp, vec, dbin = outs[:5]
            file_landed(outs[5:])
            small["w_pool"][j], small["pool_scale"][j], small["conv_d_w"][j] = dwp, vec[0], vec[1:1 + CONV_D_WIDTH]
            small["b_in_o"][j] = dbin[0]
        if i > 0:
            grads[(i, "in")] = _weight_grad(f"dw_in_l{i}", xs[i], dz, bn=in_cols * N_DEV // 2)
        dxn = dx
    grad_x = dxn.reshape(batch, seq, d_model)

    top = _weight_grad("dw_in_l0_top", xs[0], dz, a_cols=(0, half_rows))
    rider, file_landed = ride([grad_entry(0, "in", top, half=(0, half_rows))])
    outs = _weight_grad("dw_in_l0_bottom", xs[0], dz, a_cols=(1, half_rows), rider=rider)
    file_landed(outs[1:])

    small_full = {n: jnp.stack(small[n]) for n in small}
    sh_part = jnp.concatenate([_to_dest_major(small_full[n]) for n in SH_NAMES], axis=1)
    sh_part = jnp.concatenate([sh_part, jnp.zeros((N_DEV, sh_pad - sh_len), F32)], axis=1).reshape(N_DEV, sh_rows, LANE)
    name, entry = grad_entry(0, "in", outs[0], half=(1, half_rows))
    rep_width = 4 * LANE
    rep_len = sum(math.prod(weights[n].shape) for n in REP_NAMES)
    rep_block = N_DEV * SUBLANE
    rep_pad = -(-(rep_len + 1) // (rep_block * rep_width)) * (rep_block * rep_width)
    rep_sum, recv[name], sh_landed = _allreduce_rows(
        "exchange_last", _pack_rows([small_full[n] for n in REP_NAMES] + [sse[0:1, 0]], rep_pad, rep_width),
        _Rider([entry, (sh_part, _slot_view(), sds((N_DEV, sh_rows, LANE), F32), _slot_view())]))
    loss = (0.5 / d_model) * rep_sum.reshape(-1)[rep_len]

    results = {}
    for n, parts in recv.items():
        shp = weights[n].shape
        rows, cols = math.prod(shp[:-1]), shp[-1]
        two = lambda a: a.reshape(rows, cols)
        outs = _adamw_reduce("adamw_" + n, parts.reshape(N_DEV, rows, cols), two(weights[n]), two(mom_m[n]), two(mom_v[n]),
                             _block_rows(rows, cols))
        results[n] = [o.reshape(shp) for o in outs]
    pack_sh = lambda d: _pack_rows([d[n] for n in SH_NAMES], sh_pad)
    outs = _adamw_reduce("adamw_small_sharded", sh_landed, pack_sh(weights), pack_sh(mom_m), pack_sh(mom_v), sh_rows)
    for n, *vals in zip(SH_NAMES, *[_unpack_rows(o, sh_shapes) for o in outs]):
        results[n] = vals
    pack_rep = lambda d: _pack_rows([d[n] for n in REP_NAMES], rep_pad, rep_width)
    rep_shapes = [weights[n].shape for n in REP_NAMES]
    outs = _adamw_reduce("adamw_replicated", rep_sum[None], pack_rep(weights), pack_rep(mom_m), pack_rep(mom_v), rep_block)
    for n, *vals in zip(REP_NAMES, *[_unpack_rows(o, rep_shapes) for o in outs]):
        results[n] = vals

    return (loss, grad_x, *[results[n][0] for n in names], *[results[n][1] for n in names],
            *[results[n][2] for n in names], *[results[n][3] for n in names])


def _head_sums(dbs, n_heads):
    t, width = dbs.shape
    return jnp.sum(dbs.reshape(t, n_heads, width // n_heads), axis=-1).T
```

```python
import math

import jax
import jax.numpy as jnp
from jax import lax
from jax.experimental import pallas as pl
from jax.experimental.pallas import tpu as pltpu

F32 = jnp.float32
BF16 = jnp.bfloat16

N_DEV = 8
DEPTH = 4
LN_EPS = 1e-5
DEEPNORM_ALPHA = (2.0 * DEPTH) ** 0.25
POOL_WINDOWS = (2, 4, 8, 16)
CONV_A_WIDTH = 31
CONV_D_WIDTH = 3
GELU_C = math.sqrt(2.0 / math.pi)
GELU_K = 0.044715

ADAM_LR = 0.001
ADAM_B1 = 0.9
ADAM_B2 = 0.999
ADAM_EPS = 1e-08
ADAM_WD = 0.01
ADAM_STEP = 10

LANE = 128
SUBLANE = 8
HALO_A = 32
HALO_C = 16
HALO_D = 8
ROW_CHUNK = 32
VMEM_LIMIT = 56 * 2**20

ANY = pl.BlockSpec(memory_space=pl.ANY)
MESH = pl.DeviceIdType.MESH

REP_NAMES = ("b_in_e", "conv_a_b", "ln_a_g", "ln_a_b", "ln_v_g", "ln_v_b", "w_s", "b_s", "b_out_e", "w_pool", "ln_g",
             "ln_b", "b_ple_gate")
SH_NAMES = ("conv_a_w", "conv_d_w", "pool_scale", "b_in_o", "b_out_o")


def _params(n_grid_axes):
    return pltpu.CompilerParams(dimension_semantics=("arbitrary",) * n_grid_axes, vmem_limit_bytes=VMEM_LIMIT)


def _const(shape):
    nd = len(shape)
    return pl.BlockSpec(shape, lambda *_: (0,) * nd)


def _dot(a, b):
    return jnp.dot(a, b, preferred_element_type=F32)


def _dot_nt(a, b):
    return lax.dot_general(a, b, (((1,), (1,)), ((), ())), preferred_element_type=F32)


def _dot_tn(a, b):
    return lax.dot_general(a, b, (((0,), (0,)), ((), ())), preferred_element_type=F32)


def _sigmoid(x):
    return jax.nn.sigmoid(x)


def _silu(x):
    return x * _sigmoid(x)


def _silu_grad(x):
    s = _sigmoid(x)
    return x * s, s * (1.0 + x * (1.0 - s))


def _gelu(x):
    return 0.5 * x * (1.0 + jnp.tanh(GELU_C * (x + GELU_K * x * x * x)))


def _gelu_grad(x):
    x2 = x * x
    t = jnp.tanh(GELU_C * x * (1.0 + GELU_K * x2))
    return 0.5 * x * (1.0 + t), 0.5 * (1.0 + t) + 0.5 * x * (1.0 - t * t) * GELU_C * (1.0 + 3.0 * GELU_K * x2)


def _norm(v):
    mu = jnp.mean(v, axis=-1, keepdims=True)
    d = v - mu
    var = jnp.mean(d * d, axis=-1, keepdims=True)
    rstd = lax.rsqrt(var + LN_EPS)
    return d * rstd, rstd


def _norm_bwd(dxh, xh, rstd):
    return rstd * (dxh - jnp.mean(dxh, axis=-1, keepdims=True) - xh * jnp.mean(dxh * xh, axis=-1, keepdims=True))


def _sum0(v):
    return jnp.sum(v, axis=0, keepdims=True)


def _head_low_mask(n_cols):
    lane = lax.broadcasted_iota(jnp.int32, (LANE, n_cols), 1)
    return (lane & (LANE - 1)) < (LANE // 2)


def _blocks_to_lanes(v, j, nb):
    return jnp.concatenate([v[n * LANE:(n + 1) * LANE, j * LANE:(j + 1) * LANE] for n in range(nb)], axis=1)


def _axis_view(axis, size):
    def view(ref, d):
        idx = [slice(None)] * len(ref.shape)
        idx[axis] = pl.ds(pl.multiple_of(d * size, size), size)
        return ref.at[tuple(idx)]
    return view


def _slot_view(*slot):
    return lambda ref, d: ref.at[(d,) + slot]


def _whole_view(ref, d):
    return ref


PEER_BITS = (1, 2, 4, 6, 3, 5, 7)


class _Rider:
    def __init__(self, entries):
        self.n = len(entries)
        self.srcs = [e[0] for e in entries]
        self.src_views = [e[1] for e in entries]
        self.dsts = [e[2] for e in entries]
        self.dst_views = [e[3] for e in entries]
        self.passed = [a for a, d in enumerate(self.dsts) if not isinstance(d, jax.ShapeDtypeStruct)]

    def operands(self):
        return self.srcs + [self.dsts[a] for a in self.passed]

    def out_shape(self):
        return [jax.ShapeDtypeStruct(d.shape, d.dtype) for d in self.dsts]

    def scratch(self):
        return [pltpu.SemaphoreType.DMA((7 * self.n,)), pltpu.SemaphoreType.DMA((7 * self.n,)), pltpu.SemaphoreType.DMA((self.n,))]

    def aliases(self, n_in_before, n_out_before):
        return {n_in_before + self.n + q: n_out_before + a for q, a in enumerate(self.passed)}

    def _copies(self, src, dst, sems):
        send_sems, recv_sems, local_sems = sems
        x, y, c = lax.axis_index("x"), lax.axis_index("y"), lax.axis_index("c")
        me = 4 * x + 2 * y + c
        local = [pltpu.make_async_copy(self.src_views[a](src[a], me), self.dst_views[a](dst[a], me), local_sems.at[a])
                 for a in range(self.n)]
        sends, recvs = [], []
        for ki, k in enumerate(PEER_BITS):
            px, py, pc = x ^ (k >> 2), y ^ ((k >> 1) & 1), c ^ (k & 1)
            peer = 4 * px + 2 * py + pc
            for a in range(self.n):
                s = a * 7 + ki
                mk = lambda landing: pltpu.make_async_remote_copy(
                    src_ref=self.src_views[a](src[a], peer), dst_ref=self.dst_views[a](dst[a], landing),
                    send_sem=send_sems.at[s], recv_sem=recv_sems.at[s], device_id=(px, py, pc), device_id_type=MESH)
                sends.append(mk(me))
                recvs.append(mk(peer))
        return local, sends, recvs

    def start(self, src, dst, sems):
        local, sends, _ = self._copies(src, dst, sems)
        for cp in local + sends:
            cp.start()

    def wait(self, src, dst, sems):
        local, sends, recvs = self._copies(src, dst, sems)
        for cp in recvs:
            cp.wait_recv()
        for cp in sends:
            cp.wait_send()
        for cp in local:
            cp.wait()


def _call(body, *, name, grid, in_specs, args, out_shape, out_specs, scratch=(), rider=None, aliases=None):
    n_in, n_out, n_scr = len(args), len(out_shape), len(scratch)
    aliases = dict(aliases or {})
    if rider is None:
        return pl.pallas_call(body, name=name, grid=grid, out_shape=tuple(out_shape), in_specs=list(in_specs),
                              out_specs=tuple(out_specs), scratch_shapes=list(scratch), input_output_aliases=aliases,
                              compiler_params=_params(len(grid)))(*args)
    r_ops = rider.operands()

    def full_body(*refs):
        ins, refs = refs[:n_in], refs[n_in:]
        r_src, refs = refs[:rider.n], refs[len(r_ops):]
        outs, refs = refs[:n_out], refs[n_out:]
        r_dst, refs = refs[:rider.n], refs[rider.n:]
        scr, sems = refs[:n_scr], refs[n_scr:]
        if grid:
            first = last = None
            for axis, size in enumerate(grid):
                at_start, at_end = pl.program_id(axis) == 0, pl.program_id(axis) == size - 1
                first = at_start if first is None else jnp.logical_and(first, at_start)
                last = at_end if last is None else jnp.logical_and(last, at_end)

            @pl.when(first)
            def _():
                rider.start(r_src, r_dst, sems)
            body(*ins, *outs, *scr)

            @pl.when(last)
            def _():
                rider.wait(r_src, r_dst, sems)
        else:
            rider.start(r_src, r_dst, sems)
            if body is not None:
                body(*ins, *outs, *scr)
            rider.wait(r_src, r_dst, sems)

    aliases.update(rider.aliases(n_in, n_out))
    kw = dict(compiler_params=_params(len(grid))) if grid else {}
    if grid:
        kw["grid"] = grid
    return pl.pallas_call(
        full_body, name=name, out_shape=tuple(out_shape) + tuple(rider.out_shape()),
        in_specs=list(in_specs) + [ANY] * len(r_ops), out_specs=tuple(out_specs) + tuple([ANY] * rider.n),
        scratch_shapes=list(scratch) + rider.scratch(), input_output_aliases=aliases, **kw)(*args, *r_ops)


def _gather_two_level(name, shards, dst_shapes, dst_views):
    n = len(shards)

    def body(*refs):
        src, dst = refs[:n], refs[n:2 * n]
        send_sems, recv_sems, local_sems = refs[2 * n:]
        x, y, c = lax.axis_index("x"), lax.axis_index("y"), lax.axis_index("c")
        flat = lambda dev: 4 * dev[0] + 2 * dev[1] + dev[2]
        me, sibling = (x, y, c), (x, y, 1 - c)
        chips = [(1 - x, y), (x, 1 - y), (1 - x, 1 - y)]

        def copy(a, k, block, to, from_shard=False):
            place = dst_views[a](dst[a], flat(block))
            return pltpu.make_async_remote_copy(src_ref=src[a] if from_shard else place, dst_ref=place,
                                                send_sem=send_sems.at[7 * a + k], recv_sem=recv_sems.at[7 * a + k],
                                                device_id=to, device_id_type=MESH)

        mine = [pltpu.make_async_copy(src[a], dst_views[a](dst[a], flat(me)), local_sems.at[a]) for a in range(n)]
        first = [copy(a, 0, me, sibling, True) for a in range(n)]
        first += [copy(a, 1 + j, me, (*chip, c), True) for j, chip in enumerate(chips) for a in range(n)]
        for cp in mine + first:
            cp.start()
        passed = []
        for j, chip in enumerate(chips):
            for a in range(n):
                copy(a, 1 + j, (*chip, c), me).wait_recv()
                passed.append(copy(a, 4 + j, (*chip, c), sibling))
                passed[-1].start()
        for a in range(n):
            copy(a, 0, sibling, me).wait_recv()
            for j, chip in enumerate(chips):
                copy(a, 4 + j, (*chip, 1 - c), me).wait_recv()
        for cp in first + passed:
            cp.wait_send()
        for cp in mine:
            cp.wait()

    return pl.pallas_call(
        body, name=name, out_shape=tuple(dst_shapes), in_specs=[ANY] * n, out_specs=tuple([ANY] * n),
        scratch_shapes=[pltpu.SemaphoreType.DMA((7 * n,)), pltpu.SemaphoreType.DMA((7 * n,)), pltpu.SemaphoreType.DMA((n,))],
    )(*shards)


def _allreduce_rows(name, part, rider):
    n_rows, n_cols = part.shape
    sl = n_rows // N_DEV

    def body(p_ref, o_ref, recv_v, sum_v, send1, recv1, send2, recv2, local_sems):
        x, y, c = lax.axis_index("x"), lax.axis_index("y"), lax.axis_index("c")
        me = 4 * x + 2 * y + c
        rows_of = lambda d: pl.ds(pl.multiple_of(d * sl, SUBLANE), sl)
        peers = []
        for ki, k in enumerate(PEER_BITS):
            px, py, pc = x ^ (k >> 2), y ^ ((k >> 1) & 1), c ^ (k & 1)
            peers.append((ki, (px, py, pc), 4 * px + 2 * py + pc))

        def scatter(ki, dev, peer, landing):
            return pltpu.make_async_remote_copy(src_ref=p_ref.at[rows_of(peer)], dst_ref=recv_v.at[landing],
                                                send_sem=send1.at[ki], recv_sem=recv1.at[ki], device_id=dev, device_id_type=MESH)

        def gather(ki, dev, landing):
            return pltpu.make_async_remote_copy(src_ref=sum_v, dst_ref=o_ref.at[rows_of(landing)],
                                                send_sem=send2.at[ki], recv_sem=recv2.at[ki], device_id=dev, device_id_type=MESH)

        own = pltpu.make_async_copy(p_ref.at[rows_of(me)], recv_v.at[me], local_sems.at[0])
        own.start()
        for ki, dev, peer in peers:
            scatter(ki, dev, peer, me).start()
        for ki, dev, peer in peers:
            scatter(ki, dev, peer, peer).wait_recv()
        own.wait()
        total = recv_v[0]
        for d in range(1, N_DEV):
            total = total + recv_v[d]
        sum_v[...] = total
        own = pltpu.make_async_copy(sum_v, o_ref.at[rows_of(me)], local_sems.at[1])
        own.start()
        for ki, dev, peer in peers:
            gather(ki, dev, me).start()
        for ki, dev, peer in peers:
            gather(ki, dev, peer).wait_recv()
        for ki, dev, peer in peers:
            scatter(ki, dev, peer, me).wait_send()
            gather(ki, dev, me).wait_send()
        own.wait()

    return _call(
        body, name=name, grid=(), out_shape=[jax.ShapeDtypeStruct((n_rows, n_cols), F32)], in_specs=[ANY], out_specs=[ANY],
        scratch=[pltpu.VMEM((N_DEV, sl, n_cols), F32), pltpu.VMEM((sl, n_cols), F32),
                 pltpu.SemaphoreType.DMA((7,)), pltpu.SemaphoreType.DMA((7,)), pltpu.SemaphoreType.DMA((7,)),
                 pltpu.SemaphoreType.DMA((7,)), pltpu.SemaphoreType.DMA((2,))],
        args=[part], rider=rider)


def _shifted_windows(buf, shift_s, n_rows):
    for o in range(1, SUBLANE):
        shift_s[o - 1, 0:n_rows - SUBLANE, :] = buf[pl.ds(o, n_rows - SUBLANE), :]

    def window(s):
        q, o = divmod(s, SUBLANE)
        src = buf if o == 0 else shift_s.at[o - 1]
        return src[pl.ds(q * SUBLANE, ROW_CHUNK), :]
    return window


def _z_parts(x_ref, w_ref, b_ref, z_ref, width):
    xb = x_ref[...].astype(BF16)

    def part(k, keep=True, half=None):
        cols = slice(k * width, (k + 1) * width)
        if half is not None:
            cols = slice(k * width + half * (width // 2), k * width + (half + 1) * (width // 2))
        zk = (_dot(xb, w_ref[:, cols]) + b_ref[:, cols]).astype(BF16)
        z_ref[:, cols] = zk
        return zk.astype(F32) if keep else None
    return part


def _even_fwd(x, w_in, b_in, conv_w, conv_b, ln_a_g, ln_a_b, ln_v_g, ln_v_b, wcat, bs_full, *, seq, tm, rider=None):
    t_len, d_model = x.shape
    w = d_model // 2
    nt, tps, nb = t_len // tm, seq // tm, tm // LANE

    def body(x_ref, w_ref, b_ref, cw_ref, cb_ref, lag_ref, lab_ref, lvg_ref, lvb_ref, wcat_ref, bs_ref,
             z_ref, y_ref, a1_ref, sg_ref, a0_s, shift_s):
        i = pl.program_id(0)

        @pl.when(i % tps == 0)
        def _():
            a0_s[0:HALO_A, :] = jnp.zeros((HALO_A, w), F32)

        part = _z_parts(x_ref, w_ref, b_ref, z_ref, w)
        a0_s[HALO_A:HALO_A + tm, :] = part(0) * _sigmoid(part(1))
        window = _shifted_windows(a0_s, shift_s, tm + HALO_A)
        n_chunks = tm // ROW_CHUNK
        pieces = [(k, h) for k in range(2, 6) for h in range(2)]
        later = dict(zip(range(n_chunks - 1, -1, -max(1, n_chunks // len(pieces))), reversed(pieces)))
        for c, r in enumerate(range(0, tm, ROW_CHUNK)):
            acc = jnp.zeros((ROW_CHUNK, w), F32) + cb_ref[...]
            for k in range(CONV_A_WIDTH):
                acc = acc + cw_ref[k:k + 1, :] * window(HALO_A - (CONV_A_WIDTH - 1) + k + r)
            a1_ref[r:r + ROW_CHUNK, :] = acc
            if c in later:
                part(later[c][0], keep=False, half=later[c][1])
        for piece in pieces:
            if piece not in later.values():
                part(piece[0], keep=False, half=piece[1])
        zp = lambda k: z_ref[:, k * w:(k + 1) * w].astype(F32)
        a0_s[0:HALO_A, :] = a0_s[tm:tm + HALO_A, :]
        ah, _ = _norm(a1_ref[...])
        a = _silu(ah * lag_ref[...] + lab_ref[...]) * _silu(zp(2))
        y_ref[:, 0:w] = a.astype(BF16)

        u = zp(3)
        vh, _ = _norm(_gelu(zp(4)))
        v2 = vh * lvg_ref[...] + lvb_ref[...]
        low = _head_low_mask(nb * LANE)
        for j in range(w // LANE):
            vt = _blocks_to_lanes(v2, j, nb)
            rhs = jnp.concatenate([jnp.where(low, vt, 0.0), jnp.where(low, 0.0, vt)], axis=0).astype(BF16)
            out = _dot(wcat_ref[j], rhs)
            for n in range(nb):
                sg_ref[n * LANE:(n + 1) * LANE, j * LANE:(j + 1) * LANE] = (
                    out[:, n * LANE:(n + 1) * LANE] + bs_ref[:, j * LANE:(j + 1) * LANE]).astype(BF16)
        g = _gelu(u) * sg_ref[...].astype(F32) * _silu(zp(5))
        y_ref[:, w:2 * w] = g.astype(BF16)

    row = lambda cols: pl.BlockSpec((tm, cols), lambda i: (i, 0))
    return _call(
        body, name="even_fwd", grid=(nt,), rider=rider,
        out_shape=(jax.ShapeDtypeStruct((t_len, 6 * w), BF16), jax.ShapeDtypeStruct((t_len, d_model), BF16),
                   jax.ShapeDtypeStruct((t_len, w), F32), jax.ShapeDtypeStruct((t_len, w), BF16)),
        in_specs=[row(d_model), _const(w_in.shape), _const(b_in.shape), _const(conv_w.shape), _const(conv_b.shape),
                  _const(ln_a_g.shape), _const(ln_a_b.shape), _const(ln_v_g.shape), _const(ln_v_b.shape),
                  _const(wcat.shape), _const(bs_full.shape)],
        out_specs=(row(6 * w), row(d_model), row(w), row(w)),
        scratch=[pltpu.VMEM((tm + HALO_A, w), F32), pltpu.VMEM((SUBLANE - 1, tm + HALO_A, w), F32)],
        args=[x, w_in, b_in, conv_w, conv_b, ln_a_g, ln_a_b, ln_v_g, ln_v_b, wcat, bs_full])


def _doubling_sums(src, cols, lv_s, win, lo, n, step):
    read = lambda off: src[pl.ds(lo + off, n), cols]
    shift, level = 1, 0
    while shift < win:
        dst = lv_s.at[level % 2]
        dst[pl.ds(lo, n), :] = read(0) + read(step * shift)
        read = lambda off, d=dst: d[pl.ds(lo + off, n), :]
        shift, level = 2 * shift, level + 1
    final = lv_s.at[(level - 1) % 2]
    return lambda start, rows: final[pl.ds(start, rows), :]


def _pool_inverse(tile_index, tm, seq, window):
    row = tile_index * tm + lax.broadcasted_iota(jnp.int32, (tm, 1), 0)
    pos = (row % seq + 1).astype(F32)
    return 1.0 / jnp.minimum(pos, float(window))


def _odd_fwd(x, w_in, b_in, w_pool, pool_scale, conv_w, *, seq, tm, rider=None):
    t_len, d_model = x.shape
    w = d_model // 2
    nt, tps = t_len // tm, seq // tm

    def body(x_ref, w_ref, b_ref, wp_ref, ps_ref, cw_ref, z_ref, y_ref, pooled_ref, q_ref, cv_s, hc_s, lv_s):
        i = pl.program_id(0)
        first = SUBLANE + HALO_C

        @pl.when(i == 0)
        def _():
            cv_s[0:SUBLANE, :] = jnp.zeros((SUBLANE, w), F32)
            lv_s[:, 0:SUBLANE, :] = jnp.zeros((2, SUBLANE, LANE), F32)

        @pl.when(i % tps == 0)
        def _():
            cv_s[SUBLANE:first, :] = jnp.zeros((HALO_C, w), F32)
            hc_s[0:HALO_D, :] = jnp.zeros((HALO_D, w), F32)

        part = _z_parts(x_ref, w_ref, b_ref, z_ref, w)
        c_val = part(0)
        c_gate = part(1)
        cv_s[first:first + tm, :] = c_val
        for gi, win in enumerate(POOL_WINDOWS):
            cols = slice(gi * LANE, (gi + 1) * LANE)
            s = _doubling_sums(cv_s, cols, lv_s, win, SUBLANE, HALO_C + tm, -1)(first, tm)
            pooled = (s * _pool_inverse(i, tm, seq, win) - c_val[:, cols]).astype(BF16)
            pooled_ref[:, cols] = pooled
            c = _dot(pooled, wp_ref[gi]) * ps_ref[:, cols] * _silu(c_gate[:, cols])
            y_ref[:, cols] = c.astype(BF16)
        cv_s[SUBLANE:first, :] = cv_s[tm + SUBLANE:tm + first, :]

        d_h = part(2)
        d_b = part(3)
        hc_s[HALO_D:HALO_D + tm, :] = part(4) * d_h
        q = jnp.zeros((tm, w), F32)
        for k in range(CONV_D_WIDTH):
            q = q + cw_ref[k:k + 1, :] * hc_s[pl.ds(HALO_D - (CONV_D_WIDTH - 1) + k, tm), :]
        hc_s[0:HALO_D, :] = hc_s[tm:tm + HALO_D, :]
        qb = q.astype(BF16)
        q_ref[...] = qb
        y_ref[:, w:2 * w] = (d_b * qb.astype(F32) * _silu(part(5))).astype(BF16)

    row = lambda cols: pl.BlockSpec((tm, cols), lambda i: (i, 0))
    return _call(
        body, name="odd_fwd", grid=(nt,), rider=rider,
        out_shape=(jax.ShapeDtypeStruct((t_len, 6 * w), BF16), jax.ShapeDtypeStruct((t_len, d_model), BF16),
                   jax.ShapeDtypeStruct((t_len, w), BF16), jax.ShapeDtypeStruct((t_len, w), BF16)),
        in_specs=[row(d_model), _const(w_in.shape), _const(b_in.shape), _const(w_pool.shape), _const(pool_scale.shape),
                  _const(conv_w.shape)],
        out_specs=(row(6 * w), row(d_model), row(w), row(w)),
        scratch=[pltpu.VMEM((tm + HALO_C + SUBLANE, w), F32), pltpu.VMEM((tm + HALO_D, w), F32),
                 pltpu.VMEM((2, tm + HALO_C + SUBLANE, LANE), F32)],
        args=[x, w_in, b_in, w_pool, pool_scale, conv_w])


def _post_fwd(y, x, p_all, layer, w_out, b_out, ln_g, ln_b, wg, bg, wp, loss_target, *, tm, rider=None):
    t_len, d_model = x.shape
    d_ple = p_all.shape[-1]
    nt = t_len // tm
    last = loss_target is not None

    def body(*refs):
        y_ref, x_ref, p_ref, wo_ref, bo_ref, g_ref, b_ref, wg_ref, bg_ref, wp_ref = refs[:10]
        rest = refs[10:]
        if last:
            lt_ref, rest = rest[0], rest[1:]
        rh_ref, rstd_ref, gate_ref, e_ref = rest[:4]
        r = DEEPNORM_ALPHA * x_ref[...] + _dot(y_ref[...], wo_ref[...]) + bo_ref[...]
        rh, rstd = _norm(r)
        h = rh * g_ref[...] + b_ref[...]
        gate = _sigmoid(_dot(h.astype(BF16), wg_ref[...]) + bg_ref[...])
        e = _dot(p_ref[...].astype(BF16), wp_ref[...])
        xn = h + gate * e
        rh_ref[...] = rh.astype(BF16)
        rstd_ref[...] = jnp.broadcast_to(rstd, (tm, LANE))
        gate_ref[...] = gate.astype(BF16)
        e_ref[...] = e.astype(BF16)
        if not last:
            rest[4][...] = xn
        else:
            dxn_ref, sse_ref = rest[4:]
            diff = xn - lt_ref[...]
            dxn_ref[...] = (diff * (1.0 / d_model)).astype(BF16)

            @pl.when(pl.program_id(0) == 0)
            def _():
                sse_ref[...] = jnp.zeros_like(sse_ref)
            sse_ref[...] += jnp.sum(_sum0(diff * diff), axis=1, keepdims=True)

    row = lambda cols: pl.BlockSpec((tm, cols), lambda i: (i, 0))
    in_specs = [row(d_model), row(d_model), pl.BlockSpec((None, tm, d_ple), lambda i: (layer, i, 0)),
                _const(w_out.shape), _const(b_out.shape), _const(ln_g.shape), _const(ln_b.shape), _const(wg.shape),
                _const(bg.shape), _const(wp.shape)]
    args = [y, x, p_all, w_out, b_out, ln_g, ln_b, wg, bg, wp]
    out_shape = [jax.ShapeDtypeStruct((t_len, d_model), BF16), jax.ShapeDtypeStruct((t_len, LANE), F32),
                 jax.ShapeDtypeStruct((t_len, d_model), BF16), jax.ShapeDtypeStruct((t_len, d_model), BF16),
                 jax.ShapeDtypeStruct((t_len, d_model), BF16 if last else F32)]
    out_specs = [row(d_model), row(LANE), row(d_model), row(d_model), row(d_model)]
    if last:
        in_specs.append(row(d_model))
        args.append(loss_target)
        out_shape.append(jax.ShapeDtypeStruct((SUBLANE, LANE), F32))
        out_specs.append(_const((SUBLANE, LANE)))
    return _call(body, name="post_fwd_last" if last else "post_fwd", grid=(nt,), out_shape=out_shape, in_specs=in_specs,
                 out_specs=out_specs, args=args, rider=rider)


def _post_bwd(dxn, gate, e, rh, rstd, ln_g, wg, w_out, *, tm, rider=None):
    t_len, d_model = dxn.shape
    nt = t_len // tm

    def body(dxn_ref, gate_ref, e_ref, rh_ref, rstd_ref, g_ref, wg_ref, wo_ref, de_ref, dgl_ref, dy_ref, dr_ref, acc_ref):
        @pl.when(pl.program_id(0) == 0)
        def _():
            acc_ref[...] = jnp.zeros_like(acc_ref)

        d = dxn_ref[...].astype(F32)
        gt = gate_ref[...].astype(F32)
        rhat = rh_ref[...].astype(F32)
        de_ref[...] = (d * gt).astype(BF16)
        dgl = d * e_ref[...].astype(F32) * gt * (1.0 - gt)
        dglb = dgl.astype(BF16)
        dgl_ref[...] = dglb
        dh = d + _dot_nt(dglb, wg_ref[...])
        dr = _norm_bwd(dh * g_ref[...], rhat, rstd_ref[:, 0:1])
        drb = dr.astype(BF16)
        dr_ref[...] = drb
        dy_ref[...] = _dot_nt(drb, wo_ref[...]).astype(BF16)
        acc_ref[0:1, :] += _sum0(dgl)
        acc_ref[1:2, :] += _sum0(dh * rhat)
        acc_ref[2:3, :] += _sum0(dh)
        acc_ref[3:4, :] += _sum0(dr)

    row = lambda cols: pl.BlockSpec((tm, cols), lambda i: (i, 0))
    return _call(
        body, name="post_bwd", grid=(nt,), rider=rider,
        out_shape=(jax.ShapeDtypeStruct((t_len, d_model), BF16), jax.ShapeDtypeStruct((t_len, d_model), BF16),
                   jax.ShapeDtypeStruct((t_len, d_model), BF16), jax.ShapeDtypeStruct((t_len, d_model), BF16),
                   jax.ShapeDtypeStruct((SUBLANE, d_model), F32)),
        in_specs=[row(d_model), row(d_model), row(d_model), row(d_model), row(LANE), _const(ln_g.shape), _const(wg.shape),
                  _const(w_out.shape)],
        out_specs=(row(d_model), row(d_model), row(d_model), row(d_model), _const((SUBLANE, d_model))),
        args=[dxn, gate, e, rh, rstd, ln_g, wg, w_out])


def _dz_store(dz_ref, dbin_ref, width):
    def store(k, v):
        cols = slice(k * width, (k + 1) * width)
        vb = v.astype(BF16)
        dz_ref[:, cols] = vb
        dbin_ref[0:1, cols] += _sum0(v)
        return vb
    return store


def _even_bwd(dy, z, a1, sg, dr, w_in, conv_w, ln_a_g, ln_a_b, ln_v_g, ln_v_b, wcat_t, *, seq, tm, dx_dtype, rider=None):
    t_len, d_model = dr.shape
    w = d_model // 2
    nt, tps, nb = t_len // tm, seq // tm, tm // LANE
    n_heads = 2 * (w // LANE)

    def body(dy_ref, z_ref, a1_ref, sg_ref, dr_ref, w_ref, cw_ref, lag_ref, lab_ref, lvg_ref, lvb_ref, wct_ref,
             dz_ref, dx_ref, dcw_ref, vec_ref, dbin_ref, dws_ref, dbs_ref, da1_s, a0_s, da0_s, cw_acc, shift_s, dx_s):
        i = pl.program_id(0)
        tile = nt - 1 - i

        @pl.when(i == 0)
        def _():
            vec_ref[...] = jnp.zeros_like(vec_ref)
            dbin_ref[...] = jnp.zeros_like(dbin_ref)
            dws_ref[...] = jnp.zeros_like(dws_ref)
            dbs_ref[...] = jnp.zeros_like(dbs_ref)
            cw_acc[...] = jnp.zeros_like(cw_acc)

        @pl.when((tile + 1) % tps == 0)
        def _():
            da1_s[tm:tm + HALO_A, :] = jnp.zeros((HALO_A, w), F32)

        zp = lambda k: z_ref[:, k * w:(k + 1) * w].astype(F32)
        store = _dz_store(dz_ref, dbin_ref, w)

        dg = dy_ref[:, w:2 * w].astype(F32)
        u, v, gg = zp(3), zp(4), zp(5)
        sgv = sg_ref[...].astype(F32)
        gelu_u, dgelu_u = _gelu_grad(u)
        silu_gg, dsilu_gg = _silu_grad(gg)
        dzb5 = store(5, dg * gelu_u * sgv * dsilu_gg)
        t1 = dg * silu_gg
        dzb3 = store(3, t1 * sgv * dgelu_u)
        dsg = t1 * gelu_u
        gelu_v, dgelu_v = _gelu_grad(v)
        vh, rstd_v = _norm(gelu_v)
        v2 = vh * lvg_ref[...] + lvb_ref[...]
        low = _head_low_mask(nb * LANE)
        for j in range(w // LANE):
            dt = _blocks_to_lanes(dsg, j, nb)
            d_lo = jnp.where(low, dt, 0.0).astype(BF16)
            d_hi = jnp.where(low, 0.0, dt).astype(BF16)
            v2t = _blocks_to_lanes(v2, j, nb).astype(BF16)
            dv2t = _dot(wct_ref[j], jnp.concatenate([d_lo, d_hi], axis=0))
            for n in range(nb):
                da0_s[n * LANE:(n + 1) * LANE, j * LANE:(j + 1) * LANE] = dv2t[:, n * LANE:(n + 1) * LANE]
            dws_ref[2 * j] += _dot_nt(d_lo, v2t)
            dws_ref[2 * j + 1] += _dot_nt(d_hi, v2t)
            bsum = dt[:, 0:LANE]
            for n in range(1, nb):
                bsum = bsum + dt[:, n * LANE:(n + 1) * LANE]
            dbs_ref[:, j * LANE:(j + 1) * LANE] += bsum
        dv2 = da0_s[...]
        vec_ref[3:4, :] += _sum0(dv2 * vh)
        vec_ref[4:5, :] += _sum0(dv2)
        dzb4 = store(4, _norm_bwd(dv2 * lvg_ref[...], vh, rstd_v) * dgelu_v)
        dx_b = DEEPNORM_ALPHA * dr_ref[...].astype(F32)
        for k, dzb in ((3, dzb3), (4, dzb4), (5, dzb5)):
            dx_b = dx_b + _dot_nt(dzb, w_ref[:, k * w:(k + 1) * w])
        dx_s[...] = dx_b

        da = dy_ref[:, 0:w].astype(F32)
        a_val, a_glu, a_gate = zp(0), zp(1), zp(2)
        s_glu = _sigmoid(a_glu)
        a0_s[...] = a_val * s_glu
        ah, rstd_a = _norm(a1_ref[...])
        silu_a2, dsilu_a2 = _silu_grad(ah * lag_ref[...] + lab_ref[...])
        silu_ag, dsilu_ag = _silu_grad(a_gate)
        dzb2 = store(2, da * silu_a2 * dsilu_ag)
        da2 = da * silu_ag * dsilu_a2
        vec_ref[1:2, :] += _sum0(da2 * ah)
        vec_ref[2:3, :] += _sum0(da2)
        da1 = _norm_bwd(da2 * lag_ref[...], ah, rstd_a)
        vec_ref[0:1, :] += _sum0(da1)
        da1_s[0:tm, :] = da1
        window = _shifted_windows(da1_s, shift_s, tm + HALO_A)
        for r in range(0, tm, ROW_CHUNK):
            a0c = a0_s[r:r + ROW_CHUNK, :]
            acc = jnp.zeros((ROW_CHUNK, w), F32)
            for k in range(CONV_A_WIDTH):
                d = window(r + (CONV_A_WIDTH - 1) - k)
                acc = acc + cw_ref[k:k + 1, :] * d
                pw = a0c * d
                p8 = pw[0:SUBLANE]
                for q in range(1, ROW_CHUNK // SUBLANE):
                    p8 = p8 + pw[q * SUBLANE:(q + 1) * SUBLANE]
                cw_acc[k * SUBLANE:(k + 1) * SUBLANE, :] += p8
            da0_s[r:r + ROW_CHUNK, :] = acc
        da1_s[tm:tm + HALO_A, :] = da1_s[0:HALO_A, :]
        da0 = da0_s[...]
        dzb0 = store(0, da0 * s_glu)
        dzb1 = store(1, da0 * a_val * s_glu * (1.0 - s_glu))

        dx = dx_s[...]
        for k, dzb in ((0, dzb0), (1, dzb1), (2, dzb2)):
            dx = dx + _dot_nt(dzb, w_ref[:, k * w:(k + 1) * w])
        dx_ref[...] = dx.astype(dx_ref.dtype)

        @pl.when(i == nt - 1)
        def _():
            for k in range(CONV_A_WIDTH):
                dcw_ref[k:k + 1, :] = _sum0(cw_acc[k * SUBLANE:(k + 1) * SUBLANE, :])
            keep = (lax.broadcasted_iota(jnp.int32, (LANE, LANE), 0) >= lax.broadcasted_iota(jnp.int32, (LANE, LANE), 1))
            for hd in range(n_heads):
                dws_ref[hd] = jnp.where(keep, dws_ref[hd], 0.0)

    rev = lambda cols: pl.BlockSpec((tm, cols), lambda i: (nt - 1 - i, 0))
    return _call(
        body, name="even_bwd", grid=(nt,), rider=rider,
        out_shape=(jax.ShapeDtypeStruct((t_len, 6 * w), BF16), jax.ShapeDtypeStruct((t_len, d_model), dx_dtype),
                   jax.ShapeDtypeStruct((CONV_A_WIDTH, w), F32), jax.ShapeDtypeStruct((SUBLANE, w), F32),
                   jax.ShapeDtypeStruct((SUBLANE, 6 * w), F32), jax.ShapeDtypeStruct((n_heads, LANE, LANE), F32),
                   jax.ShapeDtypeStruct((LANE, w), F32)),
        in_specs=[rev(d_model), rev(6 * w), rev(w), rev(w), rev(d_model), _const(w_in.shape), _const(conv_w.shape),
                  _const(ln_a_g.shape), _const(ln_a_b.shape), _const(ln_v_g.shape), _const(ln_v_b.shape), _const(wcat_t.shape)],
        out_specs=(rev(6 * w), rev(d_model), _const((CONV_A_WIDTH, w)), _const((SUBLANE, w)), _const((SUBLANE, 6 * w)),
                   _const((n_heads, LANE, LANE)), _const((LANE, w))),
        scratch=[pltpu.VMEM((tm + HALO_A, w), F32), pltpu.VMEM((tm, w), F32), pltpu.VMEM((tm, w), F32),
                 pltpu.VMEM((CONV_A_WIDTH * SUBLANE, w), F32), pltpu.VMEM((SUBLANE - 1, tm + HALO_A, w), F32),
                 pltpu.VMEM((tm, d_model), F32)],
        args=[dy, z, a1, sg, dr, w_in, conv_w, ln_a_g, ln_a_b, ln_v_g, ln_v_b, wcat_t])


def _odd_bwd(dy, z, pooled, q, dr, w_in, w_pool, pool_scale, conv_w, *, seq, tm, dx_dtype, rider=None):
    t_len, d_model = dr.shape
    w = d_model // 2
    nt, tps = t_len // tm, seq // tm
    n_groups = len(POOL_WINDOWS)

    def body(dy_ref, z_ref, pooled_ref, q_ref, dr_ref, w_ref, wp_ref, ps_ref, cw_ref,
             dz_ref, dx_ref, dwp_ref, vec_ref, dbin_ref, dm_s, dq_s, lv_s):
        i = pl.program_id(0)
        tile = nt - 1 - i

        @pl.when(i == 0)
        def _():
            dwp_ref[...] = jnp.zeros_like(dwp_ref)
            vec_ref[...] = jnp.zeros_like(vec_ref)
            dbin_ref[...] = jnp.zeros_like(dbin_ref)
            dm_s[tm + HALO_C:tm + HALO_C + SUBLANE, :] = jnp.zeros((SUBLANE, w), F32)
            lv_s[:, tm + HALO_C:tm + HALO_C + SUBLANE, :] = jnp.zeros((2, SUBLANE, LANE), F32)

        @pl.when((tile + 1) % tps == 0)
        def _():
            dm_s[tm:tm + HALO_C, :] = jnp.zeros((HALO_C, w), F32)
            dq_s[tm:tm + HALO_D, :] = jnp.zeros((HALO_D, w), F32)

        zp = lambda k: z_ref[:, k * w:(k + 1) * w].astype(F32)
        store = _dz_store(dz_ref, dbin_ref, w)

        dc = dy_ref[:, 0:w].astype(F32)
        c_gate = zp(1)
        silu_c, dsilu_c = _silu_grad(c_gate)
        dcs = dc * silu_c
        dvg_parts, dcg_parts = [], []
        for gi, win in enumerate(POOL_WINDOWS):
            cols = slice(gi * LANE, (gi + 1) * LANE)
            pooled_g = pooled_ref[:, cols]
            wp = wp_ref[gi]
            cpre = _dot(pooled_g, wp)
            scale = ps_ref[:, cols]
            vec_ref[0:1, cols] += _sum0(dcs[:, cols] * cpre)
            dcg_parts.append(dc[:, cols] * cpre * scale * dsilu_c[:, cols])
            dcp = (dcs[:, cols] * scale).astype(BF16)
            dwp_ref[gi] += _dot_tn(pooled_g, dcp)
            dpooled = _dot_nt(dcp, wp)
            dm_s[0:tm, cols] = dpooled * _pool_inverse(tile, tm, seq, win)
            s = _doubling_sums(dm_s, cols, lv_s, win, 0, tm + HALO_C, 1)(0, tm)
            dvg_parts.append(s - dpooled)
        dm_s[tm:tm + HALO_C, :] = dm_s[0:HALO_C, :]
        dzb0 = store(0, jnp.concatenate(dvg_parts, axis=1))
        dzb1 = store(1, jnp.concatenate(dcg_parts, axis=1))

        dd = dy_ref[:, w:2 * w].astype(F32)
        d_h, d_b, d_c, d_gate = zp(2), zp(3), zp(4), zp(5)
        qv = q_ref[...].astype(F32)
        silu_d, dsilu_d = _silu_grad(d_gate)
        dzb5 = store(5, dd * d_b * qv * dsilu_d)
        dzb3 = store(3, dd * qv * silu_d)
        dq_s[0:tm, :] = dd * d_b * silu_d
        hc = d_c * d_h
        dhc = jnp.zeros((tm, w), F32)
        for k in range(CONV_D_WIDTH):
            d = dq_s[pl.ds((CONV_D_WIDTH - 1) - k, tm), :]
            dhc = dhc + cw_ref[k:k + 1, :] * d
            vec_ref[1 + k:2 + k, :] += _sum0(hc * d)
        dq_s[tm:tm + HALO_D, :] = dq_s[0:HALO_D, :]
        dzb2 = store(2, dhc * d_c)
        dzb4 = store(4, dhc * d_h)

        dx = DEEPNORM_ALPHA * dr_ref[...].astype(F32)
        for k, dzb in enumerate((dzb0, dzb1, dzb2, dzb3, dzb4, dzb5)):
            dx = dx + _dot_nt(dzb, w_ref[:, k * w:(k + 1) * w])
        dx_ref[...] = dx.astype(dx_ref.dtype)

    rev = lambda cols: pl.BlockSpec((tm, cols), lambda i: (nt - 1 - i, 0))
    return _call(
        body, name="odd_bwd", grid=(nt,), rider=rider,
        out_shape=(jax.ShapeDtypeStruct((t_len, 6 * w), BF16), jax.ShapeDtypeStruct((t_len, d_model), dx_dtype),
                   jax.ShapeDtypeStruct((n_groups, LANE, LANE), F32), jax.ShapeDtypeStruct((SUBLANE, w), F32),
                   jax.ShapeDtypeStruct((SUBLANE, 6 * w), F32)),
        in_specs=[rev(d_model), rev(6 * w), rev(w), rev(w), rev(d_model), _const(w_in.shape), _const(w_pool.shape),
                  _const(pool_scale.shape), _const(conv_w.shape)],
        out_specs=(rev(6 * w), rev(d_model), _const((n_groups, LANE, LANE)), _const((SUBLANE, w)), _const((SUBLANE, 6 * w))),
        scratch=[pltpu.VMEM((tm + HALO_C + SUBLANE, w), F32), pltpu.VMEM((tm + HALO_D, w), F32),
                 pltpu.VMEM((2, tm + HALO_C + SUBLANE, LANE), F32)],
        args=[dy, z, pooled, q, dr, w_in, w_pool, pool_scale, conv_w])


def _weight_grad(name, a, b, a_layer=None, bn=None, a_cols=None, rider=None):
    if a_layer is None:
        t_len, m = a.shape
        col = 0
        if a_cols is not None:
            col, m = a_cols
        a_spec = lambda tk: pl.BlockSpec((tk, m), lambda n, k: (k, col))
    else:
        _, t_len, m = a.shape
        a_spec = lambda tk: pl.BlockSpec((None, tk, m), lambda n, k: (a_layer, k, 0))
    n_cols = b.shape[1]
    bn = n_cols if bn is None else bn
    tk = min(t_len, 1024)
    n_k = t_len // tk

    def body(a_ref, b_ref, o_ref, acc):
        k = pl.program_id(1)

        @pl.when(k == 0)
        def _():
            acc[...] = jnp.zeros_like(acc)
        acc[...] += _dot_tn(a_ref[...].astype(BF16), b_ref[...].astype(BF16))

        @pl.when(k == n_k - 1)
        def _():
            o_ref[...] = acc[...].astype(BF16)

    outs = _call(
        body, name=name, grid=(n_cols // bn, n_k), out_shape=[jax.ShapeDtypeStruct((m, n_cols), BF16)],
        in_specs=[a_spec(tk), pl.BlockSpec((tk, bn), lambda n, k: (k, n))],
        out_specs=[pl.BlockSpec((m, bn), lambda n, k: (0, n))], scratch=[pltpu.VMEM((m, bn), F32)],
        args=[a, b], rider=rider)
    return outs[0] if rider is None else outs


def _weight_grads_post(name, y, dr, rh, ln_g, ln_b, dgl, p_all, layer, de):
    t_len, d_model = y.shape
    d_ple = p_all.shape[-1]
    tk = min(t_len, 1024)
    n_k = t_len // tk

    def body(y_ref, dr_ref, rh_ref, g_ref, b_ref, dgl_ref, p_ref, de_ref, o_out, o_gate, o_ple, acc_out, acc_gate, acc_ple):
        k = pl.program_id(0)

        @pl.when(k == 0)
        def _():
            acc_out[...] = jnp.zeros_like(acc_out)
            acc_gate[...] = jnp.zeros_like(acc_gate)
            acc_ple[...] = jnp.zeros_like(acc_ple)
        acc_out[...] += _dot_tn(y_ref[...], dr_ref[...])
        h = (rh_ref[...].astype(F32) * g_ref[...] + b_ref[...]).astype(BF16)
        acc_gate[...] += _dot_tn(h, dgl_ref[...])
        acc_ple[...] += _dot_tn(p_ref[...].astype(BF16), de_ref[...])

        @pl.when(k == n_k - 1)
        def _():
            o_out[...] = acc_out[...].astype(BF16)
            o_gate[...] = acc_gate[...].astype(BF16)
            o_ple[...] = acc_ple[...].astype(BF16)

    row = lambda cols: pl.BlockSpec((tk, cols), lambda k: (k, 0))
    return pl.pallas_call(
        body, name=name, grid=(n_k,),
        out_shape=(jax.ShapeDtypeStruct((d_model, d_model), BF16), jax.ShapeDtypeStruct((d_model, d_model), BF16),
                   jax.ShapeDtypeStruct((d_ple, d_model), BF16)),
        in_specs=[row(d_model), row(d_model), row(d_model), _const(ln_g.shape), _const(ln_b.shape), row(d_model),
                  pl.BlockSpec((None, tk, d_ple), lambda k: (layer, k, 0)), row(d_model)],
        out_specs=(_const((d_model, d_model)), _const((d_model, d_model)), _const((d_ple, d_model))),
        scratch_shapes=[pltpu.VMEM((d_model, d_model), F32), pltpu.VMEM((d_model, d_model), F32),
                        pltpu.VMEM((d_ple, d_model), F32)],
        compiler_params=_params(1),
    )(y, dr, rh, ln_g, ln_b, dgl, p_all, de)


def _adamw_reduce(name, parts, w, m, v, rows_per_block):
    n_rows, n_cols = w.shape
    br = rows_per_block
    n_parts = parts.shape[0]

    def body(p_ref, w_ref, m_ref, v_ref, g_ref, d_ref, nm_ref, nv_ref):
        g = p_ref[0].astype(F32)
        for k in range(1, n_parts):
            g = g + p_ref[k].astype(F32)
        nm = ADAM_B1 * m_ref[...] + (1.0 - ADAM_B1) * g
        nv = ADAM_B2 * v_ref[...] + (1.0 - ADAM_B2) * (g * g)
        m_hat = nm / (1.0 - ADAM_B1 ** ADAM_STEP)
        v_hat = nv / (1.0 - ADAM_B2 ** ADAM_STEP)
        g_ref[...] = g
        d_ref[...] = -ADAM_LR * (m_hat / (jnp.sqrt(v_hat) + ADAM_EPS) + ADAM_WD * w_ref[...])
        nm_ref[...] = nm
        nv_ref[...] = nv

    blk = pl.BlockSpec((br, n_cols), lambda i: (i, 0))
    shp = jax.ShapeDtypeStruct((n_rows, n_cols), F32)
    return pl.pallas_call(
        body, name=name, grid=(n_rows // br,), out_shape=(shp, shp, shp, shp),
        in_specs=[pl.BlockSpec((n_parts, br, n_cols), lambda i: (0, i, 0)), blk, blk, blk], out_specs=(blk, blk, blk, blk),
        compiler_params=_params(1),
    )(parts, w, m, v)


def _pack_rows(flat_parts, pad_to=None, width=LANE):
    flat = jnp.concatenate([a.reshape(-1) for a in flat_parts])
    if pad_to is not None and pad_to > flat.shape[0]:
        flat = jnp.concatenate([flat, jnp.zeros((pad_to - flat.shape[0],), flat.dtype)])
    return flat.reshape(-1, width)


def _unpack_rows(packed, shapes):
    flat = packed.reshape(-1)
    out, off = [], 0
    for s in shapes:
        n = math.prod(s)
        out.append(flat[off:off + n].reshape(s))
        off += n
    return out


def _to_dest_major(full):
    lead, last = full.shape[:-1], full.shape[-1]
    t = full.reshape(lead + (N_DEV, last // N_DEV))
    return jnp.moveaxis(t, -2, 0).reshape(N_DEV, -1)


def _from_source_major(blocks, shard_shape):
    t = blocks.reshape((N_DEV,) + tuple(shard_shape))
    t = jnp.moveaxis(t, 0, -2)
    return t.reshape(tuple(shard_shape[:-1]) + (N_DEV * shard_shape[-1],))


def _block_rows(n_rows, n_cols, target_elems=96 * 1024):
    best = None
    for br in range(SUBLANE, n_rows + 1, SUBLANE):
        if n_rows % br == 0 and br * n_cols <= target_elems:
            best = br
    return n_rows if best is None else best


def kernel(x, p, w_in_e, b_in_e, conv_a_w, conv_a_b, ln_a_g, ln_a_b, ln_v_g, ln_v_b, w_s, b_s, w_out_e, b_out_e, w_in_o, b_in_o, w_pool, pool_scale, conv_d_w, w_out_o, b_out_o, ln_g, ln_b, w_ple, w_ple_gate, b_ple_gate, loss_target, m_w_in_e, m_b_in_e, m_conv_a_w, m_conv_a_b, m_ln_a_g, m_ln_a_b, m_ln_v_g, m_ln_v_b, m_w_s, m_b_s, m_w_out_e, m_b_out_e, m_w_in_o, m_b_in_o, m_w_pool, m_pool_scale, m_conv_d_w, m_w_out_o, m_b_out_o, m_ln_g, m_ln_b, m_w_ple, m_w_ple_gate, m_b_ple_gate, v_w_in_e, v_b_in_e, v_conv_a_w, v_conv_a_b, v_ln_a_g, v_ln_a_b, v_ln_v_g, v_ln_v_b, v_w_s, v_b_s, v_w_out_e, v_b_out_e, v_w_in_o, v_b_in_o, v_w_pool, v_pool_scale, v_conv_d_w, v_w_out_o, v_b_out_o, v_ln_g, v_ln_b, v_w_ple, v_w_ple_gate, v_b_ple_gate):
    weights = dict(w_in_e=w_in_e, b_in_e=b_in_e, conv_a_w=conv_a_w, conv_a_b=conv_a_b, ln_a_g=ln_a_g, ln_a_b=ln_a_b,
                   ln_v_g=ln_v_g, ln_v_b=ln_v_b, w_s=w_s, b_s=b_s, w_out_e=w_out_e, b_out_e=b_out_e, w_in_o=w_in_o,
                   b_in_o=b_in_o, w_pool=w_pool, pool_scale=pool_scale, conv_d_w=conv_d_w, w_out_o=w_out_o,
                   b_out_o=b_out_o, ln_g=ln_g, ln_b=ln_b, w_ple=w_ple, w_ple_gate=w_ple_gate, b_ple_gate=b_ple_gate)
    mom_m = dict(w_in_e=m_w_in_e, b_in_e=m_b_in_e, conv_a_w=m_conv_a_w, conv_a_b=m_conv_a_b, ln_a_g=m_ln_a_g,
                 ln_a_b=m_ln_a_b, ln_v_g=m_ln_v_g, ln_v_b=m_ln_v_b, w_s=m_w_s, b_s=m_b_s, w_out_e=m_w_out_e,
                 b_out_e=m_b_out_e, w_in_o=m_w_in_o, b_in_o=m_b_in_o, w_pool=m_w_pool, pool_scale=m_pool_scale,
                 conv_d_w=m_conv_d_w, w_out_o=m_w_out_o, b_out_o=m_b_out_o, ln_g=m_ln_g, ln_b=m_ln_b, w_ple=m_w_ple,
                 w_ple_gate=m_w_ple_gate, b_ple_gate=m_b_ple_gate)
    mom_v = dict(w_in_e=v_w_in_e, b_in_e=v_b_in_e, conv_a_w=v_conv_a_w, conv_a_b=v_conv_a_b, ln_a_g=v_ln_a_g,
                 ln_a_b=v_ln_a_b, ln_v_g=v_ln_v_g, ln_v_b=v_ln_v_b, w_s=v_w_s, b_s=v_b_s, w_out_e=v_w_out_e,
                 b_out_e=v_b_out_e, w_in_o=v_w_in_o, b_in_o=v_b_in_o, w_pool=v_w_pool, pool_scale=v_pool_scale,
                 conv_d_w=v_conv_d_w, w_out_o=v_w_out_o, b_out_o=v_b_out_o, ln_g=v_ln_g, ln_b=v_ln_b, w_ple=v_w_ple,
                 w_ple_gate=v_w_ple_gate, b_ple_gate=v_b_ple_gate)
    names = tuple(weights)

    batch, seq, d_model = x.shape
    t_len = batch * seq
    w = d_model // 2
    n_even = w_in_e.shape[0]
    n_odd = w_in_o.shape[0]
    depth = ln_g.shape[0]
    d_ple = p.shape[-1]
    n_heads = w_s.shape[1]
    tm = 512 if seq % 512 == 0 and seq >= 1024 else seq // 2
    in_cols = w_in_e.shape[-1]
    out_rows = w_out_e.shape[1]
    ple_cols = w_ple.shape[-1]
    gate_rows = w_ple_gate.shape[1]

    sh_shapes = [weights[n].shape for n in SH_NAMES]
    sh_len = sum(math.prod(s) for s in sh_shapes)
    sh_pad = -(-sh_len // (SUBLANE * LANE)) * (SUBLANE * LANE)
    sh_rows = sh_pad // LANE
    sds = jax.ShapeDtypeStruct
    w_in16 = (w_in_e.astype(BF16), w_in_o.astype(BF16))
    w_out16 = (w_out_e.astype(BF16), w_out_o.astype(BF16))
    w_ple16, w_gate16 = w_ple.astype(BF16), w_ple_gate.astype(BF16)

    kinds = ("in", "out", "ple", "gate")

    def weight_entries(i, which=kinds):
        j, par = i // 2, i % 2
        all_four = {"in": (w_in16[par][j], _whole_view, sds((d_model, N_DEV * in_cols), BF16), _axis_view(1, in_cols)),
                    "out": (w_out16[par][j], _whole_view, sds((N_DEV * out_rows, d_model), BF16), _axis_view(0, out_rows)),
                    "ple": (w_ple16[i], _whole_view, sds((d_ple, N_DEV * ple_cols), BF16), _axis_view(1, ple_cols)),
                    "gate": (w_gate16[i], _whole_view, sds((N_DEV * gate_rows, d_model), BF16), _axis_view(0, gate_rows))}
        return [((i, k), all_four[k]) for k in which]

    fwd_riders = {("mixer", 0): weight_entries(0, kinds[1:]) + weight_entries(1, kinds[:1]),
                  ("post", 0): weight_entries(1, kinds[1:])}
    for i in range(1, depth - 1):
        if i % 2:
            fwd_riders[("mixer", i)] = weight_entries(i + 1, kinds[:1])
            fwd_riders[("post", i)] = weight_entries(i + 1, kinds[1:])
        else:
            fwd_riders[("mixer", i)] = weight_entries(i + 1)
    layer_w = {}

    def carried(where):
        tagged = fwd_riders.get(where)
        if tagged is None:
            return None, lambda landed: None
        return _Rider([e for _, e in tagged]), lambda landed: layer_w.update(zip([t for t, _ in tagged], landed))

    w_in_first = weight_entries(0, kinds[:1])[0][1]
    first = _gather_two_level("gather_first", [w_in_first[0], _pack_rows([weights[n] for n in SH_NAMES], sh_pad)],
                              [w_in_first[2], sds((N_DEV, sh_rows, LANE), F32)], [w_in_first[3], _slot_view()])
    layer_w[(0, "in")] = first[0]
    sh_flat = first[1].reshape(N_DEV, sh_pad)
    full_small, off = {}, 0
    for n, s in zip(SH_NAMES, sh_shapes):
        size = math.prod(s)
        full_small[n] = _from_source_major(sh_flat[:, off:off + size], s)
        off += size

    tril = jnp.tril(jnp.ones((LANE, LANE), dtype=bool))
    ws_m = jnp.where(tril[None, None], w_s, 0.0)
    pair = lambda t: jnp.concatenate([t[:, 0::2], t[:, 1::2]], axis=-1).astype(BF16)
    wcat = pair(ws_m)
    wcat_t = pair(jnp.swapaxes(ws_m, -1, -2))
    bs_full = jnp.repeat(jnp.swapaxes(b_s, -1, -2), w // n_heads, axis=-1)
    row2 = lambda a, j: a[j][None, :]

    x2 = x.reshape(t_len, d_model)
    p3 = p.reshape(depth, t_len, d_ple)
    lt2 = loss_target.reshape(t_len, d_model)

    xs, saved = [x2], []
    dxn = sse = None
    for i in range(depth):
        j = i // 2
        last = i == depth - 1
        rider, file_weights = carried(("mixer", i))
        if i % 2 == 0:
            outs = _even_fwd(xs[i], layer_w[(i, "in")], row2(b_in_e, j), full_small["conv_a_w"][j], row2(conv_a_b, j),
                             row2(ln_a_g, j), row2(ln_a_b, j), row2(ln_v_g, j), row2(ln_v_b, j), wcat[j], bs_full[j],
                             seq=seq, tm=tm, rider=rider)
            b_out = row2(b_out_e, j)
        else:
            outs = _odd_fwd(xs[i], layer_w[(i, "in")], row2(full_small["b_in_o"], j), w_pool[j].astype(BF16),
                            row2(full_small["pool_scale"], j), full_small["conv_d_w"][j], seq=seq, tm=tm, rider=rider)
            b_out = row2(full_small["b_out_o"], j)
        z, y, s1, s2 = outs[:4]
        file_weights(outs[4:])
        rider, file_weights = carried(("post", i))
        outs = _post_fwd(y, xs[i], p3, i, layer_w[(i, "out")], b_out, row2(ln_g, i), row2(ln_b, i), layer_w[(i, "gate")],
                         row2(b_ple_gate, i), layer_w[(i, "ple")], lt2 if last else None, tm=tm, rider=rider)
        rh, rstd, gate, e = outs[:4]
        if last:
            dxn, sse = outs[4:6]
        else:
            xs.append(outs[4])
        file_weights(outs[6 if last else 5:])
        saved.append(dict(z=z, y=y, s1=s1, s2=s2, rh=rh, rstd=rstd, gate=gate, e=e))


    recv = {"w_in_e": None, "w_in_o": None, "w_out_e": None, "w_out_o": None, "w_ple": None, "w_ple_gate": None}

    def grad_entry(i, kind, g, half=None):
        j, par = i // 2, i % 2
        sfx = "_o" if par else "_e"
        name, src_view, slot = {"in": ("w_in" + sfx, _axis_view(1, in_cols), j), "out": ("w_out" + sfx, _axis_view(0, out_rows), j),
                                "ple": ("w_ple", _axis_view(1, ple_cols), i), "gate": ("w_ple_gate", _axis_view(0, gate_rows), i)}[kind]
        dst = recv[name] if recv[name] is not None else sds((N_DEV,) + weights[name].shape, BF16)
        if half is None:
            dst_view = _slot_view(slot)
        else:
            rows = pl.ds(half[0] * half[1], half[1])
            dst_view = lambda ref, d: ref.at[d, slot, rows, :]
            if g.shape[0] != half[1]:
                src_view = lambda ref, d: ref.at[rows, pl.ds(pl.multiple_of(d * in_cols, in_cols), in_cols)]
        return name, (g, src_view, dst, dst_view)

    def ride(tagged):
        if not tagged:
            return None, lambda landed: None
        return _Rider([e for _, e in tagged]), lambda landed: recv.update(zip([n for n, _ in tagged], landed))

    small = {n: [None] * weights[n].shape[0] for n in REP_NAMES + SH_NAMES}
    grads = {}
    half_rows = d_model // 2
    for i in reversed(range(depth)):
        j, par = i // 2, i % 2
        sv = saved[i]
        w_in = layer_w[(i, "in")]
        dx_dtype = BF16 if i > 0 else F32
        split = (i + 1, "in") if par == 1 and (i + 1, "in") in grads else None
        rider, file_landed = ride([grad_entry(*split, grads[split], half=(0, half_rows))] if split else [])
        outs = _post_bwd(dxn, sv["gate"], sv["e"], sv["rh"], sv["rstd"], row2(ln_g, i), layer_w[(i, "gate")],
                         layer_w[(i, "out")], tm=tm, rider=rider)
        de, dgl, dy, dr, acc = outs[:5]
        file_landed(outs[5:])
        small["b_ple_gate"][i], small["ln_g"][i], small["ln_b"][i] = acc[0], acc[1], acc[2]
        small["b_out_o" if par else "b_out_e"][j] = acc[3]
        grads[(i, "out")], grads[(i, "gate")], grads[(i, "ple")] = _weight_grads_post(
            f"dw_post_l{i}", sv["y"], dr, sv["rh"], row2(ln_g, i), row2(ln_b, i), dgl, p3, i, de)
        rider, file_landed = ride([grad_entry(l, k, grads.pop((l, k)), half=(1, half_rows) if (l, k) == split else None)
                                   for l, k in list(grads)])
        if par == 0:
            outs = _even_bwd(dy, sv["z"], sv["s1"], sv["s2"], dr, w_in, full_small["conv_a_w"][j],
                             row2(ln_a_g, j), row2(ln_a_b, j), row2(ln_v_g, j), row2(ln_v_b, j), wcat_t[j], seq=seq, tm=tm,
                             dx_dtype=dx_dtype, rider=rider)
            dz, dx, dcw, vec, dbin, dws, dbs = outs[:7]
            file_landed(outs[7:])
            small["conv_a_w"][j], small["conv_a_b"][j] = dcw, vec[0]
            small["ln_a_g"][j], small["ln_a_b"][j], small["ln_v_g"][j], small["ln_v_b"][j] = vec[1], vec[2], vec[3], vec[4]
            small["b_in_e"][j], small["w_s"][j] = dbin[0], dws
            small["b_s"][j] = _head_sums(dbs, n_heads)
        else:
            outs = _odd_bwd(dy, sv["z"], sv["s1"], sv["s2"], dr, w_in, w_pool[j].astype(BF16),
                            row2(full_small["pool_scale"], j), full_small["conv_d_w"][j], seq=seq, tm=tm,
                            dx_dtype=dx_dtype, rider=rider)
            dz, dx, dwp, vec, dbin = outs[:5]
            file_landed(outs[5:])
            small["w_pool"][j], small["pool_scale"][j], small["conv_d_w"][j] = dwp, vec[0], vec[1:1 + CONV_D_WIDTH]
            small["b_in_o"][j] = dbin[0]
        if i > 0:
            grads[(i, "in")] = _weight_grad(f"dw_in_l{i}", xs[i], dz, bn=in_cols * N_DEV // 2)
        dxn = dx
    grad_x = dxn.reshape(batch, seq, d_model)

    top = _weight_grad("dw_in_l0_top", xs[0], dz, a_cols=(0, half_rows))
    rider, file_landed = ride([grad_entry(0, "in", top, half=(0, half_rows))])
    outs = _weight_grad("dw_in_l0_bottom", xs[0], dz, a_cols=(1, half_rows), rider=rider)
    file_landed(outs[1:])

    small_full = {n: jnp.stack(small[n]) for n in small}
    sh_part = jnp.concatenate([_to_dest_major(small_full[n]) for n in SH_NAMES], axis=1)
    sh_part = jnp.concatenate([sh_part, jnp.zeros((N_DEV, sh_pad - sh_len), F32)], axis=1).reshape(N_DEV, sh_rows, LANE)
    name, entry = grad_entry(0, "in", outs[0], half=(1, half_rows))
    rep_width = 4 * LANE
    rep_len = sum(math.prod(weights[n].shape) for n in REP_NAMES)
    rep_block = N_DEV * SUBLANE
    rep_pad = -(-(rep_len + 1) // (rep_block * rep_width)) * (rep_block * rep_width)
    rep_sum, recv[name], sh_landed = _allreduce_rows(
        "exchange_last", _pack_rows([small_full[n] for n in REP_NAMES] + [sse[0:1, 0]], rep_pad, rep_width),
        _Rider([entry, (sh_part, _slot_view(), sds((N_DEV, sh_rows, LANE), F32), _slot_view())]))
    loss = (0.5 / d_model) * rep_sum.reshape(-1)[rep_len]

    results = {}
    for n, parts in recv.items():
        shp = weights[n].shape
        rows, cols = math.prod(shp[:-1]), shp[-1]
        two = lambda a: a.reshape(rows, cols)
        outs = _adamw_reduce("adamw_" + n, parts.reshape(N_DEV, rows, cols), two(weights[n]), two(mom_m[n]), two(mom_v[n]),
                             _block_rows(rows, cols))
        results[n] = [o.reshape(shp) for o in outs]
    pack_sh = lambda d: _pack_rows([d[n] for n in SH_NAMES], sh_pad)
    outs = _adamw_reduce("adamw_small_sharded", sh_landed, pack_sh(weights), pack_sh(mom_m), pack_sh(mom_v), sh_rows)
    for n, *vals in zip(SH_NAMES, *[_unpack_rows(o, sh_shapes) for o in outs]):
        results[n] = vals
    pack_rep = lambda d: _pack_rows([d[n] for n in REP_NAMES], rep_pad, rep_width)
    rep_shapes = [weights[n].shape for n in REP_NAMES]
    outs = _adamw_reduce("adamw_replicated", rep_sum[None], pack_rep(weights), pack_rep(mom_m), pack_rep(mom_v), rep_block)
    for n, *vals in zip(REP_NAMES, *[_unpack_rows(o, rep_shapes) for o in outs]):
        results[n] = vals

    return (loss, grad_x, *[results[n][0] for n in names], *[results[n][1] for n in names],
            *[results[n][2] for n in names], *[results[n][3] for n in names])


def _head_sums(dbs, n_heads):
    t, width = dbs.shape
    return jnp.sum(dbs.reshape(t, n_heads, width // n_heads), axis=-1).T
```

```python
import math

import jax
import jax.numpy as jnp
from jax import lax
from jax.experimental import pallas as pl
from jax.experimental.pallas import tpu as pltpu

F32 = jnp.float32
BF16 = jnp.bfloat16

N_DEV = 8
DEPTH = 4
LN_EPS = 1e-5
DEEPNORM_ALPHA = (2.0 * DEPTH) ** 0.25
POOL_WINDOWS = (2, 4, 8, 16)
CONV_A_WIDTH = 31
CONV_D_WIDTH = 3
GELU_C = math.sqrt(2.0 / math.pi)
GELU_K = 0.044715

ADAM_LR = 0.001
ADAM_B1 = 0.9
ADAM_B2 = 0.999
ADAM_EPS = 1e-08
ADAM_WD = 0.01
ADAM_STEP = 10

LANE = 128
SUBLANE = 8
HALO_A = 32
HALO_C = 16
HALO_D = 8
ROW_CHUNK = 32
VMEM_LIMIT = 56 * 2**20

ANY = pl.BlockSpec(memory_space=pl.ANY)
MESH = pl.DeviceIdType.MESH

REP_NAMES = ("b_in_e", "conv_a_b", "ln_a_g", "ln_a_b", "ln_v_g", "ln_v_b", "w_s", "b_s", "b_out_e", "w_pool", "ln_g",
             "ln_b", "b_ple_gate")
SH_NAMES = ("conv_a_w", "conv_d_w", "pool_scale", "b_in_o", "b_out_o")


def _params(n_grid_axes):
    return pltpu.CompilerParams(dimension_semantics=("arbitrary",) * n_grid_axes, vmem_limit_bytes=VMEM_LIMIT)


def _const(shape):
    nd = len(shape)
    return pl.BlockSpec(shape, lambda *_: (0,) * nd)


def _dot(a, b):
    return jnp.dot(a, b, preferred_element_type=F32)


def _dot_nt(a, b):
    return lax.dot_general(a, b, (((1,), (1,)), ((), ())), preferred_element_type=F32)


def _dot_tn(a, b):
    return lax.dot_general(a, b, (((0,), (0,)), ((), ())), preferred_element_type=F32)


def _sigmoid(x):
    return jax.nn.sigmoid(x)


def _silu(x):
    return x * _sigmoid(x)


def _silu_grad(x):
    s = _sigmoid(x)
    return x * s, s * (1.0 + x * (1.0 - s))


def _gelu(x):
    return (0.5 * x) * (1.0 + jnp.tanh(x * (GELU_C + (GELU_C * GELU_K) * (x * x))))


def _gelu_grad(x):
    x2 = x * x
    t = jnp.tanh(x * (GELU_C + (GELU_C * GELU_K) * x2))
    up = 1.0 + t
    hx = 0.5 * x
    return hx * up, 0.5 * up + (hx * (1.0 - t * t)) * (GELU_C + (3.0 * GELU_C * GELU_K) * x2)


def _norm(v):
    mu = jnp.mean(v, axis=-1, keepdims=True)
    d = v - mu
    var = jnp.mean(d * d, axis=-1, keepdims=True)
    rstd = lax.rsqrt(var + LN_EPS)
    return d * rstd, rstd


def _norm_bwd(dxh, xh, rstd):
    return rstd * (dxh - jnp.mean(dxh, axis=-1, keepdims=True) - xh * jnp.mean(dxh * xh, axis=-1, keepdims=True))


def _sum0(v):
    return jnp.sum(v, axis=0, keepdims=True)


def _head_low_mask(n_cols):
    lane = lax.broadcasted_iota(jnp.int32, (LANE, n_cols), 1)
    return (lane & (LANE - 1)) < (LANE // 2)


def _blocks_to_lanes(v, j, nb):
    return jnp.concatenate([v[n * LANE:(n + 1) * LANE, j * LANE:(j + 1) * LANE] for n in range(nb)], axis=1)


def _axis_view(axis, size):
    def view(ref, d):
        idx = [slice(None)] * len(ref.shape)
        idx[axis] = pl.ds(pl.multiple_of(d * size, size), size)
        return ref.at[tuple(idx)]
    return view


def _slot_view(*slot):
    return lambda ref, d: ref.at[(d,) + slot]


def _whole_view(ref, d):
    return ref


PEER_BITS = (1, 2, 4, 6, 3, 5, 7)


class _Rider:
    def __init__(self, entries):
        self.n = len(entries)
        self.srcs = [e[0] for e in entries]
        self.src_views = [e[1] for e in entries]
        self.dsts = [e[2] for e in entries]
        self.dst_views = [e[3] for e in entries]
        self.passed = [a for a, d in enumerate(self.dsts) if not isinstance(d, jax.ShapeDtypeStruct)]

    def operands(self):
        return self.srcs + [self.dsts[a] for a in self.passed]

    def out_shape(self):
        return [jax.ShapeDtypeStruct(d.shape, d.dtype) for d in self.dsts]

    def scratch(self):
        return [pltpu.SemaphoreType.DMA((7 * self.n,)), pltpu.SemaphoreType.DMA((7 * self.n,)), pltpu.SemaphoreType.DMA((self.n,))]

    def aliases(self, n_in_before, n_out_before):
        return {n_in_before + self.n + q: n_out_before + a for q, a in enumerate(self.passed)}

    def _copies(self, src, dst, sems):
        send_sems, recv_sems, local_sems = sems
        x, y, c = lax.axis_index("x"), lax.axis_index("y"), lax.axis_index("c")
        me = 4 * x + 2 * y + c
        local = [pltpu.make_async_copy(self.src_views[a](src[a], me), self.dst_views[a](dst[a], me), local_sems.at[a])
                 for a in range(self.n)]
        sends, recvs = [], []
        for ki, k in enumerate(PEER_BITS):
            px, py, pc = x ^ (k >> 2), y ^ ((k >> 1) & 1), c ^ (k & 1)
            peer = 4 * px + 2 * py + pc
            for a in range(self.n):
                s = a * 7 + ki
                mk = lambda landing: pltpu.make_async_remote_copy(
                    src_ref=self.src_views[a](src[a], peer), dst_ref=self.dst_views[a](dst[a], landing),
                    send_sem=send_sems.at[s], recv_sem=recv_sems.at[s], device_id=(px, py, pc), device_id_type=MESH)
                sends.append(mk(me))
                recvs.append(mk(peer))
        return local, sends, recvs

    def start(self, src, dst, sems):
        local, sends, _ = self._copies(src, dst, sems)
        for cp in local + sends:
            cp.start()

    def wait(self, src, dst, sems):
        local, sends, recvs = self._copies(src, dst, sems)
        for cp in recvs:
            cp.wait_recv()
        for cp in sends:
            cp.wait_send()
        for cp in local:
            cp.wait()


def _call(body, *, name, grid, in_specs, args, out_shape, out_specs, scratch=(), rider=None, aliases=None):
    n_in, n_out, n_scr = len(args), len(out_shape), len(scratch)
    aliases = dict(aliases or {})
    if rider is None:
        return pl.pallas_call(body, name=name, grid=grid, out_shape=tuple(out_shape), in_specs=list(in_specs),
                              out_specs=tuple(out_specs), scratch_shapes=list(scratch), input_output_aliases=aliases,
                              compiler_params=_params(len(grid)))(*args)
    r_ops = rider.operands()

    def full_body(*refs):
        ins, refs = refs[:n_in], refs[n_in:]
        r_src, refs = refs[:rider.n], refs[len(r_ops):]
        outs, refs = refs[:n_out], refs[n_out:]
        r_dst, refs = refs[:rider.n], refs[rider.n:]
        scr, sems = refs[:n_scr], refs[n_scr:]
        if grid:
            first = last = None
            for axis, size in enumerate(grid):
                at_start, at_end = pl.program_id(axis) == 0, pl.program_id(axis) == size - 1
                first = at_start if first is None else jnp.logical_and(first, at_start)
                last = at_end if last is None else jnp.logical_and(last, at_end)

            @pl.when(first)
            def _():
                rider.start(r_src, r_dst, sems)
            body(*ins, *outs, *scr)

            @pl.when(last)
            def _():
                rider.wait(r_src, r_dst, sems)
        else:
            rider.start(r_src, r_dst, sems)
            if body is not None:
                body(*ins, *outs, *scr)
            rider.wait(r_src, r_dst, sems)

    aliases.update(rider.aliases(n_in, n_out))
    kw = dict(compiler_params=_params(len(grid))) if grid else {}
    if grid:
        kw["grid"] = grid
    return pl.pallas_call(
        full_body, name=name, out_shape=tuple(out_shape) + tuple(rider.out_shape()),
        in_specs=list(in_specs) + [ANY] * len(r_ops), out_specs=tuple(out_specs) + tuple([ANY] * rider.n),
        scratch_shapes=list(scratch) + rider.scratch(), input_output_aliases=aliases, **kw)(*args, *r_ops)


def _gather_two_level(name, shards, dst_shapes, dst_views):
    n = len(shards)

    def body(*refs):
        src, dst = refs[:n], refs[n:2 * n]
        send_sems, recv_sems, local_sems = refs[2 * n:]
        x, y, c = lax.axis_index("x"), lax.axis_index("y"), lax.axis_index("c")
        flat = lambda dev: 4 * dev[0] + 2 * dev[1] + dev[2]
        me, sibling = (x, y, c), (x, y, 1 - c)
        chips = [(1 - x, y), (x, 1 - y), (1 - x, 1 - y)]

        def copy(a, k, block, to, from_shard=False):
            place = dst_views[a](dst[a], flat(block))
            return pltpu.make_async_remote_copy(src_ref=src[a] if from_shard else place, dst_ref=place,
                                                send_sem=send_sems.at[7 * a + k], recv_sem=recv_sems.at[7 * a + k],
                                                device_id=to, device_id_type=MESH)

        mine = [pltpu.make_async_copy(src[a], dst_views[a](dst[a], flat(me)), local_sems.at[a]) for a in range(n)]
        first = [copy(a, 0, me, sibling, True) for a in range(n)]
        first += [copy(a, 1 + j, me, (*chip, c), True) for j, chip in enumerate(chips) for a in range(n)]
        for cp in mine + first:
            cp.start()
        passed = []
        for j, chip in enumerate(chips):
            for a in range(n):
                copy(a, 1 + j, (*chip, c), me).wait_recv()
                passed.append(copy(a, 4 + j, (*chip, c), sibling))
                passed[-1].start()
        for a in range(n):
            copy(a, 0, sibling, me).wait_recv()
            for j, chip in enumerate(chips):
                copy(a, 4 + j, (*chip, 1 - c), me).wait_recv()
        for cp in first + passed:
            cp.wait_send()
        for cp in mine:
            cp.wait()

    return pl.pallas_call(
        body, name=name, out_shape=tuple(dst_shapes), in_specs=[ANY] * n, out_specs=tuple([ANY] * n),
        scratch_shapes=[pltpu.SemaphoreType.DMA((7 * n,)), pltpu.SemaphoreType.DMA((7 * n,)), pltpu.SemaphoreType.DMA((n,))],
    )(*shards)


def _allreduce_rows(name, part, rider):
    n_rows, n_cols = part.shape
    sl = n_rows // N_DEV

    def body(p_ref, o_ref, recv_v, sum_v, send1, recv1, send2, recv2, local_sems):
        x, y, c = lax.axis_index("x"), lax.axis_index("y"), lax.axis_index("c")
        me = 4 * x + 2 * y + c
        rows_of = lambda d: pl.ds(pl.multiple_of(d * sl, SUBLANE), sl)
        peers = []
        for ki, k in enumerate(PEER_BITS):
            px, py, pc = x ^ (k >> 2), y ^ ((k >> 1) & 1), c ^ (k & 1)
            peers.append((ki, (px, py, pc), 4 * px + 2 * py + pc))

        def scatter(ki, dev, peer, landing):
            return pltpu.make_async_remote_copy(src_ref=p_ref.at[rows_of(peer)], dst_ref=recv_v.at[landing],
                                                send_sem=send1.at[ki], recv_sem=recv1.at[ki], device_id=dev, device_id_type=MESH)

        def gather(ki, dev, landing):
            return pltpu.make_async_remote_copy(src_ref=sum_v, dst_ref=o_ref.at[rows_of(landing)],
                                                send_sem=send2.at[ki], recv_sem=recv2.at[ki], device_id=dev, device_id_type=MESH)

        own = pltpu.make_async_copy(p_ref.at[rows_of(me)], recv_v.at[me], local_sems.at[0])
        own.start()
        for ki, dev, peer in peers:
            scatter(ki, dev, peer, me).start()
        for ki, dev, peer in peers:
            scatter(ki, dev, peer, peer).wait_recv()
        own.wait()
        total = recv_v[0]
        for d in range(1, N_DEV):
            total = total + recv_v[d]
        sum_v[...] = total
        own = pltpu.make_async_copy(sum_v, o_ref.at[rows_of(me)], local_sems.at[1])
        own.start()
        for ki, dev, peer in peers:
            gather(ki, dev, me).start()
        for ki, dev, peer in peers:
            gather(ki, dev, peer).wait_recv()
        for ki, dev, peer in peers:
            scatter(ki, dev, peer, me).wait_send()
            gather(ki, dev, me).wait_send()
        own.wait()

    return _call(
        body, name=name, grid=(), out_shape=[jax.ShapeDtypeStruct((n_rows, n_cols), F32)], in_specs=[ANY], out_specs=[ANY],
        scratch=[pltpu.VMEM((N_DEV, sl, n_cols), F32), pltpu.VMEM((sl, n_cols), F32),
                 pltpu.SemaphoreType.DMA((7,)), pltpu.SemaphoreType.DMA((7,)), pltpu.SemaphoreType.DMA((7,)),
                 pltpu.SemaphoreType.DMA((7,)), pltpu.SemaphoreType.DMA((2,))],
        args=[part], rider=rider)


def _shifted_windows(buf, shift_s, n_rows):
    for o in range(1, SUBLANE):
        shift_s[o - 1, 0:n_rows - SUBLANE, :] = buf[pl.ds(o, n_rows - SUBLANE), :]

    def window(s):
        q, o = divmod(s, SUBLANE)
        src = buf if o == 0 else shift_s.at[o - 1]
        return src[pl.ds(q * SUBLANE, ROW_CHUNK), :]
    return window


def _z_parts(x_ref, w_ref, b_ref, z_ref, width):
    xb = x_ref[...].astype(BF16)

    def part(k, keep=True, half=None):
        cols = slice(k * width, (k + 1) * width)
        if half is not None:
            cols = slice(k * width + half * (width // 2), k * width + (half + 1) * (width // 2))
        zk = (_dot(xb, w_ref[:, cols]) + b_ref[:, cols]).astype(BF16)
        z_ref[:, cols] = zk
        return zk.astype(F32) if keep else None
    return part


def _even_fwd(x, w_in, b_in, conv_w, conv_b, ln_a_g, ln_a_b, ln_v_g, ln_v_b, wcat, bs_full, *, seq, tm, rider=None):
    t_len, d_model = x.shape
    w = d_model // 2
    nt, tps, nb = t_len // tm, seq // tm, tm // LANE

    def body(x_ref, w_ref, b_ref, cw_ref, cb_ref, lag_ref, lab_ref, lvg_ref, lvb_ref, wcat_ref, bs_ref,
             z_ref, y_ref, a1_ref, sg_ref, a0_s, shift_s):
        i = pl.program_id(0)

        @pl.when(i % tps == 0)
        def _():
            a0_s[0:HALO_A, :] = jnp.zeros((HALO_A, w), F32)

        part = _z_parts(x_ref, w_ref, b_ref, z_ref, w)
        a0_s[HALO_A:HALO_A + tm, :] = part(0) * _sigmoid(part(1))
        window = _shifted_windows(a0_s, shift_s, tm + HALO_A)
        n_chunks = tm // ROW_CHUNK
        pieces = [(k, h) for k in range(2, 6) for h in range(2)]
        later = dict(zip(range(n_chunks - 1, -1, -max(1, n_chunks // len(pieces))), reversed(pieces)))
        for c, r in enumerate(range(0, tm, ROW_CHUNK)):
            acc = jnp.zeros((ROW_CHUNK, w), F32) + cb_ref[...]
            for k in range(CONV_A_WIDTH):
                acc = acc + cw_ref[k:k + 1, :] * window(HALO_A - (CONV_A_WIDTH - 1) + k + r)
            a1_ref[r:r + ROW_CHUNK, :] = acc
            if c in later:
                part(later[c][0], keep=False, half=later[c][1])
        for piece in pieces:
            if piece not in later.values():
                part(piece[0], keep=False, half=piece[1])
        zp = lambda k: z_ref[:, k * w:(k + 1) * w].astype(F32)
        a0_s[0:HALO_A, :] = a0_s[tm:tm + HALO_A, :]
        ah, _ = _norm(a1_ref[...])
        a = _silu(ah * lag_ref[...] + lab_ref[...]) * _silu(zp(2))
        y_ref[:, 0:w] = a.astype(BF16)

        u = zp(3)
        vh, _ = _norm(_gelu(zp(4)))
        v2 = vh * lvg_ref[...] + lvb_ref[...]
        low = _head_low_mask(nb * LANE)
        for j in range(w // LANE):
            vt = _blocks_to_lanes(v2, j, nb)
            rhs = jnp.concatenate([jnp.where(low, vt, 0.0), jnp.where(low, 0.0, vt)], axis=0).astype(BF16)
            out = _dot(wcat_ref[j], rhs)
            for n in range(nb):
                sg_ref[n * LANE:(n + 1) * LANE, j * LANE:(j + 1) * LANE] = (
                    out[:, n * LANE:(n + 1) * LANE] + bs_ref[:, j * LANE:(j + 1) * LANE]).astype(BF16)
        g = _gelu(u) * sg_ref[...].astype(F32) * _silu(zp(5))
        y_ref[:, w:2 * w] = g.astype(BF16)

    row = lambda cols: pl.BlockSpec((tm, cols), lambda i: (i, 0))
    return _call(
        body, name="even_fwd", grid=(nt,), rider=rider,
        out_shape=(jax.ShapeDtypeStruct((t_len, 6 * w), BF16), jax.ShapeDtypeStruct((t_len, d_model), BF16),
                   jax.ShapeDtypeStruct((t_len, w), F32), jax.ShapeDtypeStruct((t_len, w), BF16)),
        in_specs=[row(d_model), _const(w_in.shape), _const(b_in.shape), _const(conv_w.shape), _const(conv_b.shape),
                  _const(ln_a_g.shape), _const(ln_a_b.shape), _const(ln_v_g.shape), _const(ln_v_b.shape),
                  _const(wcat.shape), _const(bs_full.shape)],
        out_specs=(row(6 * w), row(d_model), row(w), row(w)),
        scratch=[pltpu.VMEM((tm + HALO_A, w), F32), pltpu.VMEM((SUBLANE - 1, tm + HALO_A, w), F32)],
        args=[x, w_in, b_in, conv_w, conv_b, ln_a_g, ln_a_b, ln_v_g, ln_v_b, wcat, bs_full])


def _doubling_sums(src, cols, lv_s, win, lo, n, step):
    read = lambda off: src[pl.ds(lo + off, n), cols]
    shift, level = 1, 0
    while shift < win:
        dst = lv_s.at[level % 2]
        dst[pl.ds(lo, n), :] = read(0) + read(step * shift)
        read = lambda off, d=dst: d[pl.ds(lo + off, n), :]
        shift, level = 2 * shift, level + 1
    final = lv_s.at[(level - 1) % 2]
    return lambda start, rows: final[pl.ds(start, rows), :]


def _pool_inverse(tile_index, tm, seq, window):
    row = tile_index * tm + lax.broadcasted_iota(jnp.int32, (tm, 1), 0)
    pos = (row % seq + 1).astype(F32)
    return 1.0 / jnp.minimum(pos, float(window))


def _odd_fwd(x, w_in, b_in, w_pool, pool_scale, conv_w, *, seq, tm, rider=None):
    t_len, d_model = x.shape
    w = d_model // 2
    nt, tps = t_len // tm, seq // tm

    def body(x_ref, w_ref, b_ref, wp_ref, ps_ref, cw_ref, z_ref, y_ref, pooled_ref, q_ref, cv_s, hc_s, lv_s):
        i = pl.program_id(0)
        first = SUBLANE + HALO_C

        @pl.when(i == 0)
        def _():
            cv_s[0:SUBLANE, :] = jnp.zeros((SUBLANE, w), F32)
            lv_s[:, 0:SUBLANE, :] = jnp.zeros((2, SUBLANE, LANE), F32)

        @pl.when(i % tps == 0)
        def _():
            cv_s[SUBLANE:first, :] = jnp.zeros((HALO_C, w), F32)
            hc_s[0:HALO_D, :] = jnp.zeros((HALO_D, w), F32)

        part = _z_parts(x_ref, w_ref, b_ref, z_ref, w)
        c_val = part(0)
        c_gate = part(1)
        cv_s[first:first + tm, :] = c_val
        for gi, win in enumerate(POOL_WINDOWS):
            cols = slice(gi * LANE, (gi + 1) * LANE)
            s = _doubling_sums(cv_s, cols, lv_s, win, SUBLANE, HALO_C + tm, -1)(first, tm)
            pooled = (s * _pool_inverse(i, tm, seq, win) - c_val[:, cols]).astype(BF16)
            pooled_ref[:, cols] = pooled
            c = _dot(pooled, wp_ref[gi]) * ps_ref[:, cols] * _silu(c_gate[:, cols])
            y_ref[:, cols] = c.astype(BF16)
        cv_s[SUBLANE:first, :] = cv_s[tm + SUBLANE:tm + first, :]

        d_h = part(2)
        d_b = part(3)
        hc_s[HALO_D:HALO_D + tm, :] = part(4) * d_h
        q = jnp.zeros((tm, w), F32)
        for k in range(CONV_D_WIDTH):
            q = q + cw_ref[k:k + 1, :] * hc_s[pl.ds(HALO_D - (CONV_D_WIDTH - 1) + k, tm), :]
        hc_s[0:HALO_D, :] = hc_s[tm:tm + HALO_D, :]
        qb = q.astype(BF16)
        q_ref[...] = qb
        y_ref[:, w:2 * w] = (d_b * qb.astype(F32) * _silu(part(5))).astype(BF16)

    row = lambda cols: pl.BlockSpec((tm, cols), lambda i: (i, 0))
    return _call(
        body, name="odd_fwd", grid=(nt,), rider=rider,
        out_shape=(jax.ShapeDtypeStruct((t_len, 6 * w), BF16), jax.ShapeDtypeStruct((t_len, d_model), BF16),
                   jax.ShapeDtypeStruct((t_len, w), BF16), jax.ShapeDtypeStruct((t_len, w), BF16)),
        in_specs=[row(d_model), _const(w_in.shape), _const(b_in.shape), _const(w_pool.shape), _const(pool_scale.shape),
                  _const(conv_w.shape)],
        out_specs=(row(6 * w), row(d_model), row(w), row(w)),
        scratch=[pltpu.VMEM((tm + HALO_C + SUBLANE, w), F32), pltpu.VMEM((tm + HALO_D, w), F32),
                 pltpu.VMEM((2, tm + HALO_C + SUBLANE, LANE), F32)],
        args=[x, w_in, b_in, w_pool, pool_scale, conv_w])


def _post_fwd(y, x, p_all, layer, w_out, b_out, ln_g, ln_b, wg, bg, wp, loss_target, *, tm, rider=None):
    t_len, d_model = x.shape
    d_ple = p_all.shape[-1]
    nt = t_len // tm
    last = loss_target is not None

    def body(*refs):
        y_ref, x_ref, p_ref, wo_ref, bo_ref, g_ref, b_ref, wg_ref, bg_ref, wp_ref = refs[:10]
        rest = refs[10:]
        if last:
            lt_ref, rest = rest[0], rest[1:]
        rh_ref, rstd_ref, gate_ref, e_ref = rest[:4]
        r = DEEPNORM_ALPHA * x_ref[...] + _dot(y_ref[...], wo_ref[...]) + bo_ref[...]
        rh, rstd = _norm(r)
        h = rh * g_ref[...] + b_ref[...]
        gate = _sigmoid(_dot(h.astype(BF16), wg_ref[...]) + bg_ref[...])
        e = _dot(p_ref[...].astype(BF16), wp_ref[...])
        xn = h + gate * e
        rh_ref[...] = rh.astype(BF16)
        rstd_ref[...] = jnp.broadcast_to(rstd, (tm, LANE))
        gate_ref[...] = gate.astype(BF16)
        e_ref[...] = e.astype(BF16)
        if not last:
            rest[4][...] = xn
        else:
            dxn_ref, sse_ref = rest[4:]
            diff = xn - lt_ref[...]
            dxn_ref[...] = (diff * (1.0 / d_model)).astype(BF16)

            @pl.when(pl.program_id(0) == 0)
            def _():
                sse_ref[...] = jnp.zeros_like(sse_ref)
            sse_ref[...] += jnp.sum(_sum0(diff * diff), axis=1, keepdims=True)

    row = lambda cols: pl.BlockSpec((tm, cols), lambda i: (i, 0))
    in_specs = [row(d_model), row(d_model), pl.BlockSpec((None, tm, d_ple), lambda i: (layer, i, 0)),
                _const(w_out.shape), _const(b_out.shape), _const(ln_g.shape), _const(ln_b.shape), _const(wg.shape),
                _const(bg.shape), _const(wp.shape)]
    args = [y, x, p_all, w_out, b_out, ln_g, ln_b, wg, bg, wp]
    out_shape = [jax.ShapeDtypeStruct((t_len, d_model), BF16), jax.ShapeDtypeStruct((t_len, LANE), F32),
                 jax.ShapeDtypeStruct((t_len, d_model), BF16), jax.ShapeDtypeStruct((t_len, d_model), BF16),
                 jax.ShapeDtypeStruct((t_len, d_model), BF16 if last else F32)]
    out_specs = [row(d_model), row(LANE), row(d_model), row(d_model), row(d_model)]
    if last:
        in_specs.append(row(d_model))
        args.append(loss_target)
        out_shape.append(jax.ShapeDtypeStruct((SUBLANE, LANE), F32))
        out_specs.append(_const((SUBLANE, LANE)))
    return _call(body, name="post_fwd_last" if last else "post_fwd", grid=(nt,), out_shape=out_shape, in_specs=in_specs,
                 out_specs=out_specs, args=args, rider=rider)


def _post_bwd(dxn, gate, e, rh, rstd, ln_g, wg, w_out, *, tm, rider=None):
    t_len, d_model = dxn.shape
    nt = t_len // tm

    def body(dxn_ref, gate_ref, e_ref, rh_ref, rstd_ref, g_ref, wg_ref, wo_ref, de_ref, dgl_ref, dy_ref, dr_ref, acc_ref):
        @pl.when(pl.program_id(0) == 0)
        def _():
            acc_ref[...] = jnp.zeros_like(acc_ref)

        d = dxn_ref[...].astype(F32)
        gt = gate_ref[...].astype(F32)
        rhat = rh_ref[...].astype(F32)
        de_ref[...] = (d * gt).astype(BF16)
        dgl = d * e_ref[...].astype(F32) * gt * (1.0 - gt)
        dglb = dgl.astype(BF16)
        dgl_ref[...] = dglb
        dh = d + _dot_nt(dglb, wg_ref[...])
        dr = _norm_bwd(dh * g_ref[...], rhat, rstd_ref[:, 0:1])
        drb = dr.astype(BF16)
        dr_ref[...] = drb
        dy_ref[...] = _dot_nt(drb, wo_ref[...]).astype(BF16)
        acc_ref[0:1, :] += _sum0(dgl)
        acc_ref[1:2, :] += _sum0(dh * rhat)
        acc_ref[2:3, :] += _sum0(dh)
        acc_ref[3:4, :] += _sum0(dr)

    row = lambda cols: pl.BlockSpec((tm, cols), lambda i: (i, 0))
    return _call(
        body, name="post_bwd", grid=(nt,), rider=rider,
        out_shape=(jax.ShapeDtypeStruct((t_len, d_model), BF16), jax.ShapeDtypeStruct((t_len, d_model), BF16),
                   jax.ShapeDtypeStruct((t_len, d_model), BF16), jax.ShapeDtypeStruct((t_len, d_model), BF16),
                   jax.ShapeDtypeStruct((SUBLANE, d_model), F32)),
        in_specs=[row(d_model), row(d_model), row(d_model), row(d_model), row(LANE), _const(ln_g.shape), _const(wg.shape),
                  _const(w_out.shape)],
        out_specs=(row(d_model), row(d_model), row(d_model), row(d_model), _const((SUBLANE, d_model))),
        args=[dxn, gate, e, rh, rstd, ln_g, wg, w_out])


def _dz_store(dz_ref, dbin_ref, width):
    def store(k, v):
        cols = slice(k * width, (k + 1) * width)
        vb = v.astype(BF16)
        dz_ref[:, cols] = vb
        dbin_ref[0:1, cols] += _sum0(v)
        return vb
    return store


def _even_bwd(dy, z, a1, sg, dr, w_in, conv_w, ln_a_g, ln_a_b, ln_v_g, ln_v_b, wcat_t, *, seq, tm, dx_dtype, rider=None):
    t_len, d_model = dr.shape
    w = d_model // 2
    nt, tps, nb = t_len // tm, seq // tm, tm // LANE
    n_heads = 2 * (w // LANE)

    def body(dy_ref, z_ref, a1_ref, sg_ref, dr_ref, w_ref, cw_ref, lag_ref, lab_ref, lvg_ref, lvb_ref, wct_ref,
             dz_ref, dx_ref, dcw_ref, vec_ref, dbin_ref, dws_ref, dbs_ref, da1_s, a0_s, da0_s, cw_acc, shift_s, dx_s):
        i = pl.program_id(0)
        tile = nt - 1 - i

        @pl.when(i == 0)
        def _():
            vec_ref[...] = jnp.zeros_like(vec_ref)
            dbin_ref[...] = jnp.zeros_like(dbin_ref)
            dws_ref[...] = jnp.zeros_like(dws_ref)
            dbs_ref[...] = jnp.zeros_like(dbs_ref)
            cw_acc[...] = jnp.zeros_like(cw_acc)

        @pl.when((tile + 1) % tps == 0)
        def _():
            da1_s[tm:tm + HALO_A, :] = jnp.zeros((HALO_A, w), F32)

        zp = lambda k: z_ref[:, k * w:(k + 1) * w].astype(F32)
        store = _dz_store(dz_ref, dbin_ref, w)

        dg = dy_ref[:, w:2 * w].astype(F32)
        u, v, gg = zp(3), zp(4), zp(5)
        sgv = sg_ref[...].astype(F32)
        gelu_u, dgelu_u = _gelu_grad(u)
        silu_gg, dsilu_gg = _silu_grad(gg)
        dzb5 = store(5, dg * gelu_u * sgv * dsilu_gg)
        t1 = dg * silu_gg
        dzb3 = store(3, t1 * sgv * dgelu_u)
        dsg = t1 * gelu_u
        gelu_v, dgelu_v = _gelu_grad(v)
        vh, rstd_v = _norm(gelu_v)
        v2 = vh * lvg_ref[...] + lvb_ref[...]
        low = _head_low_mask(nb * LANE)
        for j in range(w // LANE):
            dt = _blocks_to_lanes(dsg, j, nb)
            d_lo = jnp.where(low, dt, 0.0).astype(BF16)
            d_hi = jnp.where(low, 0.0, dt).astype(BF16)
            v2t = _blocks_to_lanes(v2, j, nb).astype(BF16)
            dv2t = _dot(wct_ref[j], jnp.concatenate([d_lo, d_hi], axis=0))
            for n in range(nb):
                da0_s[n * LANE:(n + 1) * LANE, j * LANE:(j + 1) * LANE] = dv2t[:, n * LANE:(n + 1) * LANE]
            dws_ref[2 * j] += _dot_nt(d_lo, v2t)
            dws_ref[2 * j + 1] += _dot_nt(d_hi, v2t)
            bsum = dt[:, 0:LANE]
            for n in range(1, nb):
                bsum = bsum + dt[:, n * LANE:(n + 1) * LANE]
            dbs_ref[:, j * LANE:(j + 1) * LANE] += bsum
        dv2 = da0_s[...]
        vec_ref[3:4, :] += _sum0(dv2 * vh)
        vec_ref[4:5, :] += _sum0(dv2)
        dzb4 = store(4, _norm_bwd(dv2 * lvg_ref[...], vh, rstd_v) * dgelu_v)
        dx_b = DEEPNORM_ALPHA * dr_ref[...].astype(F32)
        for k, dzb in ((3, dzb3), (4, dzb4), (5, dzb5)):
            dx_b = dx_b + _dot_nt(dzb, w_ref[:, k * w:(k + 1) * w])
        dx_s[...] = dx_b

        da = dy_ref[:, 0:w].astype(F32)
        a_val, a_glu, a_gate = zp(0), zp(1), zp(2)
        s_glu = _sigmoid(a_glu)
        a0_s[...] = a_val * s_glu
        ah, rstd_a = _norm(a1_ref[...])
        silu_a2, dsilu_a2 = _silu_grad(ah * lag_ref[...] + lab_ref[...])
        silu_ag, dsilu_ag = _silu_grad(a_gate)
        dzb2 = store(2, da * silu_a2 * dsilu_ag)
        da2 = da * silu_ag * dsilu_a2
        vec_ref[1:2, :] += _sum0(da2 * ah)
        vec_ref[2:3, :] += _sum0(da2)
        da1 = _norm_bwd(da2 * lag_ref[...], ah, rstd_a)
        vec_ref[0:1, :] += _sum0(da1)
        da1_s[0:tm, :] = da1
        window = _shifted_windows(da1_s, shift_s, tm + HALO_A)
        for r in range(0, tm, ROW_CHUNK):
            a0c = a0_s[r:r + ROW_CHUNK, :]
            acc = jnp.zeros((ROW_CHUNK, w), F32)
            for k in range(CONV_A_WIDTH):
                d = window(r + (CONV_A_WIDTH - 1) - k)
                acc = acc + cw_ref[k:k + 1, :] * d
                pw = a0c * d
                p8 = pw[0:SUBLANE]
                for q in range(1, ROW_CHUNK // SUBLANE):
                    p8 = p8 + pw[q * SUBLANE:(q + 1) * SUBLANE]
                cw_acc[k * SUBLANE:(k + 1) * SUBLANE, :] += p8
            da0_s[r:r + ROW_CHUNK, :] = acc
        da1_s[tm:tm + HALO_A, :] = da1_s[0:HALO_A, :]
        da0 = da0_s[...]
        dzb0 = store(0, da0 * s_glu)
        dzb1 = store(1, da0 * a_val * s_glu * (1.0 - s_glu))

        dx = dx_s[...]
        for k, dzb in ((0, dzb0), (1, dzb1), (2, dzb2)):
            dx = dx + _dot_nt(dzb, w_ref[:, k * w:(k + 1) * w])
        dx_ref[...] = dx.astype(dx_ref.dtype)

        @pl.when(i == nt - 1)
        def _():
            for k in range(CONV_A_WIDTH):
                dcw_ref[k:k + 1, :] = _sum0(cw_acc[k * SUBLANE:(k + 1) * SUBLANE, :])
            keep = (lax.broadcasted_iota(jnp.int32, (LANE, LANE), 0) >= lax.broadcasted_iota(jnp.int32, (LANE, LANE), 1))
            for hd in range(n_heads):
                dws_ref[hd] = jnp.where(keep, dws_ref[hd], 0.0)

    rev = lambda cols: pl.BlockSpec((tm, cols), lambda i: (nt - 1 - i, 0))
    return _call(
        body, name="even_bwd", grid=(nt,), rider=rider,
        out_shape=(jax.ShapeDtypeStruct((t_len, 6 * w), BF16), jax.ShapeDtypeStruct((t_len, d_model), dx_dtype),
                   jax.ShapeDtypeStruct((CONV_A_WIDTH, w), F32), jax.ShapeDtypeStruct((SUBLANE, w), F32),
                   jax.ShapeDtypeStruct((SUBLANE, 6 * w), F32), jax.ShapeDtypeStruct((n_heads, LANE, LANE), F32),
                   jax.ShapeDtypeStruct((LANE, w), F32)),
        in_specs=[rev(d_model), rev(6 * w), rev(w), rev(w), rev(d_model), _const(w_in.shape), _const(conv_w.shape),
                  _const(ln_a_g.shape), _const(ln_a_b.shape), _const(ln_v_g.shape), _const(ln_v_b.shape), _const(wcat_t.shape)],
        out_specs=(rev(6 * w), rev(d_model), _const((CONV_A_WIDTH, w)), _const((SUBLANE, w)), _const((SUBLANE, 6 * w)),
                   _const((n_heads, LANE, LANE)), _const((LANE, w))),
        scratch=[pltpu.VMEM((tm + HALO_A, w), F32), pltpu.VMEM((tm, w), F32), pltpu.VMEM((tm, w), F32),
                 pltpu.VMEM((CONV_A_WIDTH * SUBLANE, w), F32), pltpu.VMEM((SUBLANE - 1, tm + HALO_A, w), F32),
                 pltpu.VMEM((tm, d_model), F32)],
        args=[dy, z, a1, sg, dr, w_in, conv_w, ln_a_g, ln_a_b, ln_v_g, ln_v_b, wcat_t])


def _odd_bwd(dy, z, pooled, q, dr, w_in, w_pool, pool_scale, conv_w, *, seq, tm, dx_dtype, rider=None):
    t_len, d_model = dr.shape
    w = d_model // 2
    nt, tps = t_len // tm, seq // tm
    n_groups = len(POOL_WINDOWS)

    def body(dy_ref, z_ref, pooled_ref, q_ref, dr_ref, w_ref, wp_ref, ps_ref, cw_ref,
             dz_ref, dx_ref, dwp_ref, vec_ref, dbin_ref, dm_s, dq_s, lv_s):
        i = pl.program_id(0)
        tile = nt - 1 - i

        @pl.when(i == 0)
        def _():
            dwp_ref[...] = jnp.zeros_like(dwp_ref)
            vec_ref[...] = jnp.zeros_like(vec_ref)
            dbin_ref[...] = jnp.zeros_like(dbin_ref)
            dm_s[tm + HALO_C:tm + HALO_C + SUBLANE, :] = jnp.zeros((SUBLANE, w), F32)
            lv_s[:, tm + HALO_C:tm + HALO_C + SUBLANE, :] = jnp.zeros((2, SUBLANE, LANE), F32)

        @pl.when((tile + 1) % tps == 0)
        def _():
            dm_s[tm:tm + HALO_C, :] = jnp.zeros((HALO_C, w), F32)
            dq_s[tm:tm + HALO_D, :] = jnp.zeros((HALO_D, w), F32)

        zp = lambda k: z_ref[:, k * w:(k + 1) * w].astype(F32)
        store = _dz_store(dz_ref, dbin_ref, w)

        dc = dy_ref[:, 0:w].astype(F32)
        c_gate = zp(1)
        silu_c, dsilu_c = _silu_grad(c_gate)
        dcs = dc * silu_c
        dvg_parts, dcg_parts = [], []
        for gi, win in enumerate(POOL_WINDOWS):
            cols = slice(gi * LANE, (gi + 1) * LANE)
            pooled_g = pooled_ref[:, cols]
            wp = wp_ref[gi]
            cpre = _dot(pooled_g, wp)
            scale = ps_ref[:, cols]
            vec_ref[0:1, cols] += _sum0(dcs[:, cols] * cpre)
            dcg_parts.append(dc[:, cols] * cpre * scale * dsilu_c[:, cols])
            dcp = (dcs[:, cols] * scale).astype(BF16)
            dwp_ref[gi] += _dot_tn(pooled_g, dcp)
            dpooled = _dot_nt(dcp, wp)
            dm_s[0:tm, cols] = dpooled * _pool_inverse(tile, tm, seq, win)
            s = _doubling_sums(dm_s, cols, lv_s, win, 0, tm + HALO_C, 1)(0, tm)
            dvg_parts.append(s - dpooled)
        dm_s[tm:tm + HALO_C, :] = dm_s[0:HALO_C, :]
        dzb0 = store(0, jnp.concatenate(dvg_parts, axis=1))
        dzb1 = store(1, jnp.concatenate(dcg_parts, axis=1))

        dd = dy_ref[:, w:2 * w].astype(F32)
        d_h, d_b, d_c, d_gate = zp(2), zp(3), zp(4), zp(5)
        qv = q_ref[...].astype(F32)
        silu_d, dsilu_d = _silu_grad(d_gate)
        dzb5 = store(5, dd * d_b * qv * dsilu_d)
        dzb3 = store(3, dd * qv * silu_d)
        dq_s[0:tm, :] = dd * d_b * silu_d
        hc = d_c * d_h
        dhc = jnp.zeros((tm, w), F32)
        for k in range(CONV_D_WIDTH):
            d = dq_s[pl.ds((CONV_D_WIDTH - 1) - k, tm), :]
            dhc = dhc + cw_ref[k:k + 1, :] * d
            vec_ref[1 + k:2 + k, :] += _sum0(hc * d)
        dq_s[tm:tm + HALO_D, :] = dq_s[0:HALO_D, :]
        dzb2 = store(2, dhc * d_c)
        dzb4 = store(4, dhc * d_h)

        dx = DEEPNORM_ALPHA * dr_ref[...].astype(F32)
        for k, dzb in enumerate((dzb0, dzb1, dzb2, dzb3, dzb4, dzb5)):
            dx = dx + _dot_nt(dzb, w_ref[:, k * w:(k + 1) * w])
        dx_ref[...] = dx.astype(dx_ref.dtype)

    rev = lambda cols: pl.BlockSpec((tm, cols), lambda i: (nt - 1 - i, 0))
    return _call(
        body, name="odd_bwd", grid=(nt,), rider=rider,
        out_shape=(jax.ShapeDtypeStruct((t_len, 6 * w), BF16), jax.ShapeDtypeStruct((t_len, d_model), dx_dtype),
                   jax.ShapeDtypeStruct((n_groups, LANE, LANE), F32), jax.ShapeDtypeStruct((SUBLANE, w), F32),
                   jax.ShapeDtypeStruct((SUBLANE, 6 * w), F32)),
        in_specs=[rev(d_model), rev(6 * w), rev(w), rev(w), rev(d_model), _const(w_in.shape), _const(w_pool.shape),
                  _const(pool_scale.shape), _const(conv_w.shape)],
        out_specs=(rev(6 * w), rev(d_model), _const((n_groups, LANE, LANE)), _const((SUBLANE, w)), _const((SUBLANE, 6 * w))),
        scratch=[pltpu.VMEM((tm + HALO_C + SUBLANE, w), F32), pltpu.VMEM((tm + HALO_D, w), F32),
                 pltpu.VMEM((2, tm + HALO_C + SUBLANE, LANE), F32)],
        args=[dy, z, pooled, q, dr, w_in, w_pool, pool_scale, conv_w])


def _weight_grad(name, a, b, a_layer=None, bn=None, a_cols=None, rider=None):
    if a_layer is None:
        t_len, m = a.shape
        col = 0
        if a_cols is not None:
            col, m = a_cols
        a_spec = lambda tk: pl.BlockSpec((tk, m), lambda n, k: (k, col))
    else:
        _, t_len, m = a.shape
        a_spec = lambda tk: pl.BlockSpec((None, tk, m), lambda n, k: (a_layer, k, 0))
    n_cols = b.shape[1]
    bn = n_cols if bn is None else bn
    tk = min(t_len, 1024)
    n_k = t_len // tk

    def body(a_ref, b_ref, o_ref, acc):
        k = pl.program_id(1)

        @pl.when(k == 0)
        def _():
            acc[...] = jnp.zeros_like(acc)
        acc[...] += _dot_tn(a_ref[...].astype(BF16), b_ref[...].astype(BF16))

        @pl.when(k == n_k - 1)
        def _():
            o_ref[...] = acc[...].astype(BF16)

    outs = _call(
        body, name=name, grid=(n_cols // bn, n_k), out_shape=[jax.ShapeDtypeStruct((m, n_cols), BF16)],
        in_specs=[a_spec(tk), pl.BlockSpec((tk, bn), lambda n, k: (k, n))],
        out_specs=[pl.BlockSpec((m, bn), lambda n, k: (0, n))], scratch=[pltpu.VMEM((m, bn), F32)],
        args=[a, b], rider=rider)
    return outs[0] if rider is None else outs


def _weight_grads_post(name, y, dr, rh, ln_g, ln_b, dgl, p_all, layer, de):
    t_len, d_model = y.shape
    d_ple = p_all.shape[-1]
    tk = min(t_len, 1024)
    n_k = t_len // tk

    def body(y_ref, dr_ref, rh_ref, g_ref, b_ref, dgl_ref, p_ref, de_ref, o_out, o_gate, o_ple, acc_out, acc_gate, acc_ple):
        k = pl.program_id(0)

        @pl.when(k == 0)
        def _():
            acc_out[...] = jnp.zeros_like(acc_out)
            acc_gate[...] = jnp.zeros_like(acc_gate)
            acc_ple[...] = jnp.zeros_like(acc_ple)
        acc_out[...] += _dot_tn(y_ref[...], dr_ref[...])
        h = (rh_ref[...].astype(F32) * g_ref[...] + b_ref[...]).astype(BF16)
        acc_gate[...] += _dot_tn(h, dgl_ref[...])
        acc_ple[...] += _dot_tn(p_ref[...].astype(BF16), de_ref[...])

        @pl.when(k == n_k - 1)
        def _():
            o_out[...] = acc_out[...].astype(BF16)
            o_gate[...] = acc_gate[...].astype(BF16)
            o_ple[...] = acc_ple[...].astype(BF16)

    row = lambda cols: pl.BlockSpec((tk, cols), lambda k: (k, 0))
    return pl.pallas_call(
        body, name=name, grid=(n_k,),
        out_shape=(jax.ShapeDtypeStruct((d_model, d_model), BF16), jax.ShapeDtypeStruct((d_model, d_model), BF16),
                   jax.ShapeDtypeStruct((d_ple, d_model), BF16)),
        in_specs=[row(d_model), row(d_model), row(d_model), _const(ln_g.shape), _const(ln_b.shape), row(d_model),
                  pl.BlockSpec((None, tk, d_ple), lambda k: (layer, k, 0)), row(d_model)],
        out_specs=(_const((d_model, d_model)), _const((d_model, d_model)), _const((d_ple, d_model))),
        scratch_shapes=[pltpu.VMEM((d_model, d_model), F32), pltpu.VMEM((d_model, d_model), F32),
                        pltpu.VMEM((d_ple, d_model), F32)],
        compiler_params=_params(1),
    )(y, dr, rh, ln_g, ln_b, dgl, p_all, de)


def _adamw_reduce(name, parts, w, m, v, rows_per_block):
    n_rows, n_cols = w.shape
    br = rows_per_block
    n_parts = parts.shape[0]

    def body(p_ref, w_ref, m_ref, v_ref, g_ref, d_ref, nm_ref, nv_ref):
        g = p_ref[0].astype(F32)
        for k in range(1, n_parts):
            g = g + p_ref[k].astype(F32)
        nm = ADAM_B1 * m_ref[...] + (1.0 - ADAM_B1) * g
        nv = ADAM_B2 * v_ref[...] + (1.0 - ADAM_B2) * (g * g)
        m_hat = nm / (1.0 - ADAM_B1 ** ADAM_STEP)
        v_hat = nv / (1.0 - ADAM_B2 ** ADAM_STEP)
        g_ref[...] = g
        d_ref[...] = -ADAM_LR * (m_hat / (jnp.sqrt(v_hat) + ADAM_EPS) + ADAM_WD * w_ref[...])
        nm_ref[...] = nm
        nv_ref[...] = nv

    blk = pl.BlockSpec((br, n_cols), lambda i: (i, 0))
    shp = jax.ShapeDtypeStruct((n_rows, n_cols), F32)
    return pl.pallas_call(
        body, name=name, grid=(n_rows // br,), out_shape=(shp, shp, shp, shp),
        in_specs=[pl.BlockSpec((n_parts, br, n_cols), lambda i: (0, i, 0)), blk, blk, blk], out_specs=(blk, blk, blk, blk),
        compiler_params=_params(1),
    )(parts, w, m, v)


def _pack_rows(flat_parts, pad_to=None, width=LANE):
    flat = jnp.concatenate([a.reshape(-1) for a in flat_parts])
    if pad_to is not None and pad_to > flat.shape[0]:
        flat = jnp.concatenate([flat, jnp.zeros((pad_to - flat.shape[0],), flat.dtype)])
    return flat.reshape(-1, width)


def _unpack_rows(packed, shapes):
    flat = packed.reshape(-1)
    out, off = [], 0
    for s in shapes:
        n = math.prod(s)
        out.append(flat[off:off + n].reshape(s))
        off += n
    return out


def _to_dest_major(full):
    lead, last = full.shape[:-1], full.shape[-1]
    t = full.reshape(lead + (N_DEV, last // N_DEV))
    return jnp.moveaxis(t, -2, 0).reshape(N_DEV, -1)


def _from_source_major(blocks, shard_shape):
    t = blocks.reshape((N_DEV,) + tuple(shard_shape))
    t = jnp.moveaxis(t, 0, -2)
    return t.reshape(tuple(shard_shape[:-1]) + (N_DEV * shard_shape[-1],))


def _block_rows(n_rows, n_cols, target_elems=96 * 1024):
    best = None
    for br in range(SUBLANE, n_rows + 1, SUBLANE):
        if n_rows % br == 0 and br * n_cols <= target_elems:
            best = br
    return n_rows if best is None else best


def kernel(x, p, w_in_e, b_in_e, conv_a_w, conv_a_b, ln_a_g, ln_a_b, ln_v_g, ln_v_b, w_s, b_s, w_out_e, b_out_e, w_in_o, b_in_o, w_pool, pool_scale, conv_d_w, w_out_o, b_out_o, ln_g, ln_b, w_ple, w_ple_gate, b_ple_gate, loss_target, m_w_in_e, m_b_in_e, m_conv_a_w, m_conv_a_b, m_ln_a_g, m_ln_a_b, m_ln_v_g, m_ln_v_b, m_w_s, m_b_s, m_w_out_e, m_b_out_e, m_w_in_o, m_b_in_o, m_w_pool, m_pool_scale, m_conv_d_w, m_w_out_o, m_b_out_o, m_ln_g, m_ln_b, m_w_ple, m_w_ple_gate, m_b_ple_gate, v_w_in_e, v_b_in_e, v_conv_a_w, v_conv_a_b, v_ln_a_g, v_ln_a_b, v_ln_v_g, v_ln_v_b, v_w_s, v_b_s, v_w_out_e, v_b_out_e, v_w_in_o, v_b_in_o, v_w_pool, v_pool_scale, v_conv_d_w, v_w_out_o, v_b_out_o, v_ln_g, v_ln_b, v_w_ple, v_w_ple_gate, v_b_ple_gate):
    weights = dict(w_in_e=w_in_e, b_in_e=b_in_e, conv_a_w=conv_a_w, conv_a_b=conv_a_b, ln_a_g=ln_a_g, ln_a_b=ln_a_b,
                   ln_v_g=ln_v_g, ln_v_b=ln_v_b, w_s=w_s, b_s=b_s, w_out_e=w_out_e, b_out_e=b_out_e, w_in_o=w_in_o,
                   b_in_o=b_in_o, w_pool=w_pool, pool_scale=pool_scale, conv_d_w=conv_d_w, w_out_o=w_out_o,
                   b_out_o=b_out_o, ln_g=ln_g, ln_b=ln_b, w_ple=w_ple, w_ple_gate=w_ple_gate, b_ple_gate=b_ple_gate)
    mom_m = dict(w_in_e=m_w_in_e, b_in_e=m_b_in_e, conv_a_w=m_conv_a_w, conv_a_b=m_conv_a_b, ln_a_g=m_ln_a_g,
                 ln_a_b=m_ln_a_b, ln_v_g=m_ln_v_g, ln_v_b=m_ln_v_b, w_s=m_w_s, b_s=m_b_s, w_out_e=m_w_out_e,
                 b_out_e=m_b_out_e, w_in_o=m_w_in_o, b_in_o=m_b_in_o, w_pool=m_w_pool, pool_scale=m_pool_scale,
                 conv_d_w=m_conv_d_w, w_out_o=m_w_out_o, b_out_o=m_b_out_o, ln_g=m_ln_g, ln_b=m_ln_b, w_ple=m_w_ple,
                 w_ple_gate=m_w_ple_gate, b_ple_gate=m_b_ple_gate)
    mom_v = dict(w_in_e=v_w_in_e, b_in_e=v_b_in_e, conv_a_w=v_conv_a_w, conv_a_b=v_conv_a_b, ln_a_g=v_ln_a_g,
                 ln_a_b=v_ln_a_b, ln_v_g=v_ln_v_g, ln_v_b=v_ln_v_b, w_s=v_w_s, b_s=v_b_s, w_out_e=v_w_out_e,
                 b_out_e=v_b_out_e, w_in_o=v_w_in_o, b_in_o=v_b_in_o, w_pool=v_w_pool, pool_scale=v_pool_scale,
                 conv_d_w=v_conv_d_w, w_out_o=v_w_out_o, b_out_o=v_b_out_o, ln_g=v_ln_g, ln_b=v_ln_b, w_ple=v_w_ple,
                 w_ple_gate=v_w_ple_gate, b_ple_gate=v_b_ple_gate)
    names = tuple(weights)

    batch, seq, d_model = x.shape
    t_len = batch * seq
    w = d_model // 2
    n_even = w_in_e.shape[0]
    n_odd = w_in_o.shape[0]
    depth = ln_g.shape[0]
    d_ple = p.shape[-1]
    n_heads = w_s.shape[1]
    tm = 512 if seq % 512 == 0 and seq >= 1024 else seq // 2
    in_cols = w_in_e.shape[-1]
    out_rows = w_out_e.shape[1]
    ple_cols = w_ple.shape[-1]
    gate_rows = w_ple_gate.shape[1]

    sh_shapes = [weights[n].shape for n in SH_NAMES]
    sh_len = sum(math.prod(s) for s in sh_shapes)
    sh_pad = -(-sh_len // (SUBLANE * LANE)) * (SUBLANE * LANE)
    sh_rows = sh_pad // LANE
    sds = jax.ShapeDtypeStruct
    w_in16 = (w_in_e.astype(BF16), w_in_o.astype(BF16))
    w_out16 = (w_out_e.astype(BF16), w_out_o.astype(BF16))
    w_ple16, w_gate16 = w_ple.astype(BF16), w_ple_gate.astype(BF16)

    kinds = ("in", "out", "ple", "gate")

    def weight_entries(i, which=kinds):
        j, par = i // 2, i % 2
        all_four = {"in": (w_in16[par][j], _whole_view, sds((d_model, N_DEV * in_cols), BF16), _axis_view(1, in_cols)),
                    "out": (w_out16[par][j], _whole_view, sds((N_DEV * out_rows, d_model), BF16), _axis_view(0, out_rows)),
                    "ple": (w_ple16[i], _whole_view, sds((d_ple, N_DEV * ple_cols), BF16), _axis_view(1, ple_cols)),
                    "gate": (w_gate16[i], _whole_view, sds((N_DEV * gate_rows, d_model), BF16), _axis_view(0, gate_rows))}
        return [((i, k), all_four[k]) for k in which]

    fwd_riders = {("mixer", 0): weight_entries(0, kinds[1:]) + weight_entries(1, kinds[:1]),
                  ("post", 0): weight_entries(1, kinds[1:])}
    for i in range(1, depth - 1):
        if i % 2:
            fwd_riders[("mixer", i)] = weight_entries(i + 1, kinds[:1])
            fwd_riders[("post", i)] = weight_entries(i + 1, kinds[1:])
        else:
            fwd_riders[("mixer", i)] = weight_entries(i + 1)
    layer_w = {}

    def carried(where):
        tagged = fwd_riders.get(where)
        if tagged is None:
            return None, lambda landed: None
        return _Rider([e for _, e in tagged]), lambda landed: layer_w.update(zip([t for t, _ in tagged], landed))

    w_in_first = weight_entries(0, kinds[:1])[0][1]
    first = _gather_two_level("gather_first", [w_in_first[0], _pack_rows([weights[n] for n in SH_NAMES], sh_pad)],
                              [w_in_first[2], sds((N_DEV, sh_rows, LANE), F32)], [w_in_first[3], _slot_view()])
    layer_w[(0, "in")] = first[0]
    sh_flat = first[1].reshape(N_DEV, sh_pad)
    full_small, off = {}, 0
    for n, s in zip(SH_NAMES, sh_shapes):
        size = math.prod(s)
        full_small[n] = _from_source_major(sh_flat[:, off:off + size], s)
        off += size

    tril = jnp.tril(jnp.ones((LANE, LANE), dtype=bool))
    ws_m = jnp.where(tril[None, None], w_s, 0.0)
    pair = lambda t: jnp.concatenate([t[:, 0::2], t[:, 1::2]], axis=-1).astype(BF16)
    wcat = pair(ws_m)
    wcat_t = pair(jnp.swapaxes(ws_m, -1, -2))
    bs_full = jnp.repeat(jnp.swapaxes(b_s, -1, -2), w // n_heads, axis=-1)
    row2 = lambda a, j: a[j][None, :]

    x2 = x.reshape(t_len, d_model)
    p3 = p.reshape(depth, t_len, d_ple)
    lt2 = loss_target.reshape(t_len, d_model)

    xs, saved = [x2], []
    dxn = sse = None
    for i in range(depth):
        j = i // 2
        last = i == depth - 1
        rider, file_weights = carried(("mixer", i))
        if i % 2 == 0:
            outs = _even_fwd(xs[i], layer_w[(i, "in")], row2(b_in_e, j), full_small["conv_a_w"][j], row2(conv_a_b, j),
                             row2(ln_a_g, j), row2(ln_a_b, j), row2(ln_v_g, j), row2(ln_v_b, j), wcat[j], bs_full[j],
                             seq=seq, tm=tm, rider=rider)
            b_out = row2(b_out_e, j)
        else:
            outs = _odd_fwd(xs[i], layer_w[(i, "in")], row2(full_small["b_in_o"], j), w_pool[j].astype(BF16),
                            row2(full_small["pool_scale"], j), full_small["conv_d_w"][j], seq=seq, tm=tm, rider=rider)
            b_out = row2(full_small["b_out_o"], j)
        z, y, s1, s2 = outs[:4]
        file_weights(outs[4:])
        rider, file_weights = carried(("post", i))
        outs = _post_fwd(y, xs[i], p3, i, layer_w[(i, "out")], b_out, row2(ln_g, i), row2(ln_b, i), layer_w[(i, "gate")],
                         row2(b_ple_gate, i), layer_w[(i, "ple")], lt2 if last else None, tm=tm, rider=rider)
        rh, rstd, gate, e = outs[:4]
        if last:
            dxn, sse = outs[4:6]
        else:
            xs.append(outs[4])
        file_weights(outs[6 if last else 5:])
        saved.append(dict(z=z, y=y, s1=s1, s2=s2, rh=rh, rstd=rstd, gate=gate, e=e))


    recv = {"w_in_e": None, "w_in_o": None, "w_out_e": None, "w_out_o": None, "w_ple": None, "w_ple_gate": None}

    def grad_entry(i, kind, g, half=None):
        j, par = i // 2, i % 2
        sfx = "_o" if par else "_e"
        name, src_view, slot = {"in": ("w_in" + sfx, _axis_view(1, in_cols), j), "out": ("w_out" + sfx, _axis_view(0, out_rows), j),
                                "ple": ("w_ple", _axis_view(1, ple_cols), i), "gate": ("w_ple_gate", _axis_view(0, gate_rows), i)}[kind]
        dst = recv[name] if recv[name] is not None else sds((N_DEV,) + weights[name].shape, BF16)
        if half is None:
            dst_view = _slot_view(slot)
        else:
            rows = pl.ds(half[0] * half[1], half[1])
            dst_view = lambda ref, d: ref.at[d, slot, rows, :]
            if g.shape[0] != half[1]:
                src_view = lambda ref, d: ref.at[rows, pl.ds(pl.multiple_of(d * in_cols, in_cols), in_cols)]
        return name, (g, src_view, dst, dst_view)

    def ride(tagged):
        if not tagged:
            return None, lambda landed: None
        return _Rider([e for _, e in tagged]), lambda landed: recv.update(zip([n for n, _ in tagged], landed))

    small = {n: [None] * weights[n].shape[0] for n in REP_NAMES + SH_NAMES}
    grads = {}
    half_rows = d_model // 2
    for i in reversed(range(depth)):
        j, par = i // 2, i % 2
        sv = saved[i]
        w_in = layer_w[(i, "in")]
        dx_dtype = BF16 if i > 0 else F32
        split = (i + 1, "in") if par == 1 and (i + 1, "in") in grads else None
        rider, file_landed = ride([grad_entry(*split, grads[split], half=(0, half_rows))] if split else [])
        outs = _post_bwd(dxn, sv["gate"], sv["e"], sv["rh"], sv["rstd"], row2(ln_g, i), layer_w[(i, "gate")],
                         layer_w[(i, "out")], tm=tm, rider=rider)
        de, dgl, dy, dr, acc = outs[:5]
        file_landed(outs[5:])
        small["b_ple_gate"][i], small["ln_g"][i], small["ln_b"][i] = acc[0], acc[1], acc[2]
        small["b_out_o" if par else "b_out_e"][j] = acc[3]
        grads[(i, "out")], grads[(i, "gate")], grads[(i, "ple")] = _weight_grads_post(
            f"dw_post_l{i}", sv["y"], dr, sv["rh"], row2(ln_g, i), row2(ln_b, i), dgl, p3, i, de)
        rider, file_landed = ride([grad_entry(l, k, grads.pop((l, k)), half=(1, half_rows) if (l, k) == split else None)
                                   for l, k in list(grads)])
        if par == 0:
            outs = _even_bwd(dy, sv["z"], sv["s1"], sv["s2"], dr, w_in, full_small["conv_a_w"][j],
                             row2(ln_a_g, j), row2(ln_a_b, j), row2(ln_v_g, j), row2(ln_v_b, j), wcat_t[j], seq=seq, tm=tm,
                             dx_dtype=dx_dtype, rider=rider)
            dz, dx, dcw, vec, dbin, dws, dbs = outs[:7]
            file_landed(outs[7:])
            small["conv_a_w"][j], small["conv_a_b"][j] = dcw, vec[0]
            small["ln_a_g"][j], small["ln_a_b"][j], small["ln_v_g"][j], small["ln_v_b"][j] = vec[1], vec[2], vec[3], vec[4]
            small["b_in_e"][j], small["w_s"][j] = dbin[0], dws
            small["b_s"][j] = _head_sums(dbs, n_heads)
        else:
            outs = _odd_bwd(dy, sv["z"], sv["s1"], sv["s2"], dr, w_in, w_pool[j].astype(BF16),
                            row2(full_small["pool_scale"], j), full_small["conv_d_w"][j], seq=seq, tm=tm,
                            dx_dtype=dx_dtype, rider=rider)
            dz, dx, dwp, vec, dbin = outs[:5]
            file_landed(outs[5:])
            small["w_pool"][j], small["pool_scale"][j], small["conv_d_w"][j] = dwp, vec[0], vec[1:1 + CONV_D_WIDTH]
            small["b_in_o"][j] = dbin[0]
        if i > 0:
            grads[(i, "in")] = _weight_grad(f"dw_in_l{i}", xs[i], dz, bn=in_cols * N_DEV // 2)
        dxn = dx
    grad_x = dxn.reshape(batch, seq, d_model)

    top = _weight_grad("dw_in_l0_top", xs[0], dz, a_cols=(0, half_rows))
    rider, file_landed = ride([grad_entry(0, "in", top, half=(0, half_rows))])
    outs = _weight_grad("dw_in_l0_bottom", xs[0], dz, a_cols=(1, half_rows), rider=rider)
    file_landed(outs[1:])

    small_full = {n: jnp.stack(small[n]) for n in small}
    sh_part = jnp.concatenate([_to_dest_major(small_full[n]) for n in SH_NAMES], axis=1)
    sh_part = jnp.concatenate([sh_part, jnp.zeros((N_DEV, sh_pad - sh_len), F32)], axis=1).reshape(N_DEV, sh_rows, LANE)
    name, entry = grad_entry(0, "in", outs[0], half=(1, half_rows))
    rep_width = 4 * LANE
    rep_len = sum(math.prod(weights[n].shape) for n in REP_NAMES)
    rep_block = N_DEV * SUBLANE
    rep_pad = -(-(rep_len + 1) // (rep_block * rep_width)) * (rep_block * rep_width)
    rep_sum, recv[name], sh_landed = _allreduce_rows(
        "exchange_last", _pack_rows([small_full[n] for n in REP_NAMES] + [sse[0:1, 0]], rep_pad, rep_width),
        _Rider([entry, (sh_part, _slot_view(), sds((N_DEV, sh_rows, LANE), F32), _slot_view())]))
    loss = (0.5 / d_model) * rep_sum.reshape(-1)[rep_len]

    results = {}
    for n, parts in recv.items():
        shp = weights[n].shape
        rows, cols = math.prod(shp[:-1]), shp[-1]
        two = lambda a: a.reshape(rows, cols)
        outs = _adamw_reduce("adamw_" + n, parts.reshape(N_DEV, rows, cols), two(weights[n]), two(mom_m[n]), two(mom_v[n]),
                             _block_rows(rows, cols))
        results[n] = [o.reshape(shp) for o in outs]
    pack_sh = lambda d: _pack_rows([d[n] for n in SH_NAMES], sh_pad)
    outs = _adamw_reduce("adamw_small_sharded", sh_landed, pack_sh(weights), pack_sh(mom_m), pack_sh(mom_v), sh_rows)
    for n, *vals in zip(SH_NAMES, *[_unpack_rows(o, sh_shapes) for o in outs]):
        results[n] = vals
    pack_rep = lambda d: _pack_rows([d[n] for n in REP_NAMES], rep_pad, rep_width)
    rep_shapes = [weights[n].shape for n in REP_NAMES]
    outs = _adamw_reduce("adamw_replicated", rep_sum[None], pack_rep(weights), pack_rep(mom_m), pack_rep(mom_v), rep_block)
    for n, *vals in zip(REP_NAMES, *[_unpack_rows(o, rep_shapes) for o in outs]):
        results[n] = vals

    return (loss, grad_x, *[results[n][0] for n in names], *[results[n][1] for n in names],
            *[results[n][2] for n in names], *[results[n][3] for n in names])


def _head_sums(dbs, n_heads):
    t, width = dbs.shape
    return jnp.sum(dbs.reshape(t, n_heads, width // n_heads), axis=-1).T
```

```python
import math

import jax
import jax.numpy as jnp
from jax import lax
from jax.experimental import pallas as pl
from jax.experimental.pallas import tpu as pltpu

F32 = jnp.float32
BF16 = jnp.bfloat16

N_DEV = 8
DEPTH = 4
LN_EPS = 1e-5
DEEPNORM_ALPHA = (2.0 * DEPTH) ** 0.25
POOL_WINDOWS = (2, 4, 8, 16)
CONV_A_WIDTH = 31
CONV_D_WIDTH = 3
GELU_C = math.sqrt(2.0 / math.pi)
GELU_K = 0.044715

ADAM_LR = 0.001
ADAM_B1 = 0.9
ADAM_B2 = 0.999
ADAM_EPS = 1e-08
ADAM_WD = 0.01
ADAM_STEP = 10

LANE = 128
SUBLANE = 8
HALO_A = 32
HALO_C = 16
HALO_D = 8
ROW_CHUNK = 32
VMEM_LIMIT = 56 * 2**20

ANY = pl.BlockSpec(memory_space=pl.ANY)
MESH = pl.DeviceIdType.MESH

REP_NAMES = ("b_in_e", "conv_a_b", "ln_a_g", "ln_a_b", "ln_v_g", "ln_v_b", "w_s", "b_s", "b_out_e", "w_pool", "ln_g",
             "ln_b", "b_ple_gate")
SH_NAMES = ("conv_a_w", "conv_d_w", "pool_scale", "b_in_o", "b_out_o")


def _params(n_grid_axes):
    return pltpu.CompilerParams(dimension_semantics=("arbitrary",) * n_grid_axes, vmem_limit_bytes=VMEM_LIMIT)


def _const(shape):
    nd = len(shape)
    return pl.BlockSpec(shape, lambda *_: (0,) * nd)


def _dot(a, b):
    return jnp.dot(a, b, preferred_element_type=F32)


def _dot_nt(a, b):
    return lax.dot_general(a, b, (((1,), (1,)), ((), ())), preferred_element_type=F32)


def _dot_tn(a, b):
    return lax.dot_general(a, b, (((0,), (0,)), ((), ())), preferred_element_type=F32)


def _sigmoid(x):
    return jax.nn.sigmoid(x)


def _silu(x):
    return x * _sigmoid(x)


def _silu_grad(x):
    s = _sigmoid(x)
    return x * s, s * (1.0 + x * (1.0 - s))


def _gelu(x):
    return (0.5 * x) * (1.0 + jnp.tanh(x * (GELU_C + (GELU_C * GELU_K) * (x * x))))


def _gelu_grad(x):
    x2 = x * x
    t = jnp.tanh(x * (GELU_C + (GELU_C * GELU_K) * x2))
    up = 1.0 + t
    hx = 0.5 * x
    return hx * up, 0.5 * up + (hx * (1.0 - t * t)) * (GELU_C + (3.0 * GELU_C * GELU_K) * x2)


def _norm(v):
    mu = jnp.mean(v, axis=-1, keepdims=True)
    d = v - mu
    var = jnp.mean(d * d, axis=-1, keepdims=True)
    rstd = lax.rsqrt(var + LN_EPS)
    return d * rstd, rstd


def _norm_bwd(dxh, xh, rstd):
    return rstd * (dxh - jnp.mean(dxh, axis=-1, keepdims=True) - xh * jnp.mean(dxh * xh, axis=-1, keepdims=True))


def _sum0(v):
    return jnp.sum(v, axis=0, keepdims=True)


def _head_low_mask(n_cols):
    lane = lax.broadcasted_iota(jnp.int32, (LANE, n_cols), 1)
    return (lane & (LANE - 1)) < (LANE // 2)


def _blocks_to_lanes(v, j, nb):
    return jnp.concatenate([v[n * LANE:(n + 1) * LANE, j * LANE:(j + 1) * LANE] for n in range(nb)], axis=1)


def _axis_view(axis, size):
    def view(ref, d):
        idx = [slice(None)] * len(ref.shape)
        idx[axis] = pl.ds(pl.multiple_of(d * size, size), size)
        return ref.at[tuple(idx)]
    return view


def _slot_view(*slot):
    return lambda ref, d: ref.at[(d,) + slot]


def _whole_view(ref, d):
    return ref


PEER_BITS = (1, 2, 4, 6, 3, 5, 7)


class _Rider:
    def __init__(self, entries):
        self.n = len(entries)
        self.srcs = [e[0] for e in entries]
        self.src_views = [e[1] for e in entries]
        self.dsts = [e[2] for e in entries]
        self.dst_views = [e[3] for e in entries]
        self.passed = [a for a, d in enumerate(self.dsts) if not isinstance(d, jax.ShapeDtypeStruct)]

    def operands(self):
        return self.srcs + [self.dsts[a] for a in self.passed]

    def out_shape(self):
        return [jax.ShapeDtypeStruct(d.shape, d.dtype) for d in self.dsts]

    def scratch(self):
        return [pltpu.SemaphoreType.DMA((7 * self.n,)), pltpu.SemaphoreType.DMA((7 * self.n,)), pltpu.SemaphoreType.DMA((self.n,))]

    def aliases(self, n_in_before, n_out_before):
        return {n_in_before + self.n + q: n_out_before + a for q, a in enumerate(self.passed)}

    def _copies(self, src, dst, sems):
        send_sems, recv_sems, local_sems = sems
        x, y, c = lax.axis_index("x"), lax.axis_index("y"), lax.axis_index("c")
        me = 4 * x + 2 * y + c
        local = [pltpu.make_async_copy(self.src_views[a](src[a], me), self.dst_views[a](dst[a], me), local_sems.at[a])
                 for a in range(self.n)]
        sends, recvs = [], []
        for ki, k in enumerate(PEER_BITS):
            px, py, pc = x ^ (k >> 2), y ^ ((k >> 1) & 1), c ^ (k & 1)
            peer = 4 * px + 2 * py + pc
            for a in range(self.n):
                s = a * 7 + ki
                mk = lambda landing: pltpu.make_async_remote_copy(
                    src_ref=self.src_views[a](src[a], peer), dst_ref=self.dst_views[a](dst[a], landing),
                    send_sem=send_sems.at[s], recv_sem=recv_sems.at[s], device_id=(px, py, pc), device_id_type=MESH)
                sends.append(mk(me))
                recvs.append(mk(peer))
        return local, sends, recvs

    def start(self, src, dst, sems):
        local, sends, _ = self._copies(src, dst, sems)
        for cp in local + sends:
            cp.start()

    def wait(self, src, dst, sems):
        local, sends, recvs = self._copies(src, dst, sems)
        for cp in recvs:
            cp.wait_recv()
        for cp in sends:
            cp.wait_send()
        for cp in local:
            cp.wait()


def _call(body, *, name, grid, in_specs, args, out_shape, out_specs, scratch=(), rider=None, aliases=None):
    n_in, n_out, n_scr = len(args), len(out_shape), len(scratch)
    aliases = dict(aliases or {})
    if rider is None:
        return pl.pallas_call(body, name=name, grid=grid, out_shape=tuple(out_shape), in_specs=list(in_specs),
                              out_specs=tuple(out_specs), scratch_shapes=list(scratch), input_output_aliases=aliases,
                              compiler_params=_params(len(grid)))(*args)
    r_ops = rider.operands()

    def full_body(*refs):
        ins, refs = refs[:n_in], refs[n_in:]
        r_src, refs = refs[:rider.n], refs[len(r_ops):]
        outs, refs = refs[:n_out], refs[n_out:]
        r_dst, refs = refs[:rider.n], refs[rider.n:]
        scr, sems = refs[:n_scr], refs[n_scr:]
        if grid:
            first = last = None
            for axis, size in enumerate(grid):
                at_start, at_end = pl.program_id(axis) == 0, pl.program_id(axis) == size - 1
                first = at_start if first is None else jnp.logical_and(first, at_start)
                last = at_end if last is None else jnp.logical_and(last, at_end)

            @pl.when(first)
            def _():
                rider.start(r_src, r_dst, sems)
            body(*ins, *outs, *scr)

            @pl.when(last)
            def _():
                rider.wait(r_src, r_dst, sems)
        else:
            rider.start(r_src, r_dst, sems)
            if body is not None:
                body(*ins, *outs, *scr)
            rider.wait(r_src, r_dst, sems)

    aliases.update(rider.aliases(n_in, n_out))
    kw = dict(compiler_params=_params(len(grid))) if grid else {}
    if grid:
        kw["grid"] = grid
    return pl.pallas_call(
        full_body, name=name, out_shape=tuple(out_shape) + tuple(rider.out_shape()),
        in_specs=list(in_specs) + [ANY] * len(r_ops), out_specs=tuple(out_specs) + tuple([ANY] * rider.n),
        scratch_shapes=list(scratch) + rider.scratch(), input_output_aliases=aliases, **kw)(*args, *r_ops)


def _gather_two_level(name, shards, dst_shapes, dst_views):
    n = len(shards)

    def body(*refs):
        src, dst = refs[:n], refs[n:2 * n]
        send_sems, recv_sems, local_sems = refs[2 * n:]
        x, y, c = lax.axis_index("x"), lax.axis_index("y"), lax.axis_index("c")
        flat = lambda dev: 4 * dev[0] + 2 * dev[1] + dev[2]
        me, sibling = (x, y, c), (x, y, 1 - c)
        chips = [(1 - x, y), (x, 1 - y), (1 - x, 1 - y)]

        def copy(a, k, block, to, from_shard=False):
            place = dst_views[a](dst[a], flat(block))
            return pltpu.make_async_remote_copy(src_ref=src[a] if from_shard else place, dst_ref=place,
                                                send_sem=send_sems.at[7 * a + k], recv_sem=recv_sems.at[7 * a + k],
                                                device_id=to, device_id_type=MESH)

        mine = [pltpu.make_async_copy(src[a], dst_views[a](dst[a], flat(me)), local_sems.at[a]) for a in range(n)]
        first = [copy(a, 0, me, sibling, True) for a in range(n)]
        first += [copy(a, 1 + j, me, (*chip, c), True) for j, chip in enumerate(chips) for a in range(n)]
        for cp in mine + first:
            cp.start()
        passed = []
        for j, chip in enumerate(chips):
            for a in range(n):
                copy(a, 1 + j, (*chip, c), me).wait_recv()
                passed.append(copy(a, 4 + j, (*chip, c), sibling))
                passed[-1].start()
        for a in range(n):
            copy(a, 0, sibling, me).wait_recv()
            for j, chip in enumerate(chips):
                copy(a, 4 + j, (*chip, 1 - c), me).wait_recv()
        for cp in first + passed:
            cp.wait_send()
        for cp in mine:
            cp.wait()

    return pl.pallas_call(
        body, name=name, out_shape=tuple(dst_shapes), in_specs=[ANY] * n, out_specs=tuple([ANY] * n),
        scratch_shapes=[pltpu.SemaphoreType.DMA((7 * n,)), pltpu.SemaphoreType.DMA((7 * n,)), pltpu.SemaphoreType.DMA((n,))],
    )(*shards)


def _scatter_rows_entry(part):
    n_rows, n_cols = part.shape
    sl = n_rows // N_DEV
    rows_of = lambda ref, d: ref.at[pl.ds(pl.multiple_of(d * sl, SUBLANE), sl), :]
    return (part, rows_of, jax.ShapeDtypeStruct((N_DEV, sl, n_cols), part.dtype), _slot_view())


def _sum_and_gather_rows(name, slots, rider):
    _, sl, n_cols = slots.shape

    def body(s_ref, o_ref, sum_v, send_sems, recv_sems, local_sem):
        x, y, c = lax.axis_index("x"), lax.axis_index("y"), lax.axis_index("c")
        me = 4 * x + 2 * y + c
        rows_of = lambda d: pl.ds(pl.multiple_of(d * sl, SUBLANE), sl)
        total = s_ref[0]
        for d in range(1, N_DEV):
            total = total + s_ref[d]
        sum_v[...] = total
        peers = []
        for ki, k in enumerate(PEER_BITS):
            px, py, pc = x ^ (k >> 2), y ^ ((k >> 1) & 1), c ^ (k & 1)
            peers.append((ki, (px, py, pc), 4 * px + 2 * py + pc))

        def gather(ki, dev, landing):
            return pltpu.make_async_remote_copy(src_ref=sum_v, dst_ref=o_ref.at[rows_of(landing)],
                                                send_sem=send_sems.at[ki], recv_sem=recv_sems.at[ki], device_id=dev,
                                                device_id_type=MESH)

        own = pltpu.make_async_copy(sum_v, o_ref.at[rows_of(me)], local_sem)
        own.start()
        for ki, dev, peer in peers:
            gather(ki, dev, me).start()
        for ki, dev, peer in peers:
            gather(ki, dev, peer).wait_recv()
        for ki, dev, peer in peers:
            gather(ki, dev, me).wait_send()
        own.wait()

    return _call(
        body, name=name, grid=(), out_shape=[jax.ShapeDtypeStruct((N_DEV * sl, n_cols), F32)],
        in_specs=[pl.BlockSpec(memory_space=pltpu.VMEM)], out_specs=[ANY],
        scratch=[pltpu.VMEM((sl, n_cols), F32), pltpu.SemaphoreType.DMA((7,)), pltpu.SemaphoreType.DMA((7,)),
                 pltpu.SemaphoreType.DMA],
        args=[slots], rider=rider)


def _shifted_windows(buf, shift_s, n_rows):
    for o in range(1, SUBLANE):
        shift_s[o - 1, 0:n_rows - SUBLANE, :] = buf[pl.ds(o, n_rows - SUBLANE), :]

    def window(s):
        q, o = divmod(s, SUBLANE)
        src = buf if o == 0 else shift_s.at[o - 1]
        return src[pl.ds(q * SUBLANE, ROW_CHUNK), :]
    return window


def _z_parts(x_ref, w_ref, b_ref, z_ref, width):
    xb = x_ref[...].astype(BF16)

    def part(k, keep=True, half=None):
        cols = slice(k * width, (k + 1) * width)
        if half is not None:
            cols = slice(k * width + half * (width // 2), k * width + (half + 1) * (width // 2))
        zk = (_dot(xb, w_ref[:, cols]) + b_ref[:, cols]).astype(BF16)
        z_ref[:, cols] = zk
        return zk.astype(F32) if keep else None
    return part


def _even_fwd(x, w_in, b_in, conv_w, conv_b, ln_a_g, ln_a_b, ln_v_g, ln_v_b, wcat, bs_full, *, seq, tm, rider=None):
    t_len, d_model = x.shape
    w = d_model // 2
    nt, tps, nb = t_len // tm, seq // tm, tm // LANE

    def body(x_ref, w_ref, b_ref, cw_ref, cb_ref, lag_ref, lab_ref, lvg_ref, lvb_ref, wcat_ref, bs_ref,
             z_ref, y_ref, a1_ref, sg_ref, a0_s, shift_s):
        i = pl.program_id(0)

        @pl.when(i % tps == 0)
        def _():
            a0_s[0:HALO_A, :] = jnp.zeros((HALO_A, w), F32)

        part = _z_parts(x_ref, w_ref, b_ref, z_ref, w)
        a0_s[HALO_A:HALO_A + tm, :] = part(0) * _sigmoid(part(1))
        window = _shifted_windows(a0_s, shift_s, tm + HALO_A)
        n_chunks = tm // ROW_CHUNK
        pieces = [(k, h) for k in range(2, 6) for h in range(2)]
        later = dict(zip(range(n_chunks - 1, -1, -max(1, n_chunks // len(pieces))), reversed(pieces)))
        for c, r in enumerate(range(0, tm, ROW_CHUNK)):
            acc = jnp.zeros((ROW_CHUNK, w), F32) + cb_ref[...]
            for k in range(CONV_A_WIDTH):
                acc = acc + cw_ref[k:k + 1, :] * window(HALO_A - (CONV_A_WIDTH - 1) + k + r)
            a1_ref[r:r + ROW_CHUNK, :] = acc
            if c in later:
                part(later[c][0], keep=False, half=later[c][1])
        for piece in pieces:
            if piece not in later.values():
                part(piece[0], keep=False, half=piece[1])
        zp = lambda k: z_ref[:, k * w:(k + 1) * w].astype(F32)
        a0_s[0:HALO_A, :] = a0_s[tm:tm + HALO_A, :]
        ah, _ = _norm(a1_ref[...])
        a = _silu(ah * lag_ref[...] + lab_ref[...]) * _silu(zp(2))
        y_ref[:, 0:w] = a.astype(BF16)

        u = zp(3)
        vh, _ = _norm(_gelu(zp(4)))
        v2 = vh * lvg_ref[...] + lvb_ref[...]
        low = _head_low_mask(nb * LANE)
        for j in range(w // LANE):
            vt = _blocks_to_lanes(v2, j, nb)
            rhs = jnp.concatenate([jnp.where(low, vt, 0.0), jnp.where(low, 0.0, vt)], axis=0).astype(BF16)
            out = _dot(wcat_ref[j], rhs)
            for n in range(nb):
                sg_ref[n * LANE:(n + 1) * LANE, j * LANE:(j + 1) * LANE] = (
                    out[:, n * LANE:(n + 1) * LANE] + bs_ref[:, j * LANE:(j + 1) * LANE]).astype(BF16)
        g = _gelu(u) * sg_ref[...].astype(F32) * _silu(zp(5))
        y_ref[:, w:2 * w] = g.astype(BF16)

    row = lambda cols: pl.BlockSpec((tm, cols), lambda i: (i, 0))
    return _call(
        body, name="even_fwd", grid=(nt,), rider=rider,
        out_shape=(jax.ShapeDtypeStruct((t_len, 6 * w), BF16), jax.ShapeDtypeStruct((t_len, d_model), BF16),
                   jax.ShapeDtypeStruct((t_len, w), F32), jax.ShapeDtypeStruct((t_len, w), BF16)),
        in_specs=[row(d_model), _const(w_in.shape), _const(b_in.shape), _const(conv_w.shape), _const(conv_b.shape),
                  _const(ln_a_g.shape), _const(ln_a_b.shape), _const(ln_v_g.shape), _const(ln_v_b.shape),
                  _const(wcat.shape), _const(bs_full.shape)],
        out_specs=(row(6 * w), row(d_model), row(w), row(w)),
        scratch=[pltpu.VMEM((tm + HALO_A, w), F32), pltpu.VMEM((SUBLANE - 1, tm + HALO_A, w), F32)],
        args=[x, w_in, b_in, conv_w, conv_b, ln_a_g, ln_a_b, ln_v_g, ln_v_b, wcat, bs_full])


def _doubling_sums(src, cols, lv_s, win, lo, n, step):
    read = lambda off: src[pl.ds(lo + off, n), cols]
    shift, level = 1, 0
    while shift < win:
        dst = lv_s.at[level % 2]
        dst[pl.ds(lo, n), :] = read(0) + read(step * shift)
        read = lambda off, d=dst: d[pl.ds(lo + off, n), :]
        shift, level = 2 * shift, level + 1
    final = lv_s.at[(level - 1) % 2]
    return lambda start, rows: final[pl.ds(start, rows), :]


def _pool_inverse(tile_index, tm, seq, window):
    row = tile_index * tm + lax.broadcasted_iota(jnp.int32, (tm, 1), 0)
    pos = (row % seq + 1).astype(F32)
    return 1.0 / jnp.minimum(pos, float(window))


def _odd_fwd(x, w_in, b_in, w_pool, pool_scale, conv_w, *, seq, tm, rider=None):
    t_len, d_model = x.shape
    w = d_model // 2
    nt, tps = t_len // tm, seq // tm

    def body(x_ref, w_ref, b_ref, wp_ref, ps_ref, cw_ref, z_ref, y_ref, pooled_ref, q_ref, cv_s, hc_s, lv_s):
        i = pl.program_id(0)
        first = SUBLANE + HALO_C

        @pl.when(i == 0)
        def _():
            cv_s[0:SUBLANE, :] = jnp.zeros((SUBLANE, w), F32)
            lv_s[:, 0:SUBLANE, :] = jnp.zeros((2, SUBLANE, LANE), F32)

        @pl.when(i % tps == 0)
        def _():
            cv_s[SUBLANE:first, :] = jnp.zeros((HALO_C, w), F32)
            hc_s[0:HALO_D, :] = jnp.zeros((HALO_D, w), F32)

        part = _z_parts(x_ref, w_ref, b_ref, z_ref, w)
        c_val = part(0)
        c_gate = part(1)
        cv_s[first:first + tm, :] = c_val
        for gi, win in enumerate(POOL_WINDOWS):
            cols = slice(gi * LANE, (gi + 1) * LANE)
            s = _doubling_sums(cv_s, cols, lv_s, win, SUBLANE, HALO_C + tm, -1)(first, tm)
            pooled = (s * _pool_inverse(i, tm, seq, win) - c_val[:, cols]).astype(BF16)
            pooled_ref[:, cols] = pooled
            c = _dot(pooled, wp_ref[gi]) * ps_ref[:, cols] * _silu(c_gate[:, cols])
            y_ref[:, cols] = c.astype(BF16)
        cv_s[SUBLANE:first, :] = cv_s[tm + SUBLANE:tm + first, :]

        d_h = part(2)
        d_b = part(3)
        hc_s[HALO_D:HALO_D + tm, :] = part(4) * d_h
        q = jnp.zeros((tm, w), F32)
        for k in range(CONV_D_WIDTH):
            q = q + cw_ref[k:k + 1, :] * hc_s[pl.ds(HALO_D - (CONV_D_WIDTH - 1) + k, tm), :]
        hc_s[0:HALO_D, :] = hc_s[tm:tm + HALO_D, :]
        qb = q.astype(BF16)
        q_ref[...] = qb
        y_ref[:, w:2 * w] = (d_b * qb.astype(F32) * _silu(part(5))).astype(BF16)

    row = lambda cols: pl.BlockSpec((tm, cols), lambda i: (i, 0))
    return _call(
        body, name="odd_fwd", grid=(nt,), rider=rider,
        out_shape=(jax.ShapeDtypeStruct((t_len, 6 * w), BF16), jax.ShapeDtypeStruct((t_len, d_model), BF16),
                   jax.ShapeDtypeStruct((t_len, w), BF16), jax.ShapeDtypeStruct((t_len, w), BF16)),
        in_specs=[row(d_model), _const(w_in.shape), _const(b_in.shape), _const(w_pool.shape), _const(pool_scale.shape),
                  _const(conv_w.shape)],
        out_specs=(row(6 * w), row(d_model), row(w), row(w)),
        scratch=[pltpu.VMEM((tm + HALO_C + SUBLANE, w), F32), pltpu.VMEM((tm + HALO_D, w), F32),
                 pltpu.VMEM((2, tm + HALO_C + SUBLANE, LANE), F32)],
        args=[x, w_in, b_in, w_pool, pool_scale, conv_w])


def _post_fwd(y, x, p_all, layer, w_out, b_out, ln_g, ln_b, wg, bg, wp, loss_target, *, tm, rider=None):
    t_len, d_model = x.shape
    d_ple = p_all.shape[-1]
    nt = t_len // tm
    last = loss_target is not None

    def body(*refs):
        y_ref, x_ref, p_ref, wo_ref, bo_ref, g_ref, b_ref, wg_ref, bg_ref, wp_ref = refs[:10]
        rest = refs[10:]
        if last:
            lt_ref, rest = rest[0], rest[1:]
        rh_ref, rstd_ref, gate_ref, e_ref = rest[:4]
        r = DEEPNORM_ALPHA * x_ref[...] + _dot(y_ref[...], wo_ref[...]) + bo_ref[...]
        rh, rstd = _norm(r)
        h = rh * g_ref[...] + b_ref[...]
        gate = _sigmoid(_dot(h.astype(BF16), wg_ref[...]) + bg_ref[...])
        e = _dot(p_ref[...].astype(BF16), wp_ref[...])
        xn = h + gate * e
        rh_ref[...] = rh.astype(BF16)
        rstd_ref[...] = jnp.broadcast_to(rstd, (tm, LANE))
        gate_ref[...] = gate.astype(BF16)
        e_ref[...] = e.astype(BF16)
        if not last:
            rest[4][...] = xn
        else:
            dxn_ref, sse_ref = rest[4:]
            diff = xn - lt_ref[...]
            dxn_ref[...] = (diff * (1.0 / d_model)).astype(BF16)

            @pl.when(pl.program_id(0) == 0)
            def _():
                sse_ref[...] = jnp.zeros_like(sse_ref)
            sse_ref[...] += jnp.sum(_sum0(diff * diff), axis=1, keepdims=True)

    row = lambda cols: pl.BlockSpec((tm, cols), lambda i: (i, 0))
    in_specs = [row(d_model), row(d_model), pl.BlockSpec((None, tm, d_ple), lambda i: (layer, i, 0)),
                _const(w_out.shape), _const(b_out.shape), _const(ln_g.shape), _const(ln_b.shape), _const(wg.shape),
                _const(bg.shape), _const(wp.shape)]
    args = [y, x, p_all, w_out, b_out, ln_g, ln_b, wg, bg, wp]
    out_shape = [jax.ShapeDtypeStruct((t_len, d_model), BF16), jax.ShapeDtypeStruct((t_len, LANE), F32),
                 jax.ShapeDtypeStruct((t_len, d_model), BF16), jax.ShapeDtypeStruct((t_len, d_model), BF16),
                 jax.ShapeDtypeStruct((t_len, d_model), BF16 if last else F32)]
    out_specs = [row(d_model), row(LANE), row(d_model), row(d_model), row(d_model)]
    if last:
        in_specs.append(row(d_model))
        args.append(loss_target)
        out_shape.append(jax.ShapeDtypeStruct((SUBLANE, LANE), F32))
        out_specs.append(_const((SUBLANE, LANE)))
    return _call(body, name="post_fwd_last" if last else "post_fwd", grid=(nt,), out_shape=out_shape, in_specs=in_specs,
                 out_specs=out_specs, args=args, rider=rider)


def _post_bwd(dxn, gate, e, rh, rstd, ln_g, wg, w_out, *, tm, rider=None):
    t_len, d_model = dxn.shape
    nt = t_len // tm

    def body(dxn_ref, gate_ref, e_ref, rh_ref, rstd_ref, g_ref, wg_ref, wo_ref, de_ref, dgl_ref, dy_ref, dr_ref, acc_ref):
        @pl.when(pl.program_id(0) == 0)
        def _():
            acc_ref[...] = jnp.zeros_like(acc_ref)

        d = dxn_ref[...].astype(F32)
        gt = gate_ref[...].astype(F32)
        rhat = rh_ref[...].astype(F32)
        de_ref[...] = (d * gt).astype(BF16)
        dgl = d * e_ref[...].astype(F32) * gt * (1.0 - gt)
        dglb = dgl.astype(BF16)
        dgl_ref[...] = dglb
        dh = d + _dot_nt(dglb, wg_ref[...])
        dr = _norm_bwd(dh * g_ref[...], rhat, rstd_ref[:, 0:1])
        drb = dr.astype(BF16)
        dr_ref[...] = drb
        dy_ref[...] = _dot_nt(drb, wo_ref[...]).astype(BF16)
        acc_ref[0:1, :] += _sum0(dgl)
        acc_ref[1:2, :] += _sum0(dh * rhat)
        acc_ref[2:3, :] += _sum0(dh)
        acc_ref[3:4, :] += _sum0(dr)

    row = lambda cols: pl.BlockSpec((tm, cols), lambda i: (i, 0))
    return _call(
        body, name="post_bwd", grid=(nt,), rider=rider,
        out_shape=(jax.ShapeDtypeStruct((t_len, d_model), BF16), jax.ShapeDtypeStruct((t_len, d_model), BF16),
                   jax.ShapeDtypeStruct((t_len, d_model), BF16), jax.ShapeDtypeStruct((t_len, d_model), BF16),
                   jax.ShapeDtypeStruct((SUBLANE, d_model), F32)),
        in_specs=[row(d_model), row(d_model), row(d_model), row(d_model), row(LANE), _const(ln_g.shape), _const(wg.shape),
                  _const(w_out.shape)],
        out_specs=(row(d_model), row(d_model), row(d_model), row(d_model), _const((SUBLANE, d_model))),
        args=[dxn, gate, e, rh, rstd, ln_g, wg, w_out])


def _dz_store(dz_ref, dbin_ref, width):
    def store(k, v):
        cols = slice(k * width, (k + 1) * width)
        vb = v.astype(BF16)
        dz_ref[:, cols] = vb
        dbin_ref[0:1, cols] += _sum0(v)
        return vb
    return store


def _even_bwd(dy, z, a1, sg, dr, w_in, conv_w, ln_a_g, ln_a_b, ln_v_g, ln_v_b, wcat_t, *, seq, tm, dx_dtype, rider=None):
    t_len, d_model = dr.shape
    w = d_model // 2
    nt, tps, nb = t_len // tm, seq // tm, tm // LANE
    n_heads = 2 * (w // LANE)

    def body(dy_ref, z_ref, a1_ref, sg_ref, dr_ref, w_ref, cw_ref, lag_ref, lab_ref, lvg_ref, lvb_ref, wct_ref,
             dz_ref, dx_ref, dcw_ref, vec_ref, dbin_ref, dws_ref, dbs_ref, da1_s, a0_s, da0_s, cw_acc, shift_s, dx_s):
        i = pl.program_id(0)
        tile = nt - 1 - i

        @pl.when(i == 0)
        def _():
            vec_ref[...] = jnp.zeros_like(vec_ref)
            dbin_ref[...] = jnp.zeros_like(dbin_ref)
            dws_ref[...] = jnp.zeros_like(dws_ref)
            dbs_ref[...] = jnp.zeros_like(dbs_ref)
            cw_acc[...] = jnp.zeros_like(cw_acc)

        @pl.when((tile + 1) % tps == 0)
        def _():
            da1_s[tm:tm + HALO_A, :] = jnp.zeros((HALO_A, w), F32)

        zp = lambda k: z_ref[:, k * w:(k + 1) * w].astype(F32)
        store = _dz_store(dz_ref, dbin_ref, w)

        dg = dy_ref[:, w:2 * w].astype(F32)
        u, v, gg = zp(3), zp(4), zp(5)
        sgv = sg_ref[...].astype(F32)
        gelu_u, dgelu_u = _gelu_grad(u)
        silu_gg, dsilu_gg = _silu_grad(gg)
        dzb5 = store(5, dg * gelu_u * sgv * dsilu_gg)
        t1 = dg * silu_gg
        dzb3 = store(3, t1 * sgv * dgelu_u)
        dsg = t1 * gelu_u
        gelu_v, dgelu_v = _gelu_grad(v)
        vh, rstd_v = _norm(gelu_v)
        v2 = vh * lvg_ref[...] + lvb_ref[...]
        low = _head_low_mask(nb * LANE)
        for j in range(w // LANE):
            dt = _blocks_to_lanes(dsg, j, nb)
            d_lo = jnp.where(low, dt, 0.0).astype(BF16)
            d_hi = jnp.where(low, 0.0, dt).astype(BF16)
            v2t = _blocks_to_lanes(v2, j, nb).astype(BF16)
            dv2t = _dot(wct_ref[j], jnp.concatenate([d_lo, d_hi], axis=0))
            for n in range(nb):
                da0_s[n * LANE:(n + 1) * LANE, j * LANE:(j + 1) * LANE] = dv2t[:, n * LANE:(n + 1) * LANE]
            dws_ref[2 * j] += _dot_nt(d_lo, v2t)
            dws_ref[2 * j + 1] += _dot_nt(d_hi, v2t)
            bsum = dt[:, 0:LANE]
            for n in range(1, nb):
                bsum = bsum + dt[:, n * LANE:(n + 1) * LANE]
            dbs_ref[:, j * LANE:(j + 1) * LANE] += bsum
        dv2 = da0_s[...]
        vec_ref[3:4, :] += _sum0(dv2 * vh)
        vec_ref[4:5, :] += _sum0(dv2)
        dzb4 = store(4, _norm_bwd(dv2 * lvg_ref[...], vh, rstd_v) * dgelu_v)
        dx_b = DEEPNORM_ALPHA * dr_ref[...].astype(F32)
        for k, dzb in ((3, dzb3), (4, dzb4), (5, dzb5)):
            dx_b = dx_b + _dot_nt(dzb, w_ref[:, k * w:(k + 1) * w])
        dx_s[...] = dx_b

        da = dy_ref[:, 0:w].astype(F32)
        a_val, a_glu, a_gate = zp(0), zp(1), zp(2)
        s_glu = _sigmoid(a_glu)
        a0_s[...] = a_val * s_glu
        ah, rstd_a = _norm(a1_ref[...])
        silu_a2, dsilu_a2 = _silu_grad(ah * lag_ref[...] + lab_ref[...])
        silu_ag, dsilu_ag = _silu_grad(a_gate)
        dzb2 = store(2, da * silu_a2 * dsilu_ag)
        da2 = da * silu_ag * dsilu_a2
        vec_ref[1:2, :] += _sum0(da2 * ah)
        vec_ref[2:3, :] += _sum0(da2)
        da1 = _norm_bwd(da2 * lag_ref[...], ah, rstd_a)
        vec_ref[0:1, :] += _sum0(da1)
        da1_s[0:tm, :] = da1
        window = _shifted_windows(da1_s, shift_s, tm + HALO_A)
        for r in range(0, tm, ROW_CHUNK):
            a0c = a0_s[r:r + ROW_CHUNK, :]
            acc = jnp.zeros((ROW_CHUNK, w), F32)
            for k in range(CONV_A_WIDTH):
                d = window(r + (CONV_A_WIDTH - 1) - k)
                acc = acc + cw_ref[k:k + 1, :] * d
                pw = a0c * d
                p8 = pw[0:SUBLANE]
                for q in range(1, ROW_CHUNK // SUBLANE):
                    p8 = p8 + pw[q * SUBLANE:(q + 1) * SUBLANE]
                cw_acc[k * SUBLANE:(k + 1) * SUBLANE, :] += p8
            da0_s[r:r + ROW_CHUNK, :] = acc
        da1_s[tm:tm + HALO_A, :] = da1_s[0:HALO_A, :]
        da0 = da0_s[...]
        dzb0 = store(0, da0 * s_glu)
        dzb1 = store(1, da0 * a_val * s_glu * (1.0 - s_glu))

        dx = dx_s[...]
        for k, dzb in ((0, dzb0), (1, dzb1), (2, dzb2)):
            dx = dx + _dot_nt(dzb, w_ref[:, k * w:(k + 1) * w])
        dx_ref[...] = dx.astype(dx_ref.dtype)

        @pl.when(i == nt - 1)
        def _():
            for k in range(CONV_A_WIDTH):
                dcw_ref[k:k + 1, :] = _sum0(cw_acc[k * SUBLANE:(k + 1) * SUBLANE, :])
            keep = (lax.broadcasted_iota(jnp.int32, (LANE, LANE), 0) >= lax.broadcasted_iota(jnp.int32, (LANE, LANE), 1))
            for hd in range(n_heads):
                dws_ref[hd] = jnp.where(keep, dws_ref[hd], 0.0)

    rev = lambda cols: pl.BlockSpec((tm, cols), lambda i: (nt - 1 - i, 0))
    return _call(
        body, name="even_bwd", grid=(nt,), rider=rider,
        out_shape=(jax.ShapeDtypeStruct((t_len, 6 * w), BF16), jax.ShapeDtypeStruct((t_len, d_model), dx_dtype),
                   jax.ShapeDtypeStruct((CONV_A_WIDTH, w), F32), jax.ShapeDtypeStruct((SUBLANE, w), F32),
                   jax.ShapeDtypeStruct((SUBLANE, 6 * w), F32), jax.ShapeDtypeStruct((n_heads, LANE, LANE), F32),
                   jax.ShapeDtypeStruct((LANE, w), F32)),
        in_specs=[rev(d_model), rev(6 * w), rev(w), rev(w), rev(d_model), _const(w_in.shape), _const(conv_w.shape),
                  _const(ln_a_g.shape), _const(ln_a_b.shape), _const(ln_v_g.shape), _const(ln_v_b.shape), _const(wcat_t.shape)],
        out_specs=(rev(6 * w), rev(d_model), _const((CONV_A_WIDTH, w)), _const((SUBLANE, w)), _const((SUBLANE, 6 * w)),
                   _const((n_heads, LANE, LANE)), _const((LANE, w))),
        scratch=[pltpu.VMEM((tm + HALO_A, w), F32), pltpu.VMEM((tm, w), F32), pltpu.VMEM((tm, w), F32),
                 pltpu.VMEM((CONV_A_WIDTH * SUBLANE, w), F32), pltpu.VMEM((SUBLANE - 1, tm + HALO_A, w), F32),
                 pltpu.VMEM((tm, d_model), F32)],
        args=[dy, z, a1, sg, dr, w_in, conv_w, ln_a_g, ln_a_b, ln_v_g, ln_v_b, wcat_t])


def _odd_bwd(dy, z, pooled, q, dr, w_in, w_pool, pool_scale, conv_w, *, seq, tm, dx_dtype, rider=None):
    t_len, d_model = dr.shape
    w = d_model // 2
    nt, tps = t_len // tm, seq // tm
    n_groups = len(POOL_WINDOWS)

    def body(dy_ref, z_ref, pooled_ref, q_ref, dr_ref, w_ref, wp_ref, ps_ref, cw_ref,
             dz_ref, dx_ref, dwp_ref, vec_ref, dbin_ref, dm_s, dq_s, lv_s):
        i = pl.program_id(0)
        tile = nt - 1 - i

        @pl.when(i == 0)
        def _():
            dwp_ref[...] = jnp.zeros_like(dwp_ref)
            vec_ref[...] = jnp.zeros_like(vec_ref)
            dbin_ref[...] = jnp.zeros_like(dbin_ref)
            dm_s[tm + HALO_C:tm + HALO_C + SUBLANE, :] = jnp.zeros((SUBLANE, w), F32)
            lv_s[:, tm + HALO_C:tm + HALO_C + SUBLANE, :] = jnp.zeros((2, SUBLANE, LANE), F32)

        @pl.when((tile + 1) % tps == 0)
        def _():
            dm_s[tm:tm + HALO_C, :] = jnp.zeros((HALO_C, w), F32)
            dq_s[tm:tm + HALO_D, :] = jnp.zeros((HALO_D, w), F32)

        zp = lambda k: z_ref[:, k * w:(k + 1) * w].astype(F32)
        store = _dz_store(dz_ref, dbin_ref, w)

        dc = dy_ref[:, 0:w].astype(F32)
        c_gate = zp(1)
        silu_c, dsilu_c = _silu_grad(c_gate)
        dcs = dc * silu_c
        dvg_parts, dcg_parts = [], []
        for gi, win in enumerate(POOL_WINDOWS):
            cols = slice(gi * LANE, (gi + 1) * LANE)
            pooled_g = pooled_ref[:, cols]
            wp = wp_ref[gi]
            cpre = _dot(pooled_g, wp)
            scale = ps_ref[:, cols]
            vec_ref[0:1, cols] += _sum0(dcs[:, cols] * cpre)
            dcg_parts.append(dc[:, cols] * cpre * scale * dsilu_c[:, cols])
            dcp = (dcs[:, cols] * scale).astype(BF16)
            dwp_ref[gi] += _dot_tn(pooled_g, dcp)
            dpooled = _dot_nt(dcp, wp)
            dm_s[0:tm, cols] = dpooled * _pool_inverse(tile, tm, seq, win)
            s = _doubling_sums(dm_s, cols, lv_s, win, 0, tm + HALO_C, 1)(0, tm)
            dvg_parts.append(s - dpooled)
        dm_s[tm:tm + HALO_C, :] = dm_s[0:HALO_C, :]
        dzb0 = store(0, jnp.concatenate(dvg_parts, axis=1))
        dzb1 = store(1, jnp.concatenate(dcg_parts, axis=1))

        dd = dy_ref[:, w:2 * w].astype(F32)
        d_h, d_b, d_c, d_gate = zp(2), zp(3), zp(4), zp(5)
        qv = q_ref[...].astype(F32)
        silu_d, dsilu_d = _silu_grad(d_gate)
        dzb5 = store(5, dd * d_b * qv * dsilu_d)
        dzb3 = store(3, dd * qv * silu_d)
        dq_s[0:tm, :] = dd * d_b * silu_d
        hc = d_c * d_h
        dhc = jnp.zeros((tm, w), F32)
        for k in range(CONV_D_WIDTH):
            d = dq_s[pl.ds((CONV_D_WIDTH - 1) - k, tm), :]
            dhc = dhc + cw_ref[k:k + 1, :] * d
            vec_ref[1 + k:2 + k, :] += _sum0(hc * d)
        dq_s[tm:tm + HALO_D, :] = dq_s[0:HALO_D, :]
        dzb2 = store(2, dhc * d_c)
        dzb4 = store(4, dhc * d_h)

        dx = DEEPNORM_ALPHA * dr_ref[...].astype(F32)
        for k, dzb in enumerate((dzb0, dzb1, dzb2, dzb3, dzb4, dzb5)):
            dx = dx + _dot_nt(dzb, w_ref[:, k * w:(k + 1) * w])
        dx_ref[...] = dx.astype(dx_ref.dtype)

    rev = lambda cols: pl.BlockSpec((tm, cols), lambda i: (nt - 1 - i, 0))
    return _call(
        body, name="odd_bwd", grid=(nt,), rider=rider,
        out_shape=(jax.ShapeDtypeStruct((t_len, 6 * w), BF16), jax.ShapeDtypeStruct((t_len, d_model), dx_dtype),
                   jax.ShapeDtypeStruct((n_groups, LANE, LANE), F32), jax.ShapeDtypeStruct((SUBLANE, w), F32),
                   jax.ShapeDtypeStruct((SUBLANE, 6 * w), F32)),
        in_specs=[rev(d_model), rev(6 * w), rev(w), rev(w), rev(d_model), _const(w_in.shape), _const(w_pool.shape),
                  _const(pool_scale.shape), _const(conv_w.shape)],
        out_specs=(rev(6 * w), rev(d_model), _const((n_groups, LANE, LANE)), _const((SUBLANE, w)), _const((SUBLANE, 6 * w))),
        scratch=[pltpu.VMEM((tm + HALO_C + SUBLANE, w), F32), pltpu.VMEM((tm + HALO_D, w), F32),
                 pltpu.VMEM((2, tm + HALO_C + SUBLANE, LANE), F32)],
        args=[dy, z, pooled, q, dr, w_in, w_pool, pool_scale, conv_w])


def _weight_grad(name, a, b, a_layer=None, bn=None, a_cols=None, rider=None):
    if a_layer is None:
        t_len, m = a.shape
        col = 0
        if a_cols is not None:
            col, m = a_cols
        a_spec = lambda tk: pl.BlockSpec((tk, m), lambda n, k: (k, col))
    else:
        _, t_len, m = a.shape
        a_spec = lambda tk: pl.BlockSpec((None, tk, m), lambda n, k: (a_layer, k, 0))
    n_cols = b.shape[1]
    bn = n_cols if bn is None else bn
    tk = min(t_len, 1024)
    n_k = t_len // tk

    def body(a_ref, b_ref, o_ref, acc):
        k = pl.program_id(1)

        @pl.when(k == 0)
        def _():
            acc[...] = jnp.zeros_like(acc)
        acc[...] += _dot_tn(a_ref[...].astype(BF16), b_ref[...].astype(BF16))

        @pl.when(k == n_k - 1)
        def _():
            o_ref[...] = acc[...].astype(BF16)

    outs = _call(
        body, name=name, grid=(n_cols // bn, n_k), out_shape=[jax.ShapeDtypeStruct((m, n_cols), BF16)],
        in_specs=[a_spec(tk), pl.BlockSpec((tk, bn), lambda n, k: (k, n))],
        out_specs=[pl.BlockSpec((m, bn), lambda n, k: (0, n))], scratch=[pltpu.VMEM((m, bn), F32)],
        args=[a, b], rider=rider)
    return outs[0] if rider is None else outs


def _weight_grads_post(name, y, dr, rh, ln_g, ln_b, dgl, p_all, layer, de):
    t_len, d_model = y.shape
    d_ple = p_all.shape[-1]
    tk = min(t_len, 1024)
    n_k = t_len // tk

    def body(y_ref, dr_ref, rh_ref, g_ref, b_ref, dgl_ref, p_ref, de_ref, o_out, o_gate, o_ple, acc_out, acc_gate, acc_ple):
        k = pl.program_id(0)

        @pl.when(k == 0)
        def _():
            acc_out[...] = jnp.zeros_like(acc_out)
            acc_gate[...] = jnp.zeros_like(acc_gate)
            acc_ple[...] = jnp.zeros_like(acc_ple)
        acc_out[...] += _dot_tn(y_ref[...], dr_ref[...])
        h = (rh_ref[...].astype(F32) * g_ref[...] + b_ref[...]).astype(BF16)
        acc_gate[...] += _dot_tn(h, dgl_ref[...])
        acc_ple[...] += _dot_tn(p_ref[...].astype(BF16), de_ref[...])

        @pl.when(k == n_k - 1)
        def _():
            o_out[...] = acc_out[...].astype(BF16)
            o_gate[...] = acc_gate[...].astype(BF16)
            o_ple[...] = acc_ple[...].astype(BF16)

    row = lambda cols: pl.BlockSpec((tk, cols), lambda k: (k, 0))
    return pl.pallas_call(
        body, name=name, grid=(n_k,),
        out_shape=(jax.ShapeDtypeStruct((d_model, d_model), BF16), jax.ShapeDtypeStruct((d_model, d_model), BF16),
                   jax.ShapeDtypeStruct((d_ple, d_model), BF16)),
        in_specs=[row(d_model), row(d_model), row(d_model), _const(ln_g.shape), _const(ln_b.shape), row(d_model),
                  pl.BlockSpec((None, tk, d_ple), lambda k: (layer, k, 0)), row(d_model)],
        out_specs=(_const((d_model, d_model)), _const((d_model, d_model)), _const((d_ple, d_model))),
        scratch_shapes=[pltpu.VMEM((d_model, d_model), F32), pltpu.VMEM((d_model, d_model), F32),
                        pltpu.VMEM((d_ple, d_model), F32)],
        compiler_params=_params(1),
    )(y, dr, rh, ln_g, ln_b, dgl, p_all, de)


def _adamw_reduce(name, parts, w, m, v, rows_per_block):
    n_rows, n_cols = w.shape
    br = rows_per_block
    n_parts = parts.shape[0]

    def body(p_ref, w_ref, m_ref, v_ref, g_ref, d_ref, nm_ref, nv_ref):
        g = p_ref[0].astype(F32)
        for k in range(1, n_parts):
            g = g + p_ref[k].astype(F32)
        nm = ADAM_B1 * m_ref[...] + (1.0 - ADAM_B1) * g
        nv = ADAM_B2 * v_ref[...] + (1.0 - ADAM_B2) * (g * g)
        m_hat = nm / (1.0 - ADAM_B1 ** ADAM_STEP)
        v_hat = nv / (1.0 - ADAM_B2 ** ADAM_STEP)
        g_ref[...] = g
        d_ref[...] = -ADAM_LR * (m_hat / (jnp.sqrt(v_hat) + ADAM_EPS) + ADAM_WD * w_ref[...])
        nm_ref[...] = nm
        nv_ref[...] = nv

    blk = pl.BlockSpec((br, n_cols), lambda i: (i, 0))
    shp = jax.ShapeDtypeStruct((n_rows, n_cols), F32)
    return pl.pallas_call(
        body, name=name, grid=(n_rows // br,), out_shape=(shp, shp, shp, shp),
        in_specs=[pl.BlockSpec((n_parts, br, n_cols), lambda i: (0, i, 0)), blk, blk, blk], out_specs=(blk, blk, blk, blk),
        compiler_params=_params(1),
    )(parts, w, m, v)


def _pack_rows(flat_parts, pad_to=None, width=LANE):
    flat = jnp.concatenate([a.reshape(-1) for a in flat_parts])
    if pad_to is not None and pad_to > flat.shape[0]:
        flat = jnp.concatenate([flat, jnp.zeros((pad_to - flat.shape[0],), flat.dtype)])
    return flat.reshape(-1, width)


def _unpack_rows(packed, shapes):
    flat = packed.reshape(-1)
    out, off = [], 0
    for s in shapes:
        n = math.prod(s)
        out.append(flat[off:off + n].reshape(s))
        off += n
    return out


def _to_dest_major(full):
    lead, last = full.shape[:-1], full.shape[-1]
    t = full.reshape(lead + (N_DEV, last // N_DEV))
    return jnp.moveaxis(t, -2, 0).reshape(N_DEV, -1)


def _from_source_major(blocks, shard_shape):
    t = blocks.reshape((N_DEV,) + tuple(shard_shape))
    t = jnp.moveaxis(t, 0, -2)
    return t.reshape(tuple(shard_shape[:-1]) + (N_DEV * shard_shape[-1],))


def _block_rows(n_rows, n_cols, target_elems=96 * 1024):
    best = None
    for br in range(SUBLANE, n_rows + 1, SUBLANE):
        if n_rows % br == 0 and br * n_cols <= target_elems:
            best = br
    return n_rows if best is None else best


def kernel(x, p, w_in_e, b_in_e, conv_a_w, conv_a_b, ln_a_g, ln_a_b, ln_v_g, ln_v_b, w_s, b_s, w_out_e, b_out_e, w_in_o, b_in_o, w_pool, pool_scale, conv_d_w, w_out_o, b_out_o, ln_g, ln_b, w_ple, w_ple_gate, b_ple_gate, loss_target, m_w_in_e, m_b_in_e, m_conv_a_w, m_conv_a_b, m_ln_a_g, m_ln_a_b, m_ln_v_g, m_ln_v_b, m_w_s, m_b_s, m_w_out_e, m_b_out_e, m_w_in_o, m_b_in_o, m_w_pool, m_pool_scale, m_conv_d_w, m_w_out_o, m_b_out_o, m_ln_g, m_ln_b, m_w_ple, m_w_ple_gate, m_b_ple_gate, v_w_in_e, v_b_in_e, v_conv_a_w, v_conv_a_b, v_ln_a_g, v_ln_a_b, v_ln_v_g, v_ln_v_b, v_w_s, v_b_s, v_w_out_e, v_b_out_e, v_w_in_o, v_b_in_o, v_w_pool, v_pool_scale, v_conv_d_w, v_w_out_o, v_b_out_o, v_ln_g, v_ln_b, v_w_ple, v_w_ple_gate, v_b_ple_gate):
    weights = dict(w_in_e=w_in_e, b_in_e=b_in_e, conv_a_w=conv_a_w, conv_a_b=conv_a_b, ln_a_g=ln_a_g, ln_a_b=ln_a_b,
                   ln_v_g=ln_v_g, ln_v_b=ln_v_b, w_s=w_s, b_s=b_s, w_out_e=w_out_e, b_out_e=b_out_e, w_in_o=w_in_o,
                   b_in_o=b_in_o, w_pool=w_pool, pool_scale=pool_scale, conv_d_w=conv_d_w, w_out_o=w_out_o,
                   b_out_o=b_out_o, ln_g=ln_g, ln_b=ln_b, w_ple=w_ple, w_ple_gate=w_ple_gate, b_ple_gate=b_ple_gate)
    mom_m = dict(w_in_e=m_w_in_e, b_in_e=m_b_in_e, conv_a_w=m_conv_a_w, conv_a_b=m_conv_a_b, ln_a_g=m_ln_a_g,
                 ln_a_b=m_ln_a_b, ln_v_g=m_ln_v_g, ln_v_b=m_ln_v_b, w_s=m_w_s, b_s=m_b_s, w_out_e=m_w_out_e,
                 b_out_e=m_b_out_e, w_in_o=m_w_in_o, b_in_o=m_b_in_o, w_pool=m_w_pool, pool_scale=m_pool_scale,
                 conv_d_w=m_conv_d_w, w_out_o=m_w_out_o, b_out_o=m_b_out_o, ln_g=m_ln_g, ln_b=m_ln_b, w_ple=m_w_ple,
                 w_ple_gate=m_w_ple_gate, b_ple_gate=m_b_ple_gate)
    mom_v = dict(w_in_e=v_w_in_e, b_in_e=v_b_in_e, conv_a_w=v_conv_a_w, conv_a_b=v_conv_a_b, ln_a_g=v_ln_a_g,
                 ln_a_b=v_ln_a_b, ln_v_g=v_ln_v_g, ln_v_b=v_ln_v_b, w_s=v_w_s, b_s=v_b_s, w_out_e=v_w_out_e,
                 b_out_e=v_b_out_e, w_in_o=v_w_in_o, b_in_o=v_b_in_o, w_pool=v_w_pool, pool_scale=v_pool_scale,
                 conv_d_w=v_conv_d_w, w_out_o=v_w_out_o, b_out_o=v_b_out_o, ln_g=v_ln_g, ln_b=v_ln_b, w_ple=v_w_ple,
                 w_ple_gate=v_w_ple_gate, b_ple_gate=v_b_ple_gate)
    names = tuple(weights)

    batch, seq, d_model = x.shape
    t_len = batch * seq
    w = d_model // 2
    n_even = w_in_e.shape[0]
    n_odd = w_in_o.shape[0]
    depth = ln_g.shape[0]
    d_ple = p.shape[-1]
    n_heads = w_s.shape[1]
    tm = 512 if seq % 512 == 0 and seq >= 1024 else seq // 2
    in_cols = w_in_e.shape[-1]
    out_rows = w_out_e.shape[1]
    ple_cols = w_ple.shape[-1]
    gate_rows = w_ple_gate.shape[1]

    sh_shapes = [weights[n].shape for n in SH_NAMES]
    sh_len = sum(math.prod(s) for s in sh_shapes)
    sh_pad = -(-sh_len // (SUBLANE * LANE)) * (SUBLANE * LANE)
    sh_rows = sh_pad // LANE
    sds = jax.ShapeDtypeStruct
    w_in16 = (w_in_e.astype(BF16), w_in_o.astype(BF16))
    w_out16 = (w_out_e.astype(BF16), w_out_o.astype(BF16))
    w_ple16, w_gate16 = w_ple.astype(BF16), w_ple_gate.astype(BF16)

    kinds = ("in", "out", "ple", "gate")

    def weight_entries(i, which=kinds):
        j, par = i // 2, i % 2
        all_four = {"in": (w_in16[par][j], _whole_view, sds((d_model, N_DEV * in_cols), BF16), _axis_view(1, in_cols)),
                    "out": (w_out16[par][j], _whole_view, sds((N_DEV * out_rows, d_model), BF16), _axis_view(0, out_rows)),
                    "ple": (w_ple16[i], _whole_view, sds((d_ple, N_DEV * ple_cols), BF16), _axis_view(1, ple_cols)),
                    "gate": (w_gate16[i], _whole_view, sds((N_DEV * gate_rows, d_model), BF16), _axis_view(0, gate_rows))}
        return [((i, k), all_four[k]) for k in which]

    fwd_riders = {("mixer", 0): weight_entries(0, kinds[1:]) + weight_entries(1, kinds[:1]),
                  ("post", 0): weight_entries(1, kinds[1:])}
    for i in range(1, depth - 1):
        if i % 2:
            fwd_riders[("mixer", i)] = weight_entries(i + 1, kinds[:1])
            fwd_riders[("post", i)] = weight_entries(i + 1, kinds[1:])
        else:
            fwd_riders[("mixer", i)] = weight_entries(i + 1)
    layer_w = {}

    def carried(where):
        tagged = fwd_riders.get(where)
        if tagged is None:
            return None, lambda landed: None
        return _Rider([e for _, e in tagged]), lambda landed: layer_w.update(zip([t for t, _ in tagged], landed))

    w_in_first = weight_entries(0, kinds[:1])[0][1]
    first = _gather_two_level("gather_first", [w_in_first[0], _pack_rows([weights[n] for n in SH_NAMES], sh_pad)],
                              [w_in_first[2], sds((N_DEV, sh_rows, LANE), F32)], [w_in_first[3], _slot_view()])
    layer_w[(0, "in")] = first[0]
    sh_flat = first[1].reshape(N_DEV, sh_pad)
    full_small, off = {}, 0
    for n, s in zip(SH_NAMES, sh_shapes):
        size = math.prod(s)
        full_small[n] = _from_source_major(sh_flat[:, off:off + size], s)
        off += size

    tril = jnp.tril(jnp.ones((LANE, LANE), dtype=bool))
    ws_m = jnp.where(tril[None, None], w_s, 0.0)
    pair = lambda t: jnp.concatenate([t[:, 0::2], t[:, 1::2]], axis=-1).astype(BF16)
    wcat = pair(ws_m)
    wcat_t = pair(jnp.swapaxes(ws_m, -1, -2))
    bs_full = jnp.repeat(jnp.swapaxes(b_s, -1, -2), w // n_heads, axis=-1)
    row2 = lambda a, j: a[j][None, :]

    x2 = x.reshape(t_len, d_model)
    p3 = p.reshape(depth, t_len, d_ple)
    lt2 = loss_target.reshape(t_len, d_model)

    xs, saved = [x2], []
    dxn = sse = None
    for i in range(depth):
        j = i // 2
        last = i == depth - 1
        rider, file_weights = carried(("mixer", i))
        if i % 2 == 0:
            outs = _even_fwd(xs[i], layer_w[(i, "in")], row2(b_in_e, j), full_small["conv_a_w"][j], row2(conv_a_b, j),
                             row2(ln_a_g, j), row2(ln_a_b, j), row2(ln_v_g, j), row2(ln_v_b, j), wcat[j], bs_full[j],
                             seq=seq, tm=tm, rider=rider)
            b_out = row2(b_out_e, j)
        else:
            outs = _odd_fwd(xs[i], layer_w[(i, "in")], row2(full_small["b_in_o"], j), w_pool[j].astype(BF16),
                            row2(full_small["pool_scale"], j), full_small["conv_d_w"][j], seq=seq, tm=tm, rider=rider)
            b_out = row2(full_small["b_out_o"], j)
        z, y, s1, s2 = outs[:4]
        file_weights(outs[4:])
        rider, file_weights = carried(("post", i))
        outs = _post_fwd(y, xs[i], p3, i, layer_w[(i, "out")], b_out, row2(ln_g, i), row2(ln_b, i), layer_w[(i, "gate")],
                         row2(b_ple_gate, i), layer_w[(i, "ple")], lt2 if last else None, tm=tm, rider=rider)
        rh, rstd, gate, e = outs[:4]
        if last:
            dxn, sse = outs[4:6]
        else:
            xs.append(outs[4])
        file_weights(outs[6 if last else 5:])
        saved.append(dict(z=z, y=y, s1=s1, s2=s2, rh=rh, rstd=rstd, gate=gate, e=e))


    recv = {"w_in_e": None, "w_in_o": None, "w_out_e": None, "w_out_o": None, "w_ple": None, "w_ple_gate": None}

    def grad_entry(i, kind, g, half=None):
        j, par = i // 2, i % 2
        sfx = "_o" if par else "_e"
        name, src_view, slot = {"in": ("w_in" + sfx, _axis_view(1, in_cols), j), "out": ("w_out" + sfx, _axis_view(0, out_rows), j),
                                "ple": ("w_ple", _axis_view(1, ple_cols), i), "gate": ("w_ple_gate", _axis_view(0, gate_rows), i)}[kind]
        dst = recv[name] if recv[name] is not None else sds((N_DEV,) + weights[name].shape, BF16)
        if half is None:
            dst_view = _slot_view(slot)
        else:
            rows = pl.ds(half[0] * half[1], half[1])
            dst_view = lambda ref, d: ref.at[d, slot, rows, :]
            if g.shape[0] != half[1]:
                src_view = lambda ref, d: ref.at[rows, pl.ds(pl.multiple_of(d * in_cols, in_cols), in_cols)]
        return name, (g, src_view, dst, dst_view)

    def ride(tagged):
        if not tagged:
            return None, lambda landed: None
        return _Rider([e for _, e in tagged]), lambda landed: recv.update(zip([n for n, _ in tagged], landed))

    small = {n: [None] * weights[n].shape[0] for n in REP_NAMES + SH_NAMES}
    grads = {}
    half_rows = d_model // 2
    for i in reversed(range(depth)):
        j, par = i // 2, i % 2
        sv = saved[i]
        w_in = layer_w[(i, "in")]
        dx_dtype = BF16 if i > 0 else F32
        split = (i + 1, "in") if par == 1 and (i + 1, "in") in grads else None
        rider, file_landed = ride([grad_entry(*split, grads[split], half=(0, half_rows))] if split else [])
        outs = _post_bwd(dxn, sv["gate"], sv["e"], sv["rh"], sv["rstd"], row2(ln_g, i), layer_w[(i, "gate")],
                         layer_w[(i, "out")], tm=tm, rider=rider)
        de, dgl, dy, dr, acc = outs[:5]
        file_landed(outs[5:])
        small["b_ple_gate"][i], small["ln_g"][i], small["ln_b"][i] = acc[0], acc[1], acc[2]
        small["b_out_o" if par else "b_out_e"][j] = acc[3]
        grads[(i, "out")], grads[(i, "gate")], grads[(i, "ple")] = _weight_grads_post(
            f"dw_post_l{i}", sv["y"], dr, sv["rh"], row2(ln_g, i), row2(ln_b, i), dgl, p3, i, de)
        rider, file_landed = ride([grad_entry(l, k, grads.pop((l, k)), half=(1, half_rows) if (l, k) == split else None)
                                   for l, k in list(grads)])
        if par == 0:
            outs = _even_bwd(dy, sv["z"], sv["s1"], sv["s2"], dr, w_in, full_small["conv_a_w"][j],
                             row2(ln_a_g, j), row2(ln_a_b, j), row2(ln_v_g, j), row2(ln_v_b, j), wcat_t[j], seq=seq, tm=tm,
                             dx_dtype=dx_dtype, rider=rider)
            dz, dx, dcw, vec, dbin, dws, dbs = outs[:7]
            file_landed(outs[7:])
            small["conv_a_w"][j], small["conv_a_b"][j] = dcw, vec[0]
            small["ln_a_g"][j], small["ln_a_b"][j], small["ln_v_g"][j], small["ln_v_b"][j] = vec[1], vec[2], vec[3], vec[4]
            small["b_in_e"][j], small["w_s"][j] = dbin[0], dws
            small["b_s"][j] = _head_sums(dbs, n_heads)
        else:
            outs = _odd_bwd(dy, sv["z"], sv["s1"], sv["s2"], dr, w_in, w_pool[j].astype(BF16),
                            row2(full_small["pool_scale"], j), full_small["conv_d_w"][j], seq=seq, tm=tm,
                            dx_dtype=dx_dtype, rider=rider)
            dz, dx, dwp, vec, dbin = outs[:5]
            file_landed(outs[5:])
            small["w_pool"][j], small["pool_scale"][j], small["conv_d_w"][j] = dwp, vec[0], vec[1:1 + CONV_D_WIDTH]
            small["b_in_o"][j] = dbin[0]
        if i > 0:
            grads[(i, "in")] = _weight_grad(f"dw_in_l{i}", xs[i], dz, bn=in_cols * N_DEV // 2)
        dxn = dx
    grad_x = dxn.reshape(batch, seq, d_model)

    small_full = {n: jnp.stack(small[n]) for n in small}
    rep_width = 4 * LANE
    rep_len = sum(math.prod(weights[n].shape) for n in REP_NAMES)
    rep_block = N_DEV * SUBLANE
    rep_pad = -(-(rep_len + 1) // (rep_block * rep_width)) * (rep_block * rep_width)
    rep_part = _pack_rows([small_full[n] for n in REP_NAMES] + [sse[0:1, 0]], rep_pad, rep_width)

    top, rep_slots = _weight_grad("dw_in_l0_top", xs[0], dz, a_cols=(0, half_rows), rider=_Rider([_scatter_rows_entry(rep_part)]))
    rider, file_landed = ride([grad_entry(0, "in", top, half=(0, half_rows))])
    outs = _weight_grad("dw_in_l0_bottom", xs[0], dz, a_cols=(1, half_rows), rider=rider)
    file_landed(outs[1:])

    sh_part = jnp.concatenate([_to_dest_major(small_full[n]) for n in SH_NAMES], axis=1)
    sh_part = jnp.concatenate([sh_part, jnp.zeros((N_DEV, sh_pad - sh_len), F32)], axis=1).reshape(N_DEV, sh_rows, LANE)
    name, entry = grad_entry(0, "in", outs[0], half=(1, half_rows))
    rep_sum, recv[name], sh_landed = _sum_and_gather_rows(
        "exchange_last", rep_slots, _Rider([entry, (sh_part, _slot_view(), sds((N_DEV, sh_rows, LANE), F32), _slot_view())]))
    loss = (0.5 / d_model) * rep_sum.reshape(-1)[rep_len]

    results = {}
    for n, parts in recv.items():
        shp = weights[n].shape
        rows, cols = math.prod(shp[:-1]), shp[-1]
        two = lambda a: a.reshape(rows, cols)
        outs = _adamw_reduce("adamw_" + n, parts.reshape(N_DEV, rows, cols), two(weights[n]), two(mom_m[n]), two(mom_v[n]),
                             _block_rows(rows, cols))
        results[n] = [o.reshape(shp) for o in outs]
    pack_sh = lambda d: _pack_rows([d[n] for n in SH_NAMES], sh_pad)
    outs = _adamw_reduce("adamw_small_sharded", sh_landed, pack_sh(weights), pack_sh(mom_m), pack_sh(mom_v), sh_rows)
    for n, *vals in zip(SH_NAMES, *[_unpack_rows(o, sh_shapes) for o in outs]):
        results[n] = vals
    pack_rep = lambda d: _pack_rows([d[n] for n in REP_NAMES], rep_pad, rep_width)
    rep_shapes = [weights[n].shape for n in REP_NAMES]
    outs = _adamw_reduce("adamw_replicated", rep_sum[None], pack_rep(weights), pack_rep(mom_m), pack_rep(mom_v), rep_block)
    for n, *vals in zip(REP_NAMES, *[_unpack_rows(o, rep_shapes) for o in outs]):
        results[n] = vals

    return (loss, grad_x, *[results[n][0] for n in names], *[results[n][1] for n in names],
            *[results[n][2] for n in names], *[results[n][3] for n in names])


def _head_sums(dbs, n_heads):
    t, width = dbs.shape
    return jnp.sum(dbs.reshape(t, n_heads, width // n_heads), axis=-1).T
```
